```python
import jax
import jax.numpy as jnp
from jax import lax
import numpy as np

D_MODEL = 1024
BATCH = 1
SEQ = 16384
DEPTH = 2
DEC_BATCH = 8
DEC_SEQ = 32
PAST_LEN = 2048

CHUNK = 64
Q_BLOCK = 128
HEAD_DIM = 64
D_BRANCH = 512
N_HEADS_A = D_BRANCH // HEAD_DIM
N_IDX_HEADS = 8
IDX_DIM = 64
TOPK_MAX = 256
N_HEADS_B = D_BRANCH // HEAD_DIM
CONV_W = 3
D_FF = 2816
N_BRANCH = 3
N_IN = (3 * D_BRANCH + N_IDX_HEADS * IDX_DIM + IDX_DIM + N_IDX_HEADS
        + 3 * D_BRANCH + N_HEADS_B + 3 * D_BRANCH + N_BRANCH * D_MODEL)
EPS = 1e-6
NEG_INF = -1e30

kernel_name = 'chunk_causal_hybrid_dsa_fox_conv_step'


def _rmsnorm(x, g):
    xf = x.astype(jnp.float32)
    y = xf * lax.rsqrt(jnp.mean(xf * xf, axis=-1, keepdims=True) + EPS)
    return (y * g.astype(jnp.float32)).astype(x.dtype)


def _causal_dwconv(u, w, past):
    T = u.shape[1]
    full = jnp.concatenate([past.astype(u.dtype), u], axis=1)
    y = sum(full[:, k:k + T] * w[k] for k in range(CONV_W))
    return y, full[:, -(CONV_W - 1):]


def _in_split_points():
    sizes = (D_BRANCH, D_BRANCH, D_BRANCH, N_IDX_HEADS * IDX_DIM, IDX_DIM, N_IDX_HEADS,
             D_BRANCH, D_BRANCH, D_BRANCH, N_HEADS_B,
             D_BRANCH, D_BRANCH, D_BRANCH)
    points, acc = [], 0
    for s in sizes:
        acc += s
        points.append(acc)
    return points


def _sweep_queries(block_fn, q_inputs):
    T = q_inputs[0].shape[1]
    if T % Q_BLOCK != 0 or T <= Q_BLOCK:
        return block_fn(0, *q_inputs)
    nb = T // Q_BLOCK

    def to_blocks(a):
        a = a.reshape(a.shape[0], nb, Q_BLOCK, *a.shape[2:])
        return jnp.moveaxis(a, 1, 0)

    starts = jnp.arange(nb, dtype=jnp.int32) * Q_BLOCK
    out = lax.map(lambda args: block_fn(args[0], *args[1:]),
                  (starts, *[to_blocks(a) for a in q_inputs]))
    out = jnp.moveaxis(out, 0, 1)
    return out.reshape(out.shape[0], T, *out.shape[3:])


def _dsa_attend(q, qi, wi, k, v, ki, past_len):
    L = k.shape[1]
    n_sel = min(TOPK_MAX, L // 4)
    kpos = jnp.arange(L, dtype=jnp.int32)
    scale = HEAD_DIM ** -0.5

    def block(start, qb, qib, wib):
        nq = qb.shape[1]
        qpos = past_len + start + jnp.arange(nq, dtype=jnp.int32)
        chunk_end = (qpos // CHUNK + 1) * CHUNK
        dots = jnp.einsum('bqhe,bse->bqhs', qib, ki).astype(jnp.float32)
        score = jnp.einsum('bqhs,bqh->bqs', jax.nn.relu(dots), wib.astype(jnp.float32))
        score = jnp.where(kpos[None, None, :] < chunk_end[None, :, None], score, -jnp.inf)
        _, idx = lax.top_k(score, n_sel)
        valid = idx < chunk_end[None, :, None]
        k_sel = jax.vmap(lambda kk, ii: kk[ii])(k, idx)
        v_sel = jax.vmap(lambda vv, ii: vv[ii])(v, idx)
        logits = jnp.einsum('bqhd,bqkhd->bhqk', qb, k_sel).astype(jnp.float32) * scale
        logits = jnp.where(valid[:, None], logits, NEG_INF)
        p = jax.nn.softmax(logits, axis=-1).astype(v.dtype)
        return jnp.einsum('bhqk,bqkhd->bqhd', p, v_sel)

    return _sweep_queries(block, (q, qi, wi))


def _fox_attend(q, k, v, logf_all, past_len):
    L = k.shape[1]
    cum = jnp.cumsum(logf_all.astype(jnp.float32), axis=1)
    cum_k = jnp.moveaxis(cum, 1, 2)
    cum_q = cum[:, past_len:]
    kpos = jnp.arange(L, dtype=jnp.int32)
    scale = HEAD_DIM ** -0.5

    def block(start, qb, cq):
        nq = qb.shape[1]
        qpos = past_len + start + jnp.arange(nq, dtype=jnp.int32)
        logits = jnp.einsum('bqhd,bshd->bhqs', qb, k).astype(jnp.float32) * scale
        logits = logits + jnp.moveaxis(cq, 1, 2)[..., None] - cum_k[:, :, None, :]
        logits = jnp.where(kpos[None, None, None, :] <= qpos[None, None, :, None], logits, NEG_INF)
        p = jax.nn.softmax(logits, axis=-1).astype(v.dtype)
        return jnp.einsum('bhqs,bshd->bqhd', p, v)

    return _sweep_queries(block, (q, cum_q))


def _layer(x, c, past_idx_k, past_dsa_k, past_dsa_v, past_fox_k, past_fox_v, past_fox_logf,
           past_conv_mix, past_conv_ffn,
           w_ada, b_ada, norm_g, w_in, b_forget, conv_mix_w, w_branch, w_out, w_up,
           conv_ffn_w, w_down):
    B, T, _ = x.shape
    P = past_dsa_k.shape[1]
    mod = jnp.dot(jax.nn.silu(c), w_ada) + b_ada
    sh1, sc1, g1, sh2, sc2, g2 = [m[:, None, :] for m in jnp.split(mod, 6, axis=-1)]

    h = _rmsnorm(x, norm_g[0]) * (1 + sc1) + sh1
    z = h @ w_in
    (qa, ka, va, qi, ki, wi, qb, kb, vb, fl, cb, cc, cx, gl) = jnp.split(z, _in_split_points(), axis=-1)
    qa = qa.reshape(B, T, N_HEADS_A, HEAD_DIM)
    ka = ka.reshape(B, T, N_HEADS_A, HEAD_DIM)
    va = va.reshape(B, T, N_HEADS_A, HEAD_DIM)
    qi = qi.reshape(B, T, N_IDX_HEADS, IDX_DIM)
    qb = qb.reshape(B, T, N_HEADS_B, HEAD_DIM)
    kb = kb.reshape(B, T, N_HEADS_B, HEAD_DIM)
    vb = vb.reshape(B, T, N_HEADS_B, HEAD_DIM)

    ki_all = jnp.concatenate([past_idx_k.astype(ki.dtype), ki], axis=1)
    ka_all = jnp.concatenate([past_dsa_k.astype(ka.dtype), ka], axis=1)
    va_all = jnp.concatenate([past_dsa_v.astype(va.dtype), va], axis=1)
    ya = _dsa_attend(qa, qi, wi, ka_all, va_all, ki_all, P).reshape(B, T, D_BRANCH)

    logf = jax.nn.log_sigmoid((fl + b_forget).astype(jnp.float32))
    logf_all = jnp.concatenate([past_fox_logf.astype(jnp.float32), logf], axis=1)
    kb_all = jnp.concatenate([past_fox_k.astype(kb.dtype), kb], axis=1)
    vb_all = jnp.concatenate([past_fox_v.astype(vb.dtype), vb], axis=1)
    yb = _fox_attend(qb, kb_all, vb_all, logf_all, P).reshape(B, T, D_BRANCH)

    conv_out, new_conv_mix = _causal_dwconv(cc * cx, conv_mix_w, past_conv_mix)
    yc = cb * conv_out

    br = jnp.einsum('btnc,ncd->btnd', jnp.stack([ya, yb, yc], axis=2), w_branch)
    gates = jax.nn.sigmoid(gl.reshape(B, T, N_BRANCH, D_MODEL))
    mix = jnp.sum(gates * br, axis=2) @ w_out
    x = x + g1 * _rmsnorm(mix, norm_g[1])

    h2 = _rmsnorm(x, norm_g[2]) * (1 + sc2) + sh2
    ug, uv = jnp.split(h2 @ w_up, 2, axis=-1)
    ug_c, new_conv_ffn = _causal_dwconv(ug, conv_ffn_w, past_conv_ffn)
    f = (jax.nn.silu(ug_c) * uv) @ w_down
    x = x + g2 * _rmsnorm(f, norm_g[3])
    return x, (ki, ka, va, kb, vb, logf, new_conv_mix, new_conv_ffn)


def _stack_layers(states):
    return [jnp.stack([st[i] for st in states], axis=0) for i in range(len(states[0]))]


def setup_inputs(seed: int = 0) -> dict:
    key = jax.random.key(seed)
    ks = jax.random.split(key, 24)
    f32 = jnp.float32

    def nrm(k, shape, s):
        return jax.random.normal(k, shape, f32) * s

    return {
        'x_prompt': nrm(ks[0], (BATCH, SEQ, D_MODEL), 1.0),
        'x_sample': nrm(ks[1], (DEC_BATCH, DEC_SEQ, D_MODEL), 1.0),
        'c_prompt': nrm(ks[2], (BATCH, D_MODEL), 1.0),
        'c_sample': nrm(ks[3], (DEC_BATCH, D_MODEL), 1.0),
        'cache_idx_k': nrm(ks[4], (DEPTH, DEC_BATCH, PAST_LEN, IDX_DIM), 1.0),
        'cache_dsa_k': nrm(ks[5], (DEPTH, DEC_BATCH, PAST_LEN, N_HEADS_A, HEAD_DIM), 1.0),
        'cache_dsa_v': nrm(ks[6], (DEPTH, DEC_BATCH, PAST_LEN, N_HEADS_A, HEAD_DIM), 1.0),
        'cache_fox_k': nrm(ks[7], (DEPTH, DEC_BATCH, PAST_LEN, N_HEADS_B, HEAD_DIM), 1.0),
        'cache_fox_v': nrm(ks[8], (DEPTH, DEC_BATCH, PAST_LEN, N_HEADS_B, HEAD_DIM), 1.0),
        'cache_fox_logf': jax.nn.log_sigmoid(nrm(ks[9], (DEPTH, DEC_BATCH, PAST_LEN, N_HEADS_B), 1.0) + 1.0),
        'state_conv_mix': nrm(ks[10], (DEPTH, DEC_BATCH, CONV_W - 1, D_BRANCH), 1.0),
        'state_conv_ffn': nrm(ks[11], (DEPTH, DEC_BATCH, CONV_W - 1, D_FF), 1.0),
        'w_ada': nrm(ks[12], (DEPTH, D_MODEL, 6 * D_MODEL), 0.5 * D_MODEL ** -0.5),
        'b_ada': nrm(ks[13], (DEPTH, 6 * D_MODEL), 0.01),
        'norm_g': 1.0 + nrm(ks[14], (DEPTH, 4, D_MODEL), 0.1),
        'w_in': nrm(ks[15], (DEPTH, D_MODEL, N_IN), D_MODEL ** -0.5),
        'b_forget': 1.0 + nrm(ks[16], (DEPTH, N_HEADS_B), 0.5),
        'conv_mix_w': nrm(ks[17], (DEPTH, CONV_W, D_BRANCH), CONV_W ** -0.5),
        'w_branch': nrm(ks[18], (DEPTH, N_BRANCH, D_BRANCH, D_MODEL), D_BRANCH ** -0.5),
        'w_out': nrm(ks[19], (DEPTH, D_MODEL, D_MODEL), D_MODEL ** -0.5),
        'w_up': nrm(ks[20], (DEPTH, D_MODEL, 2 * D_FF), D_MODEL ** -0.5),
        'conv_ffn_w': nrm(ks[21], (DEPTH, CONV_W, D_FF), CONV_W ** -0.5),
        'w_down': nrm(ks[22], (DEPTH, D_FF, D_MODEL), D_FF ** -0.5),
    }


def reference(x_prompt, x_sample, c_prompt, c_sample, cache_idx_k, cache_dsa_k, cache_dsa_v,
              cache_fox_k, cache_fox_v, cache_fox_logf, state_conv_mix, state_conv_ffn,
              w_ada, b_ada, norm_g, w_in, b_forget, conv_mix_w, w_branch, w_out, w_up,
              conv_ffn_w, w_down):
    B = x_prompt.shape[0]
    dt = x_prompt.dtype
    e_idx_k = jnp.zeros((B, 0, IDX_DIM), dt)
    e_dsa = jnp.zeros((B, 0, N_HEADS_A, HEAD_DIM), dt)
    e_fox = jnp.zeros((B, 0, N_HEADS_B, HEAD_DIM), dt)
    e_logf = jnp.zeros((B, 0, N_HEADS_B), jnp.float32)
    z_conv_mix = jnp.zeros((B, CONV_W - 1, D_BRANCH), dt)
    z_conv_ffn = jnp.zeros((B, CONV_W - 1, D_FF), dt)

    yp, ys = x_prompt, x_sample
    p_states, s_states = [], []
    for l in range(DEPTH):
        lw = (w_ada[l], b_ada[l], norm_g[l], w_in[l], b_forget[l], conv_mix_w[l],
              w_branch[l], w_out[l], w_up[l], conv_ffn_w[l], w_down[l])
        yp, st_p = _layer(yp, c_prompt, e_idx_k, e_dsa, e_dsa, e_fox, e_fox, e_logf,
                          z_conv_mix, z_conv_ffn, *lw)
        ys, st_s = _layer(ys, c_sample, cache_idx_k[l], cache_dsa_k[l], cache_dsa_v[l],
                          cache_fox_k[l], cache_fox_v[l], cache_fox_logf[l],
                          state_conv_mix[l], state_conv_ffn[l], *lw)
        p_states.append(st_p)
        s_states.append(st_s)

    (p_idx_k, p_dsa_k, p_dsa_v, p_fox_k, p_fox_v, p_fox_logf,
     p_conv_mix, p_conv_ffn) = _stack_layers(p_states)
    (s_idx_k, s_dsa_k, s_dsa_v, s_fox_k, s_fox_v, s_fox_logf,
     s_conv_mix, s_conv_ffn) = _stack_layers(s_states)
    return (yp, ys, p_idx_k, p_dsa_k, p_dsa_v, p_fox_k, p_fox_v, p_fox_logf, p_conv_mix, p_conv_ffn,
            s_idx_k, s_dsa_k, s_dsa_v, s_fox_k, s_fox_v, s_fox_logf, s_conv_mix, s_conv_ffn)
```

```python
import functools

import jax
import jax.numpy as jnp
from jax import lax
from jax.experimental import pallas as pl
from jax.experimental.pallas import tpu as pltpu

F32 = jnp.float32
BF16 = jnp.bfloat16
I32 = jnp.int32

D_MODEL = 1024
HEAD_DIM = 64
D_BRANCH = 512
N_HEADS = 8
IDX_DIM = 64
CHUNK = 64
CHUNK_SHIFT = 6
TOPK_MAX = 256
CONV_W = 3
EPS = 1e-6
NEG_INF = -1e30
F32_MAX = 3.4028234663852886e38
INT32_MIN = -(2 ** 31)

LANES = 128
N_MAIN = 16 * D_BRANCH
KEY_CHUNK = 256
SCORE_ROWS = 128
COUNT_ROWS = 256
VMEM_LIMIT = 56 * 1024 * 1024


def _cparams(sem):
    return pltpu.CompilerParams(dimension_semantics=sem, vmem_limit_bytes=VMEM_LIMIT)


def _rms(x, g_row):
    return x * lax.rsqrt(jnp.mean(x * x, axis=-1, keepdims=True) + EPS) * g_row


def _group(x, nb):
    return x.reshape(nb, x.shape[0] // nb, x.shape[1])


def _mod_kernel(c_ref, w_ref, b_ref, o_ref):
    c = c_ref[...]
    s = (c * jax.nn.sigmoid(c)).astype(BF16)
    o_ref[0] = jnp.dot(s, w_ref[0].astype(BF16), preferred_element_type=F32) + b_ref[0]


def _modulation(c_all, w_ada, b_ada):
    depth = w_ada.shape[0]
    rows = c_all.shape[0]
    n = w_ada.shape[2]
    tn = D_MODEL
    return pl.pallas_call(
        _mod_kernel,
        out_shape=jax.ShapeDtypeStruct((depth, rows, n), F32),
        grid=(depth, n // tn),
        in_specs=[
            pl.BlockSpec((rows, D_MODEL), lambda l, j: (0, 0)),
            pl.BlockSpec((1, D_MODEL, tn), lambda l, j: (l, 0, j)),
            pl.BlockSpec((1, 1, tn), lambda l, j: (l, 0, j)),
        ],
        out_specs=pl.BlockSpec((1, rows, tn), lambda l, j: (l, 0, j)),
        compiler_params=_cparams(("arbitrary", "arbitrary")),
        name="adaln_mod",
    )(c_all, w_ada, b_ada.reshape(depth, 1, n))


_SEG_QA, _SEG_KA, _SEG_VA, _SEG_QI, _SEG_QB, _SEG_KB, _SEG_VB, _SEG_CB, _SEG_CC, _SEG_CX, _SEG_GL = range(11)
_WI_OFF = IDX_DIM
_FL_OFF = IDX_DIM + N_HEADS


def _inproj_kernel(x_ref, mod_ref, g_ref, wm_ref, ws_ref, bf_ref,
                   qaT_ref, ka32_ref, ka16_ref, va32_ref, vaT_ref, qiT_ref, qbT_ref,
                   kb32_ref, kb16_ref, vb32_ref, vbT_ref, cb_ref, u_ref, gl_ref,
                   ki32_ref, ki16_ref, wiT_ref, logf_ref,
                   hs_ref, cc_ref, *, nb):
    j = pl.program_id(1)
    tm = x_ref.shape[0]

    @pl.when(j == 0)
    def _():
        y = _rms(x_ref[...], g_ref[0:1, :])
        h = _group(y, nb) * (1.0 + mod_ref[:, 1:2, :]) + mod_ref[:, 0:1, :]
        hb = h.reshape(tm, D_MODEL).astype(BF16)
        hs_ref[...] = hb
        sm = jnp.dot(hb, ws_ref[...], preferred_element_type=F32)
        ki32_ref[...] = sm[:, :IDX_DIM]
        ki16_ref[...] = sm[:, :IDX_DIM].astype(BF16)
        wiT_ref[...] = sm.T[_WI_OFF:_WI_OFF + N_HEADS, :]
        z = sm + bf_ref[...]
        lf = -(jnp.maximum(-z, 0.0) + jnp.log1p(jnp.exp(-jnp.abs(z))))
        logf_ref[...] = lf[:, _FL_OFF:_FL_OFF + N_HEADS]

    acc = jnp.dot(hs_ref[...], wm_ref[...], preferred_element_type=F32)
    scale = HEAD_DIM ** -0.5

    def store_chunked_T(ref, a):
        aT = a.T.astype(BF16)
        for c in range(tm // KEY_CHUNK):
            ref[c] = aT[:, c * KEY_CHUNK:(c + 1) * KEY_CHUNK]

    @pl.when(j == _SEG_QA)
    def _():
        qaT_ref[...] = (acc * scale).T.astype(BF16)

    @pl.when(j == _SEG_KA)
    def _():
        ka32_ref[...] = acc
        ka16_ref[...] = acc.astype(BF16)

    @pl.when(j == _SEG_VA)
    def _():
        va32_ref[...] = acc
        store_chunked_T(vaT_ref, acc)

    @pl.when(j == _SEG_QI)
    def _():
        qiT_ref[...] = acc.T.astype(BF16)

    @pl.when(j == _SEG_QB)
    def _():
        qbT_ref[...] = (acc * scale).T.astype(BF16)

    @pl.when(j == _SEG_KB)
    def _():
        kb32_ref[...] = acc
        kb16_ref[...] = acc.astype(BF16)

    @pl.when(j == _SEG_VB)
    def _():
        vb32_ref[...] = acc
        store_chunked_T(vbT_ref, acc)

    @pl.when(j == _SEG_CB)
    def _():
        cb_ref[...] = acc

    @pl.when(j == _SEG_CC)
    def _():
        cc_ref[...] = acc

    @pl.when(j == _SEG_CX)
    def _():
        u_ref[...] = cc_ref[...] * acc

    @pl.when(j >= _SEG_GL)
    def _():
        gl_ref[...] = acc


def _in_projection(x, mod, g, w_main, w_small, bf_pad, *, tm, rows_per_batch):
    rows = x.shape[0]
    nt = rows // tm
    nb = max(1, tm // rows_per_batch)
    nseg = N_MAIN // D_BRANCH
    if nb == 1:
        mod_map = lambda i, j: ((i * tm) // rows_per_batch, 0, 0)
    else:
        mod_map = lambda i, j: (i, 0, 0)
    row_blk = lambda w: pl.BlockSpec((tm, w), lambda i, j: (i, 0))
    col_blk = lambda h: pl.BlockSpec((h, tm), lambda i, j: (0, i))
    chunkT = pl.BlockSpec((tm // KEY_CHUNK, D_BRANCH, KEY_CHUNK), lambda i, j: (i, 0, 0))
    f32o = lambda w: jax.ShapeDtypeStruct((rows, w), F32)
    b16o = lambda w: jax.ShapeDtypeStruct((rows, w), BF16)
    b16T = jax.ShapeDtypeStruct((D_BRANCH, rows), BF16)
    b16c = jax.ShapeDtypeStruct((rows // KEY_CHUNK, D_BRANCH, KEY_CHUNK), BF16)
    out_shape = (
        b16T, f32o(D_BRANCH), b16o(D_BRANCH), f32o(D_BRANCH), b16c, b16T, b16T,
        f32o(D_BRANCH), b16o(D_BRANCH), f32o(D_BRANCH), b16c, f32o(D_BRANCH), f32o(D_BRANCH),
        f32o(6 * D_BRANCH),
        f32o(IDX_DIM), b16o(IDX_DIM), jax.ShapeDtypeStruct((N_HEADS, rows), F32), f32o(N_HEADS),
    )
    out_specs = (
        col_blk(D_BRANCH), row_blk(D_BRANCH), row_blk(D_BRANCH), row_blk(D_BRANCH), chunkT,
        col_blk(D_BRANCH), col_blk(D_BRANCH),
        row_blk(D_BRANCH), row_blk(D_BRANCH), row_blk(D_BRANCH), chunkT, row_blk(D_BRANCH),
        row_blk(D_BRANCH),
        pl.BlockSpec((tm, D_BRANCH), lambda i, j: (i, jnp.clip(j - _SEG_GL, 0, 5))),
        row_blk(IDX_DIM), row_blk(IDX_DIM), col_blk(N_HEADS), row_blk(N_HEADS),
    )
    return pl.pallas_call(
        functools.partial(_inproj_kernel, nb=nb),
        out_shape=out_shape,
        grid=(nt, nseg),
        in_specs=[
            pl.BlockSpec((tm, D_MODEL), lambda i, j: (i, 0)),
            pl.BlockSpec((nb, 6, D_MODEL), mod_map),
            pl.BlockSpec((4, D_MODEL), lambda i, j: (0, 0)),
            pl.BlockSpec((D_MODEL, D_BRANCH), lambda i, j: (0, j)),
            pl.BlockSpec((D_MODEL, LANES), lambda i, j: (0, 0)),
            pl.BlockSpec((1, LANES), lambda i, j: (0, 0)),
        ],
        out_specs=out_specs,
        scratch_shapes=[pltpu.VMEM((tm, D_MODEL), BF16), pltpu.VMEM((tm, D_BRANCH), F32)],
        compiler_params=_cparams(("arbitrary", "arbitrary")),
        name="in_projection",
    )(x, mod, g, w_main, w_small, bf_pad)


def _cumsum_kernel(x_ref, o_ref):
    x = x_ref[0]
    n = x.shape[0]
    lane = lax.broadcasted_iota(I32, x.shape, 1)
    row = lax.broadcasted_iota(I32, x.shape, 0)
    s = N_HEADS
    while s < LANES:
        x = x + jnp.where(lane >= s, pltpu.roll(x, s, axis=1), 0.0)
        s *= 2
    t = jnp.where(lane >= LANES - N_HEADS, x, 0.0)
    s = N_HEADS
    while s < LANES:
        t = t + pltpu.roll(t, s, axis=1)
        s *= 2
    t = jnp.where(row >= 1, pltpu.roll(t, 1, axis=0), 0.0)
    s = 1
    while s < n:
        if s < 8:
            sh = jnp.where(row >= s, pltpu.roll(t, s, axis=0), 0.0)
        else:
            sh = jnp.concatenate([jnp.zeros((s, LANES), F32), t[:n - s]], axis=0)
        t = t + sh
        s *= 2
    o_ref[0] = x + t


def _forget_cumsum(logf_all):
    b, lp, h = logf_all.shape
    n = lp * h // LANES
    out = pl.pallas_call(
        _cumsum_kernel,
        out_shape=jax.ShapeDtypeStruct((b, n, LANES), F32),
        grid=(b,),
        in_specs=[pl.BlockSpec((1, n, LANES), lambda i: (i, 0, 0))],
        out_specs=pl.BlockSpec((1, n, LANES), lambda i: (i, 0, 0)),
        compiler_params=_cparams(("arbitrary",)),
        name="forget_cumsum",
    )(logf_all.reshape(b, n, LANES))
    return out.reshape(b, lp, h)


def _pair_padded(qT_ref, h):
    pr, half = divmod(h, 2)
    blk = qT_ref[0, pr * LANES + half * HEAD_DIM:pr * LANES + (half + 1) * HEAD_DIM, :]
    z = jnp.zeros_like(blk)
    return jnp.concatenate([blk, z] if half == 0 else [z, blk], axis=0)


def _softmax_step(h, lg, vT, m_ref, l_ref, acc_ref):
    m_old = m_ref[h:h + 1, :]
    m_new = jnp.maximum(m_old, jnp.max(lg, axis=0, keepdims=True))
    alpha = jnp.exp(m_old - m_new)
    p = jnp.exp(lg - m_new)
    l_ref[h:h + 1, :] = alpha * l_ref[h:h + 1, :] + jnp.sum(p, axis=0, keepdims=True)
    m_ref[h:h + 1, :] = m_new
    pv = jnp.dot(vT, p.astype(BF16), preferred_element_type=F32)
    rows = slice(h * HEAD_DIM, (h + 1) * HEAD_DIM)
    acc_ref[rows, :] = alpha * acc_ref[rows, :] + pv


def _softmax_init(m_ref, l_ref, acc_ref):
    m_ref[...] = jnp.full(m_ref.shape, NEG_INF, F32)
    l_ref[...] = jnp.zeros(l_ref.shape, F32)
    acc_ref[...] = jnp.zeros(acc_ref.shape, F32)


def _softmax_finish(o_ref, l_ref, acc_ref):
    inv = 1.0 / l_ref[...]
    parts = [acc_ref[h * HEAD_DIM:(h + 1) * HEAD_DIM, :] * inv[h:h + 1, :] for h in range(N_HEADS)]
    o_ref[0] = jnp.concatenate(parts, axis=0).T.astype(o_ref.dtype)


def _ordered_to_f32(o):
    return lax.bitcast_convert_type(jnp.where(o >= 0, o, o ^ jnp.int32(0x7FFFFFFF)), F32)


def _dsa_kernel(qiT_ref, wiT_ref, qaT_ref, ki_ref, ka_ref, vaT_ref, o_ref,
                s_ref, thr_ref, cut_ref, qp_ref, m_ref, l_ref, acc_ref,
                *, past, length, n_sel, tq, tk, nk, idx_bits):
    i = pl.program_id(1)
    j = pl.program_id(2)
    q0 = past + i * tq
    qpos = q0 + lax.broadcasted_iota(I32, (1, tq), 1)
    vis = jnp.minimum(((qpos >> CHUNK_SHIFT) + 1) * CHUNK, length)
    vmax = jnp.minimum((((q0 + tq - 1) >> CHUNK_SHIFT) + 1) * CHUNK, length)
    nvis = (vmax + tk - 1) // tk

    @pl.when(jnp.logical_and(j < nk, j < nvis))
    def _():
        def body(c, carry):
            r0 = pl.multiple_of(c * SCORE_ROWS, SCORE_ROWS)
            kblk = ki_ref[0, pl.ds(r0, SCORE_ROWS), :]
            acc = jnp.zeros((SCORE_ROWS, tq), F32)
            for h in range(N_HEADS):
                d = jnp.dot(kblk, qiT_ref[0, h * IDX_DIM:(h + 1) * IDX_DIM, :],
                            preferred_element_type=F32)
                acc = acc + jnp.maximum(d, 0.0) * wiT_ref[0, h:h + 1, :]
            kpos = j * tk + r0 + lax.broadcasted_iota(I32, (SCORE_ROWS, 1), 0)
            acc = jnp.where(kpos < vis, acc, -jnp.inf)
            s_ref[pl.ds(pl.multiple_of(j * tk + r0, SCORE_ROWS), SCORE_ROWS), :] = acc
            return carry
        lax.fori_loop(0, tk // SCORE_ROWS, body, 0)

    @pl.when(j == nk - 1)
    def _():
        nch = nvis * (tk // COUNT_ROWS)

        def count(pred):
            def body(c, acc):
                r0 = pl.multiple_of(c * COUNT_ROWS, COUNT_ROWS)
                hit = jnp.where(pred(s_ref[pl.ds(r0, COUNT_ROWS), :], r0), 1.0, 0.0)
                return acc + jnp.sum(hit.reshape(COUNT_ROWS // 8, 8, tq), axis=0)
            acc = lax.fori_loop(0, nch, body, jnp.zeros((8, tq), F32))
            return jnp.sum(acc, axis=0, keepdims=True)

        def count_ge(t):
            return count(lambda blk, r0: blk >= t)

        kf = float(n_sel)
        cur = jnp.where(count_ge(jnp.zeros((1, tq), F32)) >= kf, jnp.int32(0), jnp.int32(INT32_MIN))

        def bit_body(b, cur):
            cand = cur + lax.shift_left(jnp.int32(1), 30 - b)
            return jnp.where(count_ge(_ordered_to_f32(cand)) >= kf, cand, cur)
        cur = lax.fori_loop(0, 31, bit_body, cur)
        thr = jnp.where(vis < n_sel, -F32_MAX, _ordered_to_f32(cur))
        n_gt = count(lambda blk, r0: blk > thr)
        n_ge = count_ge(thr)
        need = kf - n_gt
        tied = (n_ge - n_gt) > need

        def tie_cut():
            def count_before(x):
                def pred(blk, r0):
                    kpos = r0 + lax.broadcasted_iota(I32, (COUNT_ROWS, 1), 0)
                    return jnp.logical_and(blk == thr, kpos < x)
                return count(pred)
            lo = jnp.zeros((1, tq), I32)
            for b in reversed(range(idx_bits)):
                cand = lo + (1 << b)
                lo = jnp.where(count_before(cand) < need, cand, lo)
            return jnp.where(tied, lo, jnp.int32(2 ** 30))

        cut = lax.cond(jnp.max(jnp.where(tied, 1.0, 0.0)) > 0.0, tie_cut,
                       lambda: jnp.full((1, tq), 2 ** 30, I32))
        thr_ref[0:1, :] = thr
        cut_ref[0:1, :] = cut
        for h in range(N_HEADS):
            qp_ref[h] = _pair_padded(qaT_ref, h)
        _softmax_init(m_ref, l_ref, acc_ref)

    jj = j - nk

    @pl.when(jnp.logical_and(j >= nk, jj < nvis))
    def _():
        thr = thr_ref[0:1, :]
        cut = cut_ref[0:1, :]

        def body(c, carry):
            r0 = pl.multiple_of(c * KEY_CHUNK, KEY_CHUNK)
            g0 = pl.multiple_of(jj * tk + r0, KEY_CHUNK)
            sblk = s_ref[pl.ds(g0, KEY_CHUNK), :]
            kpos = g0 + lax.broadcasted_iota(I32, (KEY_CHUNK, 1), 0)
            sel = jnp.logical_or(sblk > thr, jnp.logical_and(sblk == thr, kpos <= cut))
            for h in range(N_HEADS):
                pr = h // 2
                kblk = ka_ref[0, pl.ds(r0, KEY_CHUNK), pr * LANES:(pr + 1) * LANES]
                lt = jnp.dot(kblk, qp_ref[h], preferred_element_type=F32)
                lg = jnp.where(sel, lt, NEG_INF)
                vT = vaT_ref[0, c, h * HEAD_DIM:(h + 1) * HEAD_DIM, :]
                _softmax_step(h, lg, vT, m_ref, l_ref, acc_ref)
            return carry
        lax.fori_loop(0, tk // KEY_CHUNK, body, 0)

    @pl.when(j == 2 * nk - 1)
    def _():
        _softmax_finish(o_ref, l_ref, acc_ref)


def _dsa_attention(qiT, wiT, qaT, ki, ka, vaTc, *, past, length, tq, tk):
    b, _, tqp = qiT.shape
    lp = ki.shape[1]
    nq, nk = tqp // tq, lp // tk
    n_sel = min(TOPK_MAX, length // 4)

    def nvis(i):
        vmax = jnp.minimum(((past + (i + 1) * tq - 1) // CHUNK + 1) * CHUNK, length)
        return (vmax + tk - 1) // tk

    k1 = lambda bb, i, j: (bb, jnp.minimum(j, nvis(i) - 1), 0)
    k3 = lambda bb, i, j: (bb, jnp.clip(j - nk, 0, nvis(i) - 1), 0)
    k3c = lambda bb, i, j: (bb, jnp.clip(j - nk, 0, nvis(i) - 1), 0, 0)
    qmap = lambda bb, i, j: (bb, 0, i)
    kern = functools.partial(_dsa_kernel, past=past, length=length, n_sel=n_sel, tq=tq, tk=tk,
                             nk=nk, idx_bits=int(lp).bit_length())
    return pl.pallas_call(
        kern,
        out_shape=jax.ShapeDtypeStruct((b, tqp, D_BRANCH), BF16),
        grid=(b, nq, 2 * nk),
        in_specs=[
            pl.BlockSpec((1, D_BRANCH, tq), qmap),
            pl.BlockSpec((1, N_HEADS, tq), qmap),
            pl.BlockSpec((1, D_BRANCH, tq), qmap),
            pl.BlockSpec((1, tk, IDX_DIM), k1),
            pl.BlockSpec((1, tk, D_BRANCH), k3),
            pl.BlockSpec((1, tk // KEY_CHUNK, D_BRANCH, KEY_CHUNK), k3c),
        ],
        out_specs=pl.BlockSpec((1, tq, D_BRANCH), lambda bb, i, j: (bb, i, 0)),
        scratch_shapes=[
            pltpu.VMEM((lp, tq), F32),
            pltpu.VMEM((8, tq), F32),
            pltpu.VMEM((8, tq), I32),
            pltpu.VMEM((N_HEADS, LANES, tq), BF16),
            pltpu.VMEM((N_HEADS, tq), F32),
            pltpu.VMEM((N_HEADS, tq), F32),
            pltpu.VMEM((D_BRANCH, tq), F32),
        ],
        compiler_params=_cparams(("arbitrary", "arbitrary", "arbitrary")),
        name="dsa_attention",
    )(qiT, wiT, qaT, ki, ka, vaTc)


_AUG = 6


def _fox_kernel(qbT_ref, qaug_ref, kb_ref, kaug_ref, vbT_ref, o_ref,
                qf_ref, m_ref, l_ref, acc_ref, *, past, tq, tk, nk):
    i = pl.program_id(1)
    j = pl.program_id(2)
    q0 = past + i * tq
    qpos = q0 + lax.broadcasted_iota(I32, (1, tq), 1)
    nvis = (q0 + tq + tk - 1) // tk

    @pl.when(j == 0)
    def _():
        arow = lax.broadcasted_iota(I32, (LANES, 1), 0)
        qa = qaug_ref[0]
        for h in range(N_HEADS):
            mine = jnp.logical_and(arow >= _AUG * h, arow < _AUG * (h + 1))
            qf_ref[h, 0:LANES, :] = _pair_padded(qbT_ref, h)
            qf_ref[h, LANES:2 * LANES, :] = jnp.where(mine, qa, 0.0).astype(BF16)
        _softmax_init(m_ref, l_ref, acc_ref)

    @pl.when(j < nvis)
    def _():
        def body(c, carry):
            r0 = pl.multiple_of(c * KEY_CHUNK, KEY_CHUNK)
            kpos = j * tk + r0 + lax.broadcasted_iota(I32, (KEY_CHUNK, 1), 0)
            ok = kpos <= qpos
            kaug = kaug_ref[0, pl.ds(r0, KEY_CHUNK), :]
            for h in range(N_HEADS):
                pr = h // 2
                kblk = kb_ref[0, pl.ds(r0, KEY_CHUNK), pr * LANES:(pr + 1) * LANES]
                lhs = jnp.concatenate([kblk, kaug], axis=1)
                lt = jnp.dot(lhs, qf_ref[h], preferred_element_type=F32)
                lg = jnp.where(ok, lt, NEG_INF)
                vT = vbT_ref[0, c, h * HEAD_DIM:(h + 1) * HEAD_DIM, :]
                _softmax_step(h, lg, vT, m_ref, l_ref, acc_ref)
            return carry
        lax.fori_loop(0, tk // KEY_CHUNK, body, 0)

    @pl.when(j == nk - 1)
    def _():
        _softmax_finish(o_ref, l_ref, acc_ref)


def _fox_attention(qbT, qaug, kb, kaug, vbTc, *, past, tq, tk):
    b, _, tqp = qbT.shape
    lp = kb.shape[1]
    nq, nk = tqp // tq, lp // tk

    def last(i):
        return jnp.minimum((past + (i + 1) * tq + tk - 1) // tk, nk) - 1

    kmap = lambda bb, i, j: (bb, jnp.minimum(j, last(i)), 0)
    kmapc = lambda bb, i, j: (bb, jnp.minimum(j, last(i)), 0, 0)
    qmap = lambda bb, i, j: (bb, 0, i)
    return pl.pallas_call(
        functools.partial(_fox_kernel, past=past, tq=tq, tk=tk, nk=nk),
        out_shape=jax.ShapeDtypeStruct((b, tqp, D_BRANCH), BF16),
        grid=(b, nq, nk),
        in_specs=[
            pl.BlockSpec((1, D_BRANCH, tq), qmap),
            pl.BlockSpec((1, LANES, tq), qmap),
            pl.BlockSpec((1, tk, D_BRANCH), kmap),
            pl.BlockSpec((1, tk, LANES), kmap),
            pl.BlockSpec((1, tk // KEY_CHUNK, D_BRANCH, KEY_CHUNK), kmapc),
        ],
        out_specs=pl.BlockSpec((1, tq, D_BRANCH), lambda bb, i, j: (bb, i, 0)),
        scratch_shapes=[
            pltpu.VMEM((N_HEADS, 2 * LANES, tq), BF16),
            pltpu.VMEM((N_HEADS, tq), F32),
            pltpu.VMEM((N_HEADS, tq), F32),
            pltpu.VMEM((D_BRANCH, tq), F32),
        ],
        compiler_params=_cparams(("arbitrary", "arbitrary", "arbitrary")),
        name="fox_attention",
    )(qbT, qaug, kb, kaug, vbTc)


AUG_ROWS = 512


def _aug_kernel(c_ref, kaug_ref, qaugT_ref):
    x = c_ref[0]
    a = x.astype(BF16).astype(F32)
    r = x - a
    b = r.astype(BF16).astype(F32)
    c = (r - b).astype(BF16).astype(F32)
    lane = lax.broadcasted_iota(I32, (N_HEADS, LANES), 1)
    head = lax.broadcasted_iota(I32, (N_HEADS, LANES), 0)

    def place(v, slot):
        sel = jnp.where(lane == _AUG * head + slot, 1.0, 0.0)
        return jnp.dot(v, sel, preferred_element_type=F32, precision=lax.Precision.HIGHEST)

    slot = lane - _AUG * head
    ones_k = jnp.sum(jnp.where(jnp.logical_and(slot >= 0, slot < 3), 1.0, 0.0), axis=0, keepdims=True)
    ones_q = jnp.sum(jnp.where(jnp.logical_and(slot >= 3, slot < 6), 1.0, 0.0), axis=0, keepdims=True)
    kaug_ref[0] = (ones_k - (place(a, 3) + place(b, 4) + place(c, 5))).astype(BF16)
    qaugT_ref[0] = (ones_q + (place(a, 0) + place(b, 1) + place(c, 2))).T


def _fox_augment(cum):
    b, lp, h = cum.shape
    return pl.pallas_call(
        _aug_kernel,
        out_shape=(jax.ShapeDtypeStruct((b, lp, LANES), BF16),
                   jax.ShapeDtypeStruct((b, LANES, lp), F32)),
        grid=(b, lp // AUG_ROWS),
        in_specs=[pl.BlockSpec((1, AUG_ROWS, h), lambda i, j: (i, j, 0))],
        out_specs=(pl.BlockSpec((1, AUG_ROWS, LANES), lambda i, j: (i, j, 0)),
                   pl.BlockSpec((1, LANES, AUG_ROWS), lambda i, j: (i, 0, j))),
        compiler_params=_cparams(("arbitrary", "arbitrary")),
        name="fox_augment",
    )(cum)


def _shifted_rows(buf, load, store, u, past_ref, starts_batch, nb):
    tm, c = u.shape
    store(slice(8, 8 + tm), u)
    if nb == 1:
        @pl.when(starts_batch)
        def _():
            store(slice(6, 8), past_ref[0])
    s1 = load(slice(7, 7 + tm))
    s2 = load(slice(6, 6 + tm))
    if nb > 1:
        r = lax.broadcasted_iota(I32, (nb, tm // nb, 1), 1)
        p0 = past_ref[:, 0:1, :]
        p1 = past_ref[:, 1:2, :]
        s1 = jnp.where(r == 0, p1, _group(s1, nb)).reshape(tm, c)
        s2 = jnp.where(r == 0, p0, jnp.where(r == 1, p1, _group(s2, nb))).reshape(tm, c)
    store(slice(0, 8), load(slice(tm, tm + 8)))
    return s1, s2


def _merge_kernel(x_ref, ya_ref, yb_ref, cb_ref, u_ref, gl_ref, past_ref, cw_ref, wbr_ref,
                  wo_ref, g_ref, mod_ref, o_ref, ub_ref, *, nb, rows_per_batch):
    i = pl.program_id(0)
    tm = x_ref.shape[0]
    u = u_ref[...]

    def load(rows):
        return ub_ref[rows, :]

    def store(rows, v):
        ub_ref[rows, :] = v

    s1, s2 = _shifted_rows(ub_ref, load, store, u, past_ref, (i * tm) % rows_per_batch == 0, nb)
    conv = cw_ref[0:1, :] * s2 + cw_ref[1:2, :] * s1 + cw_ref[2:3, :] * u
    yc = (cb_ref[...] * conv).astype(BF16)
    mix = jnp.zeros((tm, D_MODEL), F32)
    for n, y in enumerate((ya_ref[...], yb_ref[...], yc)):
        br = jnp.dot(y, wbr_ref[n], preferred_element_type=F32)
        mix = mix + jax.nn.sigmoid(gl_ref[:, n * D_MODEL:(n + 1) * D_MODEL]) * br
    mo = jnp.dot(mix.astype(BF16), wo_ref[...], preferred_element_type=F32)
    nm = _group(_rms(mo, g_ref[1:2, :]), nb)
    o_ref[...] = x_ref[...] + (mod_ref[:, 2:3, :] * nm).reshape(tm, D_MODEL)


def _mod_spec(tm, nb, rows_per_batch, ngrid):
    if nb == 1:
        f = lambda i, *_: ((i * tm) // rows_per_batch, 0, 0)
    else:
        f = lambda i, *_: (i, 0, 0)
    return pl.BlockSpec((nb, 6, D_MODEL), f)


def _merge(x, ya, yb, cb, u, gl, past, cw, wbr, wo, g, mod, *, tm, rows_per_batch):
    rows = x.shape[0]
    nb = max(1, tm // rows_per_batch)
    row = lambda w: pl.BlockSpec((tm, w), lambda i: (i, 0))
    full = lambda shape: pl.BlockSpec(shape, lambda i: (0,) * len(shape))
    if nb == 1:
        past_spec = pl.BlockSpec((1, 2, D_BRANCH), lambda i: ((i * tm) // rows_per_batch, 0, 0))
    else:
        past_spec = pl.BlockSpec((nb, 2, D_BRANCH), lambda i: (i, 0, 0))
    return pl.pallas_call(
        functools.partial(_merge_kernel, nb=nb, rows_per_batch=rows_per_batch),
        out_shape=jax.ShapeDtypeStruct((rows, D_MODEL), F32),
        grid=(rows // tm,),
        in_specs=[
            row(D_MODEL), row(D_BRANCH), row(D_BRANCH), row(D_BRANCH), row(D_BRANCH),
            row(3 * D_MODEL), past_spec, full((CONV_W, D_BRANCH)),
            full((3, D_BRANCH, D_MODEL)), full((D_MODEL, D_MODEL)), full((4, D_MODEL)),
            _mod_spec(tm, nb, rows_per_batch, 1),
        ],
        out_specs=row(D_MODEL),
        scratch_shapes=[pltpu.VMEM((tm + 8, D_BRANCH), F32)],
        compiler_params=_cparams(("arbitrary",)),
        name="branch_merge",
    )(x, ya, yb, cb, u, gl, past, cw, wbr, wo, g, mod)


FF_CHUNK = 256


def _ffn_kernel(x_ref, g_ref, mod_ref, wg_ref, wv_ref, wd_ref, cw_ref, past_ref,
                o_ref, tail_ref, hs_ref, acc_ref, ub_ref, *, nb, rows_per_batch, nj):
    i = pl.program_id(0)
    j = pl.program_id(1)
    tm = x_ref.shape[0]

    @pl.when(j == 0)
    def _():
        y = _group(_rms(x_ref[...], g_ref[2:3, :]), nb)
        h = y * (1.0 + mod_ref[:, 4:5, :]) + mod_ref[:, 3:4, :]
        hs_ref[...] = h.reshape(tm, D_MODEL).astype(BF16)
        acc_ref[...] = jnp.zeros(acc_ref.shape, F32)

    hs = hs_ref[...]
    ug = jnp.dot(hs, wg_ref[...], preferred_element_type=F32)
    uv = jnp.dot(hs, wv_ref[...], preferred_element_type=F32)

    def load(rows):
        return ub_ref[j, rows, :]

    def store(rows, v):
        ub_ref[j, rows, :] = v

    s1, s2 = _shifted_rows(ub_ref, load, store, ug, past_ref, (i * tm) % rows_per_batch == 0, nb)
    conv = cw_ref[0:1, :] * s2 + cw_ref[1:2, :] * s1 + cw_ref[2:3, :] * ug
    f = conv * jax.nn.sigmoid(conv) * uv
    acc_ref[...] += jnp.dot(f.astype(BF16), wd_ref[...], preferred_element_type=F32)
    grp = tm // nb
    tail_ref[0] = _group(ug, nb)[:, grp - 8:, :]

    @pl.when(j == nj - 1)
    def _():
        nm = _group(_rms(acc_ref[...], g_ref[3:4, :]), nb)
        o_ref[...] = x_ref[...] + (mod_ref[:, 5:6, :] * nm).reshape(tm, D_MODEL)


def _conv_ffn(x, g, mod, w_up, w_down, cw, past, *, tm, rows_per_batch):
    rows = x.shape[0]
    dff = w_down.shape[0]
    nj = dff // FF_CHUNK
    nb = max(1, tm // rows_per_batch)
    nt = rows // tm
    if nb == 1:
        past_spec = pl.BlockSpec((1, 2, FF_CHUNK), lambda i, j: ((i * tm) // rows_per_batch, 0, j))
        mod_map = lambda i, j: ((i * tm) // rows_per_batch, 0, 0)
    else:
        past_spec = pl.BlockSpec((nb, 2, FF_CHUNK), lambda i, j: (i, 0, j))
        mod_map = lambda i, j: (i, 0, 0)
    out, tails = pl.pallas_call(
        functools.partial(_ffn_kernel, nb=nb, rows_per_batch=rows_per_batch, nj=nj),
        out_shape=(jax.ShapeDtypeStruct((rows, D_MODEL), F32),
                   jax.ShapeDtypeStruct((nt, nb, 8, dff), F32)),
        grid=(nt, nj),
        in_specs=[
            pl.BlockSpec((tm, D_MODEL), lambda i, j: (i, 0)),
            pl.BlockSpec((4, D_MODEL), lambda i, j: (0, 0)),
            pl.BlockSpec((nb, 6, D_MODEL), mod_map),
            pl.BlockSpec((D_MODEL, FF_CHUNK), lambda i, j: (0, j)),
            pl.BlockSpec((D_MODEL, FF_CHUNK), lambda i, j: (0, j + nj)),
            pl.BlockSpec((FF_CHUNK, D_MODEL), lambda i, j: (j, 0)),
            pl.BlockSpec((CONV_W, FF_CHUNK), lambda i, j: (0, j)),
            past_spec,
        ],
        out_specs=(pl.BlockSpec((tm, D_MODEL), lambda i, j: (i, 0)),
                   pl.BlockSpec((1, nb, 8, FF_CHUNK), lambda i, j: (i, 0, 0, j))),
        scratch_shapes=[
            pltpu.VMEM((tm, D_MODEL), BF16),
            pltpu.VMEM((tm, D_MODEL), F32),
            pltpu.VMEM((nj, tm + 8, FF_CHUNK), F32),
        ],
        compiler_params=_cparams(("arbitrary", "arbitrary")),
        name="conv_ffn",
    )(x, g, mod, w_up, w_up, w_down, cw, past)
    return out, tails


def _relayout_w_in(w_in, b_forget):
    db = D_BRANCH
    o_qi = 3 * db
    o_ki = o_qi + N_HEADS * IDX_DIM
    o_wi = o_ki + IDX_DIM
    o_qb = o_wi + N_HEADS
    o_fl = o_qb + 3 * db
    o_cb = o_fl + N_HEADS
    o_gl = o_cb + 3 * db
    main = jnp.concatenate([w_in[:, 0:o_ki], w_in[:, o_qb:o_fl], w_in[:, o_cb:]], axis=1)
    small = jnp.concatenate([w_in[:, o_ki:o_wi], w_in[:, o_wi:o_qb], w_in[:, o_fl:o_cb]], axis=1)
    small = jnp.pad(small, ((0, 0), (0, LANES - small.shape[1])))
    bf = jnp.zeros((1, LANES), F32).at[0, _FL_OFF:_FL_OFF + N_HEADS].set(b_forget)
    del o_gl
    return main.astype(BF16), small.astype(BF16), bf


def _round_up(x, m):
    return (x + m - 1) // m * m


def _layer(x, mod, caches, weights, *, batch, t, past, cfg):
    (g, w_main, w_small, bf_pad, cw_mix, wbr, wo, w_up, cw_ffn, w_down) = weights
    rows = batch * t
    length = past + t
    (qaT, ka32, ka16, va32, vaTc, qiT, qbT, kb32, kb16, vb32, vbTc, cb, u, gl,
     ki32, ki16, wiT, logf) = _in_projection(
        x, mod, g, w_main, w_small, bf_pad, tm=cfg["tm_in"], rows_per_batch=t)

    tq, tk = cfg["tq"], cfg["tk"]
    tqp = _round_up(t, tq)
    lp = _round_up(length, tk)

    def per_batch_T(aT):
        c = aT.shape[0]
        a = jnp.swapaxes(aT.reshape(c, batch, t), 0, 1)
        return jnp.pad(a, ((0, 0), (0, 0), (0, tqp - t)))

    def chunked_T(a):
        return jnp.swapaxes(a.reshape(batch, lp // KEY_CHUNK, KEY_CHUNK, a.shape[-1]), 2, 3)

    def with_cache(cache, new, dtype):
        new = new.reshape(batch, t, -1)
        if cache is not None:
            new = jnp.concatenate([cache.reshape(batch, past, -1).astype(dtype), new.astype(dtype)],
                                  axis=1)
        return jnp.pad(new.astype(dtype), ((0, 0), (0, lp - length), (0, 0)))

    if caches is None:
        c_idx = c_dk = c_dv = c_fk = c_fv = c_lf = None
        past_mix = jnp.zeros((batch, CONV_W - 1, D_BRANCH), F32)
        past_ffn = jnp.zeros((batch, CONV_W - 1, w_down.shape[0]), F32)
    else:
        c_idx, c_dk, c_dv, c_fk, c_fv, c_lf, past_mix, past_ffn = caches

    ki_all = with_cache(c_idx, ki16, BF16)
    ka_all = with_cache(c_dk, ka16, BF16)
    kb_all = with_cache(c_fk, kb16, BF16)
    if caches is None and lp == rows:
        vaT_all = vaTc.reshape(batch, lp // KEY_CHUNK, D_BRANCH, KEY_CHUNK)
        vbT_all = vbTc.reshape(batch, lp // KEY_CHUNK, D_BRANCH, KEY_CHUNK)
    else:
        vaT_all = chunked_T(with_cache(c_dv, va32, BF16))
        vbT_all = chunked_T(with_cache(c_fv, vb32, BF16))
    logf_all = with_cache(c_lf, logf, F32)

    ya = _dsa_attention(per_batch_T(qiT), per_batch_T(wiT), per_batch_T(qaT), ki_all, ka_all,
                        vaT_all, past=past, length=length, tq=tq, tk=tk)
    cum = _forget_cumsum(logf_all)
    kaug, qaug = _fox_augment(cum)
    qaug = jnp.pad(qaug[:, :, past:length], ((0, 0), (0, 0), (0, tqp - t)))
    yb = _fox_attention(per_batch_T(qbT), qaug, kb_all, kaug, vbT_all, past=past, tq=tq, tk=tk)
    ya = ya[:, :t].reshape(rows, D_BRANCH)
    yb = yb[:, :t].reshape(rows, D_BRANCH)

    x1 = _merge(x, ya, yb, cb, u, gl, past_mix, cw_mix, wbr, wo, g, mod,
                tm=cfg["tm_merge"], rows_per_batch=t)
    x2, tails = _conv_ffn(x1, g, mod, w_up, w_down, cw_ffn, past_ffn,
                          tm=cfg["tm_ffn"], rows_per_batch=t)

    new_mix = u.reshape(batch, t, D_BRANCH)[:, t - (CONV_W - 1):]
    nb = max(1, cfg["tm_ffn"] // t)
    if nb == 1:
        tiles_per_batch = t // cfg["tm_ffn"]
        last = tails.reshape(batch, tiles_per_batch, 8, -1)[:, -1]
    else:
        last = tails.reshape(batch, 8, -1)
    new_ffn = last[:, 8 - (CONV_W - 1):]
    hd = (batch, t, N_HEADS, HEAD_DIM)
    state = (ki32.reshape(batch, t, IDX_DIM), ka32.reshape(hd), va32.reshape(hd),
             kb32.reshape(hd), vb32.reshape(hd), logf.reshape(batch, t, N_HEADS), new_mix, new_ffn)
    return x2, state


_PROMPT_CFG = dict(tm_in=512, tm_merge=256, tm_ffn=512, tq=256, tk=2048)
_SAMPLE_CFG = dict(tm_in=256, tm_merge=256, tm_ffn=256, tq=128, tk=2560)


def kernel(x_prompt, x_sample, c_prompt, c_sample, cache_idx_k, cache_dsa_k, cache_dsa_v,
           cache_fox_k, cache_fox_v, cache_fox_logf, state_conv_mix, state_conv_ffn,
           w_ada, b_ada, norm_g, w_in, b_forget, conv_mix_w, w_branch, w_out, w_up,
           conv_ffn_w, w_down):
    bp, tp, _ = x_prompt.shape
    bs, ts, _ = x_sample.shape
    past = cache_dsa_k.shape[2]
    depth = w_ada.shape[0]

    c_all = jnp.concatenate([c_prompt, c_sample], axis=0)
    pad_rows = _round_up(c_all.shape[0], 8) - c_all.shape[0]
    c_all = jnp.pad(c_all, ((0, pad_rows), (0, 0)))
    mod_all = _modulation(c_all, w_ada, b_ada).reshape(depth, -1, 6, D_MODEL)

    yp = x_prompt.reshape(bp * tp, D_MODEL)
    ys = x_sample.reshape(bs * ts, D_MODEL)
    p_states, s_states = [], []
    for l in range(depth):
        w_main, w_small, bf_pad = _relayout_w_in(w_in[l], b_forget[l])
        weights = (norm_g[l], w_main, w_small, bf_pad, conv_mix_w[l], w_branch[l].astype(BF16),
                   w_out[l].astype(BF16), w_up[l].astype(BF16), conv_ffn_w[l],
                   w_down[l].astype(BF16))
        yp, st_p = _layer(yp, mod_all[l, :bp], None, weights, batch=bp, t=tp, past=0,
                          cfg=_PROMPT_CFG)
        caches = (cache_idx_k[l], cache_dsa_k[l], cache_dsa_v[l], cache_fox_k[l], cache_fox_v[l],
                  cache_fox_logf[l], state_conv_mix[l], state_conv_ffn[l])
        ys, st_s = _layer(ys, mod_all[l, bp:bp + bs], caches, weights, batch=bs, t=ts, past=past,
                          cfg=_SAMPLE_CFG)
        p_states.append(st_p)
        s_states.append(st_s)

    stack = lambda states: [jnp.stack([st[k] for st in states], axis=0) for k in range(8)]
    return (yp.reshape(bp, tp, D_MODEL), ys.reshape(bs, ts, D_MODEL), *stack(p_states),
            *stack(s_states))
```

```python
import functools

import jax
import jax.numpy as jnp
from jax import lax
from jax.experimental import pallas as pl
from jax.experimental.pallas import tpu as pltpu

F32 = jnp.float32
BF16 = jnp.bfloat16
I32 = jnp.int32

D_MODEL = 1024
HEAD_DIM = 64
D_BRANCH = 512
N_HEADS = 8
IDX_DIM = 64
CHUNK = 64
CHUNK_SHIFT = 6
TOPK_MAX = 256
CONV_W = 3
EPS = 1e-6
NEG_INF = -1e30
LOG2E = 1.4426950408889634
F32_MAX = 3.4028234663852886e38
INT32_MIN = -(2 ** 31)

LANES = 128
N_MAIN = 16 * D_BRANCH
KEY_CHUNK = 256
KEY_PAIR = 2 * KEY_CHUNK
SCORE_ROWS = 128
COUNT_ROWS = 256
VMEM_LIMIT = 56 * 1024 * 1024


def _cparams(sem):
    return pltpu.CompilerParams(dimension_semantics=sem, vmem_limit_bytes=VMEM_LIMIT)


def _rms(x, g_row):
    return x * lax.rsqrt(jnp.mean(x * x, axis=-1, keepdims=True) + EPS) * g_row


def _group(x, nb):
    return x.reshape(nb, x.shape[0] // nb, x.shape[1])


def _mod_kernel(c_ref, w_ref, b_ref, o_ref):
    c = c_ref[...]
    s = (c * jax.nn.sigmoid(c)).astype(BF16)
    o_ref[0] = jnp.dot(s, w_ref[0].astype(BF16), preferred_element_type=F32) + b_ref[0]


def _modulation(c_all, w_ada, b_ada):
    depth = w_ada.shape[0]
    rows = c_all.shape[0]
    n = w_ada.shape[2]
    tn = D_MODEL
    return pl.pallas_call(
        _mod_kernel,
        out_shape=jax.ShapeDtypeStruct((depth, rows, n), F32),
        grid=(depth, n // tn),
        in_specs=[
            pl.BlockSpec((rows, D_MODEL), lambda l, j: (0, 0)),
            pl.BlockSpec((1, D_MODEL, tn), lambda l, j: (l, 0, j)),
            pl.BlockSpec((1, 1, tn), lambda l, j: (l, 0, j)),
        ],
        out_specs=pl.BlockSpec((1, rows, tn), lambda l, j: (l, 0, j)),
        compiler_params=_cparams(("arbitrary", "arbitrary")),
        name="adaln_mod",
    )(c_all, w_ada, b_ada.reshape(depth, 1, n))


_SEG_QA, _SEG_KA, _SEG_VA, _SEG_QI, _SEG_QB, _SEG_KB, _SEG_VB, _SEG_CB, _SEG_CC, _SEG_CX, _SEG_GL = range(11)
_WI_OFF = IDX_DIM
_FL_OFF = IDX_DIM + N_HEADS


def _inproj_kernel(x_ref, mod_ref, g_ref, wm_ref, ws_ref, bf_ref,
                   qaT_ref, ka32_ref, ka16_ref, va32_ref, vaT_ref, qiT_ref, qbT_ref,
                   kb32_ref, kb16_ref, vb32_ref, vbT_ref, cb_ref, u_ref, gl_ref,
                   ki32_ref, ki16_ref, wiT_ref, logf_ref,
                   hs_ref, cc_ref, *, nb):
    j = pl.program_id(1)
    tm = x_ref.shape[0]

    @pl.when(j == 0)
    def _():
        y = _rms(x_ref[...], g_ref[0:1, :])
        h = _group(y, nb) * (1.0 + mod_ref[:, 1:2, :]) + mod_ref[:, 0:1, :]
        hb = h.reshape(tm, D_MODEL).astype(BF16)
        hs_ref[...] = hb
        sm = jnp.dot(hb, ws_ref[...], preferred_element_type=F32)
        ki32_ref[...] = sm[:, :IDX_DIM]
        ki16_ref[...] = sm[:, :IDX_DIM].astype(BF16)
        wiT_ref[...] = sm.T[_WI_OFF:_WI_OFF + N_HEADS, :]
        z = sm + bf_ref[...]
        lf = -(jnp.maximum(-z, 0.0) + jnp.log1p(jnp.exp(-jnp.abs(z))))
        logf_ref[...] = lf[:, _FL_OFF:_FL_OFF + N_HEADS]

    acc = jnp.dot(hs_ref[...], wm_ref[...], preferred_element_type=F32)
    scale = LOG2E * HEAD_DIM ** -0.5

    def store_chunked_T(ref, a):
        aT = a.T.astype(BF16)
        for c in range(tm // KEY_CHUNK):
            ref[c] = aT[:, c * KEY_CHUNK:(c + 1) * KEY_CHUNK]

    @pl.when(j == _SEG_QA)
    def _():
        qaT_ref[...] = (acc * scale).T.astype(BF16)

    @pl.when(j == _SEG_KA)
    def _():
        ka32_ref[...] = acc
        ka16_ref[...] = acc.astype(BF16)

    @pl.when(j == _SEG_VA)
    def _():
        va32_ref[...] = acc
        store_chunked_T(vaT_ref, acc)

    @pl.when(j == _SEG_QI)
    def _():
        qiT_ref[...] = acc.T.astype(BF16)

    @pl.when(j == _SEG_QB)
    def _():
        qbT_ref[...] = (acc * scale).T.astype(BF16)

    @pl.when(j == _SEG_KB)
    def _():
        kb32_ref[...] = acc
        kb16_ref[...] = acc.astype(BF16)

    @pl.when(j == _SEG_VB)
    def _():
        vb32_ref[...] = acc
        store_chunked_T(vbT_ref, acc)

    @pl.when(j == _SEG_CB)
    def _():
        cb_ref[...] = acc

    @pl.when(j == _SEG_CC)
    def _():
        cc_ref[...] = acc

    @pl.when(j == _SEG_CX)
    def _():
        u_ref[...] = cc_ref[...] * acc

    @pl.when(j >= _SEG_GL)
    def _():
        gl_ref[...] = acc


def _in_projection(x, mod, g, w_main, w_small, bf_pad, *, tm, rows_per_batch):
    rows = x.shape[0]
    nt = rows // tm
    nb = max(1, tm // rows_per_batch)
    nseg = N_MAIN // D_BRANCH
    if nb == 1:
        mod_map = lambda i, j: ((i * tm) // rows_per_batch, 0, 0)
    else:
        mod_map = lambda i, j: (i, 0, 0)
    row_blk = lambda w: pl.BlockSpec((tm, w), lambda i, j: (i, 0))
    col_blk = lambda h: pl.BlockSpec((h, tm), lambda i, j: (0, i))
    chunkT = pl.BlockSpec((tm // KEY_CHUNK, D_BRANCH, KEY_CHUNK), lambda i, j: (i, 0, 0))
    f32o = lambda w: jax.ShapeDtypeStruct((rows, w), F32)
    b16o = lambda w: jax.ShapeDtypeStruct((rows, w), BF16)
    b16T = jax.ShapeDtypeStruct((D_BRANCH, rows), BF16)
    b16c = jax.ShapeDtypeStruct((rows // KEY_CHUNK, D_BRANCH, KEY_CHUNK), BF16)
    out_shape = (
        b16T, f32o(D_BRANCH), b16o(D_BRANCH), f32o(D_BRANCH), b16c, b16T, b16T,
        f32o(D_BRANCH), b16o(D_BRANCH), f32o(D_BRANCH), b16c, f32o(D_BRANCH), f32o(D_BRANCH),
        f32o(6 * D_BRANCH),
        f32o(IDX_DIM), b16o(IDX_DIM), jax.ShapeDtypeStruct((N_HEADS, rows), F32), f32o(N_HEADS),
    )
    out_specs = (
        col_blk(D_BRANCH), row_blk(D_BRANCH), row_blk(D_BRANCH), row_blk(D_BRANCH), chunkT,
        col_blk(D_BRANCH), col_blk(D_BRANCH),
        row_blk(D_BRANCH), row_blk(D_BRANCH), row_blk(D_BRANCH), chunkT, row_blk(D_BRANCH),
        row_blk(D_BRANCH),
        pl.BlockSpec((tm, D_BRANCH), lambda i, j: (i, jnp.clip(j - _SEG_GL, 0, 5))),
        row_blk(IDX_DIM), row_blk(IDX_DIM), col_blk(N_HEADS), row_blk(N_HEADS),
    )
    return pl.pallas_call(
        functools.partial(_inproj_kernel, nb=nb),
        out_shape=out_shape,
        grid=(nt, nseg),
        in_specs=[
            pl.BlockSpec((tm, D_MODEL), lambda i, j: (i, 0)),
            pl.BlockSpec((nb, 6, D_MODEL), mod_map),
            pl.BlockSpec((4, D_MODEL), lambda i, j: (0, 0)),
            pl.BlockSpec((D_MODEL, D_BRANCH), lambda i, j: (0, j)),
            pl.BlockSpec((D_MODEL, LANES), lambda i, j: (0, 0)),
            pl.BlockSpec((1, LANES), lambda i, j: (0, 0)),
        ],
        out_specs=out_specs,
        scratch_shapes=[pltpu.VMEM((tm, D_MODEL), BF16), pltpu.VMEM((tm, D_BRANCH), F32)],
        compiler_params=_cparams(("arbitrary", "arbitrary")),
        name="in_projection",
    )(x, mod, g, w_main, w_small, bf_pad)


def _cumsum_kernel(x_ref, o_ref):
    x = x_ref[0]
    n = x.shape[0]
    lane = lax.broadcasted_iota(I32, x.shape, 1)
    row = lax.broadcasted_iota(I32, x.shape, 0)
    s = N_HEADS
    while s < LANES:
        x = x + jnp.where(lane >= s, pltpu.roll(x, s, axis=1), 0.0)
        s *= 2
    t = jnp.where(lane >= LANES - N_HEADS, x, 0.0)
    s = N_HEADS
    while s < LANES:
        t = t + pltpu.roll(t, s, axis=1)
        s *= 2
    t = jnp.where(row >= 1, pltpu.roll(t, 1, axis=0), 0.0)
    s = 1
    while s < n:
        if s < 8:
            sh = jnp.where(row >= s, pltpu.roll(t, s, axis=0), 0.0)
        else:
            sh = jnp.concatenate([jnp.zeros((s, LANES), F32), t[:n - s]], axis=0)
        t = t + sh
        s *= 2
    o_ref[0] = x + t


def _forget_cumsum(logf_all):
    b, lp, h = logf_all.shape
    n = lp * h // LANES
    out = pl.pallas_call(
        _cumsum_kernel,
        out_shape=jax.ShapeDtypeStruct((b, n, LANES), F32),
        grid=(b,),
        in_specs=[pl.BlockSpec((1, n, LANES), lambda i: (i, 0, 0))],
        out_specs=pl.BlockSpec((1, n, LANES), lambda i: (i, 0, 0)),
        compiler_params=_cparams(("arbitrary",)),
        name="forget_cumsum",
    )(logf_all.reshape(b, n, LANES))
    return out.reshape(b, lp, h)


def _pair_padded(qT_ref, h):
    pr, half = divmod(h, 2)
    blk = qT_ref[0, pr * LANES + half * HEAD_DIM:pr * LANES + (half + 1) * HEAD_DIM, :]
    z = jnp.zeros_like(blk)
    return jnp.concatenate([blk, z] if half == 0 else [z, blk], axis=0)


def _rows_to_heads(rows):
    tq = rows[0].shape[1]
    sub = lax.broadcasted_iota(I32, (N_HEADS, tq), 0)
    out = jnp.broadcast_to(rows[0], (N_HEADS, tq))
    for h in range(1, N_HEADS):
        out = jnp.where(sub == h, rows[h], out)
    return out


def _stage_and_consume(stage, consume, m_cur, pend, l, acc_ref):
    m_next, sums, alphas = [], [], []
    for h in range(N_HEADS):
        if stage is not None:
            logits, mask, buf = stage
            lg = logits(h)
            if mask is not None:
                lg = jnp.where(mask, lg, NEG_INF)
            buf[h] = lg
            m_next.append(jnp.maximum(m_cur[h:h + 1, :], jnp.max(lg, axis=0, keepdims=True)))
        if consume is not None:
            v_t, buf = consume
            before, after = pend
            alpha = jnp.exp2(before[h:h + 1, :] - after[h:h + 1, :])
            p = jnp.exp2(buf[h] - after[h:h + 1, :]).astype(BF16)
            ones = jnp.ones((16, p.shape[0]), BF16)
            pv = jnp.dot(jnp.concatenate([v_t(h), ones], axis=0), p, preferred_element_type=F32)
            rows = slice(h * HEAD_DIM, (h + 1) * HEAD_DIM)
            acc_ref[rows, :] = alpha * acc_ref[rows, :] + pv[:HEAD_DIM, :]
            sums.append(pv[HEAD_DIM:HEAD_DIM + 1, :])
            alphas.append(alpha)
    if consume is not None:
        l = _rows_to_heads(alphas) * l + _rows_to_heads(sums)
    return (_rows_to_heads(m_next) if stage is not None else m_cur), l


def _attend_block(n_free, n_used, logits, mask, v_t, m_ref, l_ref, acc_ref, buf0, buf1):
    m0 = m_ref[...]
    buf1[...] = jnp.full(buf1.shape, -jnp.inf, F32)

    def pair(masked, p, carry):
        m_a, m_b, l = carry
        c0 = 2 * p
        c1 = c0 + 1
        cp = jnp.maximum(c0 - 1, 0)
        mk = (lambda c: mask(c)) if masked else (lambda c: None)
        m_c, l = _stage_and_consume((lambda h: logits(h, c0), mk(c0), buf0),
                                    (lambda h: v_t(h, cp), buf1), m_b, (m_a, m_b), l, acc_ref)
        m_d, l = _stage_and_consume((lambda h: logits(h, c1), mk(c1), buf1),
                                    (lambda h: v_t(h, c0), buf0), m_c, (m_b, m_c), l, acc_ref)
        return m_c, m_d, l

    carry = lax.fori_loop(0, n_free, functools.partial(pair, False), (m0, m0, l_ref[...]))
    m_c, m_d, l = lax.fori_loop(n_free, n_used, functools.partial(pair, True), carry)
    last = 2 * n_used - 1
    _, l = _stage_and_consume(None, (lambda h: v_t(h, last), buf1), m_d, (m_c, m_d), l, acc_ref)
    m_ref[...] = m_d
    l_ref[...] = l


def _softmax_init(m_ref, l_ref, acc_ref):
    m_ref[...] = jnp.full(m_ref.shape, NEG_INF, F32)
    l_ref[...] = jnp.zeros(l_ref.shape, F32)
    acc_ref[...] = jnp.zeros(acc_ref.shape, F32)


def _softmax_finish(o_ref, l_ref, acc_ref):
    inv = 1.0 / l_ref[...]
    parts = [acc_ref[h * HEAD_DIM:(h + 1) * HEAD_DIM, :] * inv[h:h + 1, :] for h in range(N_HEADS)]
    o_ref[0] = jnp.concatenate(parts, axis=0).T.astype(o_ref.dtype)


def _ordered_to_f32(o):
    return lax.bitcast_convert_type(jnp.where(o >= 0, o, o ^ jnp.int32(0x7FFFFFFF)), F32)


def _dsa_kernel(qiT_ref, wiT_ref, qaT_ref, ki_ref, ka_ref, vaT_ref, o_ref,
                s_ref, thr_ref, cut_ref, qp_ref, m_ref, l_ref, acc_ref, lg0_ref, lg1_ref,
                *, past, length, n_sel, tq, tk, nk, idx_bits):
    i = pl.program_id(1)
    j = pl.program_id(2)
    q0 = past + i * tq
    qpos = q0 + lax.broadcasted_iota(I32, (1, tq), 1)
    vis = jnp.minimum(((qpos >> CHUNK_SHIFT) + 1) * CHUNK, length)
    vmax = jnp.minimum((((q0 + tq - 1) >> CHUNK_SHIFT) + 1) * CHUNK, length)
    nvis = (vmax + tk - 1) // tk
    vcap = ((vmax + KEY_PAIR - 1) // KEY_PAIR) * KEY_PAIR

    def rows_here(jb, step):
        return jnp.clip((vcap - jb * tk) // step, 0, tk // step)

    @pl.when(jnp.logical_and(j < nk, j < nvis))
    def _():
        def body(c, carry):
            r0 = pl.multiple_of(c * SCORE_ROWS, SCORE_ROWS)
            kblk = ki_ref[0, pl.ds(r0, SCORE_ROWS), :]
            acc = jnp.zeros((SCORE_ROWS, tq), F32)
            for h in range(N_HEADS):
                d = jnp.dot(kblk, qiT_ref[0, h * IDX_DIM:(h + 1) * IDX_DIM, :],
                            preferred_element_type=F32)
                acc = acc + jnp.maximum(d, 0.0) * wiT_ref[0, h:h + 1, :]
            kpos = j * tk + r0 + lax.broadcasted_iota(I32, (SCORE_ROWS, 1), 0)
            acc = jnp.where(kpos < vis, acc, -jnp.inf)
            s_ref[pl.ds(pl.multiple_of(j * tk + r0, SCORE_ROWS), SCORE_ROWS), :] = acc
            return carry
        lax.fori_loop(0, rows_here(j, SCORE_ROWS), body, 0)

    @pl.when(j == nk - 1)
    def _():
        nch = vcap // COUNT_ROWS

        def count(pred):
            def body(c, acc):
                r0 = pl.multiple_of(c * COUNT_ROWS, COUNT_ROWS)
                hit = jnp.where(pred(s_ref[pl.ds(r0, COUNT_ROWS), :], r0), 1.0, 0.0)
                return acc + jnp.sum(hit.reshape(COUNT_ROWS // 8, 8, tq), axis=0)
            acc = lax.fori_loop(0, nch, body, jnp.zeros((8, tq), F32))
            return jnp.sum(acc, axis=0, keepdims=True)

        def count_ge(t):
            return count(lambda blk, r0: blk >= t)

        kf = float(n_sel)
        cur = jnp.where(count_ge(jnp.zeros((1, tq), F32)) >= kf, jnp.int32(0), jnp.int32(INT32_MIN))

        def bit_body(b, cur):
            cand = cur + lax.shift_left(jnp.int32(1), 30 - b)
            return jnp.where(count_ge(_ordered_to_f32(cand)) >= kf, cand, cur)
        cur = lax.fori_loop(0, 31, bit_body, cur)
        thr = jnp.where(vis < n_sel, -F32_MAX, _ordered_to_f32(cur))
        n_gt = count(lambda blk, r0: blk > thr)
        n_ge = count_ge(thr)
        need = kf - n_gt
        tied = (n_ge - n_gt) > need

        def tie_cut():
            def count_before(x):
                def pred(blk, r0):
                    kpos = r0 + lax.broadcasted_iota(I32, (COUNT_ROWS, 1), 0)
                    return jnp.logical_and(blk == thr, kpos < x)
                return count(pred)
            lo = jnp.zeros((1, tq), I32)
            for b in reversed(range(idx_bits)):
                cand = lo + (1 << b)
                lo = jnp.where(count_before(cand) < need, cand, lo)
            return jnp.where(tied, lo, jnp.int32(2 ** 30))

        cut = lax.cond(jnp.max(jnp.where(tied, 1.0, 0.0)) > 0.0, tie_cut,
                       lambda: jnp.full((1, tq), 2 ** 30, I32))
        thr_ref[0:1, :] = thr
        cut_ref[0:1, :] = cut
        for h in range(N_HEADS):
            qp_ref[h] = _pair_padded(qaT_ref, h)
        _softmax_init(m_ref, l_ref, acc_ref)

    jj = j - nk

    @pl.when(jnp.logical_and(j >= nk, jj < nvis))
    def _():
        thr = thr_ref[0:1, :]
        cut = cut_ref[0:1, :]

        def selected(c):
            g0 = pl.multiple_of(jj * tk + c * KEY_CHUNK, KEY_CHUNK)
            sblk = s_ref[pl.ds(g0, KEY_CHUNK), :]
            kpos = g0 + lax.broadcasted_iota(I32, (KEY_CHUNK, 1), 0)
            return jnp.logical_or(sblk > thr, jnp.logical_and(sblk == thr, kpos <= cut))

        def logits(h, c):
            r0 = pl.multiple_of(c * KEY_CHUNK, KEY_CHUNK)
            kblk = ka_ref[0, pl.ds(r0, KEY_CHUNK), (h // 2) * LANES:(h // 2 + 1) * LANES]
            return jnp.dot(kblk, qp_ref[h], preferred_element_type=F32)

        def v_t(h, c):
            return vaT_ref[0, c, h * HEAD_DIM:(h + 1) * HEAD_DIM, :]

        _attend_block(0, rows_here(jj, KEY_PAIR), logits, selected, v_t, m_ref, l_ref, acc_ref,
                      lg0_ref, lg1_ref)

    @pl.when(j == 2 * nk - 1)
    def _():
        _softmax_finish(o_ref, l_ref, acc_ref)


def _dsa_attention(qiT, wiT, qaT, ki, ka, vaTc, *, past, length, tq, tk):
    b, _, tqp = qiT.shape
    lp = ki.shape[1]
    nq, nk = tqp // tq, lp // tk
    n_sel = min(TOPK_MAX, length // 4)

    def nvis(i):
        vmax = jnp.minimum(((past + (i + 1) * tq - 1) // CHUNK + 1) * CHUNK, length)
        return (vmax + tk - 1) // tk

    k1 = lambda bb, i, j: (bb, jnp.minimum(j, nvis(i) - 1), 0)
    k3 = lambda bb, i, j: (bb, jnp.clip(j - nk, 0, nvis(i) - 1), 0)
    k3c = lambda bb, i, j: (bb, jnp.clip(j - nk, 0, nvis(i) - 1), 0, 0)
    qmap = lambda bb, i, j: (bb, 0, i)
    kern = functools.partial(_dsa_kernel, past=past, length=length, n_sel=n_sel, tq=tq, tk=tk,
                             nk=nk, idx_bits=int(lp).bit_length())
    return pl.pallas_call(
        kern,
        out_shape=jax.ShapeDtypeStruct((b, tqp, D_BRANCH), BF16),
        grid=(b, nq, 2 * nk),
        in_specs=[
            pl.BlockSpec((1, D_BRANCH, tq), qmap),
            pl.BlockSpec((1, N_HEADS, tq), qmap),
            pl.BlockSpec((1, D_BRANCH, tq), qmap),
            pl.BlockSpec((1, tk, IDX_DIM), k1),
            pl.BlockSpec((1, tk, D_BRANCH), k3),
            pl.BlockSpec((1, tk // KEY_CHUNK, D_BRANCH, KEY_CHUNK), k3c),
        ],
        out_specs=pl.BlockSpec((1, tq, D_BRANCH), lambda bb, i, j: (bb, i, 0)),
        scratch_shapes=[
            pltpu.VMEM((lp, tq), F32),
            pltpu.VMEM((8, tq), F32),
            pltpu.VMEM((8, tq), I32),
            pltpu.VMEM((N_HEADS, LANES, tq), BF16),
            pltpu.VMEM((N_HEADS, tq), F32),
            pltpu.VMEM((N_HEADS, tq), F32),
            pltpu.VMEM((D_BRANCH, tq), F32),
            pltpu.VMEM((N_HEADS, KEY_CHUNK, tq), F32),
            pltpu.VMEM((N_HEADS, KEY_CHUNK, tq), F32),
        ],
        compiler_params=_cparams(("arbitrary", "arbitrary", "arbitrary")),
        name="dsa_attention",
    )(qiT, wiT, qaT, ki, ka, vaTc)


_AUG = 6


def _fox_kernel(qbT_ref, qaug_ref, kb_ref, kaug_ref, vbT_ref, o_ref,
                qf_ref, m_ref, l_ref, acc_ref, lg0_ref, lg1_ref, *, past, tq, tk, nk):
    i = pl.program_id(1)
    j = pl.program_id(2)
    q0 = past + i * tq
    qpos = q0 + lax.broadcasted_iota(I32, (1, tq), 1)
    nvis = (q0 + tq + tk - 1) // tk
    npair = tk // KEY_PAIR
    n_free = jnp.clip((q0 + 1 - j * tk) // KEY_PAIR, 0, npair)
    n_used = jnp.clip((q0 + tq - j * tk + KEY_PAIR - 1) // KEY_PAIR, 0, npair)

    @pl.when(j == 0)
    def _():
        arow = lax.broadcasted_iota(I32, (LANES, 1), 0)
        qa = qaug_ref[0]
        for h in range(N_HEADS):
            mine = jnp.logical_and(arow >= _AUG * h, arow < _AUG * (h + 1))
            qf_ref[h, 0:LANES, :] = _pair_padded(qbT_ref, h)
            qf_ref[h, LANES:2 * LANES, :] = jnp.where(mine, qa, 0.0).astype(BF16)
        _softmax_init(m_ref, l_ref, acc_ref)

    @pl.when(j < nvis)
    def _():
        def causal(c):
            kpos = j * tk + c * KEY_CHUNK + lax.broadcasted_iota(I32, (KEY_CHUNK, 1), 0)
            return kpos <= qpos

        def logits(h, c):
            r0 = pl.multiple_of(c * KEY_CHUNK, KEY_CHUNK)
            kblk = kb_ref[0, pl.ds(r0, KEY_CHUNK), (h // 2) * LANES:(h // 2 + 1) * LANES]
            lhs = jnp.concatenate([kblk, kaug_ref[0, pl.ds(r0, KEY_CHUNK), :]], axis=1)
            return jnp.dot(lhs, qf_ref[h], preferred_element_type=F32)

        def v_t(h, c):
            return vbT_ref[0, c, h * HEAD_DIM:(h + 1) * HEAD_DIM, :]

        _attend_block(n_free, n_used, logits, causal, v_t, m_ref, l_ref, acc_ref,
                      lg0_ref, lg1_ref)

    @pl.when(j == nk - 1)
    def _():
        _softmax_finish(o_ref, l_ref, acc_ref)


def _fox_attention(qbT, qaug, kb, kaug, vbTc, *, past, tq, tk):
    b, _, tqp = qbT.shape
    lp = kb.shape[1]
    nq, nk = tqp // tq, lp // tk

    def last(i):
        return jnp.minimum((past + (i + 1) * tq + tk - 1) // tk, nk) - 1

    kmap = lambda bb, i, j: (bb, jnp.minimum(j, last(i)), 0)
    kmapc = lambda bb, i, j: (bb, jnp.minimum(j, last(i)), 0, 0)
    qmap = lambda bb, i, j: (bb, 0, i)
    return pl.pallas_call(
        functools.partial(_fox_kernel, past=past, tq=tq, tk=tk, nk=nk),
        out_shape=jax.ShapeDtypeStruct((b, tqp, D_BRANCH), BF16),
        grid=(b, nq, nk),
        in_specs=[
            pl.BlockSpec((1, D_BRANCH, tq), qmap),
            pl.BlockSpec((1, LANES, tq), qmap),
            pl.BlockSpec((1, tk, D_BRANCH), kmap),
            pl.BlockSpec((1, tk, LANES), kmap),
            pl.BlockSpec((1, tk // KEY_CHUNK, D_BRANCH, KEY_CHUNK), kmapc),
        ],
        out_specs=pl.BlockSpec((1, tq, D_BRANCH), lambda bb, i, j: (bb, i, 0)),
        scratch_shapes=[
            pltpu.VMEM((N_HEADS, 2 * LANES, tq), BF16),
            pltpu.VMEM((N_HEADS, tq), F32),
            pltpu.VMEM((N_HEADS, tq), F32),
            pltpu.VMEM((D_BRANCH, tq), F32),
            pltpu.VMEM((N_HEADS, KEY_CHUNK, tq), F32),
            pltpu.VMEM((N_HEADS, KEY_CHUNK, tq), F32),
        ],
        compiler_params=_cparams(("arbitrary", "arbitrary", "arbitrary")),
        name="fox_attention",
    )(qbT, qaug, kb, kaug, vbTc)


AUG_ROWS = 512


def _aug_kernel(c_ref, kaug_ref, qaugT_ref):
    x = c_ref[0] * LOG2E
    a = x.astype(BF16).astype(F32)
    r = x - a
    b = r.astype(BF16).astype(F32)
    c = (r - b).astype(BF16).astype(F32)
    lane = lax.broadcasted_iota(I32, (N_HEADS, LANES), 1)
    head = lax.broadcasted_iota(I32, (N_HEADS, LANES), 0)

    def place(v, slot):
        sel = jnp.where(lane == _AUG * head + slot, 1.0, 0.0)
        return jnp.dot(v, sel, preferred_element_type=F32, precision=lax.Precision.HIGHEST)

    slot = lane - _AUG * head
    ones_k = jnp.sum(jnp.where(jnp.logical_and(slot >= 0, slot < 3), 1.0, 0.0), axis=0, keepdims=True)
    ones_q = jnp.sum(jnp.where(jnp.logical_and(slot >= 3, slot < 6), 1.0, 0.0), axis=0, keepdims=True)
    kaug_ref[0] = (ones_k - (place(a, 3) + place(b, 4) + place(c, 5))).astype(BF16)
    qaugT_ref[0] = (ones_q + (place(a, 0) + place(b, 1) + place(c, 2))).T


def _fox_augment(cum):
    b, lp, h = cum.shape
    return pl.pallas_call(
        _aug_kernel,
        out_shape=(jax.ShapeDtypeStruct((b, lp, LANES), BF16),
                   jax.ShapeDtypeStruct((b, LANES, lp), F32)),
        grid=(b, lp // AUG_ROWS),
        in_specs=[pl.BlockSpec((1, AUG_ROWS, h), lambda i, j: (i, j, 0))],
        out_specs=(pl.BlockSpec((1, AUG_ROWS, LANES), lambda i, j: (i, j, 0)),
                   pl.BlockSpec((1, LANES, AUG_ROWS), lambda i, j: (i, 0, j))),
        compiler_params=_cparams(("arbitrary", "arbitrary")),
        name="fox_augment",
    )(cum)


def _shifted_rows(buf, load, store, u, past_ref, starts_batch, nb):
    tm, c = u.shape
    store(slice(8, 8 + tm), u)
    if nb == 1:
        @pl.when(starts_batch)
        def _():
            store(slice(6, 8), past_ref[0])
    s1 = load(slice(7, 7 + tm))
    s2 = load(slice(6, 6 + tm))
    if nb > 1:
        r = lax.broadcasted_iota(I32, (nb, tm // nb, 1), 1)
        p0 = past_ref[:, 0:1, :]
        p1 = past_ref[:, 1:2, :]
        s1 = jnp.where(r == 0, p1, _group(s1, nb)).reshape(tm, c)
        s2 = jnp.where(r == 0, p0, jnp.where(r == 1, p1, _group(s2, nb))).reshape(tm, c)
    store(slice(0, 8), load(slice(tm, tm + 8)))
    return s1, s2


def _merge_kernel(x_ref, ya_ref, yb_ref, cb_ref, u_ref, gl_ref, past_ref, cw_ref, wbr_ref,
                  wo_ref, g_ref, mod_ref, o_ref, ub_ref, *, nb, rows_per_batch):
    i = pl.program_id(0)
    tm = x_ref.shape[0]
    u = u_ref[...]

    def load(rows):
        return ub_ref[rows, :]

    def store(rows, v):
        ub_ref[rows, :] = v

    s1, s2 = _shifted_rows(ub_ref, load, store, u, past_ref, (i * tm) % rows_per_batch == 0, nb)
    conv = cw_ref[0:1, :] * s2 + cw_ref[1:2, :] * s1 + cw_ref[2:3, :] * u
    yc = (cb_ref[...] * conv).astype(BF16)
    mix = jnp.zeros((tm, D_MODEL), F32)
    for n, y in enumerate((ya_ref[...], yb_ref[...], yc)):
        br = jnp.dot(y, wbr_ref[n], preferred_element_type=F32)
        mix = mix + jax.nn.sigmoid(gl_ref[:, n * D_MODEL:(n + 1) * D_MODEL]) * br
    mo = jnp.dot(mix.astype(BF16), wo_ref[...], preferred_element_type=F32)
    nm = _group(_rms(mo, g_ref[1:2, :]), nb)
    o_ref[...] = x_ref[...] + (mod_ref[:, 2:3, :] * nm).reshape(tm, D_MODEL)


def _mod_spec(tm, nb, rows_per_batch, ngrid):
    if nb == 1:
        f = lambda i, *_: ((i * tm) // rows_per_batch, 0, 0)
    else:
        f = lambda i, *_: (i, 0, 0)
    return pl.BlockSpec((nb, 6, D_MODEL), f)


def _merge(x, ya, yb, cb, u, gl, past, cw, wbr, wo, g, mod, *, tm, rows_per_batch):
    rows = x.shape[0]
    nb = max(1, tm // rows_per_batch)
    row = lambda w: pl.BlockSpec((tm, w), lambda i: (i, 0))
    full = lambda shape: pl.BlockSpec(shape, lambda i: (0,) * len(shape))
    if nb == 1:
        past_spec = pl.BlockSpec((1, 2, D_BRANCH), lambda i: ((i * tm) // rows_per_batch, 0, 0))
    else:
        past_spec = pl.BlockSpec((nb, 2, D_BRANCH), lambda i: (i, 0, 0))
    return pl.pallas_call(
        functools.partial(_merge_kernel, nb=nb, rows_per_batch=rows_per_batch),
        out_shape=jax.ShapeDtypeStruct((rows, D_MODEL), F32),
        grid=(rows // tm,),
        in_specs=[
            row(D_MODEL), row(D_BRANCH), row(D_BRANCH), row(D_BRANCH), row(D_BRANCH),
            row(3 * D_MODEL), past_spec, full((CONV_W, D_BRANCH)),
            full((3, D_BRANCH, D_MODEL)), full((D_MODEL, D_MODEL)), full((4, D_MODEL)),
            _mod_spec(tm, nb, rows_per_batch, 1),
        ],
        out_specs=row(D_MODEL),
        scratch_shapes=[pltpu.VMEM((tm + 8, D_BRANCH), F32)],
        compiler_params=_cparams(("arbitrary",)),
        name="branch_merge",
    )(x, ya, yb, cb, u, gl, past, cw, wbr, wo, g, mod)


FF_CHUNK = 256


def _ffn_kernel(x_ref, g_ref, mod_ref, wg_ref, wv_ref, wd_ref, cw_ref, past_ref,
                o_ref, tail_ref, hs_ref, acc_ref, ub_ref, *, nb, rows_per_batch, nj):
    i = pl.program_id(0)
    j = pl.program_id(1)
    tm = x_ref.shape[0]

    @pl.when(j == 0)
    def _():
        y = _group(_rms(x_ref[...], g_ref[2:3, :]), nb)
        h = y * (1.0 + mod_ref[:, 4:5, :]) + mod_ref[:, 3:4, :]
        hs_ref[...] = h.reshape(tm, D_MODEL).astype(BF16)
        acc_ref[...] = jnp.zeros(acc_ref.shape, F32)

    hs = hs_ref[...]
    ug = jnp.dot(hs, wg_ref[...], preferred_element_type=F32)
    uv = jnp.dot(hs, wv_ref[...], preferred_element_type=F32)

    def load(rows):
        return ub_ref[j, rows, :]

    def store(rows, v):
        ub_ref[j, rows, :] = v

    s1, s2 = _shifted_rows(ub_ref, load, store, ug, past_ref, (i * tm) % rows_per_batch == 0, nb)
    conv = cw_ref[0:1, :] * s2 + cw_ref[1:2, :] * s1 + cw_ref[2:3, :] * ug
    f = conv * jax.nn.sigmoid(conv) * uv
    acc_ref[...] += jnp.dot(f.astype(BF16), wd_ref[...], preferred_element_type=F32)
    grp = tm // nb
    tail_ref[0] = _group(ug, nb)[:, grp - 8:, :]

    @pl.when(j == nj - 1)
    def _():
        nm = _group(_rms(acc_ref[...], g_ref[3:4, :]), nb)
        o_ref[...] = x_ref[...] + (mod_ref[:, 5:6, :] * nm).reshape(tm, D_MODEL)


def _conv_ffn(x, g, mod, w_up, w_down, cw, past, *, tm, rows_per_batch):
    rows = x.shape[0]
    dff = w_down.shape[0]
    nj = dff // FF_CHUNK
    nb = max(1, tm // rows_per_batch)
    nt = rows // tm
    if nb == 1:
        past_spec = pl.BlockSpec((1, 2, FF_CHUNK), lambda i, j: ((i * tm) // rows_per_batch, 0, j))
        mod_map = lambda i, j: ((i * tm) // rows_per_batch, 0, 0)
    else:
        past_spec = pl.BlockSpec((nb, 2, FF_CHUNK), lambda i, j: (i, 0, j))
        mod_map = lambda i, j: (i, 0, 0)
    out, tails = pl.pallas_call(
        functools.partial(_ffn_kernel, nb=nb, rows_per_batch=rows_per_batch, nj=nj),
        out_shape=(jax.ShapeDtypeStruct((rows, D_MODEL), F32),
                   jax.ShapeDtypeStruct((nt, nb, 8, dff), F32)),
        grid=(nt, nj),
        in_specs=[
            pl.BlockSpec((tm, D_MODEL), lambda i, j: (i, 0)),
            pl.BlockSpec((4, D_MODEL), lambda i, j: (0, 0)),
            pl.BlockSpec((nb, 6, D_MODEL), mod_map),
            pl.BlockSpec((D_MODEL, FF_CHUNK), lambda i, j: (0, j)),
            pl.BlockSpec((D_MODEL, FF_CHUNK), lambda i, j: (0, j + nj)),
            pl.BlockSpec((FF_CHUNK, D_MODEL), lambda i, j: (j, 0)),
            pl.BlockSpec((CONV_W, FF_CHUNK), lambda i, j: (0, j)),
            past_spec,
        ],
        out_specs=(pl.BlockSpec((tm, D_MODEL), lambda i, j: (i, 0)),
                   pl.BlockSpec((1, nb, 8, FF_CHUNK), lambda i, j: (i, 0, 0, j))),
        scratch_shapes=[
            pltpu.VMEM((tm, D_MODEL), BF16),
            pltpu.VMEM((tm, D_MODEL), F32),
            pltpu.VMEM((nj, tm + 8, FF_CHUNK), F32),
        ],
        compiler_params=_cparams(("arbitrary", "arbitrary")),
        name="conv_ffn",
    )(x, g, mod, w_up, w_up, w_down, cw, past)
    return out, tails


def _relayout_w_in(w_in, b_forget):
    db = D_BRANCH
    o_qi = 3 * db
    o_ki = o_qi + N_HEADS * IDX_DIM
    o_wi = o_ki + IDX_DIM
    o_qb = o_wi + N_HEADS
    o_fl = o_qb + 3 * db
    o_cb = o_fl + N_HEADS
    o_gl = o_cb + 3 * db
    main = jnp.concatenate([w_in[:, 0:o_ki], w_in[:, o_qb:o_fl], w_in[:, o_cb:]], axis=1)
    small = jnp.concatenate([w_in[:, o_ki:o_wi], w_in[:, o_wi:o_qb], w_in[:, o_fl:o_cb]], axis=1)
    small = jnp.pad(small, ((0, 0), (0, LANES - small.shape[1])))
    bf = jnp.zeros((1, LANES), F32).at[0, _FL_OFF:_FL_OFF + N_HEADS].set(b_forget)
    del o_gl
    return main.astype(BF16), small.astype(BF16), bf


def _round_up(x, m):
    return (x + m - 1) // m * m


def _layer(x, mod, caches, weights, *, batch, t, past, cfg):
    (g, w_main, w_small, bf_pad, cw_mix, wbr, wo, w_up, cw_ffn, w_down) = weights
    rows = batch * t
    length = past + t
    (qaT, ka32, ka16, va32, vaTc, qiT, qbT, kb32, kb16, vb32, vbTc, cb, u, gl,
     ki32, ki16, wiT, logf) = _in_projection(
        x, mod, g, w_main, w_small, bf_pad, tm=cfg["tm_in"], rows_per_batch=t)

    tq, tk = cfg["tq"], cfg["tk"]
    tqp = _round_up(t, tq)
    lp = _round_up(length, tk)

    def per_batch_T(aT):
        c = aT.shape[0]
        a = jnp.swapaxes(aT.reshape(c, batch, t), 0, 1)
        return jnp.pad(a, ((0, 0), (0, 0), (0, tqp - t)))

    def chunked_T(a):
        return jnp.swapaxes(a.reshape(batch, lp // KEY_CHUNK, KEY_CHUNK, a.shape[-1]), 2, 3)

    def with_cache(cache, new, dtype):
        new = new.reshape(batch, t, -1)
        if cache is not None:
            new = jnp.concatenate([cache.reshape(batch, past, -1).astype(dtype), new.astype(dtype)],
                                  axis=1)
        return jnp.pad(new.astype(dtype), ((0, 0), (0, lp - length), (0, 0)))

    if caches is None:
        c_idx = c_dk = c_dv = c_fk = c_fv = c_lf = None
        past_mix = jnp.zeros((batch, CONV_W - 1, D_BRANCH), F32)
        past_ffn = jnp.zeros((batch, CONV_W - 1, w_down.shape[0]), F32)
    else:
        c_idx, c_dk, c_dv, c_fk, c_fv, c_lf, past_mix, past_ffn = caches

    ki_all = with_cache(c_idx, ki16, BF16)
    ka_all = with_cache(c_dk, ka16, BF16)
    kb_all = with_cache(c_fk, kb16, BF16)
    if caches is None and lp == rows:
        vaT_all = vaTc.reshape(batch, lp // KEY_CHUNK, D_BRANCH, KEY_CHUNK)
        vbT_all = vbTc.reshape(batch, lp // KEY_CHUNK, D_BRANCH, KEY_CHUNK)
    else:
        vaT_all = chunked_T(with_cache(c_dv, va32, BF16))
        vbT_all = chunked_T(with_cache(c_fv, vb32, BF16))
    logf_all = with_cache(c_lf, logf, F32)

    ya = _dsa_attention(per_batch_T(qiT), per_batch_T(wiT), per_batch_T(qaT), ki_all, ka_all,
                        vaT_all, past=past, length=length, tq=tq, tk=tk)
    cum = _forget_cumsum(logf_all)
    kaug, qaug = _fox_augment(cum)
    qaug = jnp.pad(qaug[:, :, past:length], ((0, 0), (0, 0), (0, tqp - t)))
    yb = _fox_attention(per_batch_T(qbT), qaug, kb_all, kaug, vbT_all, past=past, tq=tq, tk=tk)
    ya = ya[:, :t].reshape(rows, D_BRANCH)
    yb = yb[:, :t].reshape(rows, D_BRANCH)

    x1 = _merge(x, ya, yb, cb, u, gl, past_mix, cw_mix, wbr, wo, g, mod,
                tm=cfg["tm_merge"], rows_per_batch=t)
    x2, tails = _conv_ffn(x1, g, mod, w_up, w_down, cw_ffn, past_ffn,
                          tm=cfg["tm_ffn"], rows_per_batch=t)

    new_mix = u.reshape(batch, t, D_BRANCH)[:, t - (CONV_W - 1):]
    nb = max(1, cfg["tm_ffn"] // t)
    if nb == 1:
        tiles_per_batch = t // cfg["tm_ffn"]
        last = tails.reshape(batch, tiles_per_batch, 8, -1)[:, -1]
    else:
        last = tails.reshape(batch, 8, -1)
    new_ffn = last[:, 8 - (CONV_W - 1):]
    hd = (batch, t, N_HEADS, HEAD_DIM)
    state = (ki32.reshape(batch, t, IDX_DIM), ka32.reshape(hd), va32.reshape(hd),
             kb32.reshape(hd), vb32.reshape(hd), logf.reshape(batch, t, N_HEADS), new_mix, new_ffn)
    return x2, state


_PROMPT_CFG = dict(tm_in=512, tm_merge=256, tm_ffn=512, tq=256, tk=2048)
_SAMPLE_CFG = dict(tm_in=256, tm_merge=256, tm_ffn=256, tq=128, tk=2560)


def kernel(x_prompt, x_sample, c_prompt, c_sample, cache_idx_k, cache_dsa_k, cache_dsa_v,
           cache_fox_k, cache_fox_v, cache_fox_logf, state_conv_mix, state_conv_ffn,
           w_ada, b_ada, norm_g, w_in, b_forget, conv_mix_w, w_branch, w_out, w_up,
           conv_ffn_w, w_down):
    bp, tp, _ = x_prompt.shape
    bs, ts, _ = x_sample.shape
    past = cache_dsa_k.shape[2]
    depth = w_ada.shape[0]

    c_all = jnp.concatenate([c_prompt, c_sample], axis=0)
    pad_rows = _round_up(c_all.shape[0], 8) - c_all.shape[0]
    c_all = jnp.pad(c_all, ((0, pad_rows), (0, 0)))
    mod_all = _modulation(c_all, w_ada, b_ada).reshape(depth, -1, 6, D_MODEL)

    yp = x_prompt.reshape(bp * tp, D_MODEL)
    ys = x_sample.reshape(bs * ts, D_MODEL)
    p_states, s_states = [], []
    for l in range(depth):
        w_main, w_small, bf_pad = _relayout_w_in(w_in[l], b_forget[l])
        weights = (norm_g[l], w_main, w_small, bf_pad, conv_mix_w[l], w_branch[l].astype(BF16),
                   w_out[l].astype(BF16), w_up[l].astype(BF16), conv_ffn_w[l],
                   w_down[l].astype(BF16))
        yp, st_p = _layer(yp, mod_all[l, :bp], None, weights, batch=bp, t=tp, past=0,
                          cfg=_PROMPT_CFG)
        caches = (cache_idx_k[l], cache_dsa_k[l], cache_dsa_v[l], cache_fox_k[l], cache_fox_v[l],
                  cache_fox_logf[l], state_conv_mix[l], state_conv_ffn[l])
        ys, st_s = _layer(ys, mod_all[l, bp:bp + bs], caches, weights, batch=bs, t=ts, past=past,
                          cfg=_SAMPLE_CFG)
        p_states.append(st_p)
        s_states.append(st_s)

    stack = lambda states: [jnp.stack([st[k] for st in states], axis=0) for k in range(8)]
    return (yp.reshape(bp, tp, D_MODEL), ys.reshape(bs, ts, D_MODEL), *stack(p_states),
            *stack(s_states))
```

```python
import functools

import jax
import jax.numpy as jnp
from jax import lax
from jax.experimental import pallas as pl
from jax.experimental.pallas import tpu as pltpu

F32 = jnp.float32
BF16 = jnp.bfloat16
I32 = jnp.int32

D_MODEL = 1024
HEAD_DIM = 64
D_BRANCH = 512
N_HEADS = 8
IDX_DIM = 64
CHUNK = 64
CHUNK_SHIFT = 6
TOPK_MAX = 256
CONV_W = 3
EPS = 1e-6
NEG_INF = -1e30
LOG2E = 1.4426950408889634
F32_MAX = 3.4028234663852886e38
INT32_MIN = -(2 ** 31)

LANES = 128
N_MAIN = 16 * D_BRANCH
KEY_CHUNK = 256
KEY_PAIR = 2 * KEY_CHUNK
SCORE_ROWS = 128
COUNT_ROWS = KEY_PAIR
GROUPS = TOPK_MAX
VMEM_LIMIT = 56 * 1024 * 1024


def _cparams(sem):
    return pltpu.CompilerParams(dimension_semantics=sem, vmem_limit_bytes=VMEM_LIMIT)


def _rms(x, g_row):
    return x * lax.rsqrt(jnp.mean(x * x, axis=-1, keepdims=True) + EPS) * g_row


def _group(x, nb):
    return x.reshape(nb, x.shape[0] // nb, x.shape[1])


def _mod_kernel(c_ref, w_ref, b_ref, o_ref):
    c = c_ref[...]
    s = (c * jax.nn.sigmoid(c)).astype(BF16)
    o_ref[0] = jnp.dot(s, w_ref[0].astype(BF16), preferred_element_type=F32) + b_ref[0]


def _modulation(c_all, w_ada, b_ada):
    depth = w_ada.shape[0]
    rows = c_all.shape[0]
    n = w_ada.shape[2]
    tn = D_MODEL
    return pl.pallas_call(
        _mod_kernel,
        out_shape=jax.ShapeDtypeStruct((depth, rows, n), F32),
        grid=(depth, n // tn),
        in_specs=[
            pl.BlockSpec((rows, D_MODEL), lambda l, j: (0, 0)),
            pl.BlockSpec((1, D_MODEL, tn), lambda l, j: (l, 0, j)),
            pl.BlockSpec((1, 1, tn), lambda l, j: (l, 0, j)),
        ],
        out_specs=pl.BlockSpec((1, rows, tn), lambda l, j: (l, 0, j)),
        compiler_params=_cparams(("arbitrary", "arbitrary")),
        name="adaln_mod",
    )(c_all, w_ada, b_ada.reshape(depth, 1, n))


_SEG_QA, _SEG_KA, _SEG_VA, _SEG_QI, _SEG_QB, _SEG_KB, _SEG_VB, _SEG_CB, _SEG_CC, _SEG_CX, _SEG_GL = range(11)
_WI_OFF = IDX_DIM
_FL_OFF = IDX_DIM + N_HEADS


def _inproj_kernel(x_ref, mod_ref, g_ref, wm_ref, ws_ref, bf_ref,
                   qaT_ref, ka32_ref, ka16_ref, va32_ref, vaT_ref, qiT_ref, qbT_ref,
                   kb32_ref, kb16_ref, vb32_ref, vbT_ref, cb_ref, u_ref, gl_ref,
                   ki32_ref, ki16_ref, wiT_ref, logf_ref,
                   hs_ref, cc_ref, *, nb):
    j = pl.program_id(1)
    tm = x_ref.shape[0]

    @pl.when(j == 0)
    def _():
        y = _rms(x_ref[...], g_ref[0:1, :])
        h = _group(y, nb) * (1.0 + mod_ref[:, 1:2, :]) + mod_ref[:, 0:1, :]
        hb = h.reshape(tm, D_MODEL).astype(BF16)
        hs_ref[...] = hb
        sm = jnp.dot(hb, ws_ref[...], preferred_element_type=F32)
        ki32_ref[...] = sm[:, :IDX_DIM]
        ki16_ref[...] = sm[:, :IDX_DIM].astype(BF16)
        wiT_ref[...] = sm.T[_WI_OFF:_WI_OFF + N_HEADS, :]
        z = sm + bf_ref[...]
        lf = -(jnp.maximum(-z, 0.0) + jnp.log1p(jnp.exp(-jnp.abs(z))))
        logf_ref[...] = lf[:, _FL_OFF:_FL_OFF + N_HEADS]

    acc = jnp.dot(hs_ref[...], wm_ref[...], preferred_element_type=F32)
    scale = LOG2E * HEAD_DIM ** -0.5

    def store_chunked_T(ref, a):
        aT = a.T.astype(BF16)
        for c in range(tm // KEY_CHUNK):
            ref[c] = aT[:, c * KEY_CHUNK:(c + 1) * KEY_CHUNK]

    @pl.when(j == _SEG_QA)
    def _():
        qaT_ref[...] = (acc * scale).T.astype(BF16)

    @pl.when(j == _SEG_KA)
    def _():
        ka32_ref[...] = acc
        ka16_ref[...] = acc.astype(BF16)

    @pl.when(j == _SEG_VA)
    def _():
        va32_ref[...] = acc
        store_chunked_T(vaT_ref, acc)

    @pl.when(j == _SEG_QI)
    def _():
        qiT_ref[...] = acc.T.astype(BF16)

    @pl.when(j == _SEG_QB)
    def _():
        qbT_ref[...] = (acc * scale).T.astype(BF16)

    @pl.when(j == _SEG_KB)
    def _():
        kb32_ref[...] = acc
        kb16_ref[...] = acc.astype(BF16)

    @pl.when(j == _SEG_VB)
    def _():
        vb32_ref[...] = acc
        store_chunked_T(vbT_ref, acc)

    @pl.when(j == _SEG_CB)
    def _():
        cb_ref[...] = acc

    @pl.when(j == _SEG_CC)
    def _():
        cc_ref[...] = acc

    @pl.when(j == _SEG_CX)
    def _():
        u_ref[...] = cc_ref[...] * acc

    @pl.when(j >= _SEG_GL)
    def _():
        gl_ref[...] = acc


def _in_projection(x, mod, g, w_main, w_small, bf_pad, *, tm, rows_per_batch):
    rows = x.shape[0]
    nt = rows // tm
    nb = max(1, tm // rows_per_batch)
    nseg = N_MAIN // D_BRANCH
    if nb == 1:
        mod_map = lambda i, j: ((i * tm) // rows_per_batch, 0, 0)
    else:
        mod_map = lambda i, j: (i, 0, 0)
    row_blk = lambda w: pl.BlockSpec((tm, w), lambda i, j: (i, 0))
    col_blk = lambda h: pl.BlockSpec((h, tm), lambda i, j: (0, i))
    chunkT = pl.BlockSpec((tm // KEY_CHUNK, D_BRANCH, KEY_CHUNK), lambda i, j: (i, 0, 0))
    f32o = lambda w: jax.ShapeDtypeStruct((rows, w), F32)
    b16o = lambda w: jax.ShapeDtypeStruct((rows, w), BF16)
    b16T = jax.ShapeDtypeStruct((D_BRANCH, rows), BF16)
    b16c = jax.ShapeDtypeStruct((rows // KEY_CHUNK, D_BRANCH, KEY_CHUNK), BF16)
    out_shape = (
        b16T, f32o(D_BRANCH), b16o(D_BRANCH), f32o(D_BRANCH), b16c, b16T, b16T,
        f32o(D_BRANCH), b16o(D_BRANCH), f32o(D_BRANCH), b16c, f32o(D_BRANCH), f32o(D_BRANCH),
        f32o(6 * D_BRANCH),
        f32o(IDX_DIM), b16o(IDX_DIM), jax.ShapeDtypeStruct((N_HEADS, rows), F32), f32o(N_HEADS),
    )
    out_specs = (
        col_blk(D_BRANCH), row_blk(D_BRANCH), row_blk(D_BRANCH), row_blk(D_BRANCH), chunkT,
        col_blk(D_BRANCH), col_blk(D_BRANCH),
        row_blk(D_BRANCH), row_blk(D_BRANCH), row_blk(D_BRANCH), chunkT, row_blk(D_BRANCH),
        row_blk(D_BRANCH),
        pl.BlockSpec((tm, D_BRANCH), lambda i, j: (i, jnp.clip(j - _SEG_GL, 0, 5))),
        row_blk(IDX_DIM), row_blk(IDX_DIM), col_blk(N_HEADS), row_blk(N_HEADS),
    )
    return pl.pallas_call(
        functools.partial(_inproj_kernel, nb=nb),
        out_shape=out_shape,
        grid=(nt, nseg),
        in_specs=[
            pl.BlockSpec((tm, D_MODEL), lambda i, j: (i, 0)),
            pl.BlockSpec((nb, 6, D_MODEL), mod_map),
            pl.BlockSpec((4, D_MODEL), lambda i, j: (0, 0)),
            pl.BlockSpec((D_MODEL, D_BRANCH), lambda i, j: (0, j)),
            pl.BlockSpec((D_MODEL, LANES), lambda i, j: (0, 0)),
            pl.BlockSpec((1, LANES), lambda i, j: (0, 0)),
        ],
        out_specs=out_specs,
        scratch_shapes=[pltpu.VMEM((tm, D_MODEL), BF16), pltpu.VMEM((tm, D_BRANCH), F32)],
        compiler_params=_cparams(("arbitrary", "arbitrary")),
        name="in_projection",
    )(x, mod, g, w_main, w_small, bf_pad)


def _cumsum_kernel(x_ref, o_ref):
    x = x_ref[0]
    n = x.shape[0]
    lane = lax.broadcasted_iota(I32, x.shape, 1)
    row = lax.broadcasted_iota(I32, x.shape, 0)
    s = N_HEADS
    while s < LANES:
        x = x + jnp.where(lane >= s, pltpu.roll(x, s, axis=1), 0.0)
        s *= 2
    t = jnp.where(lane >= LANES - N_HEADS, x, 0.0)
    s = N_HEADS
    while s < LANES:
        t = t + pltpu.roll(t, s, axis=1)
        s *= 2
    t = jnp.where(row >= 1, pltpu.roll(t, 1, axis=0), 0.0)
    s = 1
    while s < n:
        if s < 8:
            sh = jnp.where(row >= s, pltpu.roll(t, s, axis=0), 0.0)
        else:
            sh = jnp.concatenate([jnp.zeros((s, LANES), F32), t[:n - s]], axis=0)
        t = t + sh
        s *= 2
    o_ref[0] = x + t


def _forget_cumsum(logf_all):
    b, lp, h = logf_all.shape
    n = lp * h // LANES
    out = pl.pallas_call(
        _cumsum_kernel,
        out_shape=jax.ShapeDtypeStruct((b, n, LANES), F32),
        grid=(b,),
        in_specs=[pl.BlockSpec((1, n, LANES), lambda i: (i, 0, 0))],
        out_specs=pl.BlockSpec((1, n, LANES), lambda i: (i, 0, 0)),
        compiler_params=_cparams(("arbitrary",)),
        name="forget_cumsum",
    )(logf_all.reshape(b, n, LANES))
    return out.reshape(b, lp, h)


def _pair_padded(qT_ref, h):
    pr, half = divmod(h, 2)
    blk = qT_ref[0, pr * LANES + half * HEAD_DIM:pr * LANES + (half + 1) * HEAD_DIM, :]
    z = jnp.zeros_like(blk)
    return jnp.concatenate([blk, z] if half == 0 else [z, blk], axis=0)


def _rows_to_heads(rows):
    tq = rows[0].shape[1]
    sub = lax.broadcasted_iota(I32, (N_HEADS, tq), 0)
    out = jnp.broadcast_to(rows[0], (N_HEADS, tq))
    for h in range(1, N_HEADS):
        out = jnp.where(sub == h, rows[h], out)
    return out


def _stage_and_consume(stage, consume, m_cur, pend, l, acc_ref):
    m_next, sums, alphas = [], [], []
    for h in range(N_HEADS):
        if stage is not None:
            logits, mask, buf = stage
            lg = logits(h)
            if mask is not None:
                lg = jnp.where(mask, lg, NEG_INF)
            buf[h] = lg
            m_next.append(jnp.maximum(m_cur[h:h + 1, :], jnp.max(lg, axis=0, keepdims=True)))
        if consume is not None:
            v_t, buf = consume
            before, after = pend
            alpha = jnp.exp2(before[h:h + 1, :] - after[h:h + 1, :])
            p = jnp.exp2(buf[h] - after[h:h + 1, :]).astype(BF16)
            ones = jnp.ones((16, p.shape[0]), BF16)
            pv = jnp.dot(jnp.concatenate([v_t(h), ones], axis=0), p, preferred_element_type=F32)
            rows = slice(h * HEAD_DIM, (h + 1) * HEAD_DIM)
            acc_ref[rows, :] = alpha * acc_ref[rows, :] + pv[:HEAD_DIM, :]
            sums.append(pv[HEAD_DIM:HEAD_DIM + 1, :])
            alphas.append(alpha)
    if consume is not None:
        l = _rows_to_heads(alphas) * l + _rows_to_heads(sums)
    return (_rows_to_heads(m_next) if stage is not None else m_cur), l


def _attend_block(n_free, n_used, logits, mask, v_t, m_ref, l_ref, acc_ref, buf0, buf1):
    m0 = m_ref[...]
    buf1[...] = jnp.full(buf1.shape, -jnp.inf, F32)

    def pair(masked, p, carry):
        m_a, m_b, l = carry
        c0 = 2 * p
        c1 = c0 + 1
        cp = jnp.maximum(c0 - 1, 0)
        mk = (lambda c: mask(c)) if masked else (lambda c: None)
        m_c, l = _stage_and_consume((lambda h: logits(h, c0), mk(c0), buf0),
                                    (lambda h: v_t(h, cp), buf1), m_b, (m_a, m_b), l, acc_ref)
        m_d, l = _stage_and_consume((lambda h: logits(h, c1), mk(c1), buf1),
                                    (lambda h: v_t(h, c0), buf0), m_c, (m_b, m_c), l, acc_ref)
        return m_c, m_d, l

    carry = lax.fori_loop(0, n_free, functools.partial(pair, False), (m0, m0, l_ref[...]))
    m_c, m_d, l = lax.fori_loop(n_free, n_used, functools.partial(pair, True), carry)
    last = 2 * n_used - 1
    _, l = _stage_and_consume(None, (lambda h: v_t(h, last), buf1), m_d, (m_c, m_d), l, acc_ref)
    m_ref[...] = m_d
    l_ref[...] = l


def _softmax_init(m_ref, l_ref, acc_ref):
    m_ref[...] = jnp.full(m_ref.shape, NEG_INF, F32)
    l_ref[...] = jnp.zeros(l_ref.shape, F32)
    acc_ref[...] = jnp.zeros(acc_ref.shape, F32)


def _softmax_finish(o_ref, l_ref, acc_ref):
    inv = 1.0 / l_ref[...]
    parts = [acc_ref[h * HEAD_DIM:(h + 1) * HEAD_DIM, :] * inv[h:h + 1, :] for h in range(N_HEADS)]
    o_ref[0] = jnp.concatenate(parts, axis=0).T.astype(o_ref.dtype)


def _f32_to_sortable(x):
    b = lax.bitcast_convert_type(x, I32)
    return jnp.where(b >= 0, b, b ^ jnp.int32(0x7FFFFFFF)) ^ jnp.int32(INT32_MIN)


def _sortable_to_f32(u):
    o = u ^ jnp.int32(INT32_MIN)
    return lax.bitcast_convert_type(jnp.where(o >= 0, o, o ^ jnp.int32(0x7FFFFFFF)), F32)


def _dsa_kernel(qiT_ref, wiT_ref, qaT_ref, ki_ref, ka_ref, vaT_ref, o_ref,
                s_ref, gm_ref, thr_ref, cut_ref, qp_ref, m_ref, l_ref, acc_ref, lg0_ref, lg1_ref,
                *, past, length, n_sel, tq, tk, nk, idx_bits):
    i = pl.program_id(1)
    j = pl.program_id(2)
    q0 = past + i * tq
    qpos = q0 + lax.broadcasted_iota(I32, (1, tq), 1)
    vis = jnp.minimum(((qpos >> CHUNK_SHIFT) + 1) * CHUNK, length)
    vmax = jnp.minimum((((q0 + tq - 1) >> CHUNK_SHIFT) + 1) * CHUNK, length)
    nvis = (vmax + tk - 1) // tk
    vcap = ((vmax + KEY_PAIR - 1) // KEY_PAIR) * KEY_PAIR

    def rows_here(jb, step):
        return jnp.clip((vcap - jb * tk) // step, 0, tk // step)

    @pl.when(jnp.logical_and(j < nk, j < nvis))
    def _():
        def body(c, carry):
            for half in range(GROUPS // SCORE_ROWS):
                r0 = pl.multiple_of(c * GROUPS + half * SCORE_ROWS, SCORE_ROWS)
                kblk = ki_ref[0, pl.ds(r0, SCORE_ROWS), :]
                acc = jnp.zeros((SCORE_ROWS, tq), F32)
                for h in range(N_HEADS):
                    d = jnp.dot(kblk, qiT_ref[0, h * IDX_DIM:(h + 1) * IDX_DIM, :],
                                preferred_element_type=F32)
                    acc = acc + jnp.maximum(d, 0.0) * wiT_ref[0, h:h + 1, :]
                kpos = j * tk + r0 + lax.broadcasted_iota(I32, (SCORE_ROWS, 1), 0)
                acc = jnp.where(kpos < vis, acc, -jnp.inf)
                s_ref[pl.ds(pl.multiple_of(j * tk + r0, SCORE_ROWS), SCORE_ROWS), :] = acc
                grp = slice(half * SCORE_ROWS, (half + 1) * SCORE_ROWS)
                gm_ref[grp, :] = jnp.maximum(gm_ref[grp, :], acc)
            return carry

        @pl.when(j == 0)
        def _():
            gm_ref[...] = jnp.full(gm_ref.shape, -jnp.inf, F32)

        lax.fori_loop(0, rows_here(j, GROUPS), body, 0)

    @pl.when(j == nk - 1)
    def _():
        nch = vcap // COUNT_ROWS

        def count(pred):
            sub = 64

            def body(c, accs):
                accs = list(accs)
                for k in range(COUNT_ROWS // sub):
                    r0 = pl.multiple_of(c * COUNT_ROWS + k * sub, sub)
                    hit = jnp.where(pred(s_ref[pl.ds(r0, sub), :], r0), 1.0, 0.0)
                    accs[k % 4] = accs[k % 4] + jnp.sum(hit.reshape(sub // 8, 8, tq), axis=0)
                return tuple(accs)
            zero = jnp.zeros((8, tq), F32)
            a0, a1, a2, a3 = lax.fori_loop(0, nch, body, (zero, zero, zero, zero))
            return jnp.sum((a0 + a1) + (a2 + a3), axis=0, keepdims=True)

        def count_ge(t):
            return count(lambda blk, r0: blk >= t)

        kf = float(n_sel)
        no_cut = jnp.full((1, tq), 2 ** 30, I32)

        gm = gm_ref[...]
        u_lo = _f32_to_sortable(jnp.min(gm, axis=0, keepdims=True))
        u_hi = _f32_to_sortable(jnp.max(gm, axis=0, keepdims=True))
        top_bit = 31 - lax.clz(u_lo ^ u_hi)
        known = jnp.where(top_bit >= 31, jnp.int32(0),
                          lax.shift_left(jnp.int32(-1), jnp.minimum(top_bit + 1, 31)))
        state = (jnp.max(top_bit), u_hi & known, jnp.full((1, tq), -1.0, F32))

        def more(state):
            b, _, c_cur = state
            return jnp.logical_and(b >= 0, jnp.max(jnp.where(c_cur != kf, 1.0, 0.0)) > 0.0)

        def bit_step(state):
            b, cur, c_cur = state
            cand = cur | lax.shift_left(jnp.int32(1), b)
            c = count_ge(_sortable_to_f32(cand))
            take = jnp.logical_and(top_bit >= b, c >= kf)
            return b - 1, jnp.where(take, cand, cur), jnp.where(take, c, c_cur)

        _, cur, c_cur = lax.while_loop(more, bit_step, state)
        thr = jnp.where(vis < n_sel, -F32_MAX, _sortable_to_f32(cur))

        def with_ties():
            n_gt = count(lambda blk, r0: blk > thr)
            n_ge = count_ge(thr)
            need = kf - n_gt
            tied = (n_ge - n_gt) > need

            def tie_cut():
                def count_before(x):
                    def pred(blk, r0):
                        kpos = r0 + lax.broadcasted_iota(I32, (blk.shape[0], 1), 0)
                        return jnp.logical_and(blk == thr, kpos < x)
                    return count(pred)
                lo = jnp.zeros((1, tq), I32)
                for b in reversed(range(idx_bits)):
                    cand = lo + (1 << b)
                    lo = jnp.where(count_before(cand) < need, cand, lo)
                return jnp.where(tied, lo, no_cut)

            return lax.cond(jnp.max(jnp.where(tied, 1.0, 0.0)) > 0.0, tie_cut, lambda: no_cut)

        unsure = jnp.logical_or(c_cur != kf, vis < n_sel)
        cut = lax.cond(jnp.max(jnp.where(unsure, 1.0, 0.0)) > 0.0, with_ties, lambda: no_cut)
        thr_ref[0:1, :] = thr
        cut_ref[0:1, :] = cut
        for h in range(N_HEADS):
            qp_ref[h] = _pair_padded(qaT_ref, h)
        _softmax_init(m_ref, l_ref, acc_ref)

    jj = j - nk

    @pl.when(jnp.logical_and(j >= nk, jj < nvis))
    def _():
        thr = thr_ref[0:1, :]
        cut = cut_ref[0:1, :]

        def selected(c):
            g0 = pl.multiple_of(jj * tk + c * KEY_CHUNK, KEY_CHUNK)
            sblk = s_ref[pl.ds(g0, KEY_CHUNK), :]
            kpos = g0 + lax.broadcasted_iota(I32, (KEY_CHUNK, 1), 0)
            return jnp.logical_or(sblk > thr, jnp.logical_and(sblk == thr, kpos <= cut))

        def logits(h, c):
            r0 = pl.multiple_of(c * KEY_CHUNK, KEY_CHUNK)
            kblk = ka_ref[0, pl.ds(r0, KEY_CHUNK), (h // 2) * LANES:(h // 2 + 1) * LANES]
            return jnp.dot(kblk, qp_ref[h], preferred_element_type=F32)

        def v_t(h, c):
            return vaT_ref[0, c, h * HEAD_DIM:(h + 1) * HEAD_DIM, :]

        _attend_block(0, rows_here(jj, KEY_PAIR), logits, selected, v_t, m_ref, l_ref, acc_ref,
                      lg0_ref, lg1_ref)

    @pl.when(j == 2 * nk - 1)
    def _():
        _softmax_finish(o_ref, l_ref, acc_ref)


def _dsa_attention(qiT, wiT, qaT, ki, ka, vaTc, *, past, length, tq, tk):
    b, _, tqp = qiT.shape
    lp = ki.shape[1]
    nq, nk = tqp // tq, lp // tk
    n_sel = min(TOPK_MAX, length // 4)

    def nvis(i):
        vmax = jnp.minimum(((past + (i + 1) * tq - 1) // CHUNK + 1) * CHUNK, length)
        return (vmax + tk - 1) // tk

    k1 = lambda bb, i, j: (bb, jnp.minimum(j, nvis(i) - 1), 0)
    k3 = lambda bb, i, j: (bb, jnp.clip(j - nk, 0, nvis(i) - 1), 0)
    k3c = lambda bb, i, j: (bb, jnp.clip(j - nk, 0, nvis(i) - 1), 0, 0)
    qmap = lambda bb, i, j: (bb, 0, i)
    kern = functools.partial(_dsa_kernel, past=past, length=length, n_sel=n_sel, tq=tq, tk=tk,
                             nk=nk, idx_bits=int(lp).bit_length())
    return pl.pallas_call(
        kern,
        out_shape=jax.ShapeDtypeStruct((b, tqp, D_BRANCH), BF16),
        grid=(b, nq, 2 * nk),
        in_specs=[
            pl.BlockSpec((1, D_BRANCH, tq), qmap),
            pl.BlockSpec((1, N_HEADS, tq), qmap),
            pl.BlockSpec((1, D_BRANCH, tq), qmap),
            pl.BlockSpec((1, tk, IDX_DIM), k1),
            pl.BlockSpec((1, tk, D_BRANCH), k3),
            pl.BlockSpec((1, tk // KEY_CHUNK, D_BRANCH, KEY_CHUNK), k3c),
        ],
        out_specs=pl.BlockSpec((1, tq, D_BRANCH), lambda bb, i, j: (bb, i, 0)),
        scratch_shapes=[
            pltpu.VMEM((lp, tq), F32),
            pltpu.VMEM((GROUPS, tq), F32),
            pltpu.VMEM((8, tq), F32),
            pltpu.VMEM((8, tq), I32),
            pltpu.VMEM((N_HEADS, LANES, tq), BF16),
            pltpu.VMEM((N_HEADS, tq), F32),
            pltpu.VMEM((N_HEADS, tq), F32),
            pltpu.VMEM((D_BRANCH, tq), F32),
            pltpu.VMEM((N_HEADS, KEY_CHUNK, tq), F32),
            pltpu.VMEM((N_HEADS, KEY_CHUNK, tq), F32),
        ],
        compiler_params=_cparams(("arbitrary", "arbitrary", "arbitrary")),
        name="dsa_attention",
    )(qiT, wiT, qaT, ki, ka, vaTc)


_AUG = 6


def _fox_kernel(qbT_ref, qaug_ref, kb_ref, kaug_ref, vbT_ref, o_ref,
                qf_ref, m_ref, l_ref, acc_ref, lg0_ref, lg1_ref, *, past, tq, tk, nk):
    i = pl.program_id(1)
    j = pl.program_id(2)
    q0 = past + i * tq
    qpos = q0 + lax.broadcasted_iota(I32, (1, tq), 1)
    nvis = (q0 + tq + tk - 1) // tk
    npair = tk // KEY_PAIR
    n_free = jnp.clip((q0 + 1 - j * tk) // KEY_PAIR, 0, npair)
    n_used = jnp.clip((q0 + tq - j * tk + KEY_PAIR - 1) // KEY_PAIR, 0, npair)

    @pl.when(j == 0)
    def _():
        arow = lax.broadcasted_iota(I32, (LANES, 1), 0)
        qa = qaug_ref[0]
        for h in range(N_HEADS):
            mine = jnp.logical_and(arow >= _AUG * h, arow < _AUG * (h + 1))
            qf_ref[h, 0:LANES, :] = _pair_padded(qbT_ref, h)
            qf_ref[h, LANES:2 * LANES, :] = jnp.where(mine, qa, 0.0).astype(BF16)
        _softmax_init(m_ref, l_ref, acc_ref)

    @pl.when(j < nvis)
    def _():
        def causal(c):
            kpos = j * tk + c * KEY_CHUNK + lax.broadcasted_iota(I32, (KEY_CHUNK, 1), 0)
            return kpos <= qpos

        def logits(h, c):
            r0 = pl.multiple_of(c * KEY_CHUNK, KEY_CHUNK)
            kblk = kb_ref[0, pl.ds(r0, KEY_CHUNK), (h // 2) * LANES:(h // 2 + 1) * LANES]
            lhs = jnp.concatenate([kblk, kaug_ref[0, pl.ds(r0, KEY_CHUNK), :]], axis=1)
            return jnp.dot(lhs, qf_ref[h], preferred_element_type=F32)

        def v_t(h, c):
            return vbT_ref[0, c, h * HEAD_DIM:(h + 1) * HEAD_DIM, :]

        _attend_block(n_free, n_used, logits, causal, v_t, m_ref, l_ref, acc_ref,
                      lg0_ref, lg1_ref)

    @pl.when(j == nk - 1)
    def _():
        _softmax_finish(o_ref, l_ref, acc_ref)


def _fox_attention(qbT, qaug, kb, kaug, vbTc, *, past, tq, tk):
    b, _, tqp = qbT.shape
    lp = kb.shape[1]
    nq, nk = tqp // tq, lp // tk

    def last(i):
        return jnp.minimum((past + (i + 1) * tq + tk - 1) // tk, nk) - 1

    kmap = lambda bb, i, j: (bb, jnp.minimum(j, last(i)), 0)
    kmapc = lambda bb, i, j: (bb, jnp.minimum(j, last(i)), 0, 0)
    qmap = lambda bb, i, j: (bb, 0, i)
    return pl.pallas_call(
        functools.partial(_fox_kernel, past=past, tq=tq, tk=tk, nk=nk),
        out_shape=jax.ShapeDtypeStruct((b, tqp, D_BRANCH), BF16),
        grid=(b, nq, nk),
        in_specs=[
            pl.BlockSpec((1, D_BRANCH, tq), qmap),
            pl.BlockSpec((1, LANES, tq), qmap),
            pl.BlockSpec((1, tk, D_BRANCH), kmap),
            pl.BlockSpec((1, tk, LANES), kmap),
            pl.BlockSpec((1, tk // KEY_CHUNK, D_BRANCH, KEY_CHUNK), kmapc),
        ],
        out_specs=pl.BlockSpec((1, tq, D_BRANCH), lambda bb, i, j: (bb, i, 0)),
        scratch_shapes=[
            pltpu.VMEM((N_HEADS, 2 * LANES, tq), BF16),
            pltpu.VMEM((N_HEADS, tq), F32),
            pltpu.VMEM((N_HEADS, tq), F32),
            pltpu.VMEM((D_BRANCH, tq), F32),
            pltpu.VMEM((N_HEADS, KEY_CHUNK, tq), F32),
            pltpu.VMEM((N_HEADS, KEY_CHUNK, tq), F32),
        ],
        compiler_params=_cparams(("arbitrary", "arbitrary", "arbitrary")),
        name="fox_attention",
    )(qbT, qaug, kb, kaug, vbTc)


AUG_ROWS = 512


def _aug_kernel(c_ref, kaug_ref, qaugT_ref):
    x = c_ref[0] * LOG2E
    a = x.astype(BF16).astype(F32)
    r = x - a
    b = r.astype(BF16).astype(F32)
    c = (r - b).astype(BF16).astype(F32)
    lane = lax.broadcasted_iota(I32, (N_HEADS, LANES), 1)
    head = lax.broadcasted_iota(I32, (N_HEADS, LANES), 0)

    def place(v, slot):
        sel = jnp.where(lane == _AUG * head + slot, 1.0, 0.0)
        return jnp.dot(v, sel, preferred_element_type=F32, precision=lax.Precision.HIGHEST)

    slot = lane - _AUG * head
    ones_k = jnp.sum(jnp.where(jnp.logical_and(slot >= 0, slot < 3), 1.0, 0.0), axis=0, keepdims=True)
    ones_q = jnp.sum(jnp.where(jnp.logical_and(slot >= 3, slot < 6), 1.0, 0.0), axis=0, keepdims=True)
    kaug_ref[0] = (ones_k - (place(a, 3) + place(b, 4) + place(c, 5))).astype(BF16)
    qaugT_ref[0] = (ones_q + (place(a, 0) + place(b, 1) + place(c, 2))).T


def _fox_augment(cum):
    b, lp, h = cum.shape
    return pl.pallas_call(
        _aug_kernel,
        out_shape=(jax.ShapeDtypeStruct((b, lp, LANES), BF16),
                   jax.ShapeDtypeStruct((b, LANES, lp), F32)),
        grid=(b, lp // AUG_ROWS),
        in_specs=[pl.BlockSpec((1, AUG_ROWS, h), lambda i, j: (i, j, 0))],
        out_specs=(pl.BlockSpec((1, AUG_ROWS, LANES), lambda i, j: (i, j, 0)),
                   pl.BlockSpec((1, LANES, AUG_ROWS), lambda i, j: (i, 0, j))),
        compiler_params=_cparams(("arbitrary", "arbitrary")),
        name="fox_augment",
    )(cum)


def _shifted_rows(buf, load, store, u, past_ref, starts_batch, nb):
    tm, c = u.shape
    store(slice(8, 8 + tm), u)
    if nb == 1:
        @pl.when(starts_batch)
        def _():
            store(slice(6, 8), past_ref[0])
    s1 = load(slice(7, 7 + tm))
    s2 = load(slice(6, 6 + tm))
    if nb > 1:
        r = lax.broadcasted_iota(I32, (nb, tm // nb, 1), 1)
        p0 = past_ref[:, 0:1, :]
        p1 = past_ref[:, 1:2, :]
        s1 = jnp.where(r == 0, p1, _group(s1, nb)).reshape(tm, c)
        s2 = jnp.where(r == 0, p0, jnp.where(r == 1, p1, _group(s2, nb))).reshape(tm, c)
    store(slice(0, 8), load(slice(tm, tm + 8)))
    return s1, s2


def _merge_kernel(x_ref, ya_ref, yb_ref, cb_ref, u_ref, gl_ref, past_ref, cw_ref, wbr_ref,
                  wo_ref, g_ref, mod_ref, o_ref, ub_ref, *, nb, rows_per_batch):
    i = pl.program_id(0)
    tm = x_ref.shape[0]
    u = u_ref[...]

    def load(rows):
        return ub_ref[rows, :]

    def store(rows, v):
        ub_ref[rows, :] = v

    s1, s2 = _shifted_rows(ub_ref, load, store, u, past_ref, (i * tm) % rows_per_batch == 0, nb)
    conv = cw_ref[0:1, :] * s2 + cw_ref[1:2, :] * s1 + cw_ref[2:3, :] * u
    yc = (cb_ref[...] * conv).astype(BF16)
    mix = jnp.zeros((tm, D_MODEL), F32)
    for n, y in enumerate((ya_ref[...], yb_ref[...], yc)):
        br = jnp.dot(y, wbr_ref[n], preferred_element_type=F32)
        mix = mix + jax.nn.sigmoid(gl_ref[:, n * D_MODEL:(n + 1) * D_MODEL]) * br
    mo = jnp.dot(mix.astype(BF16), wo_ref[...], preferred_element_type=F32)
    nm = _group(_rms(mo, g_ref[1:2, :]), nb)
    o_ref[...] = x_ref[...] + (mod_ref[:, 2:3, :] * nm).reshape(tm, D_MODEL)


def _mod_spec(tm, nb, rows_per_batch, ngrid):
    if nb == 1:
        f = lambda i, *_: ((i * tm) // rows_per_batch, 0, 0)
    else:
        f = lambda i, *_: (i, 0, 0)
    return pl.BlockSpec((nb, 6, D_MODEL), f)


def _merge(x, ya, yb, cb, u, gl, past, cw, wbr, wo, g, mod, *, tm, rows_per_batch):
    rows = x.shape[0]
    nb = max(1, tm // rows_per_batch)
    row = lambda w: pl.BlockSpec((tm, w), lambda i: (i, 0))
    full = lambda shape: pl.BlockSpec(shape, lambda i: (0,) * len(shape))
    if nb == 1:
        past_spec = pl.BlockSpec((1, 2, D_BRANCH), lambda i: ((i * tm) // rows_per_batch, 0, 0))
    else:
        past_spec = pl.BlockSpec((nb, 2, D_BRANCH), lambda i: (i, 0, 0))
    return pl.pallas_call(
        functools.partial(_merge_kernel, nb=nb, rows_per_batch=rows_per_batch),
        out_shape=jax.ShapeDtypeStruct((rows, D_MODEL), F32),
        grid=(rows // tm,),
        in_specs=[
            row(D_MODEL), row(D_BRANCH), row(D_BRANCH), row(D_BRANCH), row(D_BRANCH),
            row(3 * D_MODEL), past_spec, full((CONV_W, D_BRANCH)),
            full((3, D_BRANCH, D_MODEL)), full((D_MODEL, D_MODEL)), full((4, D_MODEL)),
            _mod_spec(tm, nb, rows_per_batch, 1),
        ],
        out_specs=row(D_MODEL),
        scratch_shapes=[pltpu.VMEM((tm + 8, D_BRANCH), F32)],
        compiler_params=_cparams(("arbitrary",)),
        name="branch_merge",
    )(x, ya, yb, cb, u, gl, past, cw, wbr, wo, g, mod)


FF_CHUNK = 256


def _ffn_kernel(x_ref, g_ref, mod_ref, wg_ref, wv_ref, wd_ref, cw_ref, past_ref,
                o_ref, tail_ref, hs_ref, acc_ref, ub_ref, *, nb, rows_per_batch, nj):
    i = pl.program_id(0)
    j = pl.program_id(1)
    tm = x_ref.shape[0]

    @pl.when(j == 0)
    def _():
        y = _group(_rms(x_ref[...], g_ref[2:3, :]), nb)
        h = y * (1.0 + mod_ref[:, 4:5, :]) + mod_ref[:, 3:4, :]
        hs_ref[...] = h.reshape(tm, D_MODEL).astype(BF16)
        acc_ref[...] = jnp.zeros(acc_ref.shape, F32)

    hs = hs_ref[...]
    ug = jnp.dot(hs, wg_ref[...], preferred_element_type=F32)
    uv = jnp.dot(hs, wv_ref[...], preferred_element_type=F32)

    def load(rows):
        return ub_ref[j, rows, :]

    def store(rows, v):
        ub_ref[j, rows, :] = v

    s1, s2 = _shifted_rows(ub_ref, load, store, ug, past_ref, (i * tm) % rows_per_batch == 0, nb)
    conv = cw_ref[0:1, :] * s2 + cw_ref[1:2, :] * s1 + cw_ref[2:3, :] * ug
    f = conv * jax.nn.sigmoid(conv) * uv
    acc_ref[...] += jnp.dot(f.astype(BF16), wd_ref[...], preferred_element_type=F32)
    grp = tm // nb
    tail_ref[0] = _group(ug, nb)[:, grp - 8:, :]

    @pl.when(j == nj - 1)
    def _():
        nm = _group(_rms(acc_ref[...], g_ref[3:4, :]), nb)
        o_ref[...] = x_ref[...] + (mod_ref[:, 5:6, :] * nm).reshape(tm, D_MODEL)


def _conv_ffn(x, g, mod, w_up, w_down, cw, past, *, tm, rows_per_batch):
    rows = x.shape[0]
    dff = w_down.shape[0]
    nj = dff // FF_CHUNK
    nb = max(1, tm // rows_per_batch)
    nt = rows // tm
    if nb == 1:
        past_spec = pl.BlockSpec((1, 2, FF_CHUNK), lambda i, j: ((i * tm) // rows_per_batch, 0, j))
        mod_map = lambda i, j: ((i * tm) // rows_per_batch, 0, 0)
    else:
        past_spec = pl.BlockSpec((nb, 2, FF_CHUNK), lambda i, j: (i, 0, j))
        mod_map = lambda i, j: (i, 0, 0)
    out, tails = pl.pallas_call(
        functools.partial(_ffn_kernel, nb=nb, rows_per_batch=rows_per_batch, nj=nj),
        out_shape=(jax.ShapeDtypeStruct((rows, D_MODEL), F32),
                   jax.ShapeDtypeStruct((nt, nb, 8, dff), F32)),
        grid=(nt, nj),
        in_specs=[
            pl.BlockSpec((tm, D_MODEL), lambda i, j: (i, 0)),
            pl.BlockSpec((4, D_MODEL), lambda i, j: (0, 0)),
            pl.BlockSpec((nb, 6, D_MODEL), mod_map),
            pl.BlockSpec((D_MODEL, FF_CHUNK), lambda i, j: (0, j)),
            pl.BlockSpec((D_MODEL, FF_CHUNK), lambda i, j: (0, j + nj)),
            pl.BlockSpec((FF_CHUNK, D_MODEL), lambda i, j: (j, 0)),
            pl.BlockSpec((CONV_W, FF_CHUNK), lambda i, j: (0, j)),
            past_spec,
        ],
        out_specs=(pl.BlockSpec((tm, D_MODEL), lambda i, j: (i, 0)),
                   pl.BlockSpec((1, nb, 8, FF_CHUNK), lambda i, j: (i, 0, 0, j))),
        scratch_shapes=[
            pltpu.VMEM((tm, D_MODEL), BF16),
            pltpu.VMEM((tm, D_MODEL), F32),
            pltpu.VMEM((nj, tm + 8, FF_CHUNK), F32),
        ],
        compiler_params=_cparams(("arbitrary", "arbitrary")),
        name="conv_ffn",
    )(x, g, mod, w_up, w_up, w_down, cw, past)
    return out, tails


def _relayout_w_in(w_in, b_forget):
    db = D_BRANCH
    o_qi = 3 * db
    o_ki = o_qi + N_HEADS * IDX_DIM
    o_wi = o_ki + IDX_DIM
    o_qb = o_wi + N_HEADS
    o_fl = o_qb + 3 * db
    o_cb = o_fl + N_HEADS
    o_gl = o_cb + 3 * db
    main = jnp.concatenate([w_in[:, 0:o_ki], w_in[:, o_qb:o_fl], w_in[:, o_cb:]], axis=1)
    small = jnp.concatenate([w_in[:, o_ki:o_wi], w_in[:, o_wi:o_qb], w_in[:, o_fl:o_cb]], axis=1)
    small = jnp.pad(small, ((0, 0), (0, LANES - small.shape[1])))
    bf = jnp.zeros((1, LANES), F32).at[0, _FL_OFF:_FL_OFF + N_HEADS].set(b_forget)
    del o_gl
    return main.astype(BF16), small.astype(BF16), bf


def _round_up(x, m):
    return (x + m - 1) // m * m


def _layer(x, mod, caches, weights, *, batch, t, past, cfg):
    (g, w_main, w_small, bf_pad, cw_mix, wbr, wo, w_up, cw_ffn, w_down) = weights
    rows = batch * t
    length = past + t
    (qaT, ka32, ka16, va32, vaTc, qiT, qbT, kb32, kb16, vb32, vbTc, cb, u, gl,
     ki32, ki16, wiT, logf) = _in_projection(
        x, mod, g, w_main, w_small, bf_pad, tm=cfg["tm_in"], rows_per_batch=t)

    tq, tk = cfg["tq"], cfg["tk"]
    tqp = _round_up(t, tq)
    lp = _round_up(length, tk)

    def per_batch_T(aT):
        c = aT.shape[0]
        a = jnp.swapaxes(aT.reshape(c, batch, t), 0, 1)
        return jnp.pad(a, ((0, 0), (0, 0), (0, tqp - t)))

    def chunked_T(a):
        return jnp.swapaxes(a.reshape(batch, lp // KEY_CHUNK, KEY_CHUNK, a.shape[-1]), 2, 3)

    def with_cache(cache, new, dtype):
        new = new.reshape(batch, t, -1)
        if cache is not None:
            new = jnp.concatenate([cache.reshape(batch, past, -1).astype(dtype), new.astype(dtype)],
                                  axis=1)
        return jnp.pad(new.astype(dtype), ((0, 0), (0, lp - length), (0, 0)))

    if caches is None:
        c_idx = c_dk = c_dv = c_fk = c_fv = c_lf = None
        past_mix = jnp.zeros((batch, CONV_W - 1, D_BRANCH), F32)
        past_ffn = jnp.zeros((batch, CONV_W - 1, w_down.shape[0]), F32)
    else:
        c_idx, c_dk, c_dv, c_fk, c_fv, c_lf, past_mix, past_ffn = caches

    ki_all = with_cache(c_idx, ki16, BF16)
    ka_all = with_cache(c_dk, ka16, BF16)
    kb_all = with_cache(c_fk, kb16, BF16)
    if caches is None and lp == rows:
        vaT_all = vaTc.reshape(batch, lp // KEY_CHUNK, D_BRANCH, KEY_CHUNK)
        vbT_all = vbTc.reshape(batch, lp // KEY_CHUNK, D_BRANCH, KEY_CHUNK)
    else:
        vaT_all = chunked_T(with_cache(c_dv, va32, BF16))
        vbT_all = chunked_T(with_cache(c_fv, vb32, BF16))
    logf_all = with_cache(c_lf, logf, F32)

    ya = _dsa_attention(per_batch_T(qiT), per_batch_T(wiT), per_batch_T(qaT), ki_all, ka_all,
                        vaT_all, past=past, length=length, tq=tq, tk=tk)
    cum = _forget_cumsum(logf_all)
    kaug, qaug = _fox_augment(cum)
    qaug = jnp.pad(qaug[:, :, past:length], ((0, 0), (0, 0), (0, tqp - t)))
    yb = _fox_attention(per_batch_T(qbT), qaug, kb_all, kaug, vbT_all, past=past, tq=tq, tk=tk)
    ya = ya[:, :t].reshape(rows, D_BRANCH)
    yb = yb[:, :t].reshape(rows, D_BRANCH)

    x1 = _merge(x, ya, yb, cb, u, gl, past_mix, cw_mix, wbr, wo, g, mod,
                tm=cfg["tm_merge"], rows_per_batch=t)
    x2, tails = _conv_ffn(x1, g, mod, w_up, w_down, cw_ffn, past_ffn,
                          tm=cfg["tm_ffn"], rows_per_batch=t)

    new_mix = u.reshape(batch, t, D_BRANCH)[:, t - (CONV_W - 1):]
    nb = max(1, cfg["tm_ffn"] // t)
    if nb == 1:
        tiles_per_batch = t // cfg["tm_ffn"]
        last = tails.reshape(batch, tiles_per_batch, 8, -1)[:, -1]
    else:
        last = tails.reshape(batch, 8, -1)
    new_ffn = last[:, 8 - (CONV_W - 1):]
    hd = (batch, t, N_HEADS, HEAD_DIM)
    state = (ki32.reshape(batch, t, IDX_DIM), ka32.reshape(hd), va32.reshape(hd),
             kb32.reshape(hd), vb32.reshape(hd), logf.reshape(batch, t, N_HEADS), new_mix, new_ffn)
    return x2, state


_PROMPT_CFG = dict(tm_in=512, tm_merge=256, tm_ffn=512, tq=256, tk=4096)
_SAMPLE_CFG = dict(tm_in=256, tm_merge=256, tm_ffn=256, tq=128, tk=2560)


def kernel(x_prompt, x_sample, c_prompt, c_sample, cache_idx_k, cache_dsa_k, cache_dsa_v,
           cache_fox_k, cache_fox_v, cache_fox_logf, state_conv_mix, state_conv_ffn,
           w_ada, b_ada, norm_g, w_in, b_forget, conv_mix_w, w_branch, w_out, w_up,
           conv_ffn_w, w_down):
    bp, tp, _ = x_prompt.shape
    bs, ts, _ = x_sample.shape
    past = cache_dsa_k.shape[2]
    depth = w_ada.shape[0]

    c_all = jnp.concatenate([c_prompt, c_sample], axis=0)
    pad_rows = _round_up(c_all.shape[0], 8) - c_all.shape[0]
    c_all = jnp.pad(c_all, ((0, pad_rows), (0, 0)))
    mod_all = _modulation(c_all, w_ada, b_ada).reshape(depth, -1, 6, D_MODEL)

    yp = x_prompt.reshape(bp * tp, D_MODEL)
    ys = x_sample.reshape(bs * ts, D_MODEL)
    p_states, s_states = [], []
    for l in range(depth):
        w_main, w_small, bf_pad = _relayout_w_in(w_in[l], b_forget[l])
        weights = (norm_g[l], w_main, w_small, bf_pad, conv_mix_w[l], w_branch[l].astype(BF16),
                   w_out[l].astype(BF16), w_up[l].astype(BF16), conv_ffn_w[l],
                   w_down[l].astype(BF16))
        yp, st_p = _layer(yp, mod_all[l, :bp], None, weights, batch=bp, t=tp, past=0,
                          cfg=_PROMPT_CFG)
        caches = (cache_idx_k[l], cache_dsa_k[l], cache_dsa_v[l], cache_fox_k[l], cache_fox_v[l],
                  cache_fox_logf[l], state_conv_mix[l], state_conv_ffn[l])
        ys, st_s = _layer(ys, mod_all[l, bp:bp + bs], caches, weights, batch=bs, t=ts, past=past,
                          cfg=_SAMPLE_CFG)
        p_states.append(st_p)
        s_states.append(st_s)

    stack = lambda states: [jnp.stack([st[k] for st in states], axis=0) for k in range(8)]
    return (yp.reshape(bp, tp, D_MODEL), ys.reshape(bs, ts, D_MODEL), *stack(p_states),
            *stack(s_states))
```

```python
import functools

import jax
import jax.numpy as jnp
from jax import lax
from jax.experimental import pallas as pl
from jax.experimental.pallas import tpu as pltpu

F32 = jnp.float32
BF16 = jnp.bfloat16
I32 = jnp.int32

D_MODEL = 1024
HEAD_DIM = 64
D_BRANCH = 512
N_HEADS = 8
IDX_DIM = 64
CHUNK = 64
CHUNK_SHIFT = 6
TOPK_MAX = 256
CONV_W = 3
EPS = 1e-6
NEG_INF = -1e30
LOG2E = 1.4426950408889634
F32_MAX = 3.4028234663852886e38
INT32_MIN = -(2 ** 31)

LANES = 128
N_MAIN = 16 * D_BRANCH
KEY_CHUNK = 256
KEY_PAIR = 2 * KEY_CHUNK
SCORE_ROWS = 128
COUNT_ROWS = KEY_PAIR
GROUPS = TOPK_MAX
VMEM_LIMIT = 56 * 1024 * 1024


def _cparams(sem):
    return pltpu.CompilerParams(dimension_semantics=sem, vmem_limit_bytes=VMEM_LIMIT)


def _rms(x, g_row):
    return x * lax.rsqrt(jnp.mean(x * x, axis=-1, keepdims=True) + EPS) * g_row


def _group(x, nb):
    return x.reshape(nb, x.shape[0] // nb, x.shape[1])


def _mod_kernel(c_ref, w_ref, b_ref, o_ref):
    c = c_ref[...]
    s = (c * jax.nn.sigmoid(c)).astype(BF16)
    o_ref[0] = jnp.dot(s, w_ref[0].astype(BF16), preferred_element_type=F32) + b_ref[0]


def _modulation(c_all, w_ada, b_ada):
    depth = w_ada.shape[0]
    rows = c_all.shape[0]
    n = w_ada.shape[2]
    tn = D_MODEL
    return pl.pallas_call(
        _mod_kernel,
        out_shape=jax.ShapeDtypeStruct((depth, rows, n), F32),
        grid=(depth, n // tn),
        in_specs=[
            pl.BlockSpec((rows, D_MODEL), lambda l, j: (0, 0)),
            pl.BlockSpec((1, D_MODEL, tn), lambda l, j: (l, 0, j)),
            pl.BlockSpec((1, 1, tn), lambda l, j: (l, 0, j)),
        ],
        out_specs=pl.BlockSpec((1, rows, tn), lambda l, j: (l, 0, j)),
        compiler_params=_cparams(("arbitrary", "arbitrary")),
        name="adaln_mod",
    )(c_all, w_ada, b_ada.reshape(depth, 1, n))


IN_STEP = 2 * D_BRANCH
_GL_STEP = 5
_WI_OFF = IDX_DIM
_FL_OFF = IDX_DIM + N_HEADS


def _inproj_kernel(x_ref, mod_ref, g_ref, wm_ref, ws_ref, bf_ref,
                   qaT_ref, ka32_ref, ka16_ref, va32_ref, vaT_ref, qiT_ref, qbT_ref,
                   kb32_ref, kb16_ref, vb32_ref, vbT_ref, cb_ref, u_ref, gl_ref,
                   ki32_ref, ki16_ref, wiT_ref, logf_ref,
                   hs_ref, *, nb):
    j = pl.program_id(1)
    tm = x_ref.shape[0]

    @pl.when(j == 0)
    def _():
        y = _rms(x_ref[...], g_ref[0:1, :])
        h = _group(y, nb) * (1.0 + mod_ref[:, 1:2, :]) + mod_ref[:, 0:1, :]
        hb = h.reshape(tm, D_MODEL).astype(BF16)
        hs_ref[...] = hb
        sm = jnp.dot(hb, ws_ref[...], preferred_element_type=F32)
        ki32_ref[...] = sm[:, :IDX_DIM]
        ki16_ref[...] = sm[:, :IDX_DIM].astype(BF16)
        wiT_ref[...] = sm.T[_WI_OFF:_WI_OFF + N_HEADS, :]
        z = sm + bf_ref[...]
        lf = -(jnp.maximum(-z, 0.0) + jnp.log1p(jnp.exp(-jnp.abs(z))))
        logf_ref[...] = lf[:, _FL_OFF:_FL_OFF + N_HEADS]

    acc = jnp.dot(hs_ref[...], wm_ref[...], preferred_element_type=F32)
    lo = acc[:, :D_BRANCH]
    hi = acc[:, D_BRANCH:]
    scale = LOG2E * HEAD_DIM ** -0.5

    def store_chunked_T(ref, a):
        aT = a.T.astype(BF16)
        for c in range(tm // KEY_CHUNK):
            ref[c] = aT[:, c * KEY_CHUNK:(c + 1) * KEY_CHUNK]

    @pl.when(j == 0)
    def _():
        qaT_ref[...] = (lo * scale).T.astype(BF16)
        ka32_ref[...] = hi
        ka16_ref[...] = hi.astype(BF16)

    @pl.when(j == 1)
    def _():
        va32_ref[...] = lo
        store_chunked_T(vaT_ref, lo)
        qiT_ref[...] = hi.T.astype(BF16)

    @pl.when(j == 2)
    def _():
        qbT_ref[...] = (lo * scale).T.astype(BF16)
        kb32_ref[...] = hi
        kb16_ref[...] = hi.astype(BF16)

    @pl.when(j == 3)
    def _():
        vb32_ref[...] = lo
        store_chunked_T(vbT_ref, lo)
        cb_ref[...] = hi

    @pl.when(j == 4)
    def _():
        u_ref[...] = lo * hi

    @pl.when(j >= _GL_STEP)
    def _():
        gl_ref[...] = acc


def _in_projection(x, mod, g, w_main, w_small, bf_pad, *, tm, rows_per_batch):
    rows = x.shape[0]
    nt = rows // tm
    nb = max(1, tm // rows_per_batch)
    nstep = N_MAIN // IN_STEP
    if nb == 1:
        mod_map = lambda i, j: ((i * tm) // rows_per_batch, 0, 0)
    else:
        mod_map = lambda i, j: (i, 0, 0)
    row_blk = lambda w: pl.BlockSpec((tm, w), lambda i, j: (i, 0))
    col_blk = lambda h: pl.BlockSpec((h, tm), lambda i, j: (0, i))
    chunkT = pl.BlockSpec((tm // KEY_CHUNK, D_BRANCH, KEY_CHUNK), lambda i, j: (i, 0, 0))
    f32o = lambda w: jax.ShapeDtypeStruct((rows, w), F32)
    b16o = lambda w: jax.ShapeDtypeStruct((rows, w), BF16)
    b16T = jax.ShapeDtypeStruct((D_BRANCH, rows), BF16)
    b16c = jax.ShapeDtypeStruct((rows // KEY_CHUNK, D_BRANCH, KEY_CHUNK), BF16)
    out_shape = (
        b16T, f32o(D_BRANCH), b16o(D_BRANCH), f32o(D_BRANCH), b16c, b16T, b16T,
        f32o(D_BRANCH), b16o(D_BRANCH), f32o(D_BRANCH), b16c, f32o(D_BRANCH), f32o(D_BRANCH),
        f32o(6 * D_BRANCH),
        f32o(IDX_DIM), b16o(IDX_DIM), jax.ShapeDtypeStruct((N_HEADS, rows), F32), f32o(N_HEADS),
    )
    out_specs = (
        col_blk(D_BRANCH), row_blk(D_BRANCH), row_blk(D_BRANCH), row_blk(D_BRANCH), chunkT,
        col_blk(D_BRANCH), col_blk(D_BRANCH),
        row_blk(D_BRANCH), row_blk(D_BRANCH), row_blk(D_BRANCH), chunkT, row_blk(D_BRANCH),
        row_blk(D_BRANCH),
        pl.BlockSpec((tm, IN_STEP), lambda i, j: (i, jnp.clip(j - _GL_STEP, 0, 2))),
        row_blk(IDX_DIM), row_blk(IDX_DIM), col_blk(N_HEADS), row_blk(N_HEADS),
    )
    return pl.pallas_call(
        functools.partial(_inproj_kernel, nb=nb),
        out_shape=out_shape,
        grid=(nt, nstep),
        in_specs=[
            pl.BlockSpec((tm, D_MODEL), lambda i, j: (i, 0)),
            pl.BlockSpec((nb, 6, D_MODEL), mod_map),
            pl.BlockSpec((4, D_MODEL), lambda i, j: (0, 0)),
            pl.BlockSpec((D_MODEL, IN_STEP), lambda i, j: (0, j)),
            pl.BlockSpec((D_MODEL, LANES), lambda i, j: (0, 0)),
            pl.BlockSpec((1, LANES), lambda i, j: (0, 0)),
        ],
        out_specs=out_specs,
        scratch_shapes=[pltpu.VMEM((tm, D_MODEL), BF16)],
        compiler_params=_cparams(("arbitrary", "arbitrary")),
        name="in_projection",
    )(x, mod, g, w_main, w_small, bf_pad)


def _cumsum_kernel(x_ref, o_ref):
    x = x_ref[0]
    n = x.shape[0]
    lane = lax.broadcasted_iota(I32, x.shape, 1)
    row = lax.broadcasted_iota(I32, x.shape, 0)
    s = N_HEADS
    while s < LANES:
        x = x + jnp.where(lane >= s, pltpu.roll(x, s, axis=1), 0.0)
        s *= 2
    t = jnp.where(lane >= LANES - N_HEADS, x, 0.0)
    s = N_HEADS
    while s < LANES:
        t = t + pltpu.roll(t, s, axis=1)
        s *= 2
    t = jnp.where(row >= 1, pltpu.roll(t, 1, axis=0), 0.0)
    s = 1
    while s < n:
        if s < 8:
            sh = jnp.where(row >= s, pltpu.roll(t, s, axis=0), 0.0)
        else:
            sh = jnp.concatenate([jnp.zeros((s, LANES), F32), t[:n - s]], axis=0)
        t = t + sh
        s *= 2
    o_ref[0] = x + t


def _forget_cumsum(logf_all):
    b, lp, h = logf_all.shape
    n = lp * h // LANES
    out = pl.pallas_call(
        _cumsum_kernel,
        out_shape=jax.ShapeDtypeStruct((b, n, LANES), F32),
        grid=(b,),
        in_specs=[pl.BlockSpec((1, n, LANES), lambda i: (i, 0, 0))],
        out_specs=pl.BlockSpec((1, n, LANES), lambda i: (i, 0, 0)),
        compiler_params=_cparams(("arbitrary",)),
        name="forget_cumsum",
    )(logf_all.reshape(b, n, LANES))
    return out.reshape(b, lp, h)


def _pair_padded(qT_ref, h):
    pr, half = divmod(h, 2)
    blk = qT_ref[0, pr * LANES + half * HEAD_DIM:pr * LANES + (half + 1) * HEAD_DIM, :]
    z = jnp.zeros_like(blk)
    return jnp.concatenate([blk, z] if half == 0 else [z, blk], axis=0)


def _rows_to_heads(rows):
    tq = rows[0].shape[1]
    sub = lax.broadcasted_iota(I32, (N_HEADS, tq), 0)
    out = jnp.broadcast_to(rows[0], (N_HEADS, tq))
    for h in range(1, N_HEADS):
        out = jnp.where(sub == h, rows[h], out)
    return out


def _stage_and_consume(stage, consume, m_cur, pend, l, acc_ref):
    m_next, sums, alphas = [], [], []
    for h in range(N_HEADS):
        if stage is not None:
            logits, mask, buf = stage
            lg = logits(h)
            if mask is not None:
                lg = jnp.where(mask, lg, NEG_INF)
            buf[h] = lg
            m_next.append(jnp.maximum(m_cur[h:h + 1, :], jnp.max(lg, axis=0, keepdims=True)))
        if consume is not None:
            v_t, buf = consume
            before, after = pend
            alpha = jnp.exp2(before[h:h + 1, :] - after[h:h + 1, :])
            p = jnp.exp2(buf[h] - after[h:h + 1, :]).astype(BF16)
            ones = jnp.ones((16, p.shape[0]), BF16)
            pv = jnp.dot(jnp.concatenate([v_t(h), ones], axis=0), p, preferred_element_type=F32)
            rows = slice(h * HEAD_DIM, (h + 1) * HEAD_DIM)
            acc_ref[rows, :] = alpha * acc_ref[rows, :] + pv[:HEAD_DIM, :]
            sums.append(pv[HEAD_DIM:HEAD_DIM + 1, :])
            alphas.append(alpha)
    if consume is not None:
        l = _rows_to_heads(alphas) * l + _rows_to_heads(sums)
    return (_rows_to_heads(m_next) if stage is not None else m_cur), l


def _attend_block(n_free, n_used, logits, mask, v_t, m_ref, l_ref, acc_ref, buf0, buf1):
    m0 = m_ref[...]
    buf1[...] = jnp.full(buf1.shape, -jnp.inf, F32)

    def pair(masked, p, carry):
        m_a, m_b, l = carry
        c0 = 2 * p
        c1 = c0 + 1
        cp = jnp.maximum(c0 - 1, 0)
        mk = (lambda c: mask(c)) if masked else (lambda c: None)
        m_c, l = _stage_and_consume((lambda h: logits(h, c0), mk(c0), buf0),
                                    (lambda h: v_t(h, cp), buf1), m_b, (m_a, m_b), l, acc_ref)
        m_d, l = _stage_and_consume((lambda h: logits(h, c1), mk(c1), buf1),
                                    (lambda h: v_t(h, c0), buf0), m_c, (m_b, m_c), l, acc_ref)
        return m_c, m_d, l

    carry = lax.fori_loop(0, n_free, functools.partial(pair, False), (m0, m0, l_ref[...]))
    m_c, m_d, l = lax.fori_loop(n_free, n_used, functools.partial(pair, True), carry)
    last = 2 * n_used - 1
    _, l = _stage_and_consume(None, (lambda h: v_t(h, last), buf1), m_d, (m_c, m_d), l, acc_ref)
    m_ref[...] = m_d
    l_ref[...] = l


def _softmax_init(m_ref, l_ref, acc_ref):
    m_ref[...] = jnp.full(m_ref.shape, NEG_INF, F32)
    l_ref[...] = jnp.zeros(l_ref.shape, F32)
    acc_ref[...] = jnp.zeros(acc_ref.shape, F32)


def _softmax_finish(o_ref, l_ref, acc_ref):
    inv = 1.0 / l_ref[...]
    parts = [acc_ref[h * HEAD_DIM:(h + 1) * HEAD_DIM, :] * inv[h:h + 1, :] for h in range(N_HEADS)]
    o_ref[0] = jnp.concatenate(parts, axis=0).T.astype(o_ref.dtype)


def _f32_to_sortable(x):
    b = lax.bitcast_convert_type(x, I32)
    return jnp.where(b >= 0, b, b ^ jnp.int32(0x7FFFFFFF)) ^ jnp.int32(INT32_MIN)


def _sortable_to_f32(u):
    o = u ^ jnp.int32(INT32_MIN)
    return lax.bitcast_convert_type(jnp.where(o >= 0, o, o ^ jnp.int32(0x7FFFFFFF)), F32)


def _dsa_kernel(qiT_ref, wiT_ref, qaT_ref, ki_ref, ka_ref, vaT_ref, o_ref,
                s_ref, gm_ref, thr_ref, cut_ref, qp_ref, m_ref, l_ref, acc_ref, lg0_ref, lg1_ref,
                *, past, length, n_sel, tq, tk, nk):
    i = pl.program_id(1)
    j = pl.program_id(2)
    q0 = past + i * tq
    qpos = q0 + lax.broadcasted_iota(I32, (1, tq), 1)
    vis = jnp.minimum(((qpos >> CHUNK_SHIFT) + 1) * CHUNK, length)
    vmax = jnp.minimum((((q0 + tq - 1) >> CHUNK_SHIFT) + 1) * CHUNK, length)
    nvis = (vmax + tk - 1) // tk
    vcap = ((vmax + KEY_PAIR - 1) // KEY_PAIR) * KEY_PAIR

    def rows_here(jb, step):
        return jnp.clip((vcap - jb * tk) // step, 0, tk // step)

    @pl.when(jnp.logical_and(j < nk, j < nvis))
    def _():
        def body(c, carry):
            for half in range(GROUPS // SCORE_ROWS):
                r0 = pl.multiple_of(c * GROUPS + half * SCORE_ROWS, SCORE_ROWS)
                kblk = ki_ref[0, pl.ds(r0, SCORE_ROWS), :]
                acc = jnp.zeros((SCORE_ROWS, tq), F32)
                for h in range(N_HEADS):
                    d = jnp.dot(kblk, qiT_ref[0, h * IDX_DIM:(h + 1) * IDX_DIM, :],
                                preferred_element_type=F32)
                    acc = acc + jnp.maximum(d, 0.0) * wiT_ref[0, h:h + 1, :]
                kpos = j * tk + r0 + lax.broadcasted_iota(I32, (SCORE_ROWS, 1), 0)
                acc = jnp.where(kpos < vis, acc, -jnp.inf)
                s_ref[pl.ds(pl.multiple_of(j * tk + r0, SCORE_ROWS), SCORE_ROWS), :] = acc
                grp = slice(half * SCORE_ROWS, (half + 1) * SCORE_ROWS)
                gm_ref[grp, :] = jnp.maximum(gm_ref[grp, :], acc)
            return carry

        @pl.when(j == 0)
        def _():
            gm_ref[...] = jnp.full(gm_ref.shape, -jnp.inf, F32)

        lax.fori_loop(0, rows_here(j, GROUPS), body, 0)

    @pl.when(j == nk - 1)
    def _():
        nch = vcap // COUNT_ROWS

        def count(pred):
            sub = 64

            def body(c, accs):
                accs = list(accs)
                for k in range(COUNT_ROWS // sub):
                    r0 = pl.multiple_of(c * COUNT_ROWS + k * sub, sub)
                    hit = jnp.where(pred(s_ref[pl.ds(r0, sub), :], r0), 1.0, 0.0)
                    accs[k % 4] = accs[k % 4] + jnp.sum(hit.reshape(sub // 8, 8, tq), axis=0)
                return tuple(accs)
            zero = jnp.zeros((8, tq), F32)
            a0, a1, a2, a3 = lax.fori_loop(0, nch, body, (zero, zero, zero, zero))
            return jnp.sum((a0 + a1) + (a2 + a3), axis=0, keepdims=True)

        def count_ge(t):
            return count(lambda blk, r0: blk >= t)

        kf = float(n_sel)
        no_cut = jnp.full((1, tq), 2 ** 30, I32)

        gm = gm_ref[...]
        u_lo = _f32_to_sortable(jnp.min(gm, axis=0, keepdims=True))
        u_hi = _f32_to_sortable(jnp.max(gm, axis=0, keepdims=True))
        top_bit = 31 - lax.clz(u_lo ^ u_hi)
        known = jnp.where(top_bit >= 31, jnp.int32(0),
                          lax.shift_left(jnp.int32(-1), jnp.minimum(top_bit + 1, 31)))
        state = (jnp.max(top_bit), u_hi & known, jnp.full((1, tq), -1.0, F32))

        def more(state):
            b, _, c_cur = state
            return jnp.logical_and(b >= 0, jnp.max(jnp.where(c_cur != kf, 1.0, 0.0)) > 0.0)

        def bit_step(state):
            b, cur, c_cur = state
            cand = cur | lax.shift_left(jnp.int32(1), b)
            c = count_ge(_sortable_to_f32(cand))
            take = jnp.logical_and(top_bit >= b, c >= kf)
            return b - 1, jnp.where(take, cand, cur), jnp.where(take, c, c_cur)

        _, cur, c_cur = lax.while_loop(more, bit_step, state)
        thr = jnp.where(vis < n_sel, -F32_MAX, _sortable_to_f32(cur))

        def with_ties():
            n_gt = count(lambda blk, r0: blk > thr)
            n_ge = count_ge(thr)
            need = kf - n_gt
            tied = (n_ge - n_gt) > need

            def tie_cut():
                rr = lax.broadcasted_iota(I32, (KEY_CHUNK, KEY_CHUNK), 0)
                cc = lax.broadcasted_iota(I32, (KEY_CHUNK, KEY_CHUNK), 1)
                lower = jnp.where(cc <= rr, 1.0, 0.0).astype(BF16)

                def body(c, carry):
                    run, last = carry
                    r0 = pl.multiple_of(c * KEY_CHUNK, KEY_CHUNK)
                    eq = s_ref[pl.ds(r0, KEY_CHUNK), :] == thr
                    ordinal = run + jnp.dot(lower, jnp.where(eq, 1.0, 0.0).astype(BF16),
                                            preferred_element_type=F32)
                    kpos = (r0 + lax.broadcasted_iota(I32, (KEY_CHUNK, 1), 0)).astype(F32)
                    take = jnp.logical_and(eq, ordinal <= need)
                    last = jnp.maximum(last, jnp.max(jnp.where(take, kpos, -1.0), axis=0,
                                                     keepdims=True))
                    return ordinal[KEY_CHUNK - 1:KEY_CHUNK, :], last

                init = (jnp.zeros((1, tq), F32), jnp.full((1, tq), -1.0, F32))
                _, last = lax.fori_loop(0, vcap // KEY_CHUNK, body, init)
                return jnp.where(tied, last.astype(I32), no_cut)

            return lax.cond(jnp.max(jnp.where(tied, 1.0, 0.0)) > 0.0, tie_cut, lambda: no_cut)

        unsure = jnp.logical_or(c_cur != kf, vis < n_sel)
        cut = lax.cond(jnp.max(jnp.where(unsure, 1.0, 0.0)) > 0.0, with_ties, lambda: no_cut)
        thr_ref[0:1, :] = thr
        cut_ref[0:1, :] = cut
        for h in range(N_HEADS):
            qp_ref[h] = _pair_padded(qaT_ref, h)
        _softmax_init(m_ref, l_ref, acc_ref)

    jj = j - nk

    @pl.when(jnp.logical_and(j >= nk, jj < nvis))
    def _():
        thr = thr_ref[0:1, :]
        cut = cut_ref[0:1, :]

        def selected(c):
            g0 = pl.multiple_of(jj * tk + c * KEY_CHUNK, KEY_CHUNK)
            sblk = s_ref[pl.ds(g0, KEY_CHUNK), :]
            kpos = g0 + lax.broadcasted_iota(I32, (KEY_CHUNK, 1), 0)
            return jnp.logical_or(sblk > thr, jnp.logical_and(sblk == thr, kpos <= cut))

        def logits(h, c):
            r0 = pl.multiple_of(c * KEY_CHUNK, KEY_CHUNK)
            kblk = ka_ref[0, pl.ds(r0, KEY_CHUNK), (h // 2) * LANES:(h // 2 + 1) * LANES]
            return jnp.dot(kblk, qp_ref[h], preferred_element_type=F32)

        def v_t(h, c):
            return vaT_ref[0, c, h * HEAD_DIM:(h + 1) * HEAD_DIM, :]

        _attend_block(0, rows_here(jj, KEY_PAIR), logits, selected, v_t, m_ref, l_ref, acc_ref,
                      lg0_ref, lg1_ref)

    @pl.when(j == 2 * nk - 1)
    def _():
        _softmax_finish(o_ref, l_ref, acc_ref)


def _dsa_attention(qiT, wiT, qaT, ki, ka, vaTc, *, past, length, tq, tk):
    b, _, tqp = qiT.shape
    lp = ki.shape[1]
    nq, nk = tqp // tq, lp // tk
    n_sel = min(TOPK_MAX, length // 4)

    def nvis(i):
        vmax = jnp.minimum(((past + (i + 1) * tq - 1) // CHUNK + 1) * CHUNK, length)
        return (vmax + tk - 1) // tk

    k1 = lambda bb, i, j: (bb, jnp.minimum(j, nvis(i) - 1), 0)
    k3 = lambda bb, i, j: (bb, jnp.clip(j - nk, 0, nvis(i) - 1), 0)
    k3c = lambda bb, i, j: (bb, jnp.clip(j - nk, 0, nvis(i) - 1), 0, 0)
    qmap = lambda bb, i, j: (bb, 0, i)
    kern = functools.partial(_dsa_kernel, past=past, length=length, n_sel=n_sel, tq=tq, tk=tk,
                             nk=nk)
    return pl.pallas_call(
        kern,
        out_shape=jax.ShapeDtypeStruct((b, tqp, D_BRANCH), BF16),
        grid=(b, nq, 2 * nk),
        in_specs=[
            pl.BlockSpec((1, D_BRANCH, tq), qmap),
            pl.BlockSpec((1, N_HEADS, tq), qmap),
            pl.BlockSpec((1, D_BRANCH, tq), qmap),
            pl.BlockSpec((1, tk, IDX_DIM), k1),
            pl.BlockSpec((1, tk, D_BRANCH), k3),
            pl.BlockSpec((1, tk // KEY_CHUNK, D_BRANCH, KEY_CHUNK), k3c),
        ],
        out_specs=pl.BlockSpec((1, tq, D_BRANCH), lambda bb, i, j: (bb, i, 0)),
        scratch_shapes=[
            pltpu.VMEM((lp, tq), F32),
            pltpu.VMEM((GROUPS, tq), F32),
            pltpu.VMEM((8, tq), F32),
            pltpu.VMEM((8, tq), I32),
            pltpu.VMEM((N_HEADS, LANES, tq), BF16),
            pltpu.VMEM((N_HEADS, tq), F32),
            pltpu.VMEM((N_HEADS, tq), F32),
            pltpu.VMEM((D_BRANCH, tq), F32),
            pltpu.VMEM((N_HEADS, KEY_CHUNK, tq), F32),
            pltpu.VMEM((N_HEADS, KEY_CHUNK, tq), F32),
        ],
        compiler_params=_cparams(("arbitrary", "arbitrary", "arbitrary")),
        name="dsa_attention",
    )(qiT, wiT, qaT, ki, ka, vaTc)


_AUG = 6


def _fox_kernel(qbT_ref, qaug_ref, kb_ref, kaug_ref, vbT_ref, o_ref,
                qf_ref, m_ref, l_ref, acc_ref, lg0_ref, lg1_ref, *, past, tq, tk, nk):
    i = pl.program_id(1)
    j = pl.program_id(2)
    q0 = past + i * tq
    qpos = q0 + lax.broadcasted_iota(I32, (1, tq), 1)
    nvis = (q0 + tq + tk - 1) // tk
    npair = tk // KEY_PAIR
    n_free = jnp.clip((q0 + 1 - j * tk) // KEY_PAIR, 0, npair)
    n_used = jnp.clip((q0 + tq - j * tk + KEY_PAIR - 1) // KEY_PAIR, 0, npair)

    @pl.when(j == 0)
    def _():
        arow = lax.broadcasted_iota(I32, (LANES, 1), 0)
        qa = qaug_ref[0]
        for h in range(N_HEADS):
            mine = jnp.logical_and(arow >= _AUG * h, arow < _AUG * (h + 1))
            qf_ref[h, 0:LANES, :] = _pair_padded(qbT_ref, h)
            qf_ref[h, LANES:2 * LANES, :] = jnp.where(mine, qa, 0.0).astype(BF16)
        _softmax_init(m_ref, l_ref, acc_ref)

    @pl.when(j < nvis)
    def _():
        def causal(c):
            kpos = j * tk + c * KEY_CHUNK + lax.broadcasted_iota(I32, (KEY_CHUNK, 1), 0)
            return kpos <= qpos

        def logits(h, c):
            r0 = pl.multiple_of(c * KEY_CHUNK, KEY_CHUNK)
            kblk = kb_ref[0, pl.ds(r0, KEY_CHUNK), (h // 2) * LANES:(h // 2 + 1) * LANES]
            lhs = jnp.concatenate([kblk, kaug_ref[0, pl.ds(r0, KEY_CHUNK), :]], axis=1)
            return jnp.dot(lhs, qf_ref[h], preferred_element_type=F32)

        def v_t(h, c):
            return vbT_ref[0, c, h * HEAD_DIM:(h + 1) * HEAD_DIM, :]

        _attend_block(n_free, n_used, logits, causal, v_t, m_ref, l_ref, acc_ref,
                      lg0_ref, lg1_ref)

    @pl.when(j == nk - 1)
    def _():
        _softmax_finish(o_ref, l_ref, acc_ref)


def _fox_attention(qbT, qaug, kb, kaug, vbTc, *, past, tq, tk):
    b, _, tqp = qbT.shape
    lp = kb.shape[1]
    nq, nk = tqp // tq, lp // tk

    def last(i):
        return jnp.minimum((past + (i + 1) * tq + tk - 1) // tk, nk) - 1

    kmap = lambda bb, i, j: (bb, jnp.minimum(j, last(i)), 0)
    kmapc = lambda bb, i, j: (bb, jnp.minimum(j, last(i)), 0, 0)
    qmap = lambda bb, i, j: (bb, 0, i)
    return pl.pallas_call(
        functools.partial(_fox_kernel, past=past, tq=tq, tk=tk, nk=nk),
        out_shape=jax.ShapeDtypeStruct((b, tqp, D_BRANCH), BF16),
        grid=(b, nq, nk),
        in_specs=[
            pl.BlockSpec((1, D_BRANCH, tq), qmap),
            pl.BlockSpec((1, LANES, tq), qmap),
            pl.BlockSpec((1, tk, D_BRANCH), kmap),
            pl.BlockSpec((1, tk, LANES), kmap),
            pl.BlockSpec((1, tk // KEY_CHUNK, D_BRANCH, KEY_CHUNK), kmapc),
        ],
        out_specs=pl.BlockSpec((1, tq, D_BRANCH), lambda bb, i, j: (bb, i, 0)),
        scratch_shapes=[
            pltpu.VMEM((N_HEADS, 2 * LANES, tq), BF16),
            pltpu.VMEM((N_HEADS, tq), F32),
            pltpu.VMEM((N_HEADS, tq), F32),
            pltpu.VMEM((D_BRANCH, tq), F32),
            pltpu.VMEM((N_HEADS, KEY_CHUNK, tq), F32),
            pltpu.VMEM((N_HEADS, KEY_CHUNK, tq), F32),
        ],
        compiler_params=_cparams(("arbitrary", "arbitrary", "arbitrary")),
        name="fox_attention",
    )(qbT, qaug, kb, kaug, vbTc)


AUG_ROWS = 512


def _aug_kernel(c_ref, kaug_ref, qaugT_ref):
    x = c_ref[0] * LOG2E
    a = x.astype(BF16).astype(F32)
    r = x - a
    b = r.astype(BF16).astype(F32)
    c = (r - b).astype(BF16).astype(F32)
    lane = lax.broadcasted_iota(I32, (N_HEADS, LANES), 1)
    head = lax.broadcasted_iota(I32, (N_HEADS, LANES), 0)

    def place(v, slot):
        sel = jnp.where(lane == _AUG * head + slot, 1.0, 0.0)
        return jnp.dot(v, sel, preferred_element_type=F32, precision=lax.Precision.HIGHEST)

    slot = lane - _AUG * head
    ones_k = jnp.sum(jnp.where(jnp.logical_and(slot >= 0, slot < 3), 1.0, 0.0), axis=0, keepdims=True)
    ones_q = jnp.sum(jnp.where(jnp.logical_and(slot >= 3, slot < 6), 1.0, 0.0), axis=0, keepdims=True)
    kaug_ref[0] = (ones_k - (place(a, 3) + place(b, 4) + place(c, 5))).astype(BF16)
    qaugT_ref[0] = (ones_q + (place(a, 0) + place(b, 1) + place(c, 2))).T


def _fox_augment(cum):
    b, lp, h = cum.shape
    return pl.pallas_call(
        _aug_kernel,
        out_shape=(jax.ShapeDtypeStruct((b, lp, LANES), BF16),
                   jax.ShapeDtypeStruct((b, LANES, lp), F32)),
        grid=(b, lp // AUG_ROWS),
        in_specs=[pl.BlockSpec((1, AUG_ROWS, h), lambda i, j: (i, j, 0))],
        out_specs=(pl.BlockSpec((1, AUG_ROWS, LANES), lambda i, j: (i, j, 0)),
                   pl.BlockSpec((1, LANES, AUG_ROWS), lambda i, j: (i, 0, j))),
        compiler_params=_cparams(("arbitrary", "arbitrary")),
        name="fox_augment",
    )(cum)


def _shifted_rows(buf, load, store, u, past_ref, starts_batch, nb):
    tm, c = u.shape
    store(slice(8, 8 + tm), u)
    if nb == 1:
        @pl.when(starts_batch)
        def _():
            store(slice(6, 8), past_ref[0])
    s1 = load(slice(7, 7 + tm))
    s2 = load(slice(6, 6 + tm))
    if nb > 1:
        r = lax.broadcasted_iota(I32, (nb, tm // nb, 1), 1)
        p0 = past_ref[:, 0:1, :]
        p1 = past_ref[:, 1:2, :]
        s1 = jnp.where(r == 0, p1, _group(s1, nb)).reshape(tm, c)
        s2 = jnp.where(r == 0, p0, jnp.where(r == 1, p1, _group(s2, nb))).reshape(tm, c)
    store(slice(0, 8), load(slice(tm, tm + 8)))
    return s1, s2


def _merge_kernel(x_ref, ya_ref, yb_ref, cb_ref, u_ref, gl_ref, past_ref, cw_ref, wbr_ref,
                  wo_ref, g_ref, mod_ref, o_ref, ub_ref, *, nb, rows_per_batch):
    i = pl.program_id(0)
    tm = x_ref.shape[0]
    u = u_ref[...]

    def load(rows):
        return ub_ref[rows, :]

    def store(rows, v):
        ub_ref[rows, :] = v

    s1, s2 = _shifted_rows(ub_ref, load, store, u, past_ref, (i * tm) % rows_per_batch == 0, nb)
    conv = cw_ref[0:1, :] * s2 + cw_ref[1:2, :] * s1 + cw_ref[2:3, :] * u
    yc = (cb_ref[...] * conv).astype(BF16)
    mix = jnp.zeros((tm, D_MODEL), F32)
    for n, y in enumerate((ya_ref[...], yb_ref[...], yc)):
        br = jnp.dot(y, wbr_ref[n], preferred_element_type=F32)
        mix = mix + jax.nn.sigmoid(gl_ref[:, n * D_MODEL:(n + 1) * D_MODEL]) * br
    mo = jnp.dot(mix.astype(BF16), wo_ref[...], preferred_element_type=F32)
    nm = _group(_rms(mo, g_ref[1:2, :]), nb)
    o_ref[...] = x_ref[...] + (mod_ref[:, 2:3, :] * nm).reshape(tm, D_MODEL)


def _mod_spec(tm, nb, rows_per_batch, ngrid):
    if nb == 1:
        f = lambda i, *_: ((i * tm) // rows_per_batch, 0, 0)
    else:
        f = lambda i, *_: (i, 0, 0)
    return pl.BlockSpec((nb, 6, D_MODEL), f)


def _merge(x, ya, yb, cb, u, gl, past, cw, wbr, wo, g, mod, *, tm, rows_per_batch):
    rows = x.shape[0]
    nb = max(1, tm // rows_per_batch)
    row = lambda w: pl.BlockSpec((tm, w), lambda i: (i, 0))
    full = lambda shape: pl.BlockSpec(shape, lambda i: (0,) * len(shape))
    if nb == 1:
        past_spec = pl.BlockSpec((1, 2, D_BRANCH), lambda i: ((i * tm) // rows_per_batch, 0, 0))
    else:
        past_spec = pl.BlockSpec((nb, 2, D_BRANCH), lambda i: (i, 0, 0))
    return pl.pallas_call(
        functools.partial(_merge_kernel, nb=nb, rows_per_batch=rows_per_batch),
        out_shape=jax.ShapeDtypeStruct((rows, D_MODEL), F32),
        grid=(rows // tm,),
        in_specs=[
            row(D_MODEL), row(D_BRANCH), row(D_BRANCH), row(D_BRANCH), row(D_BRANCH),
            row(3 * D_MODEL), past_spec, full((CONV_W, D_BRANCH)),
            full((3, D_BRANCH, D_MODEL)), full((D_MODEL, D_MODEL)), full((4, D_MODEL)),
            _mod_spec(tm, nb, rows_per_batch, 1),
        ],
        out_specs=row(D_MODEL),
        scratch_shapes=[pltpu.VMEM((tm + 8, D_BRANCH), F32)],
        compiler_params=_cparams(("arbitrary",)),
        name="branch_merge",
    )(x, ya, yb, cb, u, gl, past, cw, wbr, wo, g, mod)


FF_CHUNK = 1408


def _ffn_kernel(x_ref, g_ref, mod_ref, wg_ref, wv_ref, wd_ref, cw_ref, past_ref,
                o_ref, tail_ref, hs_ref, acc_ref, ub_ref, *, nb, rows_per_batch, nj):
    i = pl.program_id(0)
    j = pl.program_id(1)
    tm = x_ref.shape[0]

    @pl.when(j == 0)
    def _():
        y = _group(_rms(x_ref[...], g_ref[2:3, :]), nb)
        h = y * (1.0 + mod_ref[:, 4:5, :]) + mod_ref[:, 3:4, :]
        hs_ref[...] = h.reshape(tm, D_MODEL).astype(BF16)
        acc_ref[...] = jnp.zeros(acc_ref.shape, F32)

    hs = hs_ref[...]
    ug = jnp.dot(hs, wg_ref[...], preferred_element_type=F32)
    uv = jnp.dot(hs, wv_ref[...], preferred_element_type=F32)

    def load(rows):
        return ub_ref[j, rows, :]

    def store(rows, v):
        ub_ref[j, rows, :] = v

    s1, s2 = _shifted_rows(ub_ref, load, store, ug, past_ref, (i * tm) % rows_per_batch == 0, nb)
    conv = cw_ref[0:1, :] * s2 + cw_ref[1:2, :] * s1 + cw_ref[2:3, :] * ug
    f = conv * jax.nn.sigmoid(conv) * uv
    acc_ref[...] += jnp.dot(f.astype(BF16), wd_ref[...], preferred_element_type=F32)
    grp = tm // nb
    tail_ref[0] = _group(ug, nb)[:, grp - 8:, :]

    @pl.when(j == nj - 1)
    def _():
        nm = _group(_rms(acc_ref[...], g_ref[3:4, :]), nb)
        o_ref[...] = x_ref[...] + (mod_ref[:, 5:6, :] * nm).reshape(tm, D_MODEL)


def _conv_ffn(x, g, mod, w_up, w_down, cw, past, *, tm, rows_per_batch):
    rows = x.shape[0]
    dff = w_down.shape[0]
    nj = dff // FF_CHUNK
    nb = max(1, tm // rows_per_batch)
    nt = rows // tm
    if nb == 1:
        past_spec = pl.BlockSpec((1, 2, FF_CHUNK), lambda i, j: ((i * tm) // rows_per_batch, 0, j))
        mod_map = lambda i, j: ((i * tm) // rows_per_batch, 0, 0)
    else:
        past_spec = pl.BlockSpec((nb, 2, FF_CHUNK), lambda i, j: (i, 0, j))
        mod_map = lambda i, j: (i, 0, 0)
    out, tails = pl.pallas_call(
        functools.partial(_ffn_kernel, nb=nb, rows_per_batch=rows_per_batch, nj=nj),
        out_shape=(jax.ShapeDtypeStruct((rows, D_MODEL), F32),
                   jax.ShapeDtypeStruct((nt, nb, 8, dff), F32)),
        grid=(nt, nj),
        in_specs=[
            pl.BlockSpec((tm, D_MODEL), lambda i, j: (i, 0)),
            pl.BlockSpec((4, D_MODEL), lambda i, j: (0, 0)),
            pl.BlockSpec((nb, 6, D_MODEL), mod_map),
            pl.BlockSpec((D_MODEL, FF_CHUNK), lambda i, j: (0, j)),
            pl.BlockSpec((D_MODEL, FF_CHUNK), lambda i, j: (0, j + nj)),
            pl.BlockSpec((FF_CHUNK, D_MODEL), lambda i, j: (j, 0)),
            pl.BlockSpec((CONV_W, FF_CHUNK), lambda i, j: (0, j)),
            past_spec,
        ],
        out_specs=(pl.BlockSpec((tm, D_MODEL), lambda i, j: (i, 0)),
                   pl.BlockSpec((1, nb, 8, FF_CHUNK), lambda i, j: (i, 0, 0, j))),
        scratch_shapes=[
            pltpu.VMEM((tm, D_MODEL), BF16),
            pltpu.VMEM((tm, D_MODEL), F32),
            pltpu.VMEM((nj, tm + 8, FF_CHUNK), F32),
        ],
        compiler_params=_cparams(("arbitrary", "arbitrary")),
        name="conv_ffn",
    )(x, g, mod, w_up, w_up, w_down, cw, past)
    return out, tails


def _relayout_w_in(w_in, b_forget):
    db = D_BRANCH
    o_qi = 3 * db
    o_ki = o_qi + N_HEADS * IDX_DIM
    o_wi = o_ki + IDX_DIM
    o_qb = o_wi + N_HEADS
    o_fl = o_qb + 3 * db
    o_cb = o_fl + N_HEADS
    o_gl = o_cb + 3 * db
    main = jnp.concatenate([w_in[:, 0:o_ki], w_in[:, o_qb:o_fl], w_in[:, o_cb:]], axis=1)
    small = jnp.concatenate([w_in[:, o_ki:o_wi], w_in[:, o_wi:o_qb], w_in[:, o_fl:o_cb]], axis=1)
    small = jnp.pad(small, ((0, 0), (0, LANES - small.shape[1])))
    bf = jnp.zeros((1, LANES), F32).at[0, _FL_OFF:_FL_OFF + N_HEADS].set(b_forget)
    del o_gl
    return main.astype(BF16), small.astype(BF16), bf


def _round_up(x, m):
    return (x + m - 1) // m * m


def _layer(x, mod, caches, weights, *, batch, t, past, cfg):
    (g, w_main, w_small, bf_pad, cw_mix, wbr, wo, w_up, cw_ffn, w_down) = weights
    rows = batch * t
    length = past + t
    (qaT, ka32, ka16, va32, vaTc, qiT, qbT, kb32, kb16, vb32, vbTc, cb, u, gl,
     ki32, ki16, wiT, logf) = _in_projection(
        x, mod, g, w_main, w_small, bf_pad, tm=cfg["tm_in"], rows_per_batch=t)

    tq, tk = cfg["tq"], cfg["tk"]
    tqp = _round_up(t, tq)
    lp = _round_up(length, tk)

    def per_batch_T(aT):
        c = aT.shape[0]
        a = jnp.swapaxes(aT.reshape(c, batch, t), 0, 1)
        return jnp.pad(a, ((0, 0), (0, 0), (0, tqp - t)))

    def chunked_T(a):
        return jnp.swapaxes(a.reshape(batch, lp // KEY_CHUNK, KEY_CHUNK, a.shape[-1]), 2, 3)

    def with_cache(cache, new, dtype):
        new = new.reshape(batch, t, -1)
        if cache is not None:
            new = jnp.concatenate([cache.reshape(batch, past, -1).astype(dtype), new.astype(dtype)],
                                  axis=1)
        return jnp.pad(new.astype(dtype), ((0, 0), (0, lp - length), (0, 0)))

    if caches is None:
        c_idx = c_dk = c_dv = c_fk = c_fv = c_lf = None
        past_mix = jnp.zeros((batch, CONV_W - 1, D_BRANCH), F32)
        past_ffn = jnp.zeros((batch, CONV_W - 1, w_down.shape[0]), F32)
    else:
        c_idx, c_dk, c_dv, c_fk, c_fv, c_lf, past_mix, past_ffn = caches

    ki_all = with_cache(c_idx, ki16, BF16)
    ka_all = with_cache(c_dk, ka16, BF16)
    kb_all = with_cache(c_fk, kb16, BF16)
    if caches is None and lp == rows:
        vaT_all = vaTc.reshape(batch, lp // KEY_CHUNK, D_BRANCH, KEY_CHUNK)
        vbT_all = vbTc.reshape(batch, lp // KEY_CHUNK, D_BRANCH, KEY_CHUNK)
    else:
        vaT_all = chunked_T(with_cache(c_dv, va32, BF16))
        vbT_all = chunked_T(with_cache(c_fv, vb32, BF16))
    logf_all = with_cache(c_lf, logf, F32)

    ya = _dsa_attention(per_batch_T(qiT), per_batch_T(wiT), per_batch_T(qaT), ki_all, ka_all,
                        vaT_all, past=past, length=length, tq=tq, tk=tk)
    cum = _forget_cumsum(logf_all)
    kaug, qaug = _fox_augment(cum)
    qaug = jnp.pad(qaug[:, :, past:length], ((0, 0), (0, 0), (0, tqp - t)))
    yb = _fox_attention(per_batch_T(qbT), qaug, kb_all, kaug, vbT_all, past=past, tq=tq, tk=tk)
    ya = ya[:, :t].reshape(rows, D_BRANCH)
    yb = yb[:, :t].reshape(rows, D_BRANCH)

    x1 = _merge(x, ya, yb, cb, u, gl, past_mix, cw_mix, wbr, wo, g, mod,
                tm=cfg["tm_merge"], rows_per_batch=t)
    x2, tails = _conv_ffn(x1, g, mod, w_up, w_down, cw_ffn, past_ffn,
                          tm=cfg["tm_ffn"], rows_per_batch=t)

    new_mix = u.reshape(batch, t, D_BRANCH)[:, t - (CONV_W - 1):]
    nb = max(1, cfg["tm_ffn"] // t)
    if nb == 1:
        tiles_per_batch = t // cfg["tm_ffn"]
        last = tails.reshape(batch, tiles_per_batch, 8, -1)[:, -1]
    else:
        last = tails.reshape(batch, 8, -1)
    new_ffn = last[:, 8 - (CONV_W - 1):]
    hd = (batch, t, N_HEADS, HEAD_DIM)
    state = (ki32.reshape(batch, t, IDX_DIM), ka32.reshape(hd), va32.reshape(hd),
             kb32.reshape(hd), vb32.reshape(hd), logf.reshape(batch, t, N_HEADS), new_mix, new_ffn)
    return x2, state


_PROMPT_CFG = dict(tm_in=512, tm_merge=256, tm_ffn=256, tq=256, tk=4096)
_SAMPLE_CFG = dict(tm_in=256, tm_merge=256, tm_ffn=256, tq=128, tk=2560)


def kernel(x_prompt, x_sample, c_prompt, c_sample, cache_idx_k, cache_dsa_k, cache_dsa_v,
           cache_fox_k, cache_fox_v, cache_fox_logf, state_conv_mix, state_conv_ffn,
           w_ada, b_ada, norm_g, w_in, b_forget, conv_mix_w, w_branch, w_out, w_up,
           conv_ffn_w, w_down):
    bp, tp, _ = x_prompt.shape
    bs, ts, _ = x_sample.shape
    past = cache_dsa_k.shape[2]
    depth = w_ada.shape[0]

    c_all = jnp.concatenate([c_prompt, c_sample], axis=0)
    pad_rows = _round_up(c_all.shape[0], 8) - c_all.shape[0]
    c_all = jnp.pad(c_all, ((0, pad_rows), (0, 0)))
    mod_all = _modulation(c_all, w_ada, b_ada).reshape(depth, -1, 6, D_MODEL)

    yp = x_prompt.reshape(bp * tp, D_MODEL)
    ys = x_sample.reshape(bs * ts, D_MODEL)
    p_states, s_states = [], []
    for l in range(depth):
        w_main, w_small, bf_pad = _relayout_w_in(w_in[l], b_forget[l])
        weights = (norm_g[l], w_main, w_small, bf_pad, conv_mix_w[l], w_branch[l].astype(BF16),
                   w_out[l].astype(BF16), w_up[l].astype(BF16), conv_ffn_w[l],
                   w_down[l].astype(BF16))
        yp, st_p = _layer(yp, mod_all[l, :bp], None, weights, batch=bp, t=tp, past=0,
                          cfg=_PROMPT_CFG)
        caches = (cache_idx_k[l], cache_dsa_k[l], cache_dsa_v[l], cache_fox_k[l], cache_fox_v[l],
                  cache_fox_logf[l], state_conv_mix[l], state_conv_ffn[l])
        ys, st_s = _layer(ys, mod_all[l, bp:bp + bs], caches, weights, batch=bs, t=ts, past=past,
                          cfg=_SAMPLE_CFG)
        p_states.append(st_p)
        s_states.append(st_s)

    stack = lambda states: [jnp.stack([st[k] for st in states], axis=0) for k in range(8)]
    return (yp.reshape(bp, tp, D_MODEL), ys.reshape(bs, ts, D_MODEL), *stack(p_states),
            *stack(s_states))
```

```python
import functools

import jax
import jax.numpy as jnp
from jax import lax
from jax.experimental import pallas as pl
from jax.experimental.pallas import tpu as pltpu

F32 = jnp.float32
BF16 = jnp.bfloat16
I32 = jnp.int32

D_MODEL = 1024
HEAD_DIM = 64
D_BRANCH = 512
N_HEADS = 8
IDX_DIM = 64
CHUNK = 64
CHUNK_SHIFT = 6
TOPK_MAX = 256
CONV_W = 3
EPS = 1e-6
NEG_INF = -1e30
LOG2E = 1.4426950408889634
F32_MAX = 3.4028234663852886e38
INT32_MIN = -(2 ** 31)

LANES = 128
N_MAIN = 16 * D_BRANCH
KEY_CHUNK = 256
KEY_PAIR = 2 * KEY_CHUNK
KEY_QUAD = 2 * KEY_PAIR
SCORE_ROWS = 128
COUNT_ROWS = KEY_PAIR
GROUPS = TOPK_MAX
VMEM_LIMIT = 56 * 1024 * 1024


def _cparams(sem):
    return pltpu.CompilerParams(dimension_semantics=sem, vmem_limit_bytes=VMEM_LIMIT)


def _rms(x, g_row):
    return x * lax.rsqrt(jnp.mean(x * x, axis=-1, keepdims=True) + EPS) * g_row


def _group(x, nb):
    return x.reshape(nb, x.shape[0] // nb, x.shape[1])


def _mod_kernel(c_ref, w_ref, b_ref, o_ref):
    c = c_ref[...]
    s = (c * jax.nn.sigmoid(c)).astype(BF16)
    o_ref[0] = jnp.dot(s, w_ref[0].astype(BF16), preferred_element_type=F32) + b_ref[0]


def _modulation(c_all, w_ada, b_ada):
    depth = w_ada.shape[0]
    rows = c_all.shape[0]
    n = w_ada.shape[2]
    tn = D_MODEL
    return pl.pallas_call(
        _mod_kernel,
        out_shape=jax.ShapeDtypeStruct((depth, rows, n), F32),
        grid=(depth, n // tn),
        in_specs=[
            pl.BlockSpec((rows, D_MODEL), lambda l, j: (0, 0)),
            pl.BlockSpec((1, D_MODEL, tn), lambda l, j: (l, 0, j)),
            pl.BlockSpec((1, 1, tn), lambda l, j: (l, 0, j)),
        ],
        out_specs=pl.BlockSpec((1, rows, tn), lambda l, j: (l, 0, j)),
        compiler_params=_cparams(("arbitrary", "arbitrary")),
        name="adaln_mod",
    )(c_all, w_ada, b_ada.reshape(depth, 1, n))


IN_STEP = 2 * D_BRANCH
_GL_STEP = 5
_WI_OFF = IDX_DIM
_FL_OFF = IDX_DIM + N_HEADS


def _inproj_kernel(x_ref, mod_ref, g_ref, wm_ref, ws_ref, bf_ref,
                   qaT_ref, ka32_ref, ka16_ref, va32_ref, vaT_ref, qiT_ref, qbT_ref,
                   kb32_ref, kb16_ref, vb32_ref, vbT_ref, cb_ref, u_ref, gl_ref,
                   ki32_ref, ki16_ref, wiT_ref, logf_ref,
                   hs_ref, *, nb):
    j = pl.program_id(1)
    tm = x_ref.shape[0]

    @pl.when(j == 0)
    def _():
        y = _rms(x_ref[...], g_ref[0:1, :])
        h = _group(y, nb) * (1.0 + mod_ref[:, 1:2, :]) + mod_ref[:, 0:1, :]
        hb = h.reshape(tm, D_MODEL).astype(BF16)
        hs_ref[...] = hb
        sm = jnp.dot(hb, ws_ref[...], preferred_element_type=F32)
        ki32_ref[...] = sm[:, :IDX_DIM]
        ki16_ref[...] = sm[:, :IDX_DIM].astype(BF16)
        wiT_ref[...] = sm.T[_WI_OFF:_WI_OFF + N_HEADS, :]
        z = sm + bf_ref[...]
        lf = -(jnp.maximum(-z, 0.0) + jnp.log1p(jnp.exp(-jnp.abs(z))))
        logf_ref[...] = lf[:, _FL_OFF:_FL_OFF + N_HEADS]

    acc = jnp.dot(hs_ref[...], wm_ref[...], preferred_element_type=F32)
    lo = acc[:, :D_BRANCH]
    hi = acc[:, D_BRANCH:]
    scale = LOG2E * HEAD_DIM ** -0.5

    def store_chunked_T(ref, a):
        aT = a.T.astype(BF16)
        for c in range(tm // KEY_CHUNK):
            ref[c] = aT[:, c * KEY_CHUNK:(c + 1) * KEY_CHUNK]

    @pl.when(j == 0)
    def _():
        qaT_ref[...] = (lo * scale).T.astype(BF16)
        ka32_ref[...] = hi
        ka16_ref[...] = hi.astype(BF16)

    @pl.when(j == 1)
    def _():
        va32_ref[...] = lo
        store_chunked_T(vaT_ref, lo)
        qiT_ref[...] = hi.T.astype(BF16)

    @pl.when(j == 2)
    def _():
        qbT_ref[...] = (lo * scale).T.astype(BF16)
        kb32_ref[...] = hi
        kb16_ref[...] = hi.astype(BF16)

    @pl.when(j == 3)
    def _():
        vb32_ref[...] = lo
        store_chunked_T(vbT_ref, lo)
        cb_ref[...] = hi

    @pl.when(j == 4)
    def _():
        u_ref[...] = lo * hi

    @pl.when(j >= _GL_STEP)
    def _():
        gl_ref[...] = acc


def _in_projection(x, mod, g, w_main, w_small, bf_pad, *, tm, rows_per_batch):
    rows = x.shape[0]
    nt = rows // tm
    nb = max(1, tm // rows_per_batch)
    nstep = N_MAIN // IN_STEP
    if nb == 1:
        mod_map = lambda i, j: ((i * tm) // rows_per_batch, 0, 0)
    else:
        mod_map = lambda i, j: (i, 0, 0)
    row_blk = lambda w: pl.BlockSpec((tm, w), lambda i, j: (i, 0))
    col_blk = lambda h: pl.BlockSpec((h, tm), lambda i, j: (0, i))
    chunkT = pl.BlockSpec((tm // KEY_CHUNK, D_BRANCH, KEY_CHUNK), lambda i, j: (i, 0, 0))
    f32o = lambda w: jax.ShapeDtypeStruct((rows, w), F32)
    b16o = lambda w: jax.ShapeDtypeStruct((rows, w), BF16)
    b16T = jax.ShapeDtypeStruct((D_BRANCH, rows), BF16)
    b16c = jax.ShapeDtypeStruct((rows // KEY_CHUNK, D_BRANCH, KEY_CHUNK), BF16)
    out_shape = (
        b16T, f32o(D_BRANCH), b16o(D_BRANCH), f32o(D_BRANCH), b16c, b16T, b16T,
        f32o(D_BRANCH), b16o(D_BRANCH), f32o(D_BRANCH), b16c, f32o(D_BRANCH), f32o(D_BRANCH),
        f32o(6 * D_BRANCH),
        f32o(IDX_DIM), b16o(IDX_DIM), jax.ShapeDtypeStruct((N_HEADS, rows), F32), f32o(N_HEADS),
    )
    out_specs = (
        col_blk(D_BRANCH), row_blk(D_BRANCH), row_blk(D_BRANCH), row_blk(D_BRANCH), chunkT,
        col_blk(D_BRANCH), col_blk(D_BRANCH),
        row_blk(D_BRANCH), row_blk(D_BRANCH), row_blk(D_BRANCH), chunkT, row_blk(D_BRANCH),
        row_blk(D_BRANCH),
        pl.BlockSpec((tm, IN_STEP), lambda i, j: (i, jnp.clip(j - _GL_STEP, 0, 2))),
        row_blk(IDX_DIM), row_blk(IDX_DIM), col_blk(N_HEADS), row_blk(N_HEADS),
    )
    return pl.pallas_call(
        functools.partial(_inproj_kernel, nb=nb),
        out_shape=out_shape,
        grid=(nt, nstep),
        in_specs=[
            pl.BlockSpec((tm, D_MODEL), lambda i, j: (i, 0)),
            pl.BlockSpec((nb, 6, D_MODEL), mod_map),
            pl.BlockSpec((4, D_MODEL), lambda i, j: (0, 0)),
            pl.BlockSpec((D_MODEL, IN_STEP), lambda i, j: (0, j)),
            pl.BlockSpec((D_MODEL, LANES), lambda i, j: (0, 0)),
            pl.BlockSpec((1, LANES), lambda i, j: (0, 0)),
        ],
        out_specs=out_specs,
        scratch_shapes=[pltpu.VMEM((tm, D_MODEL), BF16)],
        compiler_params=_cparams(("arbitrary", "arbitrary")),
        name="in_projection",
    )(x, mod, g, w_main, w_small, bf_pad)


def _cumsum_kernel(x_ref, o_ref):
    x = x_ref[0]
    n = x.shape[0]
    lane = lax.broadcasted_iota(I32, x.shape, 1)
    row = lax.broadcasted_iota(I32, x.shape, 0)
    s = N_HEADS
    while s < LANES:
        x = x + jnp.where(lane >= s, pltpu.roll(x, s, axis=1), 0.0)
        s *= 2
    t = jnp.where(lane >= LANES - N_HEADS, x, 0.0)
    s = N_HEADS
    while s < LANES:
        t = t + pltpu.roll(t, s, axis=1)
        s *= 2
    t = jnp.where(row >= 1, pltpu.roll(t, 1, axis=0), 0.0)
    s = 1
    while s < n:
        if s < 8:
            sh = jnp.where(row >= s, pltpu.roll(t, s, axis=0), 0.0)
        else:
            sh = jnp.concatenate([jnp.zeros((s, LANES), F32), t[:n - s]], axis=0)
        t = t + sh
        s *= 2
    o_ref[0] = x + t


def _forget_cumsum(logf_all):
    b, lp, h = logf_all.shape
    n = lp * h // LANES
    out = pl.pallas_call(
        _cumsum_kernel,
        out_shape=jax.ShapeDtypeStruct((b, n, LANES), F32),
        grid=(b,),
        in_specs=[pl.BlockSpec((1, n, LANES), lambda i: (i, 0, 0))],
        out_specs=pl.BlockSpec((1, n, LANES), lambda i: (i, 0, 0)),
        compiler_params=_cparams(("arbitrary",)),
        name="forget_cumsum",
    )(logf_all.reshape(b, n, LANES))
    return out.reshape(b, lp, h)


def _pair_padded(qT_ref, h):
    pr, half = divmod(h, 2)
    blk = qT_ref[0, pr * LANES + half * HEAD_DIM:pr * LANES + (half + 1) * HEAD_DIM, :]
    z = jnp.zeros_like(blk)
    return jnp.concatenate([blk, z] if half == 0 else [z, blk], axis=0)


def _rows_to_heads(rows):
    tq = rows[0].shape[1]
    sub = lax.broadcasted_iota(I32, (N_HEADS, tq), 0)
    out = jnp.broadcast_to(rows[0], (N_HEADS, tq))
    for h in range(1, N_HEADS):
        out = jnp.where(sub == h, rows[h], out)
    return out


def _stage_and_consume(stage, consume, m_cur, pend, l, acc_ref):
    m_next, sums, alphas = [], [], []
    for h in range(N_HEADS):
        if stage is not None:
            logits, mask, buf = stage
            lg = logits(h)
            if mask is not None:
                lg = jnp.where(mask, lg, NEG_INF)
            buf[h] = lg
            m_next.append(jnp.maximum(m_cur[h:h + 1, :], jnp.max(lg, axis=0, keepdims=True)))
        if consume is not None:
            v_t, buf = consume
            before, after = pend
            alpha = jnp.exp2(before[h:h + 1, :] - after[h:h + 1, :])
            p = jnp.exp2(buf[h] - after[h:h + 1, :]).astype(BF16)
            ones = jnp.ones((16, p.shape[0]), BF16)
            pv = jnp.dot(jnp.concatenate([v_t(h), ones], axis=0), p, preferred_element_type=F32)
            rows = slice(h * HEAD_DIM, (h + 1) * HEAD_DIM)
            acc_ref[rows, :] = alpha * acc_ref[rows, :] + pv[:HEAD_DIM, :]
            sums.append(pv[HEAD_DIM:HEAD_DIM + 1, :])
            alphas.append(alpha)
    if consume is not None:
        l = _rows_to_heads(alphas) * l + _rows_to_heads(sums)
    return (_rows_to_heads(m_next) if stage is not None else m_cur), l


def _attend_block(n_free, n_used, logits, mask, v_t, m_ref, l_ref, acc_ref, buf0, buf1):
    m0 = m_ref[...]
    buf1[...] = jnp.full(buf1.shape, -jnp.inf, F32)

    def quad(masked, qd, carry):
        m_a, m_b, l = carry
        p0 = 2 * qd
        p1 = p0 + 1
        pp = jnp.maximum(p0 - 1, 0)
        mk = (lambda p: mask(p)) if masked else (lambda p: None)
        m_c, l = _stage_and_consume((lambda h: logits(h, p0), mk(p0), buf0),
                                    (lambda h: v_t(h, pp), buf1), m_b, (m_a, m_b), l, acc_ref)
        m_d, l = _stage_and_consume((lambda h: logits(h, p1), mk(p1), buf1),
                                    (lambda h: v_t(h, p0), buf0), m_c, (m_b, m_c), l, acc_ref)
        return m_c, m_d, l

    carry = lax.fori_loop(0, n_free, functools.partial(quad, False), (m0, m0, l_ref[...]))
    m_c, m_d, l = lax.fori_loop(n_free, n_used, functools.partial(quad, True), carry)
    last = 2 * n_used - 1
    _, l = _stage_and_consume(None, (lambda h: v_t(h, last), buf1), m_d, (m_c, m_d), l, acc_ref)
    m_ref[...] = m_d
    l_ref[...] = l


def _softmax_init(m_ref, l_ref, acc_ref):
    m_ref[...] = jnp.full(m_ref.shape, NEG_INF, F32)
    l_ref[...] = jnp.zeros(l_ref.shape, F32)
    acc_ref[...] = jnp.zeros(acc_ref.shape, F32)


def _softmax_finish(o_ref, l_ref, acc_ref):
    inv = 1.0 / l_ref[...]
    parts = [acc_ref[h * HEAD_DIM:(h + 1) * HEAD_DIM, :] * inv[h:h + 1, :] for h in range(N_HEADS)]
    o_ref[0] = jnp.concatenate(parts, axis=0).T.astype(o_ref.dtype)


def _f32_to_sortable(x):
    b = lax.bitcast_convert_type(x, I32)
    return jnp.where(b >= 0, b, b ^ jnp.int32(0x7FFFFFFF)) ^ jnp.int32(INT32_MIN)


def _sortable_to_f32(u):
    o = u ^ jnp.int32(INT32_MIN)
    return lax.bitcast_convert_type(jnp.where(o >= 0, o, o ^ jnp.int32(0x7FFFFFFF)), F32)


def _dsa_kernel(qiT_ref, wiT_ref, qaT_ref, ki_ref, ka_ref, vaT_ref, o_ref,
                s_ref, gm_ref, thr_ref, cut_ref, qp_ref, m_ref, l_ref, acc_ref, lg0_ref, lg1_ref,
                *, past, length, n_sel, tq, tk, nk):
    i = pl.program_id(1)
    j = pl.program_id(2)
    q0 = past + i * tq
    qpos = q0 + lax.broadcasted_iota(I32, (1, tq), 1)
    vis = jnp.minimum(((qpos >> CHUNK_SHIFT) + 1) * CHUNK, length)
    vmax = jnp.minimum((((q0 + tq - 1) >> CHUNK_SHIFT) + 1) * CHUNK, length)
    nvis = (vmax + tk - 1) // tk
    vcap = ((vmax + KEY_QUAD - 1) // KEY_QUAD) * KEY_QUAD

    def rows_here(jb, step):
        return jnp.clip((vcap - jb * tk) // step, 0, tk // step)

    @pl.when(jnp.logical_and(j < nk, j < nvis))
    def _():
        def body(c, carry):
            for half in range(GROUPS // SCORE_ROWS):
                r0 = pl.multiple_of(c * GROUPS + half * SCORE_ROWS, SCORE_ROWS)
                kblk = ki_ref[0, pl.ds(r0, SCORE_ROWS), :]
                acc = jnp.zeros((SCORE_ROWS, tq), F32)
                for h in range(N_HEADS):
                    d = jnp.dot(kblk, qiT_ref[0, h * IDX_DIM:(h + 1) * IDX_DIM, :],
                                preferred_element_type=F32)
                    acc = acc + jnp.maximum(d, 0.0) * wiT_ref[0, h:h + 1, :]
                kpos = j * tk + r0 + lax.broadcasted_iota(I32, (SCORE_ROWS, 1), 0)
                acc = jnp.where(kpos < vis, acc, -jnp.inf)
                s_ref[pl.ds(pl.multiple_of(j * tk + r0, SCORE_ROWS), SCORE_ROWS), :] = acc
                grp = slice(half * SCORE_ROWS, (half + 1) * SCORE_ROWS)
                gm_ref[grp, :] = jnp.maximum(gm_ref[grp, :], acc)
            return carry

        @pl.when(j == 0)
        def _():
            gm_ref[...] = jnp.full(gm_ref.shape, -jnp.inf, F32)

        lax.fori_loop(0, rows_here(j, GROUPS), body, 0)

    @pl.when(j == nk - 1)
    def _():
        nch = vcap // COUNT_ROWS

        def count(pred):
            sub = 64

            def body(c, accs):
                accs = list(accs)
                for k in range(COUNT_ROWS // sub):
                    r0 = pl.multiple_of(c * COUNT_ROWS + k * sub, sub)
                    hit = jnp.where(pred(s_ref[pl.ds(r0, sub), :], r0), 1.0, 0.0)
                    accs[k % 4] = accs[k % 4] + jnp.sum(hit.reshape(sub // 8, 8, tq), axis=0)
                return tuple(accs)
            zero = jnp.zeros((8, tq), F32)
            a0, a1, a2, a3 = lax.fori_loop(0, nch, body, (zero, zero, zero, zero))
            return jnp.sum((a0 + a1) + (a2 + a3), axis=0, keepdims=True)

        def count_ge(t):
            return count(lambda blk, r0: blk >= t)

        kf = float(n_sel)
        no_cut = jnp.full((1, tq), 2 ** 30, I32)

        gm = gm_ref[...]
        u_lo = _f32_to_sortable(jnp.min(gm, axis=0, keepdims=True))
        u_hi = _f32_to_sortable(jnp.max(gm, axis=0, keepdims=True))
        top_bit = 31 - lax.clz(u_lo ^ u_hi)
        known = jnp.where(top_bit >= 31, jnp.int32(0),
                          lax.shift_left(jnp.int32(-1), jnp.minimum(top_bit + 1, 31)))
        state = (jnp.max(top_bit), u_hi & known, jnp.full((1, tq), -1.0, F32))

        def more(state):
            b, _, c_cur = state
            return jnp.logical_and(b >= 0, jnp.max(jnp.where(c_cur != kf, 1.0, 0.0)) > 0.0)

        def bit_step(state):
            b, cur, c_cur = state
            cand = cur | lax.shift_left(jnp.int32(1), b)
            c = count_ge(_sortable_to_f32(cand))
            take = jnp.logical_and(top_bit >= b, c >= kf)
            return b - 1, jnp.where(take, cand, cur), jnp.where(take, c, c_cur)

        _, cur, c_cur = lax.while_loop(more, bit_step, state)
        thr = jnp.where(vis < n_sel, -F32_MAX, _sortable_to_f32(cur))

        def with_ties():
            n_gt = count(lambda blk, r0: blk > thr)
            n_ge = count_ge(thr)
            need = kf - n_gt
            tied = (n_ge - n_gt) > need

            def tie_cut():
                rr = lax.broadcasted_iota(I32, (KEY_CHUNK, KEY_CHUNK), 0)
                cc = lax.broadcasted_iota(I32, (KEY_CHUNK, KEY_CHUNK), 1)
                lower = jnp.where(cc <= rr, 1.0, 0.0).astype(BF16)

                def body(c, carry):
                    run, last = carry
                    r0 = pl.multiple_of(c * KEY_CHUNK, KEY_CHUNK)
                    eq = s_ref[pl.ds(r0, KEY_CHUNK), :] == thr
                    ordinal = run + jnp.dot(lower, jnp.where(eq, 1.0, 0.0).astype(BF16),
                                            preferred_element_type=F32)
                    kpos = (r0 + lax.broadcasted_iota(I32, (KEY_CHUNK, 1), 0)).astype(F32)
                    take = jnp.logical_and(eq, ordinal <= need)
                    last = jnp.maximum(last, jnp.max(jnp.where(take, kpos, -1.0), axis=0,
                                                     keepdims=True))
                    return ordinal[KEY_CHUNK - 1:KEY_CHUNK, :], last

                init = (jnp.zeros((1, tq), F32), jnp.full((1, tq), -1.0, F32))
                _, last = lax.fori_loop(0, vcap // KEY_CHUNK, body, init)
                return jnp.where(tied, last.astype(I32), no_cut)

            return lax.cond(jnp.max(jnp.where(tied, 1.0, 0.0)) > 0.0, tie_cut, lambda: no_cut)

        unsure = jnp.logical_or(c_cur != kf, vis < n_sel)
        cut = lax.cond(jnp.max(jnp.where(unsure, 1.0, 0.0)) > 0.0, with_ties, lambda: no_cut)
        thr_ref[0:1, :] = thr
        cut_ref[0:1, :] = cut
        for h in range(N_HEADS):
            qp_ref[h] = _pair_padded(qaT_ref, h)
        _softmax_init(m_ref, l_ref, acc_ref)

    jj = j - nk

    @pl.when(jnp.logical_and(j >= nk, jj < nvis))
    def _():
        thr = thr_ref[0:1, :]
        cut = cut_ref[0:1, :]

        def selected(p):
            g0 = pl.multiple_of(jj * tk + p * KEY_PAIR, KEY_PAIR)
            sblk = s_ref[pl.ds(g0, KEY_PAIR), :]
            kpos = g0 + lax.broadcasted_iota(I32, (KEY_PAIR, 1), 0)
            return jnp.logical_or(sblk > thr, jnp.logical_and(sblk == thr, kpos <= cut))

        def logits(h, p):
            r0 = pl.multiple_of(p * KEY_PAIR, KEY_PAIR)
            kblk = ka_ref[0, pl.ds(r0, KEY_PAIR), (h // 2) * LANES:(h // 2 + 1) * LANES]
            return jnp.dot(kblk, qp_ref[h], preferred_element_type=F32)

        def v_t(h, p):
            rows = slice(h * HEAD_DIM, (h + 1) * HEAD_DIM)
            return jnp.concatenate([vaT_ref[0, 2 * p, rows, :], vaT_ref[0, 2 * p + 1, rows, :]],
                                   axis=1)

        _attend_block(0, rows_here(jj, KEY_QUAD), logits, selected, v_t, m_ref, l_ref, acc_ref,
                      lg0_ref, lg1_ref)

    @pl.when(j == 2 * nk - 1)
    def _():
        _softmax_finish(o_ref, l_ref, acc_ref)


def _dsa_attention(qiT, wiT, qaT, ki, ka, vaTc, *, past, length, tq, tk):
    b, _, tqp = qiT.shape
    lp = ki.shape[1]
    nq, nk = tqp // tq, lp // tk
    n_sel = min(TOPK_MAX, length // 4)

    def nvis(i):
        vmax = jnp.minimum(((past + (i + 1) * tq - 1) // CHUNK + 1) * CHUNK, length)
        return (vmax + tk - 1) // tk

    k1 = lambda bb, i, j: (bb, jnp.minimum(j, nvis(i) - 1), 0)
    k3 = lambda bb, i, j: (bb, jnp.clip(j - nk, 0, nvis(i) - 1), 0)
    k3c = lambda bb, i, j: (bb, jnp.clip(j - nk, 0, nvis(i) - 1), 0, 0)
    qmap = lambda bb, i, j: (bb, 0, i)
    kern = functools.partial(_dsa_kernel, past=past, length=length, n_sel=n_sel, tq=tq, tk=tk,
                             nk=nk)
    return pl.pallas_call(
        kern,
        out_shape=jax.ShapeDtypeStruct((b, tqp, D_BRANCH), BF16),
        grid=(b, nq, 2 * nk),
        in_specs=[
            pl.BlockSpec((1, D_BRANCH, tq), qmap),
            pl.BlockSpec((1, N_HEADS, tq), qmap),
            pl.BlockSpec((1, D_BRANCH, tq), qmap),
            pl.BlockSpec((1, tk, IDX_DIM), k1),
            pl.BlockSpec((1, tk, D_BRANCH), k3),
            pl.BlockSpec((1, tk // KEY_CHUNK, D_BRANCH, KEY_CHUNK), k3c),
        ],
        out_specs=pl.BlockSpec((1, tq, D_BRANCH), lambda bb, i, j: (bb, i, 0)),
        scratch_shapes=[
            pltpu.VMEM((lp, tq), F32),
            pltpu.VMEM((GROUPS, tq), F32),
            pltpu.VMEM((8, tq), F32),
            pltpu.VMEM((8, tq), I32),
            pltpu.VMEM((N_HEADS, LANES, tq), BF16),
            pltpu.VMEM((N_HEADS, tq), F32),
            pltpu.VMEM((N_HEADS, tq), F32),
            pltpu.VMEM((D_BRANCH, tq), F32),
            pltpu.VMEM((N_HEADS, KEY_PAIR, tq), F32),
            pltpu.VMEM((N_HEADS, KEY_PAIR, tq), F32),
        ],
        compiler_params=_cparams(("arbitrary", "arbitrary", "arbitrary")),
        name="dsa_attention",
    )(qiT, wiT, qaT, ki, ka, vaTc)


_AUG = 6


def _fox_kernel(qbT_ref, qaug_ref, kb_ref, kaug_ref, vbT_ref, o_ref,
                qf_ref, m_ref, l_ref, acc_ref, lg0_ref, lg1_ref, *, past, tq, tk, nk):
    i = pl.program_id(1)
    j = pl.program_id(2)
    q0 = past + i * tq
    qpos = q0 + lax.broadcasted_iota(I32, (1, tq), 1)
    nvis = (q0 + tq + tk - 1) // tk
    nquad = tk // KEY_QUAD
    n_free = jnp.clip((q0 + 1 - j * tk) // KEY_QUAD, 0, nquad)
    n_used = jnp.clip((q0 + tq - j * tk + KEY_QUAD - 1) // KEY_QUAD, 0, nquad)

    @pl.when(j == 0)
    def _():
        arow = lax.broadcasted_iota(I32, (LANES, 1), 0)
        qa = qaug_ref[0]
        for h in range(N_HEADS):
            mine = jnp.logical_and(arow >= _AUG * h, arow < _AUG * (h + 1))
            qf_ref[h, 0:LANES, :] = _pair_padded(qbT_ref, h)
            qf_ref[h, LANES:2 * LANES, :] = jnp.where(mine, qa, 0.0).astype(BF16)
        _softmax_init(m_ref, l_ref, acc_ref)

    @pl.when(j < nvis)
    def _():
        def causal(p):
            kpos = j * tk + p * KEY_PAIR + lax.broadcasted_iota(I32, (KEY_PAIR, 1), 0)
            return kpos <= qpos

        def logits(h, p):
            r0 = pl.multiple_of(p * KEY_PAIR, KEY_PAIR)
            kblk = kb_ref[0, pl.ds(r0, KEY_PAIR), (h // 2) * LANES:(h // 2 + 1) * LANES]
            lhs = jnp.concatenate([kblk, kaug_ref[0, pl.ds(r0, KEY_PAIR), :]], axis=1)
            return jnp.dot(lhs, qf_ref[h], preferred_element_type=F32)

        def v_t(h, p):
            rows = slice(h * HEAD_DIM, (h + 1) * HEAD_DIM)
            return jnp.concatenate([vbT_ref[0, 2 * p, rows, :], vbT_ref[0, 2 * p + 1, rows, :]],
                                   axis=1)

        _attend_block(n_free, n_used, logits, causal, v_t, m_ref, l_ref, acc_ref,
                      lg0_ref, lg1_ref)

    @pl.when(j == nk - 1)
    def _():
        _softmax_finish(o_ref, l_ref, acc_ref)


def _fox_attention(qbT, qaug, kb, kaug, vbTc, *, past, tq, tk):
    b, _, tqp = qbT.shape
    lp = kb.shape[1]
    nq, nk = tqp // tq, lp // tk

    def last(i):
        return jnp.minimum((past + (i + 1) * tq + tk - 1) // tk, nk) - 1

    kmap = lambda bb, i, j: (bb, jnp.minimum(j, last(i)), 0)
    kmapc = lambda bb, i, j: (bb, jnp.minimum(j, last(i)), 0, 0)
    qmap = lambda bb, i, j: (bb, 0, i)
    return pl.pallas_call(
        functools.partial(_fox_kernel, past=past, tq=tq, tk=tk, nk=nk),
        out_shape=jax.ShapeDtypeStruct((b, tqp, D_BRANCH), BF16),
        grid=(b, nq, nk),
        in_specs=[
            pl.BlockSpec((1, D_BRANCH, tq), qmap),
            pl.BlockSpec((1, LANES, tq), qmap),
            pl.BlockSpec((1, tk, D_BRANCH), kmap),
            pl.BlockSpec((1, tk, LANES), kmap),
            pl.BlockSpec((1, tk // KEY_CHUNK, D_BRANCH, KEY_CHUNK), kmapc),
        ],
        out_specs=pl.BlockSpec((1, tq, D_BRANCH), lambda bb, i, j: (bb, i, 0)),
        scratch_shapes=[
            pltpu.VMEM((N_HEADS, 2 * LANES, tq), BF16),
            pltpu.VMEM((N_HEADS, tq), F32),
            pltpu.VMEM((N_HEADS, tq), F32),
            pltpu.VMEM((D_BRANCH, tq), F32),
            pltpu.VMEM((N_HEADS, KEY_PAIR, tq), F32),
            pltpu.VMEM((N_HEADS, KEY_PAIR, tq), F32),
        ],
        compiler_params=_cparams(("arbitrary", "arbitrary", "arbitrary")),
        name="fox_attention",
    )(qbT, qaug, kb, kaug, vbTc)


AUG_ROWS = 512


def _aug_kernel(c_ref, kaug_ref, qaugT_ref):
    x = c_ref[0] * LOG2E
    a = x.astype(BF16).astype(F32)
    r = x - a
    b = r.astype(BF16).astype(F32)
    c = (r - b).astype(BF16).astype(F32)
    lane = lax.broadcasted_iota(I32, (N_HEADS, LANES), 1)
    head = lax.broadcasted_iota(I32, (N_HEADS, LANES), 0)

    def place(v, slot):
        sel = jnp.where(lane == _AUG * head + slot, 1.0, 0.0)
        return jnp.dot(v, sel, preferred_element_type=F32, precision=lax.Precision.HIGHEST)

    slot = lane - _AUG * head
    ones_k = jnp.sum(jnp.where(jnp.logical_and(slot >= 0, slot < 3), 1.0, 0.0), axis=0, keepdims=True)
    ones_q = jnp.sum(jnp.where(jnp.logical_and(slot >= 3, slot < 6), 1.0, 0.0), axis=0, keepdims=True)
    kaug_ref[0] = (ones_k - (place(a, 3) + place(b, 4) + place(c, 5))).astype(BF16)
    qaugT_ref[0] = (ones_q + (place(a, 0) + place(b, 1) + place(c, 2))).T


def _fox_augment(cum):
    b, lp, h = cum.shape
    return pl.pallas_call(
        _aug_kernel,
        out_shape=(jax.ShapeDtypeStruct((b, lp, LANES), BF16),
                   jax.ShapeDtypeStruct((b, LANES, lp), F32)),
        grid=(b, lp // AUG_ROWS),
        in_specs=[pl.BlockSpec((1, AUG_ROWS, h), lambda i, j: (i, j, 0))],
        out_specs=(pl.BlockSpec((1, AUG_ROWS, LANES), lambda i, j: (i, j, 0)),
                   pl.BlockSpec((1, LANES, AUG_ROWS), lambda i, j: (i, 0, j))),
        compiler_params=_cparams(("arbitrary", "arbitrary")),
        name="fox_augment",
    )(cum)


def _shifted_rows(buf, load, store, u, past_ref, starts_batch, nb):
    tm, c = u.shape
    store(slice(8, 8 + tm), u)
    if nb == 1:
        @pl.when(starts_batch)
        def _():
            store(slice(6, 8), past_ref[0])
    s1 = load(slice(7, 7 + tm))
    s2 = load(slice(6, 6 + tm))
    if nb > 1:
        r = lax.broadcasted_iota(I32, (nb, tm // nb, 1), 1)
        p0 = past_ref[:, 0:1, :]
        p1 = past_ref[:, 1:2, :]
        s1 = jnp.where(r == 0, p1, _group(s1, nb)).reshape(tm, c)
        s2 = jnp.where(r == 0, p0, jnp.where(r == 1, p1, _group(s2, nb))).reshape(tm, c)
    store(slice(0, 8), load(slice(tm, tm + 8)))
    return s1, s2


def _merge_kernel(x_ref, ya_ref, yb_ref, cb_ref, u_ref, gl_ref, past_ref, cw_ref, wbr_ref,
                  wo_ref, g_ref, mod_ref, o_ref, ub_ref, *, nb, rows_per_batch):
    i = pl.program_id(0)
    tm = x_ref.shape[0]
    u = u_ref[...]

    def load(rows):
        return ub_ref[rows, :]

    def store(rows, v):
        ub_ref[rows, :] = v

    s1, s2 = _shifted_rows(ub_ref, load, store, u, past_ref, (i * tm) % rows_per_batch == 0, nb)
    conv = cw_ref[0:1, :] * s2 + cw_ref[1:2, :] * s1 + cw_ref[2:3, :] * u
    yc = (cb_ref[...] * conv).astype(BF16)
    mix = jnp.zeros((tm, D_MODEL), F32)
    for n, y in enumerate((ya_ref[...], yb_ref[...], yc)):
        br = jnp.dot(y, wbr_ref[n], preferred_element_type=F32)
        mix = mix + jax.nn.sigmoid(gl_ref[:, n * D_MODEL:(n + 1) * D_MODEL]) * br
    mo = jnp.dot(mix.astype(BF16), wo_ref[...], preferred_element_type=F32)
    nm = _group(_rms(mo, g_ref[1:2, :]), nb)
    o_ref[...] = x_ref[...] + (mod_ref[:, 2:3, :] * nm).reshape(tm, D_MODEL)


def _mod_spec(tm, nb, rows_per_batch, ngrid):
    if nb == 1:
        f = lambda i, *_: ((i * tm) // rows_per_batch, 0, 0)
    else:
        f = lambda i, *_: (i, 0, 0)
    return pl.BlockSpec((nb, 6, D_MODEL), f)


def _merge(x, ya, yb, cb, u, gl, past, cw, wbr, wo, g, mod, *, tm, rows_per_batch):
    rows = x.shape[0]
    nb = max(1, tm // rows_per_batch)
    row = lambda w: pl.BlockSpec((tm, w), lambda i: (i, 0))
    full = lambda shape: pl.BlockSpec(shape, lambda i: (0,) * len(shape))
    if nb == 1:
        past_spec = pl.BlockSpec((1, 2, D_BRANCH), lambda i: ((i * tm) // rows_per_batch, 0, 0))
    else:
        past_spec = pl.BlockSpec((nb, 2, D_BRANCH), lambda i: (i, 0, 0))
    return pl.pallas_call(
        functools.partial(_merge_kernel, nb=nb, rows_per_batch=rows_per_batch),
        out_shape=jax.ShapeDtypeStruct((rows, D_MODEL), F32),
        grid=(rows // tm,),
        in_specs=[
            row(D_MODEL), row(D_BRANCH), row(D_BRANCH), row(D_BRANCH), row(D_BRANCH),
            row(3 * D_MODEL), past_spec, full((CONV_W, D_BRANCH)),
            full((3, D_BRANCH, D_MODEL)), full((D_MODEL, D_MODEL)), full((4, D_MODEL)),
            _mod_spec(tm, nb, rows_per_batch, 1),
        ],
        out_specs=row(D_MODEL),
        scratch_shapes=[pltpu.VMEM((tm + 8, D_BRANCH), F32)],
        compiler_params=_cparams(("arbitrary",)),
        name="branch_merge",
    )(x, ya, yb, cb, u, gl, past, cw, wbr, wo, g, mod)


FF_CHUNK = 1408


def _ffn_kernel(x_ref, g_ref, mod_ref, wg_ref, wv_ref, wd_ref, cw_ref, past_ref,
                o_ref, tail_ref, hs_ref, acc_ref, ub_ref, *, nb, rows_per_batch, nj):
    i = pl.program_id(0)
    j = pl.program_id(1)
    tm = x_ref.shape[0]

    @pl.when(j == 0)
    def _():
        y = _group(_rms(x_ref[...], g_ref[2:3, :]), nb)
        h = y * (1.0 + mod_ref[:, 4:5, :]) + mod_ref[:, 3:4, :]
        hs_ref[...] = h.reshape(tm, D_MODEL).astype(BF16)
        acc_ref[...] = jnp.zeros(acc_ref.shape, F32)

    hs = hs_ref[...]
    ug = jnp.dot(hs, wg_ref[...], preferred_element_type=F32)
    uv = jnp.dot(hs, wv_ref[...], preferred_element_type=F32)

    def load(rows):
        return ub_ref[j, rows, :]

    def store(rows, v):
        ub_ref[j, rows, :] = v

    s1, s2 = _shifted_rows(ub_ref, load, store, ug, past_ref, (i * tm) % rows_per_batch == 0, nb)
    conv = cw_ref[0:1, :] * s2 + cw_ref[1:2, :] * s1 + cw_ref[2:3, :] * ug
    f = conv * jax.nn.sigmoid(conv) * uv
    acc_ref[...] += jnp.dot(f.astype(BF16), wd_ref[...], preferred_element_type=F32)
    grp = tm // nb
    tail_ref[0] = _group(ug, nb)[:, grp - 8:, :]

    @pl.when(j == nj - 1)
    def _():
        nm = _group(_rms(acc_ref[...], g_ref[3:4, :]), nb)
        o_ref[...] = x_ref[...] + (mod_ref[:, 5:6, :] * nm).reshape(tm, D_MODEL)


def _conv_ffn(x, g, mod, w_up, w_down, cw, past, *, tm, rows_per_batch):
    rows = x.shape[0]
    dff = w_down.shape[0]
    nj = dff // FF_CHUNK
    nb = max(1, tm // rows_per_batch)
    nt = rows // tm
    if nb == 1:
        past_spec = pl.BlockSpec((1, 2, FF_CHUNK), lambda i, j: ((i * tm) // rows_per_batch, 0, j))
        mod_map = lambda i, j: ((i * tm) // rows_per_batch, 0, 0)
    else:
        past_spec = pl.BlockSpec((nb, 2, FF_CHUNK), lambda i, j: (i, 0, j))
        mod_map = lambda i, j: (i, 0, 0)
    out, tails = pl.pallas_call(
        functools.partial(_ffn_kernel, nb=nb, rows_per_batch=rows_per_batch, nj=nj),
        out_shape=(jax.ShapeDtypeStruct((rows, D_MODEL), F32),
                   jax.ShapeDtypeStruct((nt, nb, 8, dff), F32)),
        grid=(nt, nj),
        in_specs=[
            pl.BlockSpec((tm, D_MODEL), lambda i, j: (i, 0)),
            pl.BlockSpec((4, D_MODEL), lambda i, j: (0, 0)),
            pl.BlockSpec((nb, 6, D_MODEL), mod_map),
            pl.BlockSpec((D_MODEL, FF_CHUNK), lambda i, j: (0, j)),
            pl.BlockSpec((D_MODEL, FF_CHUNK), lambda i, j: (0, j + nj)),
            pl.BlockSpec((FF_CHUNK, D_MODEL), lambda i, j: (j, 0)),
            pl.BlockSpec((CONV_W, FF_CHUNK), lambda i, j: (0, j)),
            past_spec,
        ],
        out_specs=(pl.BlockSpec((tm, D_MODEL), lambda i, j: (i, 0)),
                   pl.BlockSpec((1, nb, 8, FF_CHUNK), lambda i, j: (i, 0, 0, j))),
        scratch_shapes=[
            pltpu.VMEM((tm, D_MODEL), BF16),
            pltpu.VMEM((tm, D_MODEL), F32),
            pltpu.VMEM((nj, tm + 8, FF_CHUNK), F32),
        ],
        compiler_params=_cparams(("arbitrary", "arbitrary")),
        name="conv_ffn",
    )(x, g, mod, w_up, w_up, w_down, cw, past)
    return out, tails


def _relayout_w_in(w_in, b_forget):
    db = D_BRANCH
    o_qi = 3 * db
    o_ki = o_qi + N_HEADS * IDX_DIM
    o_wi = o_ki + IDX_DIM
    o_qb = o_wi + N_HEADS
    o_fl = o_qb + 3 * db
    o_cb = o_fl + N_HEADS
    o_gl = o_cb + 3 * db
    main = jnp.concatenate([w_in[:, 0:o_ki], w_in[:, o_qb:o_fl], w_in[:, o_cb:]], axis=1)
    small = jnp.concatenate([w_in[:, o_ki:o_wi], w_in[:, o_wi:o_qb], w_in[:, o_fl:o_cb]], axis=1)
    small = jnp.pad(small, ((0, 0), (0, LANES - small.shape[1])))
    bf = jnp.zeros((1, LANES), F32).at[0, _FL_OFF:_FL_OFF + N_HEADS].set(b_forget)
    del o_gl
    return main.astype(BF16), small.astype(BF16), bf


def _round_up(x, m):
    return (x + m - 1) // m * m


def _layer(x, mod, caches, weights, *, batch, t, past, cfg):
    (g, w_main, w_small, bf_pad, cw_mix, wbr, wo, w_up, cw_ffn, w_down) = weights
    rows = batch * t
    length = past + t
    (qaT, ka32, ka16, va32, vaTc, qiT, qbT, kb32, kb16, vb32, vbTc, cb, u, gl,
     ki32, ki16, wiT, logf) = _in_projection(
        x, mod, g, w_main, w_small, bf_pad, tm=cfg["tm_in"], rows_per_batch=t)

    tq, tk = cfg["tq"], cfg["tk"]
    tqp = _round_up(t, tq)
    lp = _round_up(length, tk)

    def per_batch_T(aT):
        c = aT.shape[0]
        a = jnp.swapaxes(aT.reshape(c, batch, t), 0, 1)
        return jnp.pad(a, ((0, 0), (0, 0), (0, tqp - t)))

    def chunked_T(a):
        return jnp.swapaxes(a.reshape(batch, lp // KEY_CHUNK, KEY_CHUNK, a.shape[-1]), 2, 3)

    def with_cache(cache, new, dtype):
        new = new.reshape(batch, t, -1)
        if cache is not None:
            new = jnp.concatenate([cache.reshape(batch, past, -1).astype(dtype), new.astype(dtype)],
                                  axis=1)
        return jnp.pad(new.astype(dtype), ((0, 0), (0, lp - length), (0, 0)))

    if caches is None:
        c_idx = c_dk = c_dv = c_fk = c_fv = c_lf = None
        past_mix = jnp.zeros((batch, CONV_W - 1, D_BRANCH), F32)
        past_ffn = jnp.zeros((batch, CONV_W - 1, w_down.shape[0]), F32)
    else:
        c_idx, c_dk, c_dv, c_fk, c_fv, c_lf, past_mix, past_ffn = caches

    ki_all = with_cache(c_idx, ki16, BF16)
    ka_all = with_cache(c_dk, ka16, BF16)
    kb_all = with_cache(c_fk, kb16, BF16)
    if caches is None and lp == rows:
        vaT_all = vaTc.reshape(batch, lp // KEY_CHUNK, D_BRANCH, KEY_CHUNK)
        vbT_all = vbTc.reshape(batch, lp // KEY_CHUNK, D_BRANCH, KEY_CHUNK)
    else:
        vaT_all = chunked_T(with_cache(c_dv, va32, BF16))
        vbT_all = chunked_T(with_cache(c_fv, vb32, BF16))
    logf_all = with_cache(c_lf, logf, F32)

    ya = _dsa_attention(per_batch_T(qiT), per_batch_T(wiT), per_batch_T(qaT), ki_all, ka_all,
                        vaT_all, past=past, length=length, tq=tq, tk=tk)
    cum = _forget_cumsum(logf_all)
    kaug, qaug = _fox_augment(cum)
    qaug = jnp.pad(qaug[:, :, past:length], ((0, 0), (0, 0), (0, tqp - t)))
    yb = _fox_attention(per_batch_T(qbT), qaug, kb_all, kaug, vbT_all, past=past, tq=tq, tk=tk)
    ya = ya[:, :t].reshape(rows, D_BRANCH)
    yb = yb[:, :t].reshape(rows, D_BRANCH)

    x1 = _merge(x, ya, yb, cb, u, gl, past_mix, cw_mix, wbr, wo, g, mod,
                tm=cfg["tm_merge"], rows_per_batch=t)
    x2, tails = _conv_ffn(x1, g, mod, w_up, w_down, cw_ffn, past_ffn,
                          tm=cfg["tm_ffn"], rows_per_batch=t)

    new_mix = u.reshape(batch, t, D_BRANCH)[:, t - (CONV_W - 1):]
    nb = max(1, cfg["tm_ffn"] // t)
    if nb == 1:
        tiles_per_batch = t // cfg["tm_ffn"]
        last = tails.reshape(batch, tiles_per_batch, 8, -1)[:, -1]
    else:
        last = tails.reshape(batch, 8, -1)
    new_ffn = last[:, 8 - (CONV_W - 1):]
    hd = (batch, t, N_HEADS, HEAD_DIM)
    state = (ki32.reshape(batch, t, IDX_DIM), ka32.reshape(hd), va32.reshape(hd),
             kb32.reshape(hd), vb32.reshape(hd), logf.reshape(batch, t, N_HEADS), new_mix, new_ffn)
    return x2, state


_PROMPT_CFG = dict(tm_in=512, tm_merge=256, tm_ffn=256, tq=256, tk=4096)
_SAMPLE_CFG = dict(tm_in=256, tm_merge=256, tm_ffn=256, tq=128, tk=3072)


def kernel(x_prompt, x_sample, c_prompt, c_sample, cache_idx_k, cache_dsa_k, cache_dsa_v,
           cache_fox_k, cache_fox_v, cache_fox_logf, state_conv_mix, state_conv_ffn,
           w_ada, b_ada, norm_g, w_in, b_forget, conv_mix_w, w_branch, w_out, w_up,
           conv_ffn_w, w_down):
    bp, tp, _ = x_prompt.shape
    bs, ts, _ = x_sample.shape
    past = cache_dsa_k.shape[2]
    depth = w_ada.shape[0]

    c_all = jnp.concatenate([c_prompt, c_sample], axis=0)
    pad_rows = _round_up(c_all.shape[0], 8) - c_all.shape[0]
    c_all = jnp.pad(c_all, ((0, pad_rows), (0, 0)))
    mod_all = _modulation(c_all, w_ada, b_ada).reshape(depth, -1, 6, D_MODEL)

    yp = x_prompt.reshape(bp * tp, D_MODEL)
    ys = x_sample.reshape(bs * ts, D_MODEL)
    p_states, s_states = [], []
    for l in range(depth):
        w_main, w_small, bf_pad = _relayout_w_in(w_in[l], b_forget[l])
        weights = (norm_g[l], w_main, w_small, bf_pad, conv_mix_w[l], w_branch[l].astype(BF16),
                   w_out[l].astype(BF16), w_up[l].astype(BF16), conv_ffn_w[l],
                   w_down[l].astype(BF16))
        yp, st_p = _layer(yp, mod_all[l, :bp], None, weights, batch=bp, t=tp, past=0,
                          cfg=_PROMPT_CFG)
        caches = (cache_idx_k[l], cache_dsa_k[l], cache_dsa_v[l], cache_fox_k[l], cache_fox_v[l],
                  cache_fox_logf[l], state_conv_mix[l], state_conv_ffn[l])
        ys, st_s = _layer(ys, mod_all[l, bp:bp + bs], caches, weights, batch=bs, t=ts, past=past,
                          cfg=_SAMPLE_CFG)
        p_states.append(st_p)
        s_states.append(st_s)

    stack = lambda states: [jnp.stack([st[k] for st in states], axis=0) for k in range(8)]
    return (yp.reshape(bp, tp, D_MODEL), ys.reshape(bs, ts, D_MODEL), *stack(p_states),
            *stack(s_states))
```

```python
import functools

import jax
import jax.numpy as jnp
from jax import lax
from jax.experimental import pallas as pl
from jax.experimental.pallas import tpu as pltpu

F32 = jnp.float32
BF16 = jnp.bfloat16
I32 = jnp.int32

D_MODEL = 1024
HEAD_DIM = 64
D_BRANCH = 512
N_HEADS = 8
IDX_DIM = 64
CHUNK = 64
CHUNK_SHIFT = 6
TOPK_MAX = 256
CONV_W = 3
EPS = 1e-6
NEG_INF = -1e30
LOG2E = 1.4426950408889634
F32_MAX = 3.4028234663852886e38

LANES = 128
N_MAIN = 16 * D_BRANCH
KEY_CHUNK = 256
KEY_PAIR = 2 * KEY_CHUNK
KEY_QUAD = 2 * KEY_PAIR
SCORE_ROWS = 128
COUNT_ROWS = KEY_PAIR
GROUPS = TOPK_MAX
FLOAT_STEPS = 6
VMEM_LIMIT = 56 * 1024 * 1024


def _cparams(sem):
    return pltpu.CompilerParams(dimension_semantics=sem, vmem_limit_bytes=VMEM_LIMIT)


def _rms(x, g_row):
    return x * lax.rsqrt(jnp.mean(x * x, axis=-1, keepdims=True) + EPS) * g_row


def _group(x, nb):
    return x.reshape(nb, x.shape[0] // nb, x.shape[1])


def _mod_kernel(c_ref, w_ref, b_ref, o_ref):
    c = c_ref[...]
    s = (c * jax.nn.sigmoid(c)).astype(BF16)
    o_ref[0] = jnp.dot(s, w_ref[0].astype(BF16), preferred_element_type=F32) + b_ref[0]


def _modulation(c_all, w_ada, b_ada):
    depth = w_ada.shape[0]
    rows = c_all.shape[0]
    n = w_ada.shape[2]
    tn = D_MODEL
    return pl.pallas_call(
        _mod_kernel,
        out_shape=jax.ShapeDtypeStruct((depth, rows, n), F32),
        grid=(depth, n // tn),
        in_specs=[
            pl.BlockSpec((rows, D_MODEL), lambda l, j: (0, 0)),
            pl.BlockSpec((1, D_MODEL, tn), lambda l, j: (l, 0, j)),
            pl.BlockSpec((1, 1, tn), lambda l, j: (l, 0, j)),
        ],
        out_specs=pl.BlockSpec((1, rows, tn), lambda l, j: (l, 0, j)),
        compiler_params=_cparams(("arbitrary", "arbitrary")),
        name="adaln_mod",
    )(c_all, w_ada, b_ada.reshape(depth, 1, n))


IN_STEP = 2 * D_BRANCH
_GL_STEP = 5
_WI_OFF = IDX_DIM
_FL_OFF = IDX_DIM + N_HEADS


def _inproj_kernel(x_ref, mod_ref, g_ref, wm_ref, ws_ref, bf_ref,
                   qaT_ref, ka32_ref, ka16_ref, va32_ref, vaT_ref, qiT_ref, qbT_ref,
                   kb32_ref, kb16_ref, vb32_ref, vbT_ref, cb_ref, u_ref, gl_ref,
                   ki32_ref, ki16_ref, wiT_ref, logf_ref,
                   hs_ref, *, nb):
    j = pl.program_id(1)
    tm = x_ref.shape[0]

    @pl.when(j == 0)
    def _():
        y = _rms(x_ref[...], g_ref[0:1, :])
        h = _group(y, nb) * (1.0 + mod_ref[:, 1:2, :]) + mod_ref[:, 0:1, :]
        hb = h.reshape(tm, D_MODEL).astype(BF16)
        hs_ref[...] = hb
        sm = jnp.dot(hb, ws_ref[...], preferred_element_type=F32)
        ki32_ref[...] = sm[:, :IDX_DIM]
        ki16_ref[...] = sm[:, :IDX_DIM].astype(BF16)
        wiT_ref[...] = sm.T[_WI_OFF:_WI_OFF + N_HEADS, :]
        z = sm + bf_ref[...]
        lf = -(jnp.maximum(-z, 0.0) + jnp.log1p(jnp.exp(-jnp.abs(z))))
        logf_ref[...] = lf[:, _FL_OFF:_FL_OFF + N_HEADS]

    acc = jnp.dot(hs_ref[...], wm_ref[...], preferred_element_type=F32)
    lo = acc[:, :D_BRANCH]
    hi = acc[:, D_BRANCH:]
    scale = LOG2E * HEAD_DIM ** -0.5

    def store_chunked_T(ref, a):
        aT = a.T.astype(BF16)
        for c in range(tm // KEY_CHUNK):
            ref[c] = aT[:, c * KEY_CHUNK:(c + 1) * KEY_CHUNK]

    @pl.when(j == 0)
    def _():
        qaT_ref[...] = (lo * scale).T.astype(BF16)
        ka32_ref[...] = hi
        ka16_ref[...] = hi.astype(BF16)

    @pl.when(j == 1)
    def _():
        va32_ref[...] = lo
        store_chunked_T(vaT_ref, lo)
        qiT_ref[...] = hi.T.astype(BF16)

    @pl.when(j == 2)
    def _():
        qbT_ref[...] = (lo * scale).T.astype(BF16)
        kb32_ref[...] = hi
        kb16_ref[...] = hi.astype(BF16)

    @pl.when(j == 3)
    def _():
        vb32_ref[...] = lo
        store_chunked_T(vbT_ref, lo)
        cb_ref[...] = hi

    @pl.when(j == 4)
    def _():
        u_ref[...] = lo * hi

    @pl.when(j >= _GL_STEP)
    def _():
        gl_ref[...] = acc


def _in_projection(x, mod, g, w_main, w_small, bf_pad, *, tm, rows_per_batch):
    rows = x.shape[0]
    nt = rows // tm
    nb = max(1, tm // rows_per_batch)
    nstep = N_MAIN // IN_STEP
    if nb == 1:
        mod_map = lambda i, j: ((i * tm) // rows_per_batch, 0, 0)
    else:
        mod_map = lambda i, j: (i, 0, 0)
    row_blk = lambda w: pl.BlockSpec((tm, w), lambda i, j: (i, 0))
    col_blk = lambda h: pl.BlockSpec((h, tm), lambda i, j: (0, i))
    chunkT = pl.BlockSpec((tm // KEY_CHUNK, D_BRANCH, KEY_CHUNK), lambda i, j: (i, 0, 0))
    f32o = lambda w: jax.ShapeDtypeStruct((rows, w), F32)
    b16o = lambda w: jax.ShapeDtypeStruct((rows, w), BF16)
    b16T = jax.ShapeDtypeStruct((D_BRANCH, rows), BF16)
    b16c = jax.ShapeDtypeStruct((rows // KEY_CHUNK, D_BRANCH, KEY_CHUNK), BF16)
    out_shape = (
        b16T, f32o(D_BRANCH), b16o(D_BRANCH), f32o(D_BRANCH), b16c, b16T, b16T,
        f32o(D_BRANCH), b16o(D_BRANCH), f32o(D_BRANCH), b16c, f32o(D_BRANCH), f32o(D_BRANCH),
        f32o(6 * D_BRANCH),
        f32o(IDX_DIM), b16o(IDX_DIM), jax.ShapeDtypeStruct((N_HEADS, rows), F32), f32o(N_HEADS),
    )
    out_specs = (
        col_blk(D_BRANCH), row_blk(D_BRANCH), row_blk(D_BRANCH), row_blk(D_BRANCH), chunkT,
        col_blk(D_BRANCH), col_blk(D_BRANCH),
        row_blk(D_BRANCH), row_blk(D_BRANCH), row_blk(D_BRANCH), chunkT, row_blk(D_BRANCH),
        row_blk(D_BRANCH),
        pl.BlockSpec((tm, IN_STEP), lambda i, j: (i, jnp.clip(j - _GL_STEP, 0, 2))),
        row_blk(IDX_DIM), row_blk(IDX_DIM), col_blk(N_HEADS), row_blk(N_HEADS),
    )
    return pl.pallas_call(
        functools.partial(_inproj_kernel, nb=nb),
        out_shape=out_shape,
        grid=(nt, nstep),
        in_specs=[
            pl.BlockSpec((tm, D_MODEL), lambda i, j: (i, 0)),
            pl.BlockSpec((nb, 6, D_MODEL), mod_map),
            pl.BlockSpec((4, D_MODEL), lambda i, j: (0, 0)),
            pl.BlockSpec((D_MODEL, IN_STEP), lambda i, j: (0, j)),
            pl.BlockSpec((D_MODEL, LANES), lambda i, j: (0, 0)),
            pl.BlockSpec((1, LANES), lambda i, j: (0, 0)),
        ],
        out_specs=out_specs,
        scratch_shapes=[pltpu.VMEM((tm, D_MODEL), BF16)],
        compiler_params=_cparams(("arbitrary", "arbitrary")),
        name="in_projection",
    )(x, mod, g, w_main, w_small, bf_pad)


def _cumsum_kernel(x_ref, o_ref):
    x = x_ref[0]
    n = x.shape[0]
    lane = lax.broadcasted_iota(I32, x.shape, 1)
    row = lax.broadcasted_iota(I32, x.shape, 0)
    s = N_HEADS
    while s < LANES:
        x = x + jnp.where(lane >= s, pltpu.roll(x, s, axis=1), 0.0)
        s *= 2
    t = jnp.where(lane >= LANES - N_HEADS, x, 0.0)
    s = N_HEADS
    while s < LANES:
        t = t + pltpu.roll(t, s, axis=1)
        s *= 2
    t = jnp.where(row >= 1, pltpu.roll(t, 1, axis=0), 0.0)
    s = 1
    while s < n:
        if s < 8:
            sh = jnp.where(row >= s, pltpu.roll(t, s, axis=0), 0.0)
        else:
            sh = jnp.concatenate([jnp.zeros((s, LANES), F32), t[:n - s]], axis=0)
        t = t + sh
        s *= 2
    o_ref[0] = x + t


def _forget_cumsum(logf_all):
    b, lp, h = logf_all.shape
    n = lp * h // LANES
    out = pl.pallas_call(
        _cumsum_kernel,
        out_shape=jax.ShapeDtypeStruct((b, n, LANES), F32),
        grid=(b,),
        in_specs=[pl.BlockSpec((1, n, LANES), lambda i: (i, 0, 0))],
        out_specs=pl.BlockSpec((1, n, LANES), lambda i: (i, 0, 0)),
        compiler_params=_cparams(("arbitrary",)),
        name="forget_cumsum",
    )(logf_all.reshape(b, n, LANES))
    return out.reshape(b, lp, h)


def _pair_padded(qT_ref, h):
    pr, half = divmod(h, 2)
    blk = qT_ref[0, pr * LANES + half * HEAD_DIM:pr * LANES + (half + 1) * HEAD_DIM, :]
    z = jnp.zeros_like(blk)
    return jnp.concatenate([blk, z] if half == 0 else [z, blk], axis=0)


def _rows_to_heads(rows):
    tq = rows[0].shape[1]
    sub = lax.broadcasted_iota(I32, (N_HEADS, tq), 0)
    out = jnp.broadcast_to(rows[0], (N_HEADS, tq))
    for h in range(1, N_HEADS):
        out = jnp.where(sub == h, rows[h], out)
    return out


def _stage_and_consume(stage, consume, m_cur, pend, l, acc_ref):
    m_next, sums, alphas = [], [], []
    for h in range(N_HEADS):
        if stage is not None:
            logits, mask, buf = stage
            lg = logits(h)
            if mask is not None:
                lg = jnp.where(mask, lg, NEG_INF)
            buf[h] = lg
            m_next.append(jnp.maximum(m_cur[h:h + 1, :], jnp.max(lg, axis=0, keepdims=True)))
        if consume is not None:
            v_t, buf = consume
            before, after = pend
            alpha = jnp.exp2(before[h:h + 1, :] - after[h:h + 1, :])
            p = jnp.exp2(buf[h] - after[h:h + 1, :])
            pv = jnp.dot(v_t(h), p.astype(BF16), preferred_element_type=F32)
            rows = slice(h * HEAD_DIM, (h + 1) * HEAD_DIM)
            acc_ref[rows, :] = alpha * acc_ref[rows, :] + pv
            sums.append(jnp.sum(p, axis=0, keepdims=True))
            alphas.append(alpha)
    if consume is not None:
        l = _rows_to_heads(alphas) * l + _rows_to_heads(sums)
    return (_rows_to_heads(m_next) if stage is not None else m_cur), l


def _attend_block(n_free, n_used, logits, mask, v_t, m_ref, l_ref, acc_ref, buf0, buf1):
    m0 = m_ref[...]
    buf1[...] = jnp.full(buf1.shape, -jnp.inf, F32)

    def quad(masked, qd, carry):
        m_a, m_b, l = carry
        p0 = 2 * qd
        p1 = p0 + 1
        pp = jnp.maximum(p0 - 1, 0)
        mk = (lambda p: mask(p)) if masked else (lambda p: None)
        m_c, l = _stage_and_consume((lambda h: logits(h, p0), mk(p0), buf0),
                                    (lambda h: v_t(h, pp), buf1), m_b, (m_a, m_b), l, acc_ref)
        m_d, l = _stage_and_consume((lambda h: logits(h, p1), mk(p1), buf1),
                                    (lambda h: v_t(h, p0), buf0), m_c, (m_b, m_c), l, acc_ref)
        return m_c, m_d, l

    carry = lax.fori_loop(0, n_free, functools.partial(quad, False), (m0, m0, l_ref[...]))
    m_c, m_d, l = lax.fori_loop(n_free, n_used, functools.partial(quad, True), carry)
    last = 2 * n_used - 1
    _, l = _stage_and_consume(None, (lambda h: v_t(h, last), buf1), m_d, (m_c, m_d), l, acc_ref)
    m_ref[...] = m_d
    l_ref[...] = l


def _softmax_init(m_ref, l_ref, acc_ref):
    m_ref[...] = jnp.full(m_ref.shape, NEG_INF, F32)
    l_ref[...] = jnp.zeros(l_ref.shape, F32)
    acc_ref[...] = jnp.zeros(acc_ref.shape, F32)


def _softmax_finish(o_ref, l_ref, acc_ref):
    inv = 1.0 / l_ref[...]
    parts = [acc_ref[h * HEAD_DIM:(h + 1) * HEAD_DIM, :] * inv[h:h + 1, :] for h in range(N_HEADS)]
    o_ref[0] = jnp.concatenate(parts, axis=0).T.astype(o_ref.dtype)


def _f32_to_ordered(x):
    b = lax.bitcast_convert_type(x, I32)
    return jnp.where(b >= 0, b, b ^ jnp.int32(0x7FFFFFFF))


def _ordered_to_f32(o):
    return lax.bitcast_convert_type(jnp.where(o >= 0, o, o ^ jnp.int32(0x7FFFFFFF)), F32)


def _dsa_kernel(qiT_ref, wiT_ref, qaT_ref, ki_ref, ka_ref, vaT_ref, o_ref,
                s_ref, gm_ref, thr_ref, cut_ref, qp_ref, m_ref, l_ref, acc_ref, lg0_ref, lg1_ref,
                *, past, length, n_sel, tq, tk, nk):
    i = pl.program_id(1)
    j = pl.program_id(2)
    q0 = past + i * tq
    qpos = q0 + lax.broadcasted_iota(I32, (1, tq), 1)
    vis = jnp.minimum(((qpos >> CHUNK_SHIFT) + 1) * CHUNK, length)
    vmax = jnp.minimum((((q0 + tq - 1) >> CHUNK_SHIFT) + 1) * CHUNK, length)
    nvis = (vmax + tk - 1) // tk
    vcap = ((vmax + KEY_QUAD - 1) // KEY_QUAD) * KEY_QUAD

    def rows_here(jb, step):
        return jnp.clip((vcap - jb * tk) // step, 0, tk // step)

    @pl.when(jnp.logical_and(j < nk, j < nvis))
    def _():
        def body(c, carry):
            for half in range(GROUPS // SCORE_ROWS):
                r0 = pl.multiple_of(c * GROUPS + half * SCORE_ROWS, SCORE_ROWS)
                kblk = ki_ref[0, pl.ds(r0, SCORE_ROWS), :]
                acc = jnp.zeros((SCORE_ROWS, tq), F32)
                for h in range(N_HEADS):
                    d = jnp.dot(kblk, qiT_ref[0, h * IDX_DIM:(h + 1) * IDX_DIM, :],
                                preferred_element_type=F32)
                    acc = acc + jnp.maximum(d, 0.0) * wiT_ref[0, h:h + 1, :]
                kpos = j * tk + r0 + lax.broadcasted_iota(I32, (SCORE_ROWS, 1), 0)
                acc = jnp.where(kpos < vis, acc, -jnp.inf)
                s_ref[pl.ds(pl.multiple_of(j * tk + r0, SCORE_ROWS), SCORE_ROWS), :] = acc
                grp = slice(half * SCORE_ROWS, (half + 1) * SCORE_ROWS)
                gm_ref[grp, :] = jnp.maximum(gm_ref[grp, :], acc)
            return carry

        @pl.when(j == 0)
        def _():
            gm_ref[...] = jnp.full(gm_ref.shape, -jnp.inf, F32)

        lax.fori_loop(0, rows_here(j, GROUPS), body, 0)

    @pl.when(j == nk - 1)
    def _():
        nch = vcap // COUNT_ROWS

        def count(pred):
            sub = 64

            def body(c, accs):
                accs = list(accs)
                for k in range(COUNT_ROWS // sub):
                    r0 = pl.multiple_of(c * COUNT_ROWS + k * sub, sub)
                    hit = jnp.where(pred(s_ref[pl.ds(r0, sub), :], r0), 1.0, 0.0)
                    accs[k % 4] = accs[k % 4] + jnp.sum(hit.reshape(sub // 8, 8, tq), axis=0)
                return tuple(accs)
            zero = jnp.zeros((8, tq), F32)
            a0, a1, a2, a3 = lax.fori_loop(0, nch, body, (zero, zero, zero, zero))
            return jnp.sum((a0 + a1) + (a2 + a3), axis=0, keepdims=True)

        def count_ge(t):
            return count(lambda blk, r0: blk >= t)

        kf = float(n_sel)
        no_cut = jnp.full((1, tq), 2 ** 30, I32)

        gm = gm_ref[...]
        few = vis < n_sel
        lo0 = _f32_to_ordered(jnp.min(gm, axis=0, keepdims=True))
        hi0 = _f32_to_ordered(jnp.max(gm, axis=0, keepdims=True)) + 1

        def finished(lo, hi, c_lo):
            return jnp.logical_or(few, jnp.logical_or(c_lo == kf, hi <= lo + 1))

        def more(state):
            _, lo, hi, c_lo = state
            return jnp.max(jnp.where(finished(lo, hi, c_lo), 0.0, 1.0)) > 0.0

        def step(state):
            k, lo, hi, c_lo = state
            mid_f = _f32_to_ordered(0.5 * _ordered_to_f32(lo) + 0.5 * _ordered_to_f32(hi - 1))
            mid_i = (lo >> 1) + (hi >> 1) + (lo & hi & 1)
            piv = jnp.clip(jnp.where(k < FLOAT_STEPS, mid_f, mid_i), lo + 1, hi - 1)
            c = count_ge(_ordered_to_f32(piv))
            live = jnp.logical_not(finished(lo, hi, c_lo))
            up = jnp.logical_and(live, c >= kf)
            down = jnp.logical_and(live, c < kf)
            return (k + 1, jnp.where(up, piv, lo), jnp.where(down, piv, hi),
                    jnp.where(up, c, c_lo))

        state = (jnp.int32(0), lo0, hi0, jnp.full((1, tq), -1.0, F32))
        _, cur, _, c_cur = lax.while_loop(more, step, state)
        thr = jnp.where(few, -F32_MAX, _ordered_to_f32(cur))

        def with_ties():
            n_gt = count(lambda blk, r0: blk > thr)
            n_ge = count_ge(thr)
            need = kf - n_gt
            tied = (n_ge - n_gt) > need

            def tie_cut():
                rr = lax.broadcasted_iota(I32, (KEY_CHUNK, KEY_CHUNK), 0)
                cc = lax.broadcasted_iota(I32, (KEY_CHUNK, KEY_CHUNK), 1)
                lower = jnp.where(cc <= rr, 1.0, 0.0).astype(BF16)

                def body(c, carry):
                    run, last = carry
                    r0 = pl.multiple_of(c * KEY_CHUNK, KEY_CHUNK)
                    eq = s_ref[pl.ds(r0, KEY_CHUNK), :] == thr
                    ordinal = run + jnp.dot(lower, jnp.where(eq, 1.0, 0.0).astype(BF16),
                                            preferred_element_type=F32)
                    kpos = (r0 + lax.broadcasted_iota(I32, (KEY_CHUNK, 1), 0)).astype(F32)
                    take = jnp.logical_and(eq, ordinal <= need)
                    last = jnp.maximum(last, jnp.max(jnp.where(take, kpos, -1.0), axis=0,
                                                     keepdims=True))
                    return ordinal[KEY_CHUNK - 1:KEY_CHUNK, :], last

                init = (jnp.zeros((1, tq), F32), jnp.full((1, tq), -1.0, F32))
                _, last = lax.fori_loop(0, vcap // KEY_CHUNK, body, init)
                return jnp.where(tied, last.astype(I32), no_cut)

            return lax.cond(jnp.max(jnp.where(tied, 1.0, 0.0)) > 0.0, tie_cut, lambda: no_cut)

        unsure = jnp.logical_or(c_cur != kf, few)
        cut = lax.cond(jnp.max(jnp.where(unsure, 1.0, 0.0)) > 0.0, with_ties, lambda: no_cut)
        thr_ref[0:1, :] = thr
        cut_ref[0:1, :] = cut
        for h in range(N_HEADS):
            qp_ref[h] = _pair_padded(qaT_ref, h)
        _softmax_init(m_ref, l_ref, acc_ref)

    jj = j - nk

    @pl.when(jnp.logical_and(j >= nk, jj < nvis))
    def _():
        thr = thr_ref[0:1, :]
        cut = cut_ref[0:1, :]

        def selected(p):
            g0 = pl.multiple_of(jj * tk + p * KEY_PAIR, KEY_PAIR)
            sblk = s_ref[pl.ds(g0, KEY_PAIR), :]
            kpos = g0 + lax.broadcasted_iota(I32, (KEY_PAIR, 1), 0)
            return jnp.logical_or(sblk > thr, jnp.logical_and(sblk == thr, kpos <= cut))

        def logits(h, p):
            r0 = pl.multiple_of(p * KEY_PAIR, KEY_PAIR)
            kblk = ka_ref[0, pl.ds(r0, KEY_PAIR), (h // 2) * LANES:(h // 2 + 1) * LANES]
            return jnp.dot(kblk, qp_ref[h], preferred_element_type=F32)

        def v_t(h, p):
            rows = slice(h * HEAD_DIM, (h + 1) * HEAD_DIM)
            return jnp.concatenate([vaT_ref[0, 2 * p, rows, :], vaT_ref[0, 2 * p + 1, rows, :]],
                                   axis=1)

        _attend_block(0, rows_here(jj, KEY_QUAD), logits, selected, v_t, m_ref, l_ref, acc_ref,
                      lg0_ref, lg1_ref)

    @pl.when(j == 2 * nk - 1)
    def _():
        _softmax_finish(o_ref, l_ref, acc_ref)


def _dsa_attention(qiT, wiT, qaT, ki, ka, vaTc, *, past, length, tq, tk):
    b, _, tqp = qiT.shape
    lp = ki.shape[1]
    nq, nk = tqp // tq, lp // tk
    n_sel = min(TOPK_MAX, length // 4)

    def nvis(i):
        vmax = jnp.minimum(((past + (i + 1) * tq - 1) // CHUNK + 1) * CHUNK, length)
        return (vmax + tk - 1) // tk

    k1 = lambda bb, i, j: (bb, jnp.minimum(j, nvis(i) - 1), 0)
    k3 = lambda bb, i, j: (bb, jnp.clip(j - nk, 0, nvis(i) - 1), 0)
    k3c = lambda bb, i, j: (bb, jnp.clip(j - nk, 0, nvis(i) - 1), 0, 0)
    qmap = lambda bb, i, j: (bb, 0, i)
    kern = functools.partial(_dsa_kernel, past=past, length=length, n_sel=n_sel, tq=tq, tk=tk,
                             nk=nk)
    return pl.pallas_call(
        kern,
        out_shape=jax.ShapeDtypeStruct((b, tqp, D_BRANCH), BF16),
        grid=(b, nq, 2 * nk),
        in_specs=[
            pl.BlockSpec((1, D_BRANCH, tq), qmap),
            pl.BlockSpec((1, N_HEADS, tq), qmap),
            pl.BlockSpec((1, D_BRANCH, tq), qmap),
            pl.BlockSpec((1, tk, IDX_DIM), k1),
            pl.BlockSpec((1, tk, D_BRANCH), k3),
            pl.BlockSpec((1, tk // KEY_CHUNK, D_BRANCH, KEY_CHUNK), k3c),
        ],
        out_specs=pl.BlockSpec((1, tq, D_BRANCH), lambda bb, i, j: (bb, i, 0)),
        scratch_shapes=[
            pltpu.VMEM((lp, tq), F32),
            pltpu.VMEM((GROUPS, tq), F32),
            pltpu.VMEM((8, tq), F32),
            pltpu.VMEM((8, tq), I32),
            pltpu.VMEM((N_HEADS, LANES, tq), BF16),
            pltpu.VMEM((N_HEADS, tq), F32),
            pltpu.VMEM((N_HEADS, tq), F32),
            pltpu.VMEM((D_BRANCH, tq), F32),
            pltpu.VMEM((N_HEADS, KEY_PAIR, tq), F32),
            pltpu.VMEM((N_HEADS, KEY_PAIR, tq), F32),
        ],
        compiler_params=_cparams(("arbitrary", "arbitrary", "arbitrary")),
        name="dsa_attention",
    )(qiT, wiT, qaT, ki, ka, vaTc)


_AUG = 6


def _fox_kernel(qbT_ref, qaug_ref, kb_ref, kaug_ref, vbT_ref, o_ref,
                qf_ref, m_ref, l_ref, acc_ref, lg0_ref, lg1_ref, *, past, tq, tk, nk):
    i = pl.program_id(1)
    j = pl.program_id(2)
    q0 = past + i * tq
    qpos = q0 + lax.broadcasted_iota(I32, (1, tq), 1)
    nvis = (q0 + tq + tk - 1) // tk
    nquad = tk // KEY_QUAD
    n_free = jnp.clip((q0 + 1 - j * tk) // KEY_QUAD, 0, nquad)
    n_used = jnp.clip((q0 + tq - j * tk + KEY_QUAD - 1) // KEY_QUAD, 0, nquad)

    @pl.when(j == 0)
    def _():
        arow = lax.broadcasted_iota(I32, (LANES, 1), 0)
        qa = qaug_ref[0]
        for h in range(N_HEADS):
            mine = jnp.logical_and(arow >= _AUG * h, arow < _AUG * (h + 1))
            qf_ref[h, 0:LANES, :] = _pair_padded(qbT_ref, h)
            qf_ref[h, LANES:2 * LANES, :] = jnp.where(mine, qa, 0.0).astype(BF16)
        _softmax_init(m_ref, l_ref, acc_ref)

    @pl.when(j < nvis)
    def _():
        def causal(p):
            kpos = j * tk + p * KEY_PAIR + lax.broadcasted_iota(I32, (KEY_PAIR, 1), 0)
            return kpos <= qpos

        def logits(h, p):
            r0 = pl.multiple_of(p * KEY_PAIR, KEY_PAIR)
            kblk = kb_ref[0, pl.ds(r0, KEY_PAIR), (h // 2) * LANES:(h // 2 + 1) * LANES]
            lhs = jnp.concatenate([kblk, kaug_ref[0, pl.ds(r0, KEY_PAIR), :]], axis=1)
            return jnp.dot(lhs, qf_ref[h], preferred_element_type=F32)

        def v_t(h, p):
            rows = slice(h * HEAD_DIM, (h + 1) * HEAD_DIM)
            return jnp.concatenate([vbT_ref[0, 2 * p, rows, :], vbT_ref[0, 2 * p + 1, rows, :]],
                                   axis=1)

        _attend_block(n_free, n_used, logits, causal, v_t, m_ref, l_ref, acc_ref,
                      lg0_ref, lg1_ref)

    @pl.when(j == nk - 1)
    def _():
        _softmax_finish(o_ref, l_ref, acc_ref)


def _fox_attention(qbT, qaug, kb, kaug, vbTc, *, past, tq, tk):
    b, _, tqp = qbT.shape
    lp = kb.shape[1]
    nq, nk = tqp // tq, lp // tk

    def last(i):
        return jnp.minimum((past + (i + 1) * tq + tk - 1) // tk, nk) - 1

    kmap = lambda bb, i, j: (bb, jnp.minimum(j, last(i)), 0)
    kmapc = lambda bb, i, j: (bb, jnp.minimum(j, last(i)), 0, 0)
    qmap = lambda bb, i, j: (bb, 0, i)
    return pl.pallas_call(
        functools.partial(_fox_kernel, past=past, tq=tq, tk=tk, nk=nk),
        out_shape=jax.ShapeDtypeStruct((b, tqp, D_BRANCH), BF16),
        grid=(b, nq, nk),
        in_specs=[
            pl.BlockSpec((1, D_BRANCH, tq), qmap),
            pl.BlockSpec((1, LANES, tq), qmap),
            pl.BlockSpec((1, tk, D_BRANCH), kmap),
            pl.BlockSpec((1, tk, LANES), kmap),
            pl.BlockSpec((1, tk // KEY_CHUNK, D_BRANCH, KEY_CHUNK), kmapc),
        ],
        out_specs=pl.BlockSpec((1, tq, D_BRANCH), lambda bb, i, j: (bb, i, 0)),
        scratch_shapes=[
            pltpu.VMEM((N_HEADS, 2 * LANES, tq), BF16),
            pltpu.VMEM((N_HEADS, tq), F32),
            pltpu.VMEM((N_HEADS, tq), F32),
            pltpu.VMEM((D_BRANCH, tq), F32),
            pltpu.VMEM((N_HEADS, KEY_PAIR, tq), F32),
            pltpu.VMEM((N_HEADS, KEY_PAIR, tq), F32),
        ],
        compiler_params=_cparams(("arbitrary", "arbitrary", "arbitrary")),
        name="fox_attention",
    )(qbT, qaug, kb, kaug, vbTc)


AUG_ROWS = 512


def _aug_kernel(c_ref, kaug_ref, qaugT_ref):
    x = c_ref[0] * LOG2E
    a = x.astype(BF16).astype(F32)
    r = x - a
    b = r.astype(BF16).astype(F32)
    c = (r - b).astype(BF16).astype(F32)
    lane = lax.broadcasted_iota(I32, (N_HEADS, LANES), 1)
    head = lax.broadcasted_iota(I32, (N_HEADS, LANES), 0)

    def place(v, slot):
        sel = jnp.where(lane == _AUG * head + slot, 1.0, 0.0)
        return jnp.dot(v, sel, preferred_element_type=F32, precision=lax.Precision.HIGHEST)

    slot = lane - _AUG * head
    ones_k = jnp.sum(jnp.where(jnp.logical_and(slot >= 0, slot < 3), 1.0, 0.0), axis=0, keepdims=True)
    ones_q = jnp.sum(jnp.where(jnp.logical_and(slot >= 3, slot < 6), 1.0, 0.0), axis=0, keepdims=True)
    kaug_ref[0] = (ones_k - (place(a, 3) + place(b, 4) + place(c, 5))).astype(BF16)
    qaugT_ref[0] = (ones_q + (place(a, 0) + place(b, 1) + place(c, 2))).T


def _fox_augment(cum):
    b, lp, h = cum.shape
    return pl.pallas_call(
        _aug_kernel,
        out_shape=(jax.ShapeDtypeStruct((b, lp, LANES), BF16),
                   jax.ShapeDtypeStruct((b, LANES, lp), F32)),
        grid=(b, lp // AUG_ROWS),
        in_specs=[pl.BlockSpec((1, AUG_ROWS, h), lambda i, j: (i, j, 0))],
        out_specs=(pl.BlockSpec((1, AUG_ROWS, LANES), lambda i, j: (i, j, 0)),
                   pl.BlockSpec((1, LANES, AUG_ROWS), lambda i, j: (i, 0, j))),
        compiler_params=_cparams(("arbitrary", "arbitrary")),
        name="fox_augment",
    )(cum)


def _shifted_rows(buf, load, store, u, past_ref, starts_batch, nb):
    tm, c = u.shape
    store(slice(8, 8 + tm), u)
    if nb == 1:
        @pl.when(starts_batch)
        def _():
            store(slice(6, 8), past_ref[0])
    s1 = load(slice(7, 7 + tm))
    s2 = load(slice(6, 6 + tm))
    if nb > 1:
        r = lax.broadcasted_iota(I32, (nb, tm // nb, 1), 1)
        p0 = past_ref[:, 0:1, :]
        p1 = past_ref[:, 1:2, :]
        s1 = jnp.where(r == 0, p1, _group(s1, nb)).reshape(tm, c)
        s2 = jnp.where(r == 0, p0, jnp.where(r == 1, p1, _group(s2, nb))).reshape(tm, c)
    store(slice(0, 8), load(slice(tm, tm + 8)))
    return s1, s2


def _merge_kernel(x_ref, ya_ref, yb_ref, cb_ref, u_ref, gl_ref, past_ref, cw_ref, wbr_ref,
                  wo_ref, g_ref, mod_ref, o_ref, ub_ref, *, nb, rows_per_batch):
    i = pl.program_id(0)
    tm = x_ref.shape[0]
    u = u_ref[...]

    def load(rows):
        return ub_ref[rows, :]

    def store(rows, v):
        ub_ref[rows, :] = v

    s1, s2 = _shifted_rows(ub_ref, load, store, u, past_ref, (i * tm) % rows_per_batch == 0, nb)
    conv = cw_ref[0:1, :] * s2 + cw_ref[1:2, :] * s1 + cw_ref[2:3, :] * u
    yc = (cb_ref[...] * conv).astype(BF16)
    mix = jnp.zeros((tm, D_MODEL), F32)
    for n, y in enumerate((ya_ref[...], yb_ref[...], yc)):
        br = jnp.dot(y, wbr_ref[n], preferred_element_type=F32)
        mix = mix + jax.nn.sigmoid(gl_ref[:, n * D_MODEL:(n + 1) * D_MODEL]) * br
    mo = jnp.dot(mix.astype(BF16), wo_ref[...], preferred_element_type=F32)
    nm = _group(_rms(mo, g_ref[1:2, :]), nb)
    o_ref[...] = x_ref[...] + (mod_ref[:, 2:3, :] * nm).reshape(tm, D_MODEL)


def _mod_spec(tm, nb, rows_per_batch, ngrid):
    if nb == 1:
        f = lambda i, *_: ((i * tm) // rows_per_batch, 0, 0)
    else:
        f = lambda i, *_: (i, 0, 0)
    return pl.BlockSpec((nb, 6, D_MODEL), f)


def _merge(x, ya, yb, cb, u, gl, past, cw, wbr, wo, g, mod, *, tm, rows_per_batch):
    rows = x.shape[0]
    nb = max(1, tm // rows_per_batch)
    row = lambda w: pl.BlockSpec((tm, w), lambda i: (i, 0))
    full = lambda shape: pl.BlockSpec(shape, lambda i: (0,) * len(shape))
    if nb == 1:
        past_spec = pl.BlockSpec((1, 2, D_BRANCH), lambda i: ((i * tm) // rows_per_batch, 0, 0))
    else:
        past_spec = pl.BlockSpec((nb, 2, D_BRANCH), lambda i: (i, 0, 0))
    return pl.pallas_call(
        functools.partial(_merge_kernel, nb=nb, rows_per_batch=rows_per_batch),
        out_shape=jax.ShapeDtypeStruct((rows, D_MODEL), F32),
        grid=(rows // tm,),
        in_specs=[
            row(D_MODEL), row(D_BRANCH), row(D_BRANCH), row(D_BRANCH), row(D_BRANCH),
            row(3 * D_MODEL), past_spec, full((CONV_W, D_BRANCH)),
            full((3, D_BRANCH, D_MODEL)), full((D_MODEL, D_MODEL)), full((4, D_MODEL)),
            _mod_spec(tm, nb, rows_per_batch, 1),
        ],
        out_specs=row(D_MODEL),
        scratch_shapes=[pltpu.VMEM((tm + 8, D_BRANCH), F32)],
        compiler_params=_cparams(("arbitrary",)),
        name="branch_merge",
    )(x, ya, yb, cb, u, gl, past, cw, wbr, wo, g, mod)


FF_CHUNK = 1408


def _ffn_kernel(x_ref, g_ref, mod_ref, wg_ref, wv_ref, wd_ref, cw_ref, past_ref,
                o_ref, tail_ref, hs_ref, acc_ref, ub_ref, *, nb, rows_per_batch, nj):
    i = pl.program_id(0)
    j = pl.program_id(1)
    tm = x_ref.shape[0]

    @pl.when(j == 0)
    def _():
        y = _group(_rms(x_ref[...], g_ref[2:3, :]), nb)
        h = y * (1.0 + mod_ref[:, 4:5, :]) + mod_ref[:, 3:4, :]
        hs_ref[...] = h.reshape(tm, D_MODEL).astype(BF16)
        acc_ref[...] = jnp.zeros(acc_ref.shape, F32)

    hs = hs_ref[...]
    ug = jnp.dot(hs, wg_ref[...], preferred_element_type=F32)
    uv = jnp.dot(hs, wv_ref[...], preferred_element_type=F32)

    def load(rows):
        return ub_ref[j, rows, :]

    def store(rows, v):
        ub_ref[j, rows, :] = v

    s1, s2 = _shifted_rows(ub_ref, load, store, ug, past_ref, (i * tm) % rows_per_batch == 0, nb)
    conv = cw_ref[0:1, :] * s2 + cw_ref[1:2, :] * s1 + cw_ref[2:3, :] * ug
    f = conv * jax.nn.sigmoid(conv) * uv
    acc_ref[...] += jnp.dot(f.astype(BF16), wd_ref[...], preferred_element_type=F32)
    grp = tm // nb
    tail_ref[0] = _group(ug, nb)[:, grp - 8:, :]

    @pl.when(j == nj - 1)
    def _():
        nm = _group(_rms(acc_ref[...], g_ref[3:4, :]), nb)
        o_ref[...] = x_ref[...] + (mod_ref[:, 5:6, :] * nm).reshape(tm, D_MODEL)


def _conv_ffn(x, g, mod, w_up, w_down, cw, past, *, tm, rows_per_batch):
    rows = x.shape[0]
    dff = w_down.shape[0]
    nj = dff // FF_CHUNK
    nb = max(1, tm // rows_per_batch)
    nt = rows // tm
    if nb == 1:
        past_spec = pl.BlockSpec((1, 2, FF_CHUNK), lambda i, j: ((i * tm) // rows_per_batch, 0, j))
        mod_map = lambda i, j: ((i * tm) // rows_per_batch, 0, 0)
    else:
        past_spec = pl.BlockSpec((nb, 2, FF_CHUNK), lambda i, j: (i, 0, j))
        mod_map = lambda i, j: (i, 0, 0)
    out, tails = pl.pallas_call(
        functools.partial(_ffn_kernel, nb=nb, rows_per_batch=rows_per_batch, nj=nj),
        out_shape=(jax.ShapeDtypeStruct((rows, D_MODEL), F32),
                   jax.ShapeDtypeStruct((nt, nb, 8, dff), F32)),
        grid=(nt, nj),
        in_specs=[
            pl.BlockSpec((tm, D_MODEL), lambda i, j: (i, 0)),
            pl.BlockSpec((4, D_MODEL), lambda i, j: (0, 0)),
            pl.BlockSpec((nb, 6, D_MODEL), mod_map),
            pl.BlockSpec((D_MODEL, FF_CHUNK), lambda i, j: (0, j)),
            pl.BlockSpec((D_MODEL, FF_CHUNK), lambda i, j: (0, j + nj)),
            pl.BlockSpec((FF_CHUNK, D_MODEL), lambda i, j: (j, 0)),
            pl.BlockSpec((CONV_W, FF_CHUNK), lambda i, j: (0, j)),
            past_spec,
        ],
        out_specs=(pl.BlockSpec((tm, D_MODEL), lambda i, j: (i, 0)),
                   pl.BlockSpec((1, nb, 8, FF_CHUNK), lambda i, j: (i, 0, 0, j))),
        scratch_shapes=[
            pltpu.VMEM((tm, D_MODEL), BF16),
            pltpu.VMEM((tm, D_MODEL), F32),
            pltpu.VMEM((nj, tm + 8, FF_CHUNK), F32),
        ],
        compiler_params=_cparams(("arbitrary", "arbitrary")),
        name="conv_ffn",
    )(x, g, mod, w_up, w_up, w_down, cw, past)
    return out, tails


def _relayout_w_in(w_in, b_forget):
    db = D_BRANCH
    o_qi = 3 * db
    o_ki = o_qi + N_HEADS * IDX_DIM
    o_wi = o_ki + IDX_DIM
    o_qb = o_wi + N_HEADS
    o_fl = o_qb + 3 * db
    o_cb = o_fl + N_HEADS
    o_gl = o_cb + 3 * db
    main = jnp.concatenate([w_in[:, 0:o_ki], w_in[:, o_qb:o_fl], w_in[:, o_cb:]], axis=1)
    small = jnp.concatenate([w_in[:, o_ki:o_wi], w_in[:, o_wi:o_qb], w_in[:, o_fl:o_cb]], axis=1)
    small = jnp.pad(small, ((0, 0), (0, LANES - small.shape[1])))
    bf = jnp.zeros((1, LANES), F32).at[0, _FL_OFF:_FL_OFF + N_HEADS].set(b_forget)
    del o_gl
    return main.astype(BF16), small.astype(BF16), bf


def _round_up(x, m):
    return (x + m - 1) // m * m


def _layer(x, mod, caches, weights, *, batch, t, past, cfg):
    (g, w_main, w_small, bf_pad, cw_mix, wbr, wo, w_up, cw_ffn, w_down) = weights
    rows = batch * t
    length = past + t
    (qaT, ka32, ka16, va32, vaTc, qiT, qbT, kb32, kb16, vb32, vbTc, cb, u, gl,
     ki32, ki16, wiT, logf) = _in_projection(
        x, mod, g, w_main, w_small, bf_pad, tm=cfg["tm_in"], rows_per_batch=t)

    tq, tk = cfg["tq"], cfg["tk"]
    tqp = _round_up(t, tq)
    lp = _round_up(length, tk)

    def per_batch_T(aT):
        c = aT.shape[0]
        a = jnp.swapaxes(aT.reshape(c, batch, t), 0, 1)
        return jnp.pad(a, ((0, 0), (0, 0), (0, tqp - t)))

    def chunked_T(a):
        return jnp.swapaxes(a.reshape(batch, lp // KEY_CHUNK, KEY_CHUNK, a.shape[-1]), 2, 3)

    def with_cache(cache, new, dtype):
        new = new.reshape(batch, t, -1)
        if cache is not None:
            new = jnp.concatenate([cache.reshape(batch, past, -1).astype(dtype), new.astype(dtype)],
                                  axis=1)
        return jnp.pad(new.astype(dtype), ((0, 0), (0, lp - length), (0, 0)))

    if caches is None:
        c_idx = c_dk = c_dv = c_fk = c_fv = c_lf = None
        past_mix = jnp.zeros((batch, CONV_W - 1, D_BRANCH), F32)
        past_ffn = jnp.zeros((batch, CONV_W - 1, w_down.shape[0]), F32)
    else:
        c_idx, c_dk, c_dv, c_fk, c_fv, c_lf, past_mix, past_ffn = caches

    ki_all = with_cache(c_idx, ki16, BF16)
    ka_all = with_cache(c_dk, ka16, BF16)
    kb_all = with_cache(c_fk, kb16, BF16)
    if caches is None and lp == rows:
        vaT_all = vaTc.reshape(batch, lp // KEY_CHUNK, D_BRANCH, KEY_CHUNK)
        vbT_all = vbTc.reshape(batch, lp // KEY_CHUNK, D_BRANCH, KEY_CHUNK)
    else:
        vaT_all = chunked_T(with_cache(c_dv, va32, BF16))
        vbT_all = chunked_T(with_cache(c_fv, vb32, BF16))
    logf_all = with_cache(c_lf, logf, F32)

    ya = _dsa_attention(per_batch_T(qiT), per_batch_T(wiT), per_batch_T(qaT), ki_all, ka_all,
                        vaT_all, past=past, length=length, tq=tq, tk=tk)
    cum = _forget_cumsum(logf_all)
    kaug, qaug = _fox_augment(cum)
    qaug = jnp.pad(qaug[:, :, past:length], ((0, 0), (0, 0), (0, tqp - t)))
    yb = _fox_attention(per_batch_T(qbT), qaug, kb_all, kaug, vbT_all, past=past, tq=tq, tk=tk)
    ya = ya[:, :t].reshape(rows, D_BRANCH)
    yb = yb[:, :t].reshape(rows, D_BRANCH)

    x1 = _merge(x, ya, yb, cb, u, gl, past_mix, cw_mix, wbr, wo, g, mod,
                tm=cfg["tm_merge"], rows_per_batch=t)
    x2, tails = _conv_ffn(x1, g, mod, w_up, w_down, cw_ffn, past_ffn,
                          tm=cfg["tm_ffn"], rows_per_batch=t)

    new_mix = u.reshape(batch, t, D_BRANCH)[:, t - (CONV_W - 1):]
    nb = max(1, cfg["tm_ffn"] // t)
    if nb == 1:
        tiles_per_batch = t // cfg["tm_ffn"]
        last = tails.reshape(batch, tiles_per_batch, 8, -1)[:, -1]
    else:
        last = tails.reshape(batch, 8, -1)
    new_ffn = last[:, 8 - (CONV_W - 1):]
    hd = (batch, t, N_HEADS, HEAD_DIM)
    state = (ki32.reshape(batch, t, IDX_DIM), ka32.reshape(hd), va32.reshape(hd),
             kb32.reshape(hd), vb32.reshape(hd), logf.reshape(batch, t, N_HEADS), new_mix, new_ffn)
    return x2, state


_PROMPT_CFG = dict(tm_in=512, tm_merge=256, tm_ffn=256, tq=256, tk=4096)
_SAMPLE_CFG = dict(tm_in=256, tm_merge=256, tm_ffn=256, tq=128, tk=3072)


def kernel(x_prompt, x_sample, c_prompt, c_sample, cache_idx_k, cache_dsa_k, cache_dsa_v,
           cache_fox_k, cache_fox_v, cache_fox_logf, state_conv_mix, state_conv_ffn,
           w_ada, b_ada, norm_g, w_in, b_forget, conv_mix_w, w_branch, w_out, w_up,
           conv_ffn_w, w_down):
    bp, tp, _ = x_prompt.shape
    bs, ts, _ = x_sample.shape
    past = cache_dsa_k.shape[2]
    depth = w_ada.shape[0]

    c_all = jnp.concatenate([c_prompt, c_sample], axis=0)
    pad_rows = _round_up(c_all.shape[0], 8) - c_all.shape[0]
    c_all = jnp.pad(c_all, ((0, pad_rows), (0, 0)))
    mod_all = _modulation(c_all, w_ada, b_ada).reshape(depth, -1, 6, D_MODEL)

    yp = x_prompt.reshape(bp * tp, D_MODEL)
    ys = x_sample.reshape(bs * ts, D_MODEL)
    p_states, s_states = [], []
    for l in range(depth):
        w_main, w_small, bf_pad = _relayout_w_in(w_in[l], b_forget[l])
        weights = (norm_g[l], w_main, w_small, bf_pad, conv_mix_w[l], w_branch[l].astype(BF16),
                   w_out[l].astype(BF16), w_up[l].astype(BF16), conv_ffn_w[l],
                   w_down[l].astype(BF16))
        yp, st_p = _layer(yp, mod_all[l, :bp], None, weights, batch=bp, t=tp, past=0,
                          cfg=_PROMPT_CFG)
        caches = (cache_idx_k[l], cache_dsa_k[l], cache_dsa_v[l], cache_fox_k[l], cache_fox_v[l],
                  cache_fox_logf[l], state_conv_mix[l], state_conv_ffn[l])
        ys, st_s = _layer(ys, mod_all[l, bp:bp + bs], caches, weights, batch=bs, t=ts, past=past,
                          cfg=_SAMPLE_CFG)
        p_states.append(st_p)
        s_states.append(st_s)

    stack = lambda states: [jnp.stack([st[k] for st in states], axis=0) for k in range(8)]
    return (yp.reshape(bp, tp, D_MODEL), ys.reshape(bs, ts, D_MODEL), *stack(p_states),
            *stack(s_states))
```

```python
import functools

import jax
import jax.numpy as jnp
from jax import lax
from jax.experimental import pallas as pl
from jax.experimental.pallas import tpu as pltpu

F32 = jnp.float32
BF16 = jnp.bfloat16
I32 = jnp.int32

D_MODEL = 1024
HEAD_DIM = 64
D_BRANCH = 512
N_HEADS = 8
IDX_DIM = 64
CHUNK = 64
CHUNK_SHIFT = 6
TOPK_MAX = 256
CONV_W = 3
EPS = 1e-6
NEG_INF = -1e30
LOG2E = 1.4426950408889634
F32_MAX = 3.4028234663852886e38

LANES = 128
N_MAIN = 16 * D_BRANCH
KEY_CHUNK = 256
KEY_PAIR = 2 * KEY_CHUNK
KEY_QUAD = 2 * KEY_PAIR
SCORE_ROWS = 128
COUNT_ROWS = KEY_PAIR
GROUPS = TOPK_MAX
FLOAT_STEPS = 12
VMEM_LIMIT = 56 * 1024 * 1024


def _cparams(sem):
    return pltpu.CompilerParams(dimension_semantics=sem, vmem_limit_bytes=VMEM_LIMIT)


def _rms(x, g_row):
    return x * lax.rsqrt(jnp.mean(x * x, axis=-1, keepdims=True) + EPS) * g_row


def _group(x, nb):
    return x.reshape(nb, x.shape[0] // nb, x.shape[1])


def _mod_kernel(c_ref, w_ref, b_ref, o_ref):
    c = c_ref[...]
    s = (c * jax.nn.sigmoid(c)).astype(BF16)
    o_ref[0] = jnp.dot(s, w_ref[0].astype(BF16), preferred_element_type=F32) + b_ref[0]


def _modulation(c_all, w_ada, b_ada):
    depth = w_ada.shape[0]
    rows = c_all.shape[0]
    n = w_ada.shape[2]
    tn = D_MODEL
    return pl.pallas_call(
        _mod_kernel,
        out_shape=jax.ShapeDtypeStruct((depth, rows, n), F32),
        grid=(depth, n // tn),
        in_specs=[
            pl.BlockSpec((rows, D_MODEL), lambda l, j: (0, 0)),
            pl.BlockSpec((1, D_MODEL, tn), lambda l, j: (l, 0, j)),
            pl.BlockSpec((1, 1, tn), lambda l, j: (l, 0, j)),
        ],
        out_specs=pl.BlockSpec((1, rows, tn), lambda l, j: (l, 0, j)),
        compiler_params=_cparams(("arbitrary", "arbitrary")),
        name="adaln_mod",
    )(c_all, w_ada, b_ada.reshape(depth, 1, n))


IN_STEP = 2 * D_BRANCH
_GL_STEP = 5
_WI_OFF = IDX_DIM
_FL_OFF = IDX_DIM + N_HEADS


def _inproj_kernel(x_ref, mod_ref, g_ref, wm_ref, ws_ref, bf_ref,
                   qaT_ref, ka32_ref, ka16_ref, va32_ref, vaT_ref, qiT_ref, qbT_ref,
                   kb32_ref, kb16_ref, vb32_ref, vbT_ref, cb_ref, u_ref, gl_ref,
                   ki32_ref, ki16_ref, wiT_ref, logf_ref,
                   hs_ref, *, nb):
    j = pl.program_id(1)
    tm = x_ref.shape[0]

    @pl.when(j == 0)
    def _():
        y = _rms(x_ref[...], g_ref[0:1, :])
        h = _group(y, nb) * (1.0 + mod_ref[:, 1:2, :]) + mod_ref[:, 0:1, :]
        hb = h.reshape(tm, D_MODEL).astype(BF16)
        hs_ref[...] = hb
        sm = jnp.dot(hb, ws_ref[...], preferred_element_type=F32)
        ki32_ref[...] = sm[:, :IDX_DIM]
        ki16_ref[...] = sm[:, :IDX_DIM].astype(BF16)
        wiT_ref[...] = sm.T[_WI_OFF:_WI_OFF + N_HEADS, :]
        z = sm + bf_ref[...]
        lf = -(jnp.maximum(-z, 0.0) + jnp.log1p(jnp.exp(-jnp.abs(z))))
        logf_ref[...] = lf[:, _FL_OFF:_FL_OFF + N_HEADS]

    acc = jnp.dot(hs_ref[...], wm_ref[...], preferred_element_type=F32)
    lo = acc[:, :D_BRANCH]
    hi = acc[:, D_BRANCH:]
    scale = LOG2E * HEAD_DIM ** -0.5

    def store_chunked_T(ref, a):
        aT = a.T.astype(BF16)
        for c in range(tm // KEY_CHUNK):
            ref[c] = aT[:, c * KEY_CHUNK:(c + 1) * KEY_CHUNK]

    @pl.when(j == 0)
    def _():
        qaT_ref[...] = (lo * scale).T.astype(BF16)
        ka32_ref[...] = hi
        ka16_ref[...] = hi.astype(BF16)

    @pl.when(j == 1)
    def _():
        va32_ref[...] = lo
        store_chunked_T(vaT_ref, lo)
        qiT_ref[...] = hi.T.astype(BF16)

    @pl.when(j == 2)
    def _():
        qbT_ref[...] = (lo * scale).T.astype(BF16)
        kb32_ref[...] = hi
        kb16_ref[...] = hi.astype(BF16)

    @pl.when(j == 3)
    def _():
        vb32_ref[...] = lo
        store_chunked_T(vbT_ref, lo)
        cb_ref[...] = hi

    @pl.when(j == 4)
    def _():
        u_ref[...] = lo * hi

    @pl.when(j >= _GL_STEP)
    def _():
        gl_ref[...] = acc


def _in_projection(x, mod, g, w_main, w_small, bf_pad, *, tm, rows_per_batch):
    rows = x.shape[0]
    nt = rows // tm
    nb = max(1, tm // rows_per_batch)
    nstep = N_MAIN // IN_STEP
    if nb == 1:
        mod_map = lambda i, j: ((i * tm) // rows_per_batch, 0, 0)
    else:
        mod_map = lambda i, j: (i, 0, 0)
    row_blk = lambda w: pl.BlockSpec((tm, w), lambda i, j: (i, 0))
    col_blk = lambda h: pl.BlockSpec((h, tm), lambda i, j: (0, i))
    chunkT = pl.BlockSpec((tm // KEY_CHUNK, D_BRANCH, KEY_CHUNK), lambda i, j: (i, 0, 0))
    f32o = lambda w: jax.ShapeDtypeStruct((rows, w), F32)
    b16o = lambda w: jax.ShapeDtypeStruct((rows, w), BF16)
    b16T = jax.ShapeDtypeStruct((D_BRANCH, rows), BF16)
    b16c = jax.ShapeDtypeStruct((rows // KEY_CHUNK, D_BRANCH, KEY_CHUNK), BF16)
    out_shape = (
        b16T, f32o(D_BRANCH), b16o(D_BRANCH), f32o(D_BRANCH), b16c, b16T, b16T,
        f32o(D_BRANCH), b16o(D_BRANCH), f32o(D_BRANCH), b16c, f32o(D_BRANCH), f32o(D_BRANCH),
        f32o(6 * D_BRANCH),
        f32o(IDX_DIM), b16o(IDX_DIM), jax.ShapeDtypeStruct((N_HEADS, rows), F32), f32o(N_HEADS),
    )
    out_specs = (
        col_blk(D_BRANCH), row_blk(D_BRANCH), row_blk(D_BRANCH), row_blk(D_BRANCH), chunkT,
        col_blk(D_BRANCH), col_blk(D_BRANCH),
        row_blk(D_BRANCH), row_blk(D_BRANCH), row_blk(D_BRANCH), chunkT, row_blk(D_BRANCH),
        row_blk(D_BRANCH),
        pl.BlockSpec((tm, IN_STEP), lambda i, j: (i, jnp.clip(j - _GL_STEP, 0, 2))),
        row_blk(IDX_DIM), row_blk(IDX_DIM), col_blk(N_HEADS), row_blk(N_HEADS),
    )
    return pl.pallas_call(
        functools.partial(_inproj_kernel, nb=nb),
        out_shape=out_shape,
        grid=(nt, nstep),
        in_specs=[
            pl.BlockSpec((tm, D_MODEL), lambda i, j: (i, 0)),
            pl.BlockSpec((nb, 6, D_MODEL), mod_map),
            pl.BlockSpec((4, D_MODEL), lambda i, j: (0, 0)),
            pl.BlockSpec((D_MODEL, IN_STEP), lambda i, j: (0, j)),
            pl.BlockSpec((D_MODEL, LANES), lambda i, j: (0, 0)),
            pl.BlockSpec((1, LANES), lambda i, j: (0, 0)),
        ],
        out_specs=out_specs,
        scratch_shapes=[pltpu.VMEM((tm, D_MODEL), BF16)],
        compiler_params=_cparams(("arbitrary", "arbitrary")),
        name="in_projection",
    )(x, mod, g, w_main, w_small, bf_pad)


def _cumsum_kernel(x_ref, o_ref):
    x = x_ref[0]
    n = x.shape[0]
    lane = lax.broadcasted_iota(I32, x.shape, 1)
    row = lax.broadcasted_iota(I32, x.shape, 0)
    s = N_HEADS
    while s < LANES:
        x = x + jnp.where(lane >= s, pltpu.roll(x, s, axis=1), 0.0)
        s *= 2
    t = jnp.where(lane >= LANES - N_HEADS, x, 0.0)
    s = N_HEADS
    while s < LANES:
        t = t + pltpu.roll(t, s, axis=1)
        s *= 2
    t = jnp.where(row >= 1, pltpu.roll(t, 1, axis=0), 0.0)
    s = 1
    while s < n:
        if s < 8:
            sh = jnp.where(row >= s, pltpu.roll(t, s, axis=0), 0.0)
        else:
            sh = jnp.concatenate([jnp.zeros((s, LANES), F32), t[:n - s]], axis=0)
        t = t + sh
        s *= 2
    o_ref[0] = x + t


def _forget_cumsum(logf_all):
    b, lp, h = logf_all.shape
    n = lp * h // LANES
    out = pl.pallas_call(
        _cumsum_kernel,
        out_shape=jax.ShapeDtypeStruct((b, n, LANES), F32),
        grid=(b,),
        in_specs=[pl.BlockSpec((1, n, LANES), lambda i: (i, 0, 0))],
        out_specs=pl.BlockSpec((1, n, LANES), lambda i: (i, 0, 0)),
        compiler_params=_cparams(("arbitrary",)),
        name="forget_cumsum",
    )(logf_all.reshape(b, n, LANES))
    return out.reshape(b, lp, h)


def _pair_padded(qT_ref, h):
    pr, half = divmod(h, 2)
    blk = qT_ref[0, pr * LANES + half * HEAD_DIM:pr * LANES + (half + 1) * HEAD_DIM, :]
    z = jnp.zeros_like(blk)
    return jnp.concatenate([blk, z] if half == 0 else [z, blk], axis=0)


def _rows_to_heads(rows):
    tq = rows[0].shape[1]
    sub = lax.broadcasted_iota(I32, (N_HEADS, tq), 0)
    out = jnp.broadcast_to(rows[0], (N_HEADS, tq))
    for h in range(1, N_HEADS):
        out = jnp.where(sub == h, rows[h], out)
    return out


def _stage_and_consume(stage, consume, m_cur, pend, l, acc_ref):
    m_next, sums, alphas = [], [], []
    for h in range(N_HEADS):
        if stage is not None:
            logits, mask, buf = stage
            lg = logits(h)
            if mask is not None:
                lg = jnp.where(mask, lg, NEG_INF)
            buf[h] = lg
            m_next.append(jnp.maximum(m_cur[h:h + 1, :], jnp.max(lg, axis=0, keepdims=True)))
        if consume is not None:
            v_t, buf = consume
            before, after = pend
            alpha = jnp.exp2(before[h:h + 1, :] - after[h:h + 1, :])
            p = jnp.exp2(buf[h] - after[h:h + 1, :]).astype(BF16)
            ones = jnp.ones((16, p.shape[0]), BF16)
            pv = jnp.dot(jnp.concatenate([v_t(h), ones], axis=0), p, preferred_element_type=F32)
            rows = slice(h * HEAD_DIM, (h + 1) * HEAD_DIM)
            acc_ref[rows, :] = alpha * acc_ref[rows, :] + pv[:HEAD_DIM, :]
            sums.append(pv[HEAD_DIM:HEAD_DIM + 1, :])
            alphas.append(alpha)
    if consume is not None:
        l = _rows_to_heads(alphas) * l + _rows_to_heads(sums)
    return (_rows_to_heads(m_next) if stage is not None else m_cur), l


def _attend_block(n_free, n_used, logits, mask, v_t, m_ref, l_ref, acc_ref, buf0, buf1):
    m0 = m_ref[...]
    buf1[...] = jnp.full(buf1.shape, -jnp.inf, F32)

    def quad(masked, qd, carry):
        m_a, m_b, l = carry
        p0 = 2 * qd
        p1 = p0 + 1
        pp = jnp.maximum(p0 - 1, 0)
        mk = (lambda p: mask(p)) if masked else (lambda p: None)
        m_c, l = _stage_and_consume((lambda h: logits(h, p0), mk(p0), buf0),
                                    (lambda h: v_t(h, pp), buf1), m_b, (m_a, m_b), l, acc_ref)
        m_d, l = _stage_and_consume((lambda h: logits(h, p1), mk(p1), buf1),
                                    (lambda h: v_t(h, p0), buf0), m_c, (m_b, m_c), l, acc_ref)
        return m_c, m_d, l

    carry = lax.fori_loop(0, n_free, functools.partial(quad, False), (m0, m0, l_ref[...]))
    m_c, m_d, l = lax.fori_loop(n_free, n_used, functools.partial(quad, True), carry)
    last = 2 * n_used - 1
    _, l = _stage_and_consume(None, (lambda h: v_t(h, last), buf1), m_d, (m_c, m_d), l, acc_ref)
    m_ref[...] = m_d
    l_ref[...] = l


def _softmax_init(m_ref, l_ref, acc_ref):
    m_ref[...] = jnp.full(m_ref.shape, NEG_INF, F32)
    l_ref[...] = jnp.zeros(l_ref.shape, F32)
    acc_ref[...] = jnp.zeros(acc_ref.shape, F32)


def _softmax_finish(o_ref, l_ref, acc_ref):
    inv = 1.0 / l_ref[...]
    parts = [acc_ref[h * HEAD_DIM:(h + 1) * HEAD_DIM, :] * inv[h:h + 1, :] for h in range(N_HEADS)]
    o_ref[0] = jnp.concatenate(parts, axis=0).T.astype(o_ref.dtype)


def _f32_to_ordered(x):
    b = lax.bitcast_convert_type(x, I32)
    return jnp.where(b >= 0, b, b ^ jnp.int32(0x7FFFFFFF))


def _ordered_to_f32(o):
    return lax.bitcast_convert_type(jnp.where(o >= 0, o, o ^ jnp.int32(0x7FFFFFFF)), F32)


def _dsa_kernel(qiT_ref, wiT_ref, qaT_ref, ki_ref, ka_ref, vaT_ref, o_ref,
                s_ref, gm_ref, thr_ref, cut_ref, qp_ref, m_ref, l_ref, acc_ref, lg0_ref, lg1_ref,
                *, past, length, n_sel, tq, tk, nk):
    i = pl.program_id(1)
    j = pl.program_id(2)
    q0 = past + i * tq
    qpos = q0 + lax.broadcasted_iota(I32, (1, tq), 1)
    vis = jnp.minimum(((qpos >> CHUNK_SHIFT) + 1) * CHUNK, length)
    vmax = jnp.minimum((((q0 + tq - 1) >> CHUNK_SHIFT) + 1) * CHUNK, length)
    nvis = (vmax + tk - 1) // tk
    vcap = ((vmax + KEY_QUAD - 1) // KEY_QUAD) * KEY_QUAD

    def rows_here(jb, step):
        return jnp.clip((vcap - jb * tk) // step, 0, tk // step)

    @pl.when(jnp.logical_and(j < nk, j < nvis))
    def _():
        def body(c, carry):
            for half in range(GROUPS // SCORE_ROWS):
                r0 = pl.multiple_of(c * GROUPS + half * SCORE_ROWS, SCORE_ROWS)
                kblk = ki_ref[0, pl.ds(r0, SCORE_ROWS), :]
                acc = jnp.zeros((SCORE_ROWS, tq), F32)
                for h in range(N_HEADS):
                    d = jnp.dot(kblk, qiT_ref[0, h * IDX_DIM:(h + 1) * IDX_DIM, :],
                                preferred_element_type=F32)
                    acc = acc + jnp.maximum(d, 0.0) * wiT_ref[0, h:h + 1, :]
                kpos = j * tk + r0 + lax.broadcasted_iota(I32, (SCORE_ROWS, 1), 0)
                acc = jnp.where(kpos < vis, acc, -jnp.inf)
                s_ref[pl.ds(pl.multiple_of(j * tk + r0, SCORE_ROWS), SCORE_ROWS), :] = acc
                grp = slice(half * SCORE_ROWS, (half + 1) * SCORE_ROWS)
                gm_ref[grp, :] = jnp.maximum(gm_ref[grp, :], acc)
            return carry

        @pl.when(j == 0)
        def _():
            gm_ref[...] = jnp.full(gm_ref.shape, -jnp.inf, F32)

        lax.fori_loop(0, rows_here(j, GROUPS), body, 0)

    @pl.when(j == nk - 1)
    def _():
        nch = vcap // COUNT_ROWS

        def count(pred):
            sub = 64

            def body(c, accs):
                accs = list(accs)
                for k in range(COUNT_ROWS // sub):
                    r0 = pl.multiple_of(c * COUNT_ROWS + k * sub, sub)
                    hit = jnp.where(pred(s_ref[pl.ds(r0, sub), :], r0), 1.0, 0.0)
                    accs[k % 4] = accs[k % 4] + jnp.sum(hit.reshape(sub // 8, 8, tq), axis=0)
                return tuple(accs)
            zero = jnp.zeros((8, tq), F32)
            a0, a1, a2, a3 = lax.fori_loop(0, nch, body, (zero, zero, zero, zero))
            return jnp.sum((a0 + a1) + (a2 + a3), axis=0, keepdims=True)

        def count_ge(t):
            return count(lambda blk, r0: blk >= t)

        kf = float(n_sel)
        no_cut = jnp.full((1, tq), 2 ** 30, I32)

        gm = gm_ref[...]
        few = vis < n_sel
        lo0 = _f32_to_ordered(jnp.min(gm, axis=0, keepdims=True))
        hi0 = _f32_to_ordered(jnp.max(gm, axis=0, keepdims=True)) + 1

        def finished(lo, hi, c_lo):
            return jnp.logical_or(few, jnp.logical_or(c_lo == kf, hi <= lo + 1))

        def more(state):
            _, lo, hi, c_lo = state
            return jnp.max(jnp.where(finished(lo, hi, c_lo), 0.0, 1.0)) > 0.0

        def step(state):
            k, lo, hi, c_lo = state
            mid_f = _f32_to_ordered(0.5 * _ordered_to_f32(lo) + 0.5 * _ordered_to_f32(hi - 1))
            mid_i = (lo >> 1) + (hi >> 1) + (lo & hi & 1)
            wide = ((lo ^ (hi - 1)) >> 23) != 0
            piv = jnp.where(jnp.logical_and(wide, k < FLOAT_STEPS), mid_f, mid_i)
            piv = jnp.where(jnp.logical_and(lo == 0, hi > 1), 1, piv)
            piv = jnp.where(jnp.logical_and(lo < 0, hi > 0), 0, piv)
            piv = jnp.clip(piv, lo + 1, hi - 1)
            c = count_ge(_ordered_to_f32(piv))
            live = jnp.logical_not(finished(lo, hi, c_lo))
            up = jnp.logical_and(live, c >= kf)
            down = jnp.logical_and(live, c < kf)
            return (k + 1, jnp.where(up, piv, lo), jnp.where(down, piv, hi),
                    jnp.where(up, c, c_lo))

        state = (jnp.int32(0), lo0, hi0, jnp.full((1, tq), -1.0, F32))
        _, cur, _, c_cur = lax.while_loop(more, step, state)
        thr = jnp.where(few, -F32_MAX, _ordered_to_f32(cur))

        def with_ties():
            n_gt = count(lambda blk, r0: blk > thr)
            n_ge = count_ge(thr)
            need = kf - n_gt
            tied = (n_ge - n_gt) > need

            def tie_cut():
                rr = lax.broadcasted_iota(I32, (KEY_CHUNK, KEY_CHUNK), 0)
                cc = lax.broadcasted_iota(I32, (KEY_CHUNK, KEY_CHUNK), 1)
                lower = jnp.where(cc <= rr, 1.0, 0.0).astype(BF16)

                def body(c, carry):
                    run, last = carry
                    r0 = pl.multiple_of(c * KEY_CHUNK, KEY_CHUNK)
                    eq = s_ref[pl.ds(r0, KEY_CHUNK), :] == thr
                    ordinal = run + jnp.dot(lower, jnp.where(eq, 1.0, 0.0).astype(BF16),
                                            preferred_element_type=F32)
                    kpos = (r0 + lax.broadcasted_iota(I32, (KEY_CHUNK, 1), 0)).astype(F32)
                    take = jnp.logical_and(eq, ordinal <= need)
                    last = jnp.maximum(last, jnp.max(jnp.where(take, kpos, -1.0), axis=0,
                                                     keepdims=True))
                    return ordinal[KEY_CHUNK - 1:KEY_CHUNK, :], last

                init = (jnp.zeros((1, tq), F32), jnp.full((1, tq), -1.0, F32))
                _, last = lax.fori_loop(0, vcap // KEY_CHUNK, body, init)
                return jnp.where(tied, last.astype(I32), no_cut)

            return lax.cond(jnp.max(jnp.where(tied, 1.0, 0.0)) > 0.0, tie_cut, lambda: no_cut)

        unsure = jnp.logical_or(c_cur != kf, few)
        cut = lax.cond(jnp.max(jnp.where(unsure, 1.0, 0.0)) > 0.0, with_ties, lambda: no_cut)
        thr_ref[0:1, :] = thr
        cut_ref[0:1, :] = cut
        for h in range(N_HEADS):
            qp_ref[h] = _pair_padded(qaT_ref, h)
        _softmax_init(m_ref, l_ref, acc_ref)

    jj = j - nk

    @pl.when(jnp.logical_and(j >= nk, jj < nvis))
    def _():
        thr = thr_ref[0:1, :]
        cut = cut_ref[0:1, :]

        def selected(p):
            g0 = pl.multiple_of(jj * tk + p * KEY_PAIR, KEY_PAIR)
            sblk = s_ref[pl.ds(g0, KEY_PAIR), :]
            kpos = g0 + lax.broadcasted_iota(I32, (KEY_PAIR, 1), 0)
            return jnp.logical_or(sblk > thr, jnp.logical_and(sblk == thr, kpos <= cut))

        def logits(h, p):
            r0 = pl.multiple_of(p * KEY_PAIR, KEY_PAIR)
            kblk = ka_ref[0, pl.ds(r0, KEY_PAIR), (h // 2) * LANES:(h // 2 + 1) * LANES]
            return jnp.dot(kblk, qp_ref[h], preferred_element_type=F32)

        def v_t(h, p):
            rows = slice(h * HEAD_DIM, (h + 1) * HEAD_DIM)
            return jnp.concatenate([vaT_ref[0, 2 * p, rows, :], vaT_ref[0, 2 * p + 1, rows, :]],
                                   axis=1)

        _attend_block(0, rows_here(jj, KEY_QUAD), logits, selected, v_t, m_ref, l_ref, acc_ref,
                      lg0_ref, lg1_ref)

    @pl.when(j == 2 * nk - 1)
    def _():
        _softmax_finish(o_ref, l_ref, acc_ref)


def _dsa_attention(qiT, wiT, qaT, ki, ka, vaTc, *, past, length, tq, tk):
    b, _, tqp = qiT.shape
    lp = ki.shape[1]
    nq, nk = tqp // tq, lp // tk
    n_sel = min(TOPK_MAX, length // 4)

    def nvis(i):
        vmax = jnp.minimum(((past + (i + 1) * tq - 1) // CHUNK + 1) * CHUNK, length)
        return (vmax + tk - 1) // tk

    k1 = lambda bb, i, j: (bb, jnp.minimum(j, nvis(i) - 1), 0)
    k3 = lambda bb, i, j: (bb, jnp.clip(j - nk, 0, nvis(i) - 1), 0)
    k3c = lambda bb, i, j: (bb, jnp.clip(j - nk, 0, nvis(i) - 1), 0, 0)
    qmap = lambda bb, i, j: (bb, 0, i)
    kern = functools.partial(_dsa_kernel, past=past, length=length, n_sel=n_sel, tq=tq, tk=tk,
                             nk=nk)
    return pl.pallas_call(
        kern,
        out_shape=jax.ShapeDtypeStruct((b, tqp, D_BRANCH), BF16),
        grid=(b, nq, 2 * nk),
        in_specs=[
            pl.BlockSpec((1, D_BRANCH, tq), qmap),
            pl.BlockSpec((1, N_HEADS, tq), qmap),
            pl.BlockSpec((1, D_BRANCH, tq), qmap),
            pl.BlockSpec((1, tk, IDX_DIM), k1),
            pl.BlockSpec((1, tk, D_BRANCH), k3),
            pl.BlockSpec((1, tk // KEY_CHUNK, D_BRANCH, KEY_CHUNK), k3c),
        ],
        out_specs=pl.BlockSpec((1, tq, D_BRANCH), lambda bb, i, j: (bb, i, 0)),
        scratch_shapes=[
            pltpu.VMEM((lp, tq), F32),
            pltpu.VMEM((GROUPS, tq), F32),
            pltpu.VMEM((8, tq), F32),
            pltpu.VMEM((8, tq), I32),
            pltpu.VMEM((N_HEADS, LANES, tq), BF16),
            pltpu.VMEM((N_HEADS, tq), F32),
            pltpu.VMEM((N_HEADS, tq), F32),
            pltpu.VMEM((D_BRANCH, tq), F32),
            pltpu.VMEM((N_HEADS, KEY_PAIR, tq), F32),
            pltpu.VMEM((N_HEADS, KEY_PAIR, tq), F32),
        ],
        compiler_params=_cparams(("arbitrary", "arbitrary", "arbitrary")),
        name="dsa_attention",
    )(qiT, wiT, qaT, ki, ka, vaTc)


_AUG = 6


def _fox_kernel(qbT_ref, qaug_ref, kb_ref, kaug_ref, vbT_ref, o_ref,
                qf_ref, m_ref, l_ref, acc_ref, lg0_ref, lg1_ref, *, past, tq, tk, nk):
    i = pl.program_id(1)
    j = pl.program_id(2)
    q0 = past + i * tq
    qpos = q0 + lax.broadcasted_iota(I32, (1, tq), 1)
    nvis = (q0 + tq + tk - 1) // tk
    nquad = tk // KEY_QUAD
    n_free = jnp.clip((q0 + 1 - j * tk) // KEY_QUAD, 0, nquad)
    n_used = jnp.clip((q0 + tq - j * tk + KEY_QUAD - 1) // KEY_QUAD, 0, nquad)

    @pl.when(j == 0)
    def _():
        arow = lax.broadcasted_iota(I32, (LANES, 1), 0)
        qa = qaug_ref[0]
        for h in range(N_HEADS):
            mine = jnp.logical_and(arow >= _AUG * h, arow < _AUG * (h + 1))
            qf_ref[h, 0:LANES, :] = _pair_padded(qbT_ref, h)
            qf_ref[h, LANES:2 * LANES, :] = jnp.where(mine, qa, 0.0).astype(BF16)
        _softmax_init(m_ref, l_ref, acc_ref)

    @pl.when(j < nvis)
    def _():
        def causal(p):
            kpos = j * tk + p * KEY_PAIR + lax.broadcasted_iota(I32, (KEY_PAIR, 1), 0)
            return kpos <= qpos

        def logits(h, p):
            r0 = pl.multiple_of(p * KEY_PAIR, KEY_PAIR)
            kblk = kb_ref[0, pl.ds(r0, KEY_PAIR), (h // 2) * LANES:(h // 2 + 1) * LANES]
            lhs = jnp.concatenate([kblk, kaug_ref[0, pl.ds(r0, KEY_PAIR), :]], axis=1)
            return jnp.dot(lhs, qf_ref[h], preferred_element_type=F32)

        def v_t(h, p):
            rows = slice(h * HEAD_DIM, (h + 1) * HEAD_DIM)
            return jnp.concatenate([vbT_ref[0, 2 * p, rows, :], vbT_ref[0, 2 * p + 1, rows, :]],
                                   axis=1)

        _attend_block(n_free, n_used, logits, causal, v_t, m_ref, l_ref, acc_ref,
                      lg0_ref, lg1_ref)

    @pl.when(j == nk - 1)
    def _():
        _softmax_finish(o_ref, l_ref, acc_ref)


def _fox_attention(qbT, qaug, kb, kaug, vbTc, *, past, tq, tk):
    b, _, tqp = qbT.shape
    lp = kb.shape[1]
    nq, nk = tqp // tq, lp // tk

    def last(i):
        return jnp.minimum((past + (i + 1) * tq + tk - 1) // tk, nk) - 1

    kmap = lambda bb, i, j: (bb, jnp.minimum(j, last(i)), 0)
    kmapc = lambda bb, i, j: (bb, jnp.minimum(j, last(i)), 0, 0)
    qmap = lambda bb, i, j: (bb, 0, i)
    return pl.pallas_call(
        functools.partial(_fox_kernel, past=past, tq=tq, tk=tk, nk=nk),
        out_shape=jax.ShapeDtypeStruct((b, tqp, D_BRANCH), BF16),
        grid=(b, nq, nk),
        in_specs=[
            pl.BlockSpec((1, D_BRANCH, tq), qmap),
            pl.BlockSpec((1, LANES, tq), qmap),
            pl.BlockSpec((1, tk, D_BRANCH), kmap),
            pl.BlockSpec((1, tk, LANES), kmap),
            pl.BlockSpec((1, tk // KEY_CHUNK, D_BRANCH, KEY_CHUNK), kmapc),
        ],
        out_specs=pl.BlockSpec((1, tq, D_BRANCH), lambda bb, i, j: (bb, i, 0)),
        scratch_shapes=[
            pltpu.VMEM((N_HEADS, 2 * LANES, tq), BF16),
            pltpu.VMEM((N_HEADS, tq), F32),
            pltpu.VMEM((N_HEADS, tq), F32),
            pltpu.VMEM((D_BRANCH, tq), F32),
            pltpu.VMEM((N_HEADS, KEY_PAIR, tq), F32),
            pltpu.VMEM((N_HEADS, KEY_PAIR, tq), F32),
        ],
        compiler_params=_cparams(("arbitrary", "arbitrary", "arbitrary")),
        name="fox_attention",
    )(qbT, qaug, kb, kaug, vbTc)


AUG_ROWS = 512


def _aug_kernel(c_ref, kaug_ref, qaugT_ref):
    x = c_ref[0] * LOG2E
    a = x.astype(BF16).astype(F32)
    r = x - a
    b = r.astype(BF16).astype(F32)
    c = (r - b).astype(BF16).astype(F32)
    lane = lax.broadcasted_iota(I32, (N_HEADS, LANES), 1)
    head = lax.broadcasted_iota(I32, (N_HEADS, LANES), 0)

    def place(v, slot):
        sel = jnp.where(lane == _AUG * head + slot, 1.0, 0.0)
        return jnp.dot(v, sel, preferred_element_type=F32, precision=lax.Precision.HIGHEST)

    slot = lane - _AUG * head
    ones_k = jnp.sum(jnp.where(jnp.logical_and(slot >= 0, slot < 3), 1.0, 0.0), axis=0, keepdims=True)
    ones_q = jnp.sum(jnp.where(jnp.logical_and(slot >= 3, slot < 6), 1.0, 0.0), axis=0, keepdims=True)
    kaug_ref[0] = (ones_k - (place(a, 3) + place(b, 4) + place(c, 5))).astype(BF16)
    qaugT_ref[0] = (ones_q + (place(a, 0) + place(b, 1) + place(c, 2))).T


def _fox_augment(cum):
    b, lp, h = cum.shape
    return pl.pallas_call(
        _aug_kernel,
        out_shape=(jax.ShapeDtypeStruct((b, lp, LANES), BF16),
                   jax.ShapeDtypeStruct((b, LANES, lp), F32)),
        grid=(b, lp // AUG_ROWS),
        in_specs=[pl.BlockSpec((1, AUG_ROWS, h), lambda i, j: (i, j, 0))],
        out_specs=(pl.BlockSpec((1, AUG_ROWS, LANES), lambda i, j: (i, j, 0)),
                   pl.BlockSpec((1, LANES, AUG_ROWS), lambda i, j: (i, 0, j))),
        compiler_params=_cparams(("arbitrary", "arbitrary")),
        name="fox_augment",
    )(cum)


def _shifted_rows(buf, load, store, u, past_ref, starts_batch, nb):
    tm, c = u.shape
    store(slice(8, 8 + tm), u)
    if nb == 1:
        @pl.when(starts_batch)
        def _():
            store(slice(6, 8), past_ref[0])
    s1 = load(slice(7, 7 + tm))
    s2 = load(slice(6, 6 + tm))
    if nb > 1:
        r = lax.broadcasted_iota(I32, (nb, tm // nb, 1), 1)
        p0 = past_ref[:, 0:1, :]
        p1 = past_ref[:, 1:2, :]
        s1 = jnp.where(r == 0, p1, _group(s1, nb)).reshape(tm, c)
        s2 = jnp.where(r == 0, p0, jnp.where(r == 1, p1, _group(s2, nb))).reshape(tm, c)
    store(slice(0, 8), load(slice(tm, tm + 8)))
    return s1, s2


def _merge_kernel(x_ref, ya_ref, yb_ref, cb_ref, u_ref, gl_ref, past_ref, cw_ref, wbr_ref,
                  wo_ref, g_ref, mod_ref, o_ref, ub_ref, *, nb, rows_per_batch):
    i = pl.program_id(0)
    tm = x_ref.shape[0]
    u = u_ref[...]

    def load(rows):
        return ub_ref[rows, :]

    def store(rows, v):
        ub_ref[rows, :] = v

    s1, s2 = _shifted_rows(ub_ref, load, store, u, past_ref, (i * tm) % rows_per_batch == 0, nb)
    conv = cw_ref[0:1, :] * s2 + cw_ref[1:2, :] * s1 + cw_ref[2:3, :] * u
    yc = (cb_ref[...] * conv).astype(BF16)
    mix = jnp.zeros((tm, D_MODEL), F32)
    for n, y in enumerate((ya_ref[...], yb_ref[...], yc)):
        br = jnp.dot(y, wbr_ref[n], preferred_element_type=F32)
        mix = mix + jax.nn.sigmoid(gl_ref[:, n * D_MODEL:(n + 1) * D_MODEL]) * br
    mo = jnp.dot(mix.astype(BF16), wo_ref[...], preferred_element_type=F32)
    nm = _group(_rms(mo, g_ref[1:2, :]), nb)
    o_ref[...] = x_ref[...] + (mod_ref[:, 2:3, :] * nm).reshape(tm, D_MODEL)


def _mod_spec(tm, nb, rows_per_batch, ngrid):
    if nb == 1:
        f = lambda i, *_: ((i * tm) // rows_per_batch, 0, 0)
    else:
        f = lambda i, *_: (i, 0, 0)
    return pl.BlockSpec((nb, 6, D_MODEL), f)


def _merge(x, ya, yb, cb, u, gl, past, cw, wbr, wo, g, mod, *, tm, rows_per_batch):
    rows = x.shape[0]
    nb = max(1, tm // rows_per_batch)
    row = lambda w: pl.BlockSpec((tm, w), lambda i: (i, 0))
    full = lambda shape: pl.BlockSpec(shape, lambda i: (0,) * len(shape))
    if nb == 1:
        past_spec = pl.BlockSpec((1, 2, D_BRANCH), lambda i: ((i * tm) // rows_per_batch, 0, 0))
    else:
        past_spec = pl.BlockSpec((nb, 2, D_BRANCH), lambda i: (i, 0, 0))
    return pl.pallas_call(
        functools.partial(_merge_kernel, nb=nb, rows_per_batch=rows_per_batch),
        out_shape=jax.ShapeDtypeStruct((rows, D_MODEL), F32),
        grid=(rows // tm,),
        in_specs=[
            row(D_MODEL), row(D_BRANCH), row(D_BRANCH), row(D_BRANCH), row(D_BRANCH),
            row(3 * D_MODEL), past_spec, full((CONV_W, D_BRANCH)),
            full((3, D_BRANCH, D_MODEL)), full((D_MODEL, D_MODEL)), full((4, D_MODEL)),
            _mod_spec(tm, nb, rows_per_batch, 1),
        ],
        out_specs=row(D_MODEL),
        scratch_shapes=[pltpu.VMEM((tm + 8, D_BRANCH), F32)],
        compiler_params=_cparams(("arbitrary",)),
        name="branch_merge",
    )(x, ya, yb, cb, u, gl, past, cw, wbr, wo, g, mod)


FF_CHUNK = 1408


def _ffn_kernel(x_ref, g_ref, mod_ref, wg_ref, wv_ref, wd_ref, cw_ref, past_ref,
                o_ref, tail_ref, hs_ref, acc_ref, ub_ref, *, nb, rows_per_batch, nj):
    i = pl.program_id(0)
    j = pl.program_id(1)
    tm = x_ref.shape[0]

    @pl.when(j == 0)
    def _():
        y = _group(_rms(x_ref[...], g_ref[2:3, :]), nb)
        h = y * (1.0 + mod_ref[:, 4:5, :]) + mod_ref[:, 3:4, :]
        hs_ref[...] = h.reshape(tm, D_MODEL).astype(BF16)
        acc_ref[...] = jnp.zeros(acc_ref.shape, F32)

    hs = hs_ref[...]
    ug = jnp.dot(hs, wg_ref[...], preferred_element_type=F32)
    uv = jnp.dot(hs, wv_ref[...], preferred_element_type=F32)

    def load(rows):
        return ub_ref[j, rows, :]

    def store(rows, v):
        ub_ref[j, rows, :] = v

    s1, s2 = _shifted_rows(ub_ref, load, store, ug, past_ref, (i * tm) % rows_per_batch == 0, nb)
    conv = cw_ref[0:1, :] * s2 + cw_ref[1:2, :] * s1 + cw_ref[2:3, :] * ug
    f = conv * jax.nn.sigmoid(conv) * uv
    acc_ref[...] += jnp.dot(f.astype(BF16), wd_ref[...], preferred_element_type=F32)
    grp = tm // nb
    tail_ref[0] = _group(ug, nb)[:, grp - 8:, :]

    @pl.when(j == nj - 1)
    def _():
        nm = _group(_rms(acc_ref[...], g_ref[3:4, :]), nb)
        o_ref[...] = x_ref[...] + (mod_ref[:, 5:6, :] * nm).reshape(tm, D_MODEL)


def _conv_ffn(x, g, mod, w_up, w_down, cw, past, *, tm, rows_per_batch):
    rows = x.shape[0]
    dff = w_down.shape[0]
    nj = dff // FF_CHUNK
    nb = max(1, tm // rows_per_batch)
    nt = rows // tm
    if nb == 1:
        past_spec = pl.BlockSpec((1, 2, FF_CHUNK), lambda i, j: ((i * tm) // rows_per_batch, 0, j))
        mod_map = lambda i, j: ((i * tm) // rows_per_batch, 0, 0)
    else:
        past_spec = pl.BlockSpec((nb, 2, FF_CHUNK), lambda i, j: (i, 0, j))
        mod_map = lambda i, j: (i, 0, 0)
    out, tails = pl.pallas_call(
        functools.partial(_ffn_kernel, nb=nb, rows_per_batch=rows_per_batch, nj=nj),
        out_shape=(jax.ShapeDtypeStruct((rows, D_MODEL), F32),
                   jax.ShapeDtypeStruct((nt, nb, 8, dff), F32)),
        grid=(nt, nj),
        in_specs=[
            pl.BlockSpec((tm, D_MODEL), lambda i, j: (i, 0)),
            pl.BlockSpec((4, D_MODEL), lambda i, j: (0, 0)),
            pl.BlockSpec((nb, 6, D_MODEL), mod_map),
            pl.BlockSpec((D_MODEL, FF_CHUNK), lambda i, j: (0, j)),
            pl.BlockSpec((D_MODEL, FF_CHUNK), lambda i, j: (0, j + nj)),
            pl.BlockSpec((FF_CHUNK, D_MODEL), lambda i, j: (j, 0)),
            pl.BlockSpec((CONV_W, FF_CHUNK), lambda i, j: (0, j)),
            past_spec,
        ],
        out_specs=(pl.BlockSpec((tm, D_MODEL), lambda i, j: (i, 0)),
                   pl.BlockSpec((1, nb, 8, FF_CHUNK), lambda i, j: (i, 0, 0, j))),
        scratch_shapes=[
            pltpu.VMEM((tm, D_MODEL), BF16),
            pltpu.VMEM((tm, D_MODEL), F32),
            pltpu.VMEM((nj, tm + 8, FF_CHUNK), F32),
        ],
        compiler_params=_cparams(("arbitrary", "arbitrary")),
        name="conv_ffn",
    )(x, g, mod, w_up, w_up, w_down, cw, past)
    return out, tails


def _relayout_w_in(w_in, b_forget):
    db = D_BRANCH
    o_qi = 3 * db
    o_ki = o_qi + N_HEADS * IDX_DIM
    o_wi = o_ki + IDX_DIM
    o_qb = o_wi + N_HEADS
    o_fl = o_qb + 3 * db
    o_cb = o_fl + N_HEADS
    o_gl = o_cb + 3 * db
    main = jnp.concatenate([w_in[:, 0:o_ki], w_in[:, o_qb:o_fl], w_in[:, o_cb:]], axis=1)
    small = jnp.concatenate([w_in[:, o_ki:o_wi], w_in[:, o_wi:o_qb], w_in[:, o_fl:o_cb]], axis=1)
    small = jnp.pad(small, ((0, 0), (0, LANES - small.shape[1])))
    bf = jnp.zeros((1, LANES), F32).at[0, _FL_OFF:_FL_OFF + N_HEADS].set(b_forget)
    del o_gl
    return main.astype(BF16), small.astype(BF16), bf


def _round_up(x, m):
    return (x + m - 1) // m * m


def _layer(x, mod, caches, weights, *, batch, t, past, cfg):
    (g, w_main, w_small, bf_pad, cw_mix, wbr, wo, w_up, cw_ffn, w_down) = weights
    rows = batch * t
    length = past + t
    (qaT, ka32, ka16, va32, vaTc, qiT, qbT, kb32, kb16, vb32, vbTc, cb, u, gl,
     ki32, ki16, wiT, logf) = _in_projection(
        x, mod, g, w_main, w_small, bf_pad, tm=cfg["tm_in"], rows_per_batch=t)

    tq, tk = cfg["tq"], cfg["tk"]
    tqp = _round_up(t, tq)
    lp = _round_up(length, tk)

    def per_batch_T(aT):
        c = aT.shape[0]
        a = jnp.swapaxes(aT.reshape(c, batch, t), 0, 1)
        return jnp.pad(a, ((0, 0), (0, 0), (0, tqp - t)))

    def chunked_T(a):
        return jnp.swapaxes(a.reshape(batch, lp // KEY_CHUNK, KEY_CHUNK, a.shape[-1]), 2, 3)

    def with_cache(cache, new, dtype):
        new = new.reshape(batch, t, -1)
        if cache is not None:
            new = jnp.concatenate([cache.reshape(batch, past, -1).astype(dtype), new.astype(dtype)],
                                  axis=1)
        return jnp.pad(new.astype(dtype), ((0, 0), (0, lp - length), (0, 0)))

    if caches is None:
        c_idx = c_dk = c_dv = c_fk = c_fv = c_lf = None
        past_mix = jnp.zeros((batch, CONV_W - 1, D_BRANCH), F32)
        past_ffn = jnp.zeros((batch, CONV_W - 1, w_down.shape[0]), F32)
    else:
        c_idx, c_dk, c_dv, c_fk, c_fv, c_lf, past_mix, past_ffn = caches

    ki_all = with_cache(c_idx, ki16, BF16)
    ka_all = with_cache(c_dk, ka16, BF16)
    kb_all = with_cache(c_fk, kb16, BF16)
    if caches is None and lp == rows:
        vaT_all = vaTc.reshape(batch, lp // KEY_CHUNK, D_BRANCH, KEY_CHUNK)
        vbT_all = vbTc.reshape(batch, lp // KEY_CHUNK, D_BRANCH, KEY_CHUNK)
    else:
        vaT_all = chunked_T(with_cache(c_dv, va32, BF16))
        vbT_all = chunked_T(with_cache(c_fv, vb32, BF16))
    logf_all = with_cache(c_lf, logf, F32)

    ya = _dsa_attention(per_batch_T(qiT), per_batch_T(wiT), per_batch_T(qaT), ki_all, ka_all,
                        vaT_all, past=past, length=length, tq=tq, tk=tk)
    cum = _forget_cumsum(logf_all)
    kaug, qaug = _fox_augment(cum)
    qaug = jnp.pad(qaug[:, :, past:length], ((0, 0), (0, 0), (0, tqp - t)))
    yb = _fox_attention(per_batch_T(qbT), qaug, kb_all, kaug, vbT_all, past=past, tq=tq, tk=tk)
    ya = ya[:, :t].reshape(rows, D_BRANCH)
    yb = yb[:, :t].reshape(rows, D_BRANCH)

    x1 = _merge(x, ya, yb, cb, u, gl, past_mix, cw_mix, wbr, wo, g, mod,
                tm=cfg["tm_merge"], rows_per_batch=t)
    x2, tails = _conv_ffn(x1, g, mod, w_up, w_down, cw_ffn, past_ffn,
                          tm=cfg["tm_ffn"], rows_per_batch=t)

    new_mix = u.reshape(batch, t, D_BRANCH)[:, t - (CONV_W - 1):]
    nb = max(1, cfg["tm_ffn"] // t)
    if nb == 1:
        tiles_per_batch = t // cfg["tm_ffn"]
        last = tails.reshape(batch, tiles_per_batch, 8, -1)[:, -1]
    else:
        last = tails.reshape(batch, 8, -1)
    new_ffn = last[:, 8 - (CONV_W - 1):]
    hd = (batch, t, N_HEADS, HEAD_DIM)
    state = (ki32.reshape(batch, t, IDX_DIM), ka32.reshape(hd), va32.reshape(hd),
             kb32.reshape(hd), vb32.reshape(hd), logf.reshape(batch, t, N_HEADS), new_mix, new_ffn)
    return x2, state


_PROMPT_CFG = dict(tm_in=512, tm_merge=256, tm_ffn=256, tq=256, tk=4096)
_SAMPLE_CFG = dict(tm_in=256, tm_merge=256, tm_ffn=256, tq=128, tk=3072)


def kernel(x_prompt, x_sample, c_prompt, c_sample, cache_idx_k, cache_dsa_k, cache_dsa_v,
           cache_fox_k, cache_fox_v, cache_fox_logf, state_conv_mix, state_conv_ffn,
           w_ada, b_ada, norm_g, w_in, b_forget, conv_mix_w, w_branch, w_out, w_up,
           conv_ffn_w, w_down):
    bp, tp, _ = x_prompt.shape
    bs, ts, _ = x_sample.shape
    past = cache_dsa_k.shape[2]
    depth = w_ada.shape[0]

    c_all = jnp.concatenate([c_prompt, c_sample], axis=0)
    pad_rows = _round_up(c_all.shape[0], 8) - c_all.shape[0]
    c_all = jnp.pad(c_all, ((0, pad_rows), (0, 0)))
    mod_all = _modulation(c_all, w_ada, b_ada).reshape(depth, -1, 6, D_MODEL)

    yp = x_prompt.reshape(bp * tp, D_MODEL)
    ys = x_sample.reshape(bs * ts, D_MODEL)
    p_states, s_states = [], []
    for l in range(depth):
        w_main, w_small, bf_pad = _relayout_w_in(w_in[l], b_forget[l])
        weights = (norm_g[l], w_main, w_small, bf_pad, conv_mix_w[l], w_branch[l].astype(BF16),
                   w_out[l].astype(BF16), w_up[l].astype(BF16), conv_ffn_w[l],
                   w_down[l].astype(BF16))
        yp, st_p = _layer(yp, mod_all[l, :bp], None, weights, batch=bp, t=tp, past=0,
                          cfg=_PROMPT_CFG)
        caches = (cache_idx_k[l], cache_dsa_k[l], cache_dsa_v[l], cache_fox_k[l], cache_fox_v[l],
                  cache_fox_logf[l], state_conv_mix[l], state_conv_ffn[l])
        ys, st_s = _layer(ys, mod_all[l, bp:bp + bs], caches, weights, batch=bs, t=ts, past=past,
                          cfg=_SAMPLE_CFG)
        p_states.append(st_p)
        s_states.append(st_s)

    stack = lambda states: [jnp.stack([st[k] for st in states], axis=0) for k in range(8)]
    return (yp.reshape(bp, tp, D_MODEL), ys.reshape(bs, ts, D_MODEL), *stack(p_states),
            *stack(s_states))
```

```python
import functools

import jax
import jax.numpy as jnp
from jax import lax
from jax.experimental import pallas as pl
from jax.experimental.pallas import tpu as pltpu

F32 = jnp.float32
BF16 = jnp.bfloat16
I32 = jnp.int32

D_MODEL = 1024
HEAD_DIM = 64
D_BRANCH = 512
N_HEADS = 8
IDX_DIM = 64
CHUNK = 64
CHUNK_SHIFT = 6
TOPK_MAX = 256
CONV_W = 3
EPS = 1e-6
NEG_INF = -1e30
LOG2E = 1.4426950408889634
F32_MAX = 3.4028234663852886e38

LANES = 128
N_MAIN = 16 * D_BRANCH
KEY_CHUNK = 256
KEY_PAIR = 2 * KEY_CHUNK
KEY_QUAD = 2 * KEY_PAIR
SCORE_ROWS = 128
COUNT_ROWS = KEY_QUAD
GROUPS = TOPK_MAX
FLOAT_STEPS = 12
VMEM_LIMIT = 56 * 1024 * 1024


def _cparams(sem):
    return pltpu.CompilerParams(dimension_semantics=sem, vmem_limit_bytes=VMEM_LIMIT)


def _rms(x, g_row):
    return x * lax.rsqrt(jnp.mean(x * x, axis=-1, keepdims=True) + EPS) * g_row


def _group(x, nb):
    return x.reshape(nb, x.shape[0] // nb, x.shape[1])


def _mod_kernel(c_ref, w_ref, b_ref, o_ref):
    c = c_ref[...]
    s = (c * jax.nn.sigmoid(c)).astype(BF16)
    o_ref[0] = jnp.dot(s, w_ref[0].astype(BF16), preferred_element_type=F32) + b_ref[0]


def _modulation(c_all, w_ada, b_ada):
    depth = w_ada.shape[0]
    rows = c_all.shape[0]
    n = w_ada.shape[2]
    tn = D_MODEL
    return pl.pallas_call(
        _mod_kernel,
        out_shape=jax.ShapeDtypeStruct((depth, rows, n), F32),
        grid=(depth, n // tn),
        in_specs=[
            pl.BlockSpec((rows, D_MODEL), lambda l, j: (0, 0)),
            pl.BlockSpec((1, D_MODEL, tn), lambda l, j: (l, 0, j)),
            pl.BlockSpec((1, 1, tn), lambda l, j: (l, 0, j)),
        ],
        out_specs=pl.BlockSpec((1, rows, tn), lambda l, j: (l, 0, j)),
        compiler_params=_cparams(("arbitrary", "arbitrary")),
        name="adaln_mod",
    )(c_all, w_ada, b_ada.reshape(depth, 1, n))


IN_STEP = 2 * D_BRANCH
_GL_STEP = 5
_WI_OFF = IDX_DIM
_FL_OFF = IDX_DIM + N_HEADS


def _inproj_kernel(x_ref, mod_ref, g_ref, wm_ref, ws_ref, bf_ref,
                   qaT_ref, ka32_ref, ka16_ref, va32_ref, vaT_ref, qiT_ref, qbT_ref,
                   kb32_ref, kb16_ref, vb32_ref, vbT_ref, cb_ref, u_ref, gl_ref,
                   ki32_ref, ki16_ref, wiT_ref, logf_ref,
                   hs_ref, *, nb):
    j = pl.program_id(1)
    tm = x_ref.shape[0]

    @pl.when(j == 0)
    def _():
        y = _rms(x_ref[...], g_ref[0:1, :])
        h = _group(y, nb) * (1.0 + mod_ref[:, 1:2, :]) + mod_ref[:, 0:1, :]
        hb = h.reshape(tm, D_MODEL).astype(BF16)
        hs_ref[...] = hb
        sm = jnp.dot(hb, ws_ref[...], preferred_element_type=F32)
        ki32_ref[...] = sm[:, :IDX_DIM]
        ki16_ref[...] = sm[:, :IDX_DIM].astype(BF16)
        wiT_ref[...] = sm.T[_WI_OFF:_WI_OFF + N_HEADS, :]
        z = sm + bf_ref[...]
        lf = -(jnp.maximum(-z, 0.0) + jnp.log1p(jnp.exp(-jnp.abs(z))))
        logf_ref[...] = lf[:, _FL_OFF:_FL_OFF + N_HEADS]

    acc = jnp.dot(hs_ref[...], wm_ref[...], preferred_element_type=F32)
    lo = acc[:, :D_BRANCH]
    hi = acc[:, D_BRANCH:]
    scale = LOG2E * HEAD_DIM ** -0.5

    def store_chunked_T(ref, a):
        aT = a.T.astype(BF16)
        for c in range(tm // KEY_CHUNK):
            ref[c] = aT[:, c * KEY_CHUNK:(c + 1) * KEY_CHUNK]

    @pl.when(j == 0)
    def _():
        qaT_ref[...] = (lo * scale).T.astype(BF16)
        ka32_ref[...] = hi
        ka16_ref[...] = hi.astype(BF16)

    @pl.when(j == 1)
    def _():
        va32_ref[...] = lo
        store_chunked_T(vaT_ref, lo)
        qiT_ref[...] = hi.T.astype(BF16)

    @pl.when(j == 2)
    def _():
        qbT_ref[...] = (lo * scale).T.astype(BF16)
        kb32_ref[...] = hi
        kb16_ref[...] = hi.astype(BF16)

    @pl.when(j == 3)
    def _():
        vb32_ref[...] = lo
        store_chunked_T(vbT_ref, lo)
        cb_ref[...] = hi

    @pl.when(j == 4)
    def _():
        u_ref[...] = lo * hi

    @pl.when(j >= _GL_STEP)
    def _():
        gl_ref[...] = acc


def _in_projection(x, mod, g, w_main, w_small, bf_pad, *, tm, rows_per_batch):
    rows = x.shape[0]
    nt = rows // tm
    nb = max(1, tm // rows_per_batch)
    nstep = N_MAIN // IN_STEP
    if nb == 1:
        mod_map = lambda i, j: ((i * tm) // rows_per_batch, 0, 0)
    else:
        mod_map = lambda i, j: (i, 0, 0)
    row_blk = lambda w: pl.BlockSpec((tm, w), lambda i, j: (i, 0))
    col_blk = lambda h: pl.BlockSpec((h, tm), lambda i, j: (0, i))
    chunkT = pl.BlockSpec((tm // KEY_CHUNK, D_BRANCH, KEY_CHUNK), lambda i, j: (i, 0, 0))
    f32o = lambda w: jax.ShapeDtypeStruct((rows, w), F32)
    b16o = lambda w: jax.ShapeDtypeStruct((rows, w), BF16)
    b16T = jax.ShapeDtypeStruct((D_BRANCH, rows), BF16)
    b16c = jax.ShapeDtypeStruct((rows // KEY_CHUNK, D_BRANCH, KEY_CHUNK), BF16)
    out_shape = (
        b16T, f32o(D_BRANCH), b16o(D_BRANCH), f32o(D_BRANCH), b16c, b16T, b16T,
        f32o(D_BRANCH), b16o(D_BRANCH), f32o(D_BRANCH), b16c, f32o(D_BRANCH), f32o(D_BRANCH),
        f32o(6 * D_BRANCH),
        f32o(IDX_DIM), b16o(IDX_DIM), jax.ShapeDtypeStruct((N_HEADS, rows), F32), f32o(N_HEADS),
    )
    out_specs = (
        col_blk(D_BRANCH), row_blk(D_BRANCH), row_blk(D_BRANCH), row_blk(D_BRANCH), chunkT,
        col_blk(D_BRANCH), col_blk(D_BRANCH),
        row_blk(D_BRANCH), row_blk(D_BRANCH), row_blk(D_BRANCH), chunkT, row_blk(D_BRANCH),
        row_blk(D_BRANCH),
        pl.BlockSpec((tm, IN_STEP), lambda i, j: (i, jnp.clip(j - _GL_STEP, 0, 2))),
        row_blk(IDX_DIM), row_blk(IDX_DIM), col_blk(N_HEADS), row_blk(N_HEADS),
    )
    return pl.pallas_call(
        functools.partial(_inproj_kernel, nb=nb),
        out_shape=out_shape,
        grid=(nt, nstep),
        in_specs=[
            pl.BlockSpec((tm, D_MODEL), lambda i, j: (i, 0)),
            pl.BlockSpec((nb, 6, D_MODEL), mod_map),
            pl.BlockSpec((4, D_MODEL), lambda i, j: (0, 0)),
            pl.BlockSpec((D_MODEL, IN_STEP), lambda i, j: (0, j)),
            pl.BlockSpec((D_MODEL, LANES), lambda i, j: (0, 0)),
            pl.BlockSpec((1, LANES), lambda i, j: (0, 0)),
        ],
        out_specs=out_specs,
        scratch_shapes=[pltpu.VMEM((tm, D_MODEL), BF16)],
        compiler_params=_cparams(("arbitrary", "arbitrary")),
        name="in_projection",
    )(x, mod, g, w_main, w_small, bf_pad)


def _cumsum_kernel(x_ref, o_ref):
    x = x_ref[0]
    n = x.shape[0]
    lane = lax.broadcasted_iota(I32, x.shape, 1)
    row = lax.broadcasted_iota(I32, x.shape, 0)
    s = N_HEADS
    while s < LANES:
        x = x + jnp.where(lane >= s, pltpu.roll(x, s, axis=1), 0.0)
        s *= 2
    t = jnp.where(lane >= LANES - N_HEADS, x, 0.0)
    s = N_HEADS
    while s < LANES:
        t = t + pltpu.roll(t, s, axis=1)
        s *= 2
    t = jnp.where(row >= 1, pltpu.roll(t, 1, axis=0), 0.0)
    s = 1
    while s < n:
        if s < 8:
            sh = jnp.where(row >= s, pltpu.roll(t, s, axis=0), 0.0)
        else:
            sh = jnp.concatenate([jnp.zeros((s, LANES), F32), t[:n - s]], axis=0)
        t = t + sh
        s *= 2
    o_ref[0] = x + t


def _forget_cumsum(logf_all):
    b, lp, h = logf_all.shape
    n = lp * h // LANES
    out = pl.pallas_call(
        _cumsum_kernel,
        out_shape=jax.ShapeDtypeStruct((b, n, LANES), F32),
        grid=(b,),
        in_specs=[pl.BlockSpec((1, n, LANES), lambda i: (i, 0, 0))],
        out_specs=pl.BlockSpec((1, n, LANES), lambda i: (i, 0, 0)),
        compiler_params=_cparams(("arbitrary",)),
        name="forget_cumsum",
    )(logf_all.reshape(b, n, LANES))
    return out.reshape(b, lp, h)


def _pair_padded(qT_ref, h):
    pr, half = divmod(h, 2)
    blk = qT_ref[0, pr * LANES + half * HEAD_DIM:pr * LANES + (half + 1) * HEAD_DIM, :]
    z = jnp.zeros_like(blk)
    return jnp.concatenate([blk, z] if half == 0 else [z, blk], axis=0)


def _rows_to_heads(rows):
    tq = rows[0].shape[1]
    sub = lax.broadcasted_iota(I32, (N_HEADS, tq), 0)
    out = jnp.broadcast_to(rows[0], (N_HEADS, tq))
    for h in range(1, N_HEADS):
        out = jnp.where(sub == h, rows[h], out)
    return out


def _stage_and_consume(stage, consume, m_cur, pend, l, acc_ref):
    m_next, sums, alphas = [], [], []
    for h in range(N_HEADS):
        if stage is not None:
            logits, mask, buf = stage
            lg = logits(h)
            if mask is not None:
                lg = jnp.where(mask, lg, NEG_INF)
            buf[h] = lg
            m_next.append(jnp.maximum(m_cur[h:h + 1, :], jnp.max(lg, axis=0, keepdims=True)))
        if consume is not None:
            v_t, buf = consume
            before, after = pend
            alpha = jnp.exp2(before[h:h + 1, :] - after[h:h + 1, :])
            p = jnp.exp2(buf[h] - after[h:h + 1, :]).astype(BF16)
            ones = jnp.ones((16, p.shape[0]), BF16)
            pv = jnp.dot(jnp.concatenate([v_t(h), ones], axis=0), p, preferred_element_type=F32)
            rows = slice(h * HEAD_DIM, (h + 1) * HEAD_DIM)
            acc_ref[rows, :] = alpha * acc_ref[rows, :] + pv[:HEAD_DIM, :]
            sums.append(pv[HEAD_DIM:HEAD_DIM + 1, :])
            alphas.append(alpha)
    if consume is not None:
        l = _rows_to_heads(alphas) * l + _rows_to_heads(sums)
    return (_rows_to_heads(m_next) if stage is not None else m_cur), l


def _attend_block(n_free, n_used, logits, mask, v_t, m_ref, l_ref, acc_ref, buf0, buf1):
    m0 = m_ref[...]
    buf1[...] = jnp.full(buf1.shape, -jnp.inf, F32)

    def quad(masked, qd, carry):
        m_a, m_b, l = carry
        p0 = 2 * qd
        p1 = p0 + 1
        pp = jnp.maximum(p0 - 1, 0)
        mk = (lambda p: mask(p)) if masked else (lambda p: None)
        m_c, l = _stage_and_consume((lambda h: logits(h, p0), mk(p0), buf0),
                                    (lambda h: v_t(h, pp), buf1), m_b, (m_a, m_b), l, acc_ref)
        m_d, l = _stage_and_consume((lambda h: logits(h, p1), mk(p1), buf1),
                                    (lambda h: v_t(h, p0), buf0), m_c, (m_b, m_c), l, acc_ref)
        return m_c, m_d, l

    carry = lax.fori_loop(0, n_free, functools.partial(quad, False), (m0, m0, l_ref[...]))
    m_c, m_d, l = lax.fori_loop(n_free, n_used, functools.partial(quad, True), carry)
    last = 2 * n_used - 1
    _, l = _stage_and_consume(None, (lambda h: v_t(h, last), buf1), m_d, (m_c, m_d), l, acc_ref)
    m_ref[...] = m_d
    l_ref[...] = l


def _softmax_init(m_ref, l_ref, acc_ref):
    m_ref[...] = jnp.full(m_ref.shape, NEG_INF, F32)
    l_ref[...] = jnp.zeros(l_ref.shape, F32)
    acc_ref[...] = jnp.zeros(acc_ref.shape, F32)


def _softmax_finish(o_ref, l_ref, acc_ref):
    inv = 1.0 / l_ref[...]
    parts = [acc_ref[h * HEAD_DIM:(h + 1) * HEAD_DIM, :] * inv[h:h + 1, :] for h in range(N_HEADS)]
    o_ref[0] = jnp.concatenate(parts, axis=0).T.astype(o_ref.dtype)


def _f32_to_ordered(x):
    b = lax.bitcast_convert_type(x, I32)
    return jnp.where(b >= 0, b, b ^ jnp.int32(0x7FFFFFFF))


def _ordered_to_f32(o):
    return lax.bitcast_convert_type(jnp.where(o >= 0, o, o ^ jnp.int32(0x7FFFFFFF)), F32)


def _dsa_kernel(qiT_ref, wiT_ref, qaT_ref, ki_ref, ka_ref, vaT_ref, o_ref,
                s_ref, gm_ref, thr_ref, cut_ref, qp_ref, m_ref, l_ref, acc_ref, lg0_ref, lg1_ref,
                *, past, length, n_sel, tq, tk, nk):
    i = pl.program_id(1)
    j = pl.program_id(2)
    q0 = past + i * tq
    qpos = q0 + lax.broadcasted_iota(I32, (1, tq), 1)
    vis = jnp.minimum(((qpos >> CHUNK_SHIFT) + 1) * CHUNK, length)
    vmax = jnp.minimum((((q0 + tq - 1) >> CHUNK_SHIFT) + 1) * CHUNK, length)
    nvis = (vmax + tk - 1) // tk
    vcap = ((vmax + KEY_QUAD - 1) // KEY_QUAD) * KEY_QUAD

    def rows_here(jb, step):
        return jnp.clip((vcap - jb * tk) // step, 0, tk // step)

    @pl.when(jnp.logical_and(j < nk, j < nvis))
    def _():
        def body(c, carry):
            for half in range(GROUPS // SCORE_ROWS):
                r0 = pl.multiple_of(c * GROUPS + half * SCORE_ROWS, SCORE_ROWS)
                kblk = ki_ref[0, pl.ds(r0, SCORE_ROWS), :]
                acc = jnp.zeros((SCORE_ROWS, tq), F32)
                for h in range(N_HEADS):
                    d = jnp.dot(kblk, qiT_ref[0, h * IDX_DIM:(h + 1) * IDX_DIM, :],
                                preferred_element_type=F32)
                    acc = acc + jnp.maximum(d, 0.0) * wiT_ref[0, h:h + 1, :]
                kpos = j * tk + r0 + lax.broadcasted_iota(I32, (SCORE_ROWS, 1), 0)
                acc = jnp.where(kpos < vis, acc, -jnp.inf)
                s_ref[pl.ds(pl.multiple_of(j * tk + r0, SCORE_ROWS), SCORE_ROWS), :] = acc
                grp = slice(half * SCORE_ROWS, (half + 1) * SCORE_ROWS)
                gm_ref[grp, :] = jnp.maximum(gm_ref[grp, :], acc)
            return carry

        @pl.when(j == 0)
        def _():
            gm_ref[...] = jnp.full(gm_ref.shape, -jnp.inf, F32)

        lax.fori_loop(0, rows_here(j, GROUPS), body, 0)

    @pl.when(j == nk - 1)
    def _():
        nch = vcap // COUNT_ROWS

        def count(pred):
            sub = 64

            def body(c, accs):
                accs = list(accs)
                for k in range(COUNT_ROWS // sub):
                    r0 = pl.multiple_of(c * COUNT_ROWS + k * sub, sub)
                    hit = jnp.where(pred(s_ref[pl.ds(r0, sub), :], r0), 1.0, 0.0)
                    accs[k % 4] = accs[k % 4] + jnp.sum(hit.reshape(sub // 8, 8, tq), axis=0)
                return tuple(accs)
            zero = jnp.zeros((8, tq), F32)
            a0, a1, a2, a3 = lax.fori_loop(0, nch, body, (zero, zero, zero, zero))
            return jnp.sum((a0 + a1) + (a2 + a3), axis=0, keepdims=True)

        def count_ge(t):
            return count(lambda blk, r0: blk >= t)

        kf = float(n_sel)
        no_cut = jnp.full((1, tq), 2 ** 30, I32)

        gm = gm_ref[...]
        few = vis < n_sel
        lo0 = _f32_to_ordered(jnp.min(gm, axis=0, keepdims=True))
        hi0 = _f32_to_ordered(jnp.max(gm, axis=0, keepdims=True)) + 1

        def finished(lo, hi, c_lo):
            return jnp.logical_or(few, jnp.logical_or(c_lo == kf, hi <= lo + 1))

        def more(state):
            _, lo, hi, c_lo = state
            return jnp.max(jnp.where(finished(lo, hi, c_lo), 0.0, 1.0)) > 0.0

        def step(state):
            k, lo, hi, c_lo = state
            mid_f = _f32_to_ordered(0.5 * _ordered_to_f32(lo) + 0.5 * _ordered_to_f32(hi - 1))
            mid_i = (lo >> 1) + (hi >> 1) + (lo & hi & 1)
            wide = ((lo ^ (hi - 1)) >> 23) != 0
            piv = jnp.where(jnp.logical_and(wide, k < FLOAT_STEPS), mid_f, mid_i)
            piv = jnp.where(jnp.logical_and(lo == 0, hi > 1), 1, piv)
            piv = jnp.where(jnp.logical_and(lo < 0, hi > 0), 0, piv)
            piv = jnp.clip(piv, lo + 1, hi - 1)
            c = count_ge(_ordered_to_f32(piv))
            live = jnp.logical_not(finished(lo, hi, c_lo))
            up = jnp.logical_and(live, c >= kf)
            down = jnp.logical_and(live, c < kf)
            return (k + 1, jnp.where(up, piv, lo), jnp.where(down, piv, hi),
                    jnp.where(up, c, c_lo))

        state = (jnp.int32(0), lo0, hi0, jnp.full((1, tq), -1.0, F32))
        _, cur, _, c_cur = lax.while_loop(more, step, state)
        thr = jnp.where(few, -F32_MAX, _ordered_to_f32(cur))

        def with_ties():
            n_gt = count(lambda blk, r0: blk > thr)
            n_ge = count_ge(thr)
            need = kf - n_gt
            tied = (n_ge - n_gt) > need

            def tie_cut():
                rr = lax.broadcasted_iota(I32, (KEY_CHUNK, KEY_CHUNK), 0)
                cc = lax.broadcasted_iota(I32, (KEY_CHUNK, KEY_CHUNK), 1)
                lower = jnp.where(cc <= rr, 1.0, 0.0).astype(BF16)

                def body(c, carry):
                    run, last = carry
                    for u in range(KEY_QUAD // KEY_CHUNK):
                        r0 = pl.multiple_of(c * KEY_QUAD + u * KEY_CHUNK, KEY_CHUNK)
                        eq = s_ref[pl.ds(r0, KEY_CHUNK), :] == thr
                        hit = jnp.where(eq, 1.0, 0.0)
                        ordinal = run + jnp.dot(lower, hit.astype(BF16),
                                                preferred_element_type=F32)
                        kpos = (r0 + lax.broadcasted_iota(I32, (KEY_CHUNK, 1), 0)).astype(F32)
                        take = jnp.logical_and(eq, ordinal <= need)
                        last = jnp.maximum(last, jnp.max(jnp.where(take, kpos, -1.0), axis=0,
                                                         keepdims=True))
                        run = run + jnp.sum(hit, axis=0, keepdims=True)
                    return run, last

                init = (jnp.zeros((1, tq), F32), jnp.full((1, tq), -1.0, F32))
                _, last = lax.fori_loop(0, vcap // KEY_QUAD, body, init)
                return jnp.where(tied, last.astype(I32), no_cut)

            return lax.cond(jnp.max(jnp.where(tied, 1.0, 0.0)) > 0.0, tie_cut, lambda: no_cut)

        unsure = jnp.logical_or(c_cur != kf, few)
        cut = lax.cond(jnp.max(jnp.where(unsure, 1.0, 0.0)) > 0.0, with_ties, lambda: no_cut)
        thr_ref[0:1, :] = thr
        cut_ref[0:1, :] = cut
        for h in range(N_HEADS):
            qp_ref[h] = _pair_padded(qaT_ref, h)
        _softmax_init(m_ref, l_ref, acc_ref)

    jj = j - nk

    @pl.when(jnp.logical_and(j >= nk, jj < nvis))
    def _():
        thr = thr_ref[0:1, :]
        cut = cut_ref[0:1, :]

        def selected(p):
            g0 = pl.multiple_of(jj * tk + p * KEY_PAIR, KEY_PAIR)
            sblk = s_ref[pl.ds(g0, KEY_PAIR), :]
            kpos = g0 + lax.broadcasted_iota(I32, (KEY_PAIR, 1), 0)
            return jnp.logical_or(sblk > thr, jnp.logical_and(sblk == thr, kpos <= cut))

        def logits(h, p):
            r0 = pl.multiple_of(p * KEY_PAIR, KEY_PAIR)
            kblk = ka_ref[0, pl.ds(r0, KEY_PAIR), (h // 2) * LANES:(h // 2 + 1) * LANES]
            return jnp.dot(kblk, qp_ref[h], preferred_element_type=F32)

        def v_t(h, p):
            rows = slice(h * HEAD_DIM, (h + 1) * HEAD_DIM)
            return jnp.concatenate([vaT_ref[0, 2 * p, rows, :], vaT_ref[0, 2 * p + 1, rows, :]],
                                   axis=1)

        _attend_block(0, rows_here(jj, KEY_QUAD), logits, selected, v_t, m_ref, l_ref, acc_ref,
                      lg0_ref, lg1_ref)

    @pl.when(j == 2 * nk - 1)
    def _():
        _softmax_finish(o_ref, l_ref, acc_ref)


def _dsa_attention(qiT, wiT, qaT, ki, ka, vaTc, *, past, length, tq, tk):
    b, _, tqp = qiT.shape
    lp = ki.shape[1]
    nq, nk = tqp // tq, lp // tk
    n_sel = min(TOPK_MAX, length // 4)

    def nvis(i):
        vmax = jnp.minimum(((past + (i + 1) * tq - 1) // CHUNK + 1) * CHUNK, length)
        return (vmax + tk - 1) // tk

    k1 = lambda bb, i, j: (bb, jnp.minimum(j, nvis(i) - 1), 0)
    k3 = lambda bb, i, j: (bb, jnp.clip(j - nk, 0, nvis(i) - 1), 0)
    k3c = lambda bb, i, j: (bb, jnp.clip(j - nk, 0, nvis(i) - 1), 0, 0)
    qmap = lambda bb, i, j: (bb, 0, i)
    kern = functools.partial(_dsa_kernel, past=past, length=length, n_sel=n_sel, tq=tq, tk=tk,
                             nk=nk)
    return pl.pallas_call(
        kern,
        out_shape=jax.ShapeDtypeStruct((b, tqp, D_BRANCH), BF16),
        grid=(b, nq, 2 * nk),
        in_specs=[
            pl.BlockSpec((1, D_BRANCH, tq), qmap),
            pl.BlockSpec((1, N_HEADS, tq), qmap),
            pl.BlockSpec((1, D_BRANCH, tq), qmap),
            pl.BlockSpec((1, tk, IDX_DIM), k1),
            pl.BlockSpec((1, tk, D_BRANCH), k3),
            pl.BlockSpec((1, tk // KEY_CHUNK, D_BRANCH, KEY_CHUNK), k3c),
        ],
        out_specs=pl.BlockSpec((1, tq, D_BRANCH), lambda bb, i, j: (bb, i, 0)),
        scratch_shapes=[
            pltpu.VMEM((lp, tq), F32),
            pltpu.VMEM((GROUPS, tq), F32),
            pltpu.VMEM((8, tq), F32),
            pltpu.VMEM((8, tq), I32),
            pltpu.VMEM((N_HEADS, LANES, tq), BF16),
            pltpu.VMEM((N_HEADS, tq), F32),
            pltpu.VMEM((N_HEADS, tq), F32),
            pltpu.VMEM((D_BRANCH, tq), F32),
            pltpu.VMEM((N_HEADS, KEY_PAIR, tq), F32),
            pltpu.VMEM((N_HEADS, KEY_PAIR, tq), F32),
        ],
        compiler_params=_cparams(("arbitrary", "arbitrary", "arbitrary")),
        name="dsa_attention",
    )(qiT, wiT, qaT, ki, ka, vaTc)


_AUG = 6


def _fox_kernel(qbT_ref, qaug_ref, kb_ref, kaug_ref, vbT_ref, o_ref,
                qf_ref, m_ref, l_ref, acc_ref, lg0_ref, lg1_ref, *, past, tq, tk, nk):
    i = pl.program_id(1)
    j = pl.program_id(2)
    q0 = past + i * tq
    qpos = q0 + lax.broadcasted_iota(I32, (1, tq), 1)
    nvis = (q0 + tq + tk - 1) // tk
    nquad = tk // KEY_QUAD
    n_free = jnp.clip((q0 + 1 - j * tk) // KEY_QUAD, 0, nquad)
    n_used = jnp.clip((q0 + tq - j * tk + KEY_QUAD - 1) // KEY_QUAD, 0, nquad)

    @pl.when(j == 0)
    def _():
        arow = lax.broadcasted_iota(I32, (LANES, 1), 0)
        qa = qaug_ref[0]
        for h in range(N_HEADS):
            mine = jnp.logical_and(arow >= _AUG * h, arow < _AUG * (h + 1))
            qf_ref[h, 0:LANES, :] = _pair_padded(qbT_ref, h)
            qf_ref[h, LANES:2 * LANES, :] = jnp.where(mine, qa, 0.0).astype(BF16)
        _softmax_init(m_ref, l_ref, acc_ref)

    @pl.when(j < nvis)
    def _():
        def causal(p):
            kpos = j * tk + p * KEY_PAIR + lax.broadcasted_iota(I32, (KEY_PAIR, 1), 0)
            return kpos <= qpos

        def logits(h, p):
            r0 = pl.multiple_of(p * KEY_PAIR, KEY_PAIR)
            kblk = kb_ref[0, pl.ds(r0, KEY_PAIR), (h // 2) * LANES:(h // 2 + 1) * LANES]
            lhs = jnp.concatenate([kblk, kaug_ref[0, pl.ds(r0, KEY_PAIR), :]], axis=1)
            return jnp.dot(lhs, qf_ref[h], preferred_element_type=F32)

        def v_t(h, p):
            rows = slice(h * HEAD_DIM, (h + 1) * HEAD_DIM)
            return jnp.concatenate([vbT_ref[0, 2 * p, rows, :], vbT_ref[0, 2 * p + 1, rows, :]],
                                   axis=1)

        _attend_block(n_free, n_used, logits, causal, v_t, m_ref, l_ref, acc_ref,
                      lg0_ref, lg1_ref)

    @pl.when(j == nk - 1)
    def _():
        _softmax_finish(o_ref, l_ref, acc_ref)


def _fox_attention(qbT, qaug, kb, kaug, vbTc, *, past, tq, tk):
    b, _, tqp = qbT.shape
    lp = kb.shape[1]
    nq, nk = tqp // tq, lp // tk

    def last(i):
        return jnp.minimum((past + (i + 1) * tq + tk - 1) // tk, nk) - 1

    kmap = lambda bb, i, j: (bb, jnp.minimum(j, last(i)), 0)
    kmapc = lambda bb, i, j: (bb, jnp.minimum(j, last(i)), 0, 0)
    qmap = lambda bb, i, j: (bb, 0, i)
    return pl.pallas_call(
        functools.partial(_fox_kernel, past=past, tq=tq, tk=tk, nk=nk),
        out_shape=jax.ShapeDtypeStruct((b, tqp, D_BRANCH), BF16),
        grid=(b, nq, nk),
        in_specs=[
            pl.BlockSpec((1, D_BRANCH, tq), qmap),
            pl.BlockSpec((1, LANES, tq), qmap),
            pl.BlockSpec((1, tk, D_BRANCH), kmap),
            pl.BlockSpec((1, tk, LANES), kmap),
            pl.BlockSpec((1, tk // KEY_CHUNK, D_BRANCH, KEY_CHUNK), kmapc),
        ],
        out_specs=pl.BlockSpec((1, tq, D_BRANCH), lambda bb, i, j: (bb, i, 0)),
        scratch_shapes=[
            pltpu.VMEM((N_HEADS, 2 * LANES, tq), BF16),
            pltpu.VMEM((N_HEADS, tq), F32),
            pltpu.VMEM((N_HEADS, tq), F32),
            pltpu.VMEM((D_BRANCH, tq), F32),
            pltpu.VMEM((N_HEADS, KEY_PAIR, tq), F32),
            pltpu.VMEM((N_HEADS, KEY_PAIR, tq), F32),
        ],
        compiler_params=_cparams(("arbitrary", "arbitrary", "arbitrary")),
        name="fox_attention",
    )(qbT, qaug, kb, kaug, vbTc)


AUG_ROWS = 1024


def _aug_kernel(c_ref, kaug_ref, qaugT_ref):
    x = c_ref[0] * LOG2E
    a = x.astype(BF16).astype(F32)
    r = x - a
    b = r.astype(BF16).astype(F32)
    c = (r - b).astype(BF16).astype(F32)
    lane = lax.broadcasted_iota(I32, (N_HEADS, LANES), 1)
    head = lax.broadcasted_iota(I32, (N_HEADS, LANES), 0)

    def place(v, slot):
        sel = jnp.where(lane == _AUG * head + slot, 1.0, 0.0)
        return jnp.dot(v, sel, preferred_element_type=F32, precision=lax.Precision.HIGHEST)

    slot = lane - _AUG * head
    ones_k = jnp.sum(jnp.where(jnp.logical_and(slot >= 0, slot < 3), 1.0, 0.0), axis=0, keepdims=True)
    ones_q = jnp.sum(jnp.where(jnp.logical_and(slot >= 3, slot < 6), 1.0, 0.0), axis=0, keepdims=True)
    kaug_ref[0] = (ones_k - (place(a, 3) + place(b, 4) + place(c, 5))).astype(BF16)
    qaugT_ref[0] = (ones_q + (place(a, 0) + place(b, 1) + place(c, 2))).T


def _fox_augment(cum):
    b, lp, h = cum.shape
    return pl.pallas_call(
        _aug_kernel,
        out_shape=(jax.ShapeDtypeStruct((b, lp, LANES), BF16),
                   jax.ShapeDtypeStruct((b, LANES, lp), F32)),
        grid=(b, lp // AUG_ROWS),
        in_specs=[pl.BlockSpec((1, AUG_ROWS, h), lambda i, j: (i, j, 0))],
        out_specs=(pl.BlockSpec((1, AUG_ROWS, LANES), lambda i, j: (i, j, 0)),
                   pl.BlockSpec((1, LANES, AUG_ROWS), lambda i, j: (i, 0, j))),
        compiler_params=_cparams(("arbitrary", "arbitrary")),
        name="fox_augment",
    )(cum)


def _shifted_rows(buf, load, store, u, past_ref, starts_batch, nb):
    tm, c = u.shape
    store(slice(8, 8 + tm), u)
    if nb == 1:
        @pl.when(starts_batch)
        def _():
            store(slice(6, 8), past_ref[0])
    s1 = load(slice(7, 7 + tm))
    s2 = load(slice(6, 6 + tm))
    if nb > 1:
        r = lax.broadcasted_iota(I32, (nb, tm // nb, 1), 1)
        p0 = past_ref[:, 0:1, :]
        p1 = past_ref[:, 1:2, :]
        s1 = jnp.where(r == 0, p1, _group(s1, nb)).reshape(tm, c)
        s2 = jnp.where(r == 0, p0, jnp.where(r == 1, p1, _group(s2, nb))).reshape(tm, c)
    store(slice(0, 8), load(slice(tm, tm + 8)))
    return s1, s2


def _merge_kernel(x_ref, ya_ref, yb_ref, cb_ref, u_ref, gl_ref, past_ref, cw_ref, wbr_ref,
                  wo_ref, g_ref, mod_ref, o_ref, ub_ref, *, nb, rows_per_batch):
    i = pl.program_id(0)
    tm = x_ref.shape[0]
    u = u_ref[...]

    def load(rows):
        return ub_ref[rows, :]

    def store(rows, v):
        ub_ref[rows, :] = v

    s1, s2 = _shifted_rows(ub_ref, load, store, u, past_ref, (i * tm) % rows_per_batch == 0, nb)
    conv = cw_ref[0:1, :] * s2 + cw_ref[1:2, :] * s1 + cw_ref[2:3, :] * u
    yc = (cb_ref[...] * conv).astype(BF16)
    mix = jnp.zeros((tm, D_MODEL), F32)
    for n, y in enumerate((ya_ref[...], yb_ref[...], yc)):
        br = jnp.dot(y, wbr_ref[n], preferred_element_type=F32)
        mix = mix + jax.nn.sigmoid(gl_ref[:, n * D_MODEL:(n + 1) * D_MODEL]) * br
    mo = jnp.dot(mix.astype(BF16), wo_ref[...], preferred_element_type=F32)
    nm = _group(_rms(mo, g_ref[1:2, :]), nb)
    o_ref[...] = x_ref[...] + (mod_ref[:, 2:3, :] * nm).reshape(tm, D_MODEL)


def _mod_spec(tm, nb, rows_per_batch, ngrid):
    if nb == 1:
        f = lambda i, *_: ((i * tm) // rows_per_batch, 0, 0)
    else:
        f = lambda i, *_: (i, 0, 0)
    return pl.BlockSpec((nb, 6, D_MODEL), f)


def _merge(x, ya, yb, cb, u, gl, past, cw, wbr, wo, g, mod, *, tm, rows_per_batch):
    rows = x.shape[0]
    nb = max(1, tm // rows_per_batch)
    row = lambda w: pl.BlockSpec((tm, w), lambda i: (i, 0))
    full = lambda shape: pl.BlockSpec(shape, lambda i: (0,) * len(shape))
    if nb == 1:
        past_spec = pl.BlockSpec((1, 2, D_BRANCH), lambda i: ((i * tm) // rows_per_batch, 0, 0))
    else:
        past_spec = pl.BlockSpec((nb, 2, D_BRANCH), lambda i: (i, 0, 0))
    return pl.pallas_call(
        functools.partial(_merge_kernel, nb=nb, rows_per_batch=rows_per_batch),
        out_shape=jax.ShapeDtypeStruct((rows, D_MODEL), F32),
        grid=(rows // tm,),
        in_specs=[
            row(D_MODEL), row(D_BRANCH), row(D_BRANCH), row(D_BRANCH), row(D_BRANCH),
            row(3 * D_MODEL), past_spec, full((CONV_W, D_BRANCH)),
            full((3, D_BRANCH, D_MODEL)), full((D_MODEL, D_MODEL)), full((4, D_MODEL)),
            _mod_spec(tm, nb, rows_per_batch, 1),
        ],
        out_specs=row(D_MODEL),
        scratch_shapes=[pltpu.VMEM((tm + 8, D_BRANCH), F32)],
        compiler_params=_cparams(("arbitrary",)),
        name="branch_merge",
    )(x, ya, yb, cb, u, gl, past, cw, wbr, wo, g, mod)


FF_CHUNK = 1408


def _ffn_kernel(x_ref, g_ref, mod_ref, wg_ref, wv_ref, wd_ref, cw_ref, past_ref,
                o_ref, tail_ref, hs_ref, acc_ref, ub_ref, *, nb, rows_per_batch, nj):
    i = pl.program_id(0)
    j = pl.program_id(1)
    tm = x_ref.shape[0]

    @pl.when(j == 0)
    def _():
        y = _group(_rms(x_ref[...], g_ref[2:3, :]), nb)
        h = y * (1.0 + mod_ref[:, 4:5, :]) + mod_ref[:, 3:4, :]
        hs_ref[...] = h.reshape(tm, D_MODEL).astype(BF16)
        acc_ref[...] = jnp.zeros(acc_ref.shape, F32)

    hs = hs_ref[...]
    ug = jnp.dot(hs, wg_ref[...], preferred_element_type=F32)
    uv = jnp.dot(hs, wv_ref[...], preferred_element_type=F32)

    def load(rows):
        return ub_ref[j, rows, :]

    def store(rows, v):
        ub_ref[j, rows, :] = v

    s1, s2 = _shifted_rows(ub_ref, load, store, ug, past_ref, (i * tm) % rows_per_batch == 0, nb)
    conv = cw_ref[0:1, :] * s2 + cw_ref[1:2, :] * s1 + cw_ref[2:3, :] * ug
    f = conv * jax.nn.sigmoid(conv) * uv
    acc_ref[...] += jnp.dot(f.astype(BF16), wd_ref[...], preferred_element_type=F32)
    grp = tm // nb
    tail_ref[0] = _group(ug, nb)[:, grp - 8:, :]

    @pl.when(j == nj - 1)
    def _():
        nm = _group(_rms(acc_ref[...], g_ref[3:4, :]), nb)
        o_ref[...] = x_ref[...] + (mod_ref[:, 5:6, :] * nm).reshape(tm, D_MODEL)


def _conv_ffn(x, g, mod, w_up, w_down, cw, past, *, tm, rows_per_batch):
    rows = x.shape[0]
    dff = w_down.shape[0]
    nj = dff // FF_CHUNK
    nb = max(1, tm // rows_per_batch)
    nt = rows // tm
    if nb == 1:
        past_spec = pl.BlockSpec((1, 2, FF_CHUNK), lambda i, j: ((i * tm) // rows_per_batch, 0, j))
        mod_map = lambda i, j: ((i * tm) // rows_per_batch, 0, 0)
    else:
        past_spec = pl.BlockSpec((nb, 2, FF_CHUNK), lambda i, j: (i, 0, j))
        mod_map = lambda i, j: (i, 0, 0)
    out, tails = pl.pallas_call(
        functools.partial(_ffn_kernel, nb=nb, rows_per_batch=rows_per_batch, nj=nj),
        out_shape=(jax.ShapeDtypeStruct((rows, D_MODEL), F32),
                   jax.ShapeDtypeStruct((nt, nb, 8, dff), F32)),
        grid=(nt, nj),
        in_specs=[
            pl.BlockSpec((tm, D_MODEL), lambda i, j: (i, 0)),
            pl.BlockSpec((4, D_MODEL), lambda i, j: (0, 0)),
            pl.BlockSpec((nb, 6, D_MODEL), mod_map),
            pl.BlockSpec((D_MODEL, FF_CHUNK), lambda i, j: (0, j)),
            pl.BlockSpec((D_MODEL, FF_CHUNK), lambda i, j: (0, j + nj)),
            pl.BlockSpec((FF_CHUNK, D_MODEL), lambda i, j: (j, 0)),
            pl.BlockSpec((CONV_W, FF_CHUNK), lambda i, j: (0, j)),
            past_spec,
        ],
        out_specs=(pl.BlockSpec((tm, D_MODEL), lambda i, j: (i, 0)),
                   pl.BlockSpec((1, nb, 8, FF_CHUNK), lambda i, j: (i, 0, 0, j))),
        scratch_shapes=[
            pltpu.VMEM((tm, D_MODEL), BF16),
            pltpu.VMEM((tm, D_MODEL), F32),
            pltpu.VMEM((nj, tm + 8, FF_CHUNK), F32),
        ],
        compiler_params=_cparams(("arbitrary", "arbitrary")),
        name="conv_ffn",
    )(x, g, mod, w_up, w_up, w_down, cw, past)
    return out, tails


def _relayout_w_in(w_in, b_forget):
    db = D_BRANCH
    o_qi = 3 * db
    o_ki = o_qi + N_HEADS * IDX_DIM
    o_wi = o_ki + IDX_DIM
    o_qb = o_wi + N_HEADS
    o_fl = o_qb + 3 * db
    o_cb = o_fl + N_HEADS
    o_gl = o_cb + 3 * db
    main = jnp.concatenate([w_in[:, 0:o_ki], w_in[:, o_qb:o_fl], w_in[:, o_cb:]], axis=1)
    small = jnp.concatenate([w_in[:, o_ki:o_wi], w_in[:, o_wi:o_qb], w_in[:, o_fl:o_cb]], axis=1)
    small = jnp.pad(small, ((0, 0), (0, LANES - small.shape[1])))
    bf = jnp.zeros((1, LANES), F32).at[0, _FL_OFF:_FL_OFF + N_HEADS].set(b_forget)
    del o_gl
    return main.astype(BF16), small.astype(BF16), bf


def _round_up(x, m):
    return (x + m - 1) // m * m


def _layer(x, mod, caches, weights, *, batch, t, past, cfg):
    (g, w_main, w_small, bf_pad, cw_mix, wbr, wo, w_up, cw_ffn, w_down) = weights
    rows = batch * t
    length = past + t
    (qaT, ka32, ka16, va32, vaTc, qiT, qbT, kb32, kb16, vb32, vbTc, cb, u, gl,
     ki32, ki16, wiT, logf) = _in_projection(
        x, mod, g, w_main, w_small, bf_pad, tm=cfg["tm_in"], rows_per_batch=t)

    tq, tk = cfg["tq"], cfg["tk"]
    tqp = _round_up(t, tq)
    lp = _round_up(length, tk)

    def per_batch_T(aT):
        c = aT.shape[0]
        a = jnp.swapaxes(aT.reshape(c, batch, t), 0, 1)
        return jnp.pad(a, ((0, 0), (0, 0), (0, tqp - t)))

    def chunked_T(a):
        return jnp.swapaxes(a.reshape(batch, lp // KEY_CHUNK, KEY_CHUNK, a.shape[-1]), 2, 3)

    def with_cache(cache, new, dtype):
        new = new.reshape(batch, t, -1)
        if cache is not None:
            new = jnp.concatenate([cache.reshape(batch, past, -1).astype(dtype), new.astype(dtype)],
                                  axis=1)
        return jnp.pad(new.astype(dtype), ((0, 0), (0, lp - length), (0, 0)))

    if caches is None:
        c_idx = c_dk = c_dv = c_fk = c_fv = c_lf = None
        past_mix = jnp.zeros((batch, CONV_W - 1, D_BRANCH), F32)
        past_ffn = jnp.zeros((batch, CONV_W - 1, w_down.shape[0]), F32)
    else:
        c_idx, c_dk, c_dv, c_fk, c_fv, c_lf, past_mix, past_ffn = caches

    ki_all = with_cache(c_idx, ki16, BF16)
    ka_all = with_cache(c_dk, ka16, BF16)
    kb_all = with_cache(c_fk, kb16, BF16)
    if caches is None and lp == rows:
        vaT_all = vaTc.reshape(batch, lp // KEY_CHUNK, D_BRANCH, KEY_CHUNK)
        vbT_all = vbTc.reshape(batch, lp // KEY_CHUNK, D_BRANCH, KEY_CHUNK)
    else:
        vaT_all = chunked_T(with_cache(c_dv, va32, BF16))
        vbT_all = chunked_T(with_cache(c_fv, vb32, BF16))
    logf_all = with_cache(c_lf, logf, F32)

    ya = _dsa_attention(per_batch_T(qiT), per_batch_T(wiT), per_batch_T(qaT), ki_all, ka_all,
                        vaT_all, past=past, length=length, tq=tq, tk=tk)
    cum = _forget_cumsum(logf_all)
    kaug, qaug = _fox_augment(cum)
    qaug = jnp.pad(qaug[:, :, past:length], ((0, 0), (0, 0), (0, tqp - t)))
    yb = _fox_attention(per_batch_T(qbT), qaug, kb_all, kaug, vbT_all, past=past, tq=tq, tk=tk)
    ya = ya[:, :t].reshape(rows, D_BRANCH)
    yb = yb[:, :t].reshape(rows, D_BRANCH)

    x1 = _merge(x, ya, yb, cb, u, gl, past_mix, cw_mix, wbr, wo, g, mod,
                tm=cfg["tm_merge"], rows_per_batch=t)
    x2, tails = _conv_ffn(x1, g, mod, w_up, w_down, cw_ffn, past_ffn,
                          tm=cfg["tm_ffn"], rows_per_batch=t)

    new_mix = u.reshape(batch, t, D_BRANCH)[:, t - (CONV_W - 1):]
    nb = max(1, cfg["tm_ffn"] // t)
    if nb == 1:
        tiles_per_batch = t // cfg["tm_ffn"]
        last = tails.reshape(batch, tiles_per_batch, 8, -1)[:, -1]
    else:
        last = tails.reshape(batch, 8, -1)
    new_ffn = last[:, 8 - (CONV_W - 1):]
    hd = (batch, t, N_HEADS, HEAD_DIM)
    state = (ki32.reshape(batch, t, IDX_DIM), ka32.reshape(hd), va32.reshape(hd),
             kb32.reshape(hd), vb32.reshape(hd), logf.reshape(batch, t, N_HEADS), new_mix, new_ffn)
    return x2, state


_PROMPT_CFG = dict(tm_in=512, tm_merge=256, tm_ffn=256, tq=256, tk=4096)
_SAMPLE_CFG = dict(tm_in=256, tm_merge=256, tm_ffn=256, tq=128, tk=3072)


def kernel(x_prompt, x_sample, c_prompt, c_sample, cache_idx_k, cache_dsa_k, cache_dsa_v,
           cache_fox_k, cache_fox_v, cache_fox_logf, state_conv_mix, state_conv_ffn,
           w_ada, b_ada, norm_g, w_in, b_forget, conv_mix_w, w_branch, w_out, w_up,
           conv_ffn_w, w_down):
    bp, tp, _ = x_prompt.shape
    bs, ts, _ = x_sample.shape
    past = cache_dsa_k.shape[2]
    depth = w_ada.shape[0]

    c_all = jnp.concatenate([c_prompt, c_sample], axis=0)
    pad_rows = _round_up(c_all.shape[0], 8) - c_all.shape[0]
    c_all = jnp.pad(c_all, ((0, pad_rows), (0, 0)))
    mod_all = _modulation(c_all, w_ada, b_ada).reshape(depth, -1, 6, D_MODEL)

    yp = x_prompt.reshape(bp * tp, D_MODEL)
    ys = x_sample.reshape(bs * ts, D_MODEL)
    p_states, s_states = [], []
    for l in range(depth):
        w_main, w_small, bf_pad = _relayout_w_in(w_in[l], b_forget[l])
        weights = (norm_g[l], w_main, w_small, bf_pad, conv_mix_w[l], w_branch[l].astype(BF16),
                   w_out[l].astype(BF16), w_up[l].astype(BF16), conv_ffn_w[l],
                   w_down[l].astype(BF16))
        yp, st_p = _layer(yp, mod_all[l, :bp], None, weights, batch=bp, t=tp, past=0,
                          cfg=_PROMPT_CFG)
        caches = (cache_idx_k[l], cache_dsa_k[l], cache_dsa_v[l], cache_fox_k[l], cache_fox_v[l],
                  cache_fox_logf[l], state_conv_mix[l], state_conv_ffn[l])
        ys, st_s = _layer(ys, mod_all[l, bp:bp + bs], caches, weights, batch=bs, t=ts, past=past,
                          cfg=_SAMPLE_CFG)
        p_states.append(st_p)
        s_states.append(st_s)

    stack = lambda states: [jnp.stack([st[k] for st in states], axis=0) for k in range(8)]
    return (yp.reshape(bp, tp, D_MODEL), ys.reshape(bs, ts, D_MODEL), *stack(p_states),
            *stack(s_states))
```

```python
import functools

import jax
import jax.numpy as jnp
from jax import lax
from jax.experimental import pallas as pl
from jax.experimental.pallas import tpu as pltpu

F32 = jnp.float32
BF16 = jnp.bfloat16
I32 = jnp.int32

D_MODEL = 1024
HEAD_DIM = 64
D_BRANCH = 512
N_HEADS = 8
IDX_DIM = 64
CHUNK = 64
CHUNK_SHIFT = 6
TOPK_MAX = 256
CONV_W = 3
EPS = 1e-6
NEG_INF = -1e30
LOG2E = 1.4426950408889634
F32_MAX = 3.4028234663852886e38

LANES = 128
N_MAIN = 16 * D_BRANCH
KEY_CHUNK = 256
KEY_PAIR = 2 * KEY_CHUNK
KEY_QUAD = 2 * KEY_PAIR
SCORE_ROWS = 128
COUNT_ROWS = KEY_QUAD
GROUPS = TOPK_MAX
FLOAT_STEPS = 12
VMEM_LIMIT = 56 * 1024 * 1024


def _cparams(sem):
    return pltpu.CompilerParams(dimension_semantics=sem, vmem_limit_bytes=VMEM_LIMIT)


def _rms(x, g_row):
    return x * lax.rsqrt(jnp.mean(x * x, axis=-1, keepdims=True) + EPS) * g_row


def _group(x, nb):
    return x.reshape(nb, x.shape[0] // nb, x.shape[1])


def _mod_kernel(c_ref, w_ref, b_ref, o_ref):
    c = c_ref[...]
    s = (c * jax.nn.sigmoid(c)).astype(BF16)
    o_ref[0] = jnp.dot(s, w_ref[0].astype(BF16), preferred_element_type=F32) + b_ref[0]


def _modulation(c_all, w_ada, b_ada):
    depth = w_ada.shape[0]
    rows = c_all.shape[0]
    n = w_ada.shape[2]
    tn = D_MODEL
    return pl.pallas_call(
        _mod_kernel,
        out_shape=jax.ShapeDtypeStruct((depth, rows, n), F32),
        grid=(depth, n // tn),
        in_specs=[
            pl.BlockSpec((rows, D_MODEL), lambda l, j: (0, 0)),
            pl.BlockSpec((1, D_MODEL, tn), lambda l, j: (l, 0, j)),
            pl.BlockSpec((1, 1, tn), lambda l, j: (l, 0, j)),
        ],
        out_specs=pl.BlockSpec((1, rows, tn), lambda l, j: (l, 0, j)),
        compiler_params=_cparams(("arbitrary", "arbitrary")),
        name="adaln_mod",
    )(c_all, w_ada, b_ada.reshape(depth, 1, n))


IN_STEP = 2 * D_BRANCH
_GL_STEP = 5
_WI_OFF = IDX_DIM
_FL_OFF = IDX_DIM + N_HEADS


def _inproj_kernel(x_ref, mod_ref, g_ref, wm_ref, ws_ref, bf_ref,
                   qaT_ref, ka32_ref, ka16_ref, va32_ref, vaT_ref, qiT_ref, qbT_ref,
                   kb32_ref, kb16_ref, vb32_ref, vbT_ref, cb_ref, u_ref, gl_ref,
                   ki32_ref, ki16_ref, wiT_ref, logf_ref,
                   hs_ref, *, nb):
    j = pl.program_id(1)
    tm = x_ref.shape[0]

    @pl.when(j == 0)
    def _():
        y = _rms(x_ref[...], g_ref[0:1, :])
        h = _group(y, nb) * (1.0 + mod_ref[:, 1:2, :]) + mod_ref[:, 0:1, :]
        hb = h.reshape(tm, D_MODEL).astype(BF16)
        hs_ref[...] = hb
        sm = jnp.dot(hb, ws_ref[...], preferred_element_type=F32)
        ki32_ref[...] = sm[:, :IDX_DIM]
        ki16_ref[...] = sm[:, :IDX_DIM].astype(BF16)
        wiT_ref[...] = sm.T[_WI_OFF:_WI_OFF + N_HEADS, :]
        z = sm + bf_ref[...]
        lf = -(jnp.maximum(-z, 0.0) + jnp.log1p(jnp.exp(-jnp.abs(z))))
        logf_ref[...] = lf[:, _FL_OFF:_FL_OFF + N_HEADS]

    acc = jnp.dot(hs_ref[...], wm_ref[...], preferred_element_type=F32)
    lo = acc[:, :D_BRANCH]
    hi = acc[:, D_BRANCH:]
    scale = LOG2E * HEAD_DIM ** -0.5

    def store_chunked_T(ref, a):
        aT = a.T.astype(BF16)
        for c in range(tm // KEY_CHUNK):
            ref[c] = aT[:, c * KEY_CHUNK:(c + 1) * KEY_CHUNK]

    @pl.when(j == 0)
    def _():
        qaT_ref[...] = (lo * scale).T.astype(BF16)
        ka32_ref[...] = hi
        ka16_ref[...] = hi.astype(BF16)

    @pl.when(j == 1)
    def _():
        va32_ref[...] = lo
        store_chunked_T(vaT_ref, lo)
        qiT_ref[...] = hi.T.astype(BF16)

    @pl.when(j == 2)
    def _():
        qbT_ref[...] = (lo * scale).T.astype(BF16)
        kb32_ref[...] = hi
        kb16_ref[...] = hi.astype(BF16)

    @pl.when(j == 3)
    def _():
        vb32_ref[...] = lo
        store_chunked_T(vbT_ref, lo)
        cb_ref[...] = hi

    @pl.when(j == 4)
    def _():
        u_ref[...] = lo * hi

    @pl.when(j >= _GL_STEP)
    def _():
        gl_ref[...] = acc


def _in_projection(x, mod, g, w_main, w_small, bf_pad, *, tm, rows_per_batch):
    rows = x.shape[0]
    nt = rows // tm
    nb = max(1, tm // rows_per_batch)
    nstep = N_MAIN // IN_STEP
    if nb == 1:
        mod_map = lambda i, j: ((i * tm) // rows_per_batch, 0, 0)
    else:
        mod_map = lambda i, j: (i, 0, 0)
    row_blk = lambda w: pl.BlockSpec((tm, w), lambda i, j: (i, 0))
    col_blk = lambda h: pl.BlockSpec((h, tm), lambda i, j: (0, i))
    chunkT = pl.BlockSpec((tm // KEY_CHUNK, D_BRANCH, KEY_CHUNK), lambda i, j: (i, 0, 0))
    f32o = lambda w: jax.ShapeDtypeStruct((rows, w), F32)
    b16o = lambda w: jax.ShapeDtypeStruct((rows, w), BF16)
    b16T = jax.ShapeDtypeStruct((D_BRANCH, rows), BF16)
    b16c = jax.ShapeDtypeStruct((rows // KEY_CHUNK, D_BRANCH, KEY_CHUNK), BF16)
    out_shape = (
        b16T, f32o(D_BRANCH), b16o(D_BRANCH), f32o(D_BRANCH), b16c, b16T, b16T,
        f32o(D_BRANCH), b16o(D_BRANCH), f32o(D_BRANCH), b16c, f32o(D_BRANCH), f32o(D_BRANCH),
        f32o(6 * D_BRANCH),
        f32o(IDX_DIM), b16o(IDX_DIM), jax.ShapeDtypeStruct((N_HEADS, rows), F32), f32o(N_HEADS),
    )
    out_specs = (
        col_blk(D_BRANCH), row_blk(D_BRANCH), row_blk(D_BRANCH), row_blk(D_BRANCH), chunkT,
        col_blk(D_BRANCH), col_blk(D_BRANCH),
        row_blk(D_BRANCH), row_blk(D_BRANCH), row_blk(D_BRANCH), chunkT, row_blk(D_BRANCH),
        row_blk(D_BRANCH),
        pl.BlockSpec((tm, IN_STEP), lambda i, j: (i, jnp.clip(j - _GL_STEP, 0, 2))),
        row_blk(IDX_DIM), row_blk(IDX_DIM), col_blk(N_HEADS), row_blk(N_HEADS),
    )
    return pl.pallas_call(
        functools.partial(_inproj_kernel, nb=nb),
        out_shape=out_shape,
        grid=(nt, nstep),
        in_specs=[
            pl.BlockSpec((tm, D_MODEL), lambda i, j: (i, 0)),
            pl.BlockSpec((nb, 6, D_MODEL), mod_map),
            pl.BlockSpec((4, D_MODEL), lambda i, j: (0, 0)),
            pl.BlockSpec((D_MODEL, IN_STEP), lambda i, j: (0, j)),
            pl.BlockSpec((D_MODEL, LANES), lambda i, j: (0, 0)),
            pl.BlockSpec((1, LANES), lambda i, j: (0, 0)),
        ],
        out_specs=out_specs,
        scratch_shapes=[pltpu.VMEM((tm, D_MODEL), BF16)],
        compiler_params=_cparams(("arbitrary", "arbitrary")),
        name="in_projection",
    )(x, mod, g, w_main, w_small, bf_pad)


def _cumsum_kernel(x_ref, o_ref):
    x = x_ref[0]
    n = x.shape[0]
    lane = lax.broadcasted_iota(I32, x.shape, 1)
    row = lax.broadcasted_iota(I32, x.shape, 0)
    s = N_HEADS
    while s < LANES:
        x = x + jnp.where(lane >= s, pltpu.roll(x, s, axis=1), 0.0)
        s *= 2
    t = jnp.where(lane >= LANES - N_HEADS, x, 0.0)
    s = N_HEADS
    while s < LANES:
        t = t + pltpu.roll(t, s, axis=1)
        s *= 2
    t = jnp.where(row >= 1, pltpu.roll(t, 1, axis=0), 0.0)
    s = 1
    while s < n:
        if s < 8:
            sh = jnp.where(row >= s, pltpu.roll(t, s, axis=0), 0.0)
        else:
            sh = jnp.concatenate([jnp.zeros((s, LANES), F32), t[:n - s]], axis=0)
        t = t + sh
        s *= 2
    o_ref[0] = x + t


def _forget_cumsum(logf_all):
    b, lp, h = logf_all.shape
    n = lp * h // LANES
    out = pl.pallas_call(
        _cumsum_kernel,
        out_shape=jax.ShapeDtypeStruct((b, n, LANES), F32),
        grid=(b,),
        in_specs=[pl.BlockSpec((1, n, LANES), lambda i: (i, 0, 0))],
        out_specs=pl.BlockSpec((1, n, LANES), lambda i: (i, 0, 0)),
        compiler_params=_cparams(("arbitrary",)),
        name="forget_cumsum",
    )(logf_all.reshape(b, n, LANES))
    return out.reshape(b, lp, h)


def _pair_padded(qT_ref, h):
    pr, half = divmod(h, 2)
    blk = qT_ref[0, pr * LANES + half * HEAD_DIM:pr * LANES + (half + 1) * HEAD_DIM, :]
    z = jnp.zeros_like(blk)
    return jnp.concatenate([blk, z] if half == 0 else [z, blk], axis=0)


def _rows_to_heads(rows):
    tq = rows[0].shape[1]
    sub = lax.broadcasted_iota(I32, (N_HEADS, tq), 0)
    out = jnp.broadcast_to(rows[0], (N_HEADS, tq))
    for h in range(1, N_HEADS):
        out = jnp.where(sub == h, rows[h], out)
    return out


def _stage_and_consume(stage, consume, m_cur, pend, l, acc_ref):
    m_next, sums, alphas = [], [], []
    for h in range(N_HEADS):
        if stage is not None:
            logits, mask, buf = stage
            lg = logits(h)
            if mask is not None:
                lg = jnp.where(mask, lg, NEG_INF)
            buf[h] = lg
            m_next.append(jnp.maximum(m_cur[h:h + 1, :], jnp.max(lg, axis=0, keepdims=True)))
        if consume is not None:
            v_t, buf = consume
            before, after = pend
            alpha = jnp.exp2(before[h:h + 1, :] - after[h:h + 1, :])
            p = jnp.exp2(buf[h] - after[h:h + 1, :]).astype(BF16)
            ones = jnp.ones((16, p.shape[0]), BF16)
            pv = jnp.dot(jnp.concatenate([v_t(h), ones], axis=0), p, preferred_element_type=F32)
            rows = slice(h * HEAD_DIM, (h + 1) * HEAD_DIM)
            acc_ref[rows, :] = alpha * acc_ref[rows, :] + pv[:HEAD_DIM, :]
            sums.append(pv[HEAD_DIM:HEAD_DIM + 1, :])
            alphas.append(alpha)
    if consume is not None:
        l = _rows_to_heads(alphas) * l + _rows_to_heads(sums)
    return (_rows_to_heads(m_next) if stage is not None else m_cur), l


def _attend_block(n_free, n_used, logits, mask, v_t, m_ref, l_ref, acc_ref, buf0, buf1):
    def fused(p_new, masked, buf_new, buf_old, m_old, m_cur, l):
        return _stage_and_consume((lambda h: logits(h, p_new), mask(p_new) if masked else None,
                                   buf_new),
                                  (lambda h: v_t(h, p_new - 1), buf_old), m_cur, (m_old, m_cur), l,
                                  acc_ref)

    def trip(masked, t, carry):
        m_a, m_b, l = carry
        m_c, l = fused(2 * t + 1, masked, buf1, buf0, m_a, m_b, l)
        m_d, l = fused(2 * t + 2, masked, buf0, buf1, m_b, m_c, l)
        return m_c, m_d, l

    m0 = m_ref[...]
    m1, l = _stage_and_consume((lambda h: logits(h, 0), mask(0), buf0), None, m0, None,
                               l_ref[...], acc_ref)
    split = jnp.maximum(n_free - 1, 0)
    carry = lax.fori_loop(0, split, functools.partial(trip, False), (m0, m1, l))
    m_a, m_b, l = lax.fori_loop(split, n_used - 1, functools.partial(trip, True), carry)
    last = 2 * n_used - 1
    m_c, l = fused(last, True, buf1, buf0, m_a, m_b, l)
    _, l = _stage_and_consume(None, (lambda h: v_t(h, last), buf1), m_c, (m_b, m_c), l, acc_ref)
    m_ref[...] = m_c
    l_ref[...] = l


def _softmax_init(m_ref, l_ref, acc_ref):
    m_ref[...] = jnp.full(m_ref.shape, NEG_INF, F32)
    l_ref[...] = jnp.zeros(l_ref.shape, F32)
    acc_ref[...] = jnp.zeros(acc_ref.shape, F32)


def _softmax_finish(o_ref, l_ref, acc_ref):
    inv = 1.0 / l_ref[...]
    parts = [acc_ref[h * HEAD_DIM:(h + 1) * HEAD_DIM, :] * inv[h:h + 1, :] for h in range(N_HEADS)]
    o_ref[0] = jnp.concatenate(parts, axis=0).T.astype(o_ref.dtype)


def _f32_to_ordered(x):
    b = lax.bitcast_convert_type(x, I32)
    return jnp.where(b >= 0, b, b ^ jnp.int32(0x7FFFFFFF))


def _ordered_to_f32(o):
    return lax.bitcast_convert_type(jnp.where(o >= 0, o, o ^ jnp.int32(0x7FFFFFFF)), F32)


def _dsa_kernel(qiT_ref, wiT_ref, qaT_ref, ki_ref, ka_ref, vaT_ref, o_ref,
                s_ref, gm_ref, thr_ref, cut_ref, qp_ref, m_ref, l_ref, acc_ref, lg0_ref, lg1_ref,
                *, past, length, n_sel, tq, tk, nk):
    i = pl.program_id(1)
    j = pl.program_id(2)
    q0 = past + i * tq
    qpos = q0 + lax.broadcasted_iota(I32, (1, tq), 1)
    vis = jnp.minimum(((qpos >> CHUNK_SHIFT) + 1) * CHUNK, length)
    vmax = jnp.minimum((((q0 + tq - 1) >> CHUNK_SHIFT) + 1) * CHUNK, length)
    nvis = (vmax + tk - 1) // tk
    vcap = ((vmax + KEY_QUAD - 1) // KEY_QUAD) * KEY_QUAD

    def rows_here(jb, step):
        return jnp.clip((vcap - jb * tk) // step, 0, tk // step)

    @pl.when(jnp.logical_and(j < nk, j < nvis))
    def _():
        def body(c, carry):
            for half in range(GROUPS // SCORE_ROWS):
                r0 = pl.multiple_of(c * GROUPS + half * SCORE_ROWS, SCORE_ROWS)
                kblk = ki_ref[0, pl.ds(r0, SCORE_ROWS), :]
                acc = jnp.zeros((SCORE_ROWS, tq), F32)
                for h in range(N_HEADS):
                    d = jnp.dot(kblk, qiT_ref[0, h * IDX_DIM:(h + 1) * IDX_DIM, :],
                                preferred_element_type=F32)
                    acc = acc + jnp.maximum(d, 0.0) * wiT_ref[0, h:h + 1, :]
                kpos = j * tk + r0 + lax.broadcasted_iota(I32, (SCORE_ROWS, 1), 0)
                acc = jnp.where(kpos < vis, acc, -jnp.inf)
                s_ref[pl.ds(pl.multiple_of(j * tk + r0, SCORE_ROWS), SCORE_ROWS), :] = acc
                grp = slice(half * SCORE_ROWS, (half + 1) * SCORE_ROWS)
                gm_ref[grp, :] = jnp.maximum(gm_ref[grp, :], acc)
            return carry

        @pl.when(j == 0)
        def _():
            gm_ref[...] = jnp.full(gm_ref.shape, -jnp.inf, F32)

        lax.fori_loop(0, rows_here(j, GROUPS), body, 0)

    @pl.when(j == nk - 1)
    def _():
        nch = vcap // COUNT_ROWS

        def count(pred):
            sub = 64

            def body(c, accs):
                accs = list(accs)
                for k in range(COUNT_ROWS // sub):
                    r0 = pl.multiple_of(c * COUNT_ROWS + k * sub, sub)
                    hit = jnp.where(pred(s_ref[pl.ds(r0, sub), :], r0), 1.0, 0.0)
                    accs[k % 4] = accs[k % 4] + jnp.sum(hit.reshape(sub // 8, 8, tq), axis=0)
                return tuple(accs)
            zero = jnp.zeros((8, tq), F32)
            a0, a1, a2, a3 = lax.fori_loop(0, nch, body, (zero, zero, zero, zero))
            return jnp.sum((a0 + a1) + (a2 + a3), axis=0, keepdims=True)

        def count_ge(t):
            return count(lambda blk, r0: blk >= t)

        kf = float(n_sel)
        no_cut = jnp.full((1, tq), 2 ** 30, I32)

        gm = gm_ref[...]
        few = vis < n_sel
        lo0 = _f32_to_ordered(jnp.min(gm, axis=0, keepdims=True))
        hi0 = _f32_to_ordered(jnp.max(gm, axis=0, keepdims=True)) + 1

        def finished(lo, hi, c_lo):
            return jnp.logical_or(few, jnp.logical_or(c_lo == kf, hi <= lo + 1))

        def more(state):
            _, lo, hi, c_lo = state
            return jnp.max(jnp.where(finished(lo, hi, c_lo), 0.0, 1.0)) > 0.0

        def step(state):
            k, lo, hi, c_lo = state
            mid_f = _f32_to_ordered(0.5 * _ordered_to_f32(lo) + 0.5 * _ordered_to_f32(hi - 1))
            mid_i = (lo >> 1) + (hi >> 1) + (lo & hi & 1)
            wide = ((lo ^ (hi - 1)) >> 23) != 0
            piv = jnp.where(jnp.logical_and(wide, k < FLOAT_STEPS), mid_f, mid_i)
            piv = jnp.where(jnp.logical_and(lo == 0, hi > 1), 1, piv)
            piv = jnp.where(jnp.logical_and(lo < 0, hi > 0), 0, piv)
            piv = jnp.clip(piv, lo + 1, hi - 1)
            c = count_ge(_ordered_to_f32(piv))
            live = jnp.logical_not(finished(lo, hi, c_lo))
            up = jnp.logical_and(live, c >= kf)
            down = jnp.logical_and(live, c < kf)
            return (k + 1, jnp.where(up, piv, lo), jnp.where(down, piv, hi),
                    jnp.where(up, c, c_lo))

        state = (jnp.int32(0), lo0, hi0, jnp.full((1, tq), -1.0, F32))
        _, cur, _, c_cur = lax.while_loop(more, step, state)
        thr = jnp.where(few, -F32_MAX, _ordered_to_f32(cur))

        def with_ties():
            n_gt = count(lambda blk, r0: blk > thr)
            n_ge = count_ge(thr)
            need = kf - n_gt
            tied = (n_ge - n_gt) > need

            def tie_cut():
                rr = lax.broadcasted_iota(I32, (KEY_CHUNK, KEY_CHUNK), 0)
                cc = lax.broadcasted_iota(I32, (KEY_CHUNK, KEY_CHUNK), 1)
                lower = jnp.where(cc <= rr, 1.0, 0.0).astype(BF16)

                def body(c, carry):
                    run, last = carry
                    for u in range(KEY_QUAD // KEY_CHUNK):
                        r0 = pl.multiple_of(c * KEY_QUAD + u * KEY_CHUNK, KEY_CHUNK)
                        eq = s_ref[pl.ds(r0, KEY_CHUNK), :] == thr
                        hit = jnp.where(eq, 1.0, 0.0)
                        ordinal = run + jnp.dot(lower, hit.astype(BF16),
                                                preferred_element_type=F32)
                        kpos = (r0 + lax.broadcasted_iota(I32, (KEY_CHUNK, 1), 0)).astype(F32)
                        take = jnp.logical_and(eq, ordinal <= need)
                        last = jnp.maximum(last, jnp.max(jnp.where(take, kpos, -1.0), axis=0,
                                                         keepdims=True))
                        run = run + jnp.sum(hit, axis=0, keepdims=True)
                    return run, last

                init = (jnp.zeros((1, tq), F32), jnp.full((1, tq), -1.0, F32))
                _, last = lax.fori_loop(0, vcap // KEY_QUAD, body, init)
                return jnp.where(tied, last.astype(I32), no_cut)

            return lax.cond(jnp.max(jnp.where(tied, 1.0, 0.0)) > 0.0, tie_cut, lambda: no_cut)

        unsure = jnp.logical_or(c_cur != kf, few)
        cut = lax.cond(jnp.max(jnp.where(unsure, 1.0, 0.0)) > 0.0, with_ties, lambda: no_cut)
        thr_ref[0:1, :] = thr
        cut_ref[0:1, :] = cut
        for h in range(N_HEADS):
            qp_ref[h] = _pair_padded(qaT_ref, h)
        _softmax_init(m_ref, l_ref, acc_ref)

    jj = j - nk

    @pl.when(jnp.logical_and(j >= nk, jj < nvis))
    def _():
        thr = thr_ref[0:1, :]
        cut = cut_ref[0:1, :]

        def selected(p):
            g0 = pl.multiple_of(jj * tk + p * KEY_PAIR, KEY_PAIR)
            sblk = s_ref[pl.ds(g0, KEY_PAIR), :]
            kpos = g0 + lax.broadcasted_iota(I32, (KEY_PAIR, 1), 0)
            return jnp.logical_or(sblk > thr, jnp.logical_and(sblk == thr, kpos <= cut))

        def logits(h, p):
            r0 = pl.multiple_of(p * KEY_PAIR, KEY_PAIR)
            kblk = ka_ref[0, pl.ds(r0, KEY_PAIR), (h // 2) * LANES:(h // 2 + 1) * LANES]
            return jnp.dot(kblk, qp_ref[h], preferred_element_type=F32)

        def v_t(h, p):
            rows = slice(h * HEAD_DIM, (h + 1) * HEAD_DIM)
            return jnp.concatenate([vaT_ref[0, 2 * p, rows, :], vaT_ref[0, 2 * p + 1, rows, :]],
                                   axis=1)

        _attend_block(0, rows_here(jj, KEY_QUAD), logits, selected, v_t, m_ref, l_ref, acc_ref,
                      lg0_ref, lg1_ref)

    @pl.when(j == 2 * nk - 1)
    def _():
        _softmax_finish(o_ref, l_ref, acc_ref)


def _dsa_attention(qiT, wiT, qaT, ki, ka, vaTc, *, past, length, tq, tk):
    b, _, tqp = qiT.shape
    lp = ki.shape[1]
    nq, nk = tqp // tq, lp // tk
    n_sel = min(TOPK_MAX, length // 4)

    def nvis(i):
        vmax = jnp.minimum(((past + (i + 1) * tq - 1) // CHUNK + 1) * CHUNK, length)
        return (vmax + tk - 1) // tk

    k1 = lambda bb, i, j: (bb, jnp.minimum(j, nvis(i) - 1), 0)
    k3 = lambda bb, i, j: (bb, jnp.clip(j - nk, 0, nvis(i) - 1), 0)
    k3c = lambda bb, i, j: (bb, jnp.clip(j - nk, 0, nvis(i) - 1), 0, 0)
    qmap = lambda bb, i, j: (bb, 0, i)
    kern = functools.partial(_dsa_kernel, past=past, length=length, n_sel=n_sel, tq=tq, tk=tk,
                             nk=nk)
    return pl.pallas_call(
        kern,
        out_shape=jax.ShapeDtypeStruct((b, tqp, D_BRANCH), BF16),
        grid=(b, nq, 2 * nk),
        in_specs=[
            pl.BlockSpec((1, D_BRANCH, tq), qmap),
            pl.BlockSpec((1, N_HEADS, tq), qmap),
            pl.BlockSpec((1, D_BRANCH, tq), qmap),
            pl.BlockSpec((1, tk, IDX_DIM), k1),
            pl.BlockSpec((1, tk, D_BRANCH), k3),
            pl.BlockSpec((1, tk // KEY_CHUNK, D_BRANCH, KEY_CHUNK), k3c),
        ],
        out_specs=pl.BlockSpec((1, tq, D_BRANCH), lambda bb, i, j: (bb, i, 0)),
        scratch_shapes=[
            pltpu.VMEM((lp, tq), F32),
            pltpu.VMEM((GROUPS, tq), F32),
            pltpu.VMEM((8, tq), F32),
            pltpu.VMEM((8, tq), I32),
            pltpu.VMEM((N_HEADS, LANES, tq), BF16),
            pltpu.VMEM((N_HEADS, tq), F32),
            pltpu.VMEM((N_HEADS, tq), F32),
            pltpu.VMEM((D_BRANCH, tq), F32),
            pltpu.VMEM((N_HEADS, KEY_PAIR, tq), F32),
            pltpu.VMEM((N_HEADS, KEY_PAIR, tq), F32),
        ],
        compiler_params=_cparams(("arbitrary", "arbitrary", "arbitrary")),
        name="dsa_attention",
    )(qiT, wiT, qaT, ki, ka, vaTc)


_AUG = 6


def _fox_kernel(qbT_ref, qaug_ref, kb_ref, kaug_ref, vbT_ref, o_ref,
                qf_ref, m_ref, l_ref, acc_ref, lg0_ref, lg1_ref, *, past, tq, tk, nk):
    i = pl.program_id(1)
    j = pl.program_id(2)
    q0 = past + i * tq
    qpos = q0 + lax.broadcasted_iota(I32, (1, tq), 1)
    nvis = (q0 + tq + tk - 1) // tk
    nquad = tk // KEY_QUAD
    n_free = jnp.clip((q0 + 1 - j * tk) // KEY_QUAD, 0, nquad)
    n_used = jnp.clip((q0 + tq - j * tk + KEY_QUAD - 1) // KEY_QUAD, 0, nquad)

    @pl.when(j == 0)
    def _():
        arow = lax.broadcasted_iota(I32, (LANES, 1), 0)
        qa = qaug_ref[0]
        for h in range(N_HEADS):
            mine = jnp.logical_and(arow >= _AUG * h, arow < _AUG * (h + 1))
            qf_ref[h, 0:LANES, :] = _pair_padded(qbT_ref, h)
            qf_ref[h, LANES:2 * LANES, :] = jnp.where(mine, qa, 0.0).astype(BF16)
        _softmax_init(m_ref, l_ref, acc_ref)

    @pl.when(j < nvis)
    def _():
        def causal(p):
            kpos = j * tk + p * KEY_PAIR + lax.broadcasted_iota(I32, (KEY_PAIR, 1), 0)
            return kpos <= qpos

        def logits(h, p):
            r0 = pl.multiple_of(p * KEY_PAIR, KEY_PAIR)
            kblk = kb_ref[0, pl.ds(r0, KEY_PAIR), (h // 2) * LANES:(h // 2 + 1) * LANES]
            lhs = jnp.concatenate([kblk, kaug_ref[0, pl.ds(r0, KEY_PAIR), :]], axis=1)
            return jnp.dot(lhs, qf_ref[h], preferred_element_type=F32)

        def v_t(h, p):
            rows = slice(h * HEAD_DIM, (h + 1) * HEAD_DIM)
            return jnp.concatenate([vbT_ref[0, 2 * p, rows, :], vbT_ref[0, 2 * p + 1, rows, :]],
                                   axis=1)

        _attend_block(n_free, n_used, logits, causal, v_t, m_ref, l_ref, acc_ref,
                      lg0_ref, lg1_ref)

    @pl.when(j == nk - 1)
    def _():
        _softmax_finish(o_ref, l_ref, acc_ref)


def _fox_attention(qbT, qaug, kb, kaug, vbTc, *, past, tq, tk):
    b, _, tqp = qbT.shape
    lp = kb.shape[1]
    nq, nk = tqp // tq, lp // tk

    def last(i):
        return jnp.minimum((past + (i + 1) * tq + tk - 1) // tk, nk) - 1

    kmap = lambda bb, i, j: (bb, jnp.minimum(j, last(i)), 0)
    kmapc = lambda bb, i, j: (bb, jnp.minimum(j, last(i)), 0, 0)
    qmap = lambda bb, i, j: (bb, 0, i)
    return pl.pallas_call(
        functools.partial(_fox_kernel, past=past, tq=tq, tk=tk, nk=nk),
        out_shape=jax.ShapeDtypeStruct((b, tqp, D_BRANCH), BF16),
        grid=(b, nq, nk),
        in_specs=[
            pl.BlockSpec((1, D_BRANCH, tq), qmap),
            pl.BlockSpec((1, LANES, tq), qmap),
            pl.BlockSpec((1, tk, D_BRANCH), kmap),
            pl.BlockSpec((1, tk, LANES), kmap),
            pl.BlockSpec((1, tk // KEY_CHUNK, D_BRANCH, KEY_CHUNK), kmapc),
        ],
        out_specs=pl.BlockSpec((1, tq, D_BRANCH), lambda bb, i, j: (bb, i, 0)),
        scratch_shapes=[
            pltpu.VMEM((N_HEADS, 2 * LANES, tq), BF16),
            pltpu.VMEM((N_HEADS, tq), F32),
            pltpu.VMEM((N_HEADS, tq), F32),
            pltpu.VMEM((D_BRANCH, tq), F32),
            pltpu.VMEM((N_HEADS, KEY_PAIR, tq), F32),
            pltpu.VMEM((N_HEADS, KEY_PAIR, tq), F32),
        ],
        compiler_params=_cparams(("arbitrary", "arbitrary", "arbitrary")),
        name="fox_attention",
    )(qbT, qaug, kb, kaug, vbTc)


AUG_ROWS = 1024


def _aug_kernel(c_ref, kaug_ref, qaugT_ref):
    x = c_ref[0] * LOG2E
    a = x.astype(BF16).astype(F32)
    r = x - a
    b = r.astype(BF16).astype(F32)
    c = (r - b).astype(BF16).astype(F32)
    lane = lax.broadcasted_iota(I32, (N_HEADS, LANES), 1)
    head = lax.broadcasted_iota(I32, (N_HEADS, LANES), 0)

    def place(v, slot):
        sel = jnp.where(lane == _AUG * head + slot, 1.0, 0.0)
        return jnp.dot(v, sel, preferred_element_type=F32)

    slot = lane - _AUG * head
    ones_k = jnp.sum(jnp.where(jnp.logical_and(slot >= 0, slot < 3), 1.0, 0.0), axis=0, keepdims=True)
    ones_q = jnp.sum(jnp.where(jnp.logical_and(slot >= 3, slot < 6), 1.0, 0.0), axis=0, keepdims=True)
    kaug_ref[0] = (ones_k - (place(a, 3) + place(b, 4) + place(c, 5))).astype(BF16)
    qaugT_ref[0] = (ones_q + (place(a, 0) + place(b, 1) + place(c, 2))).T


def _fox_augment(cum):
    b, lp, h = cum.shape
    return pl.pallas_call(
        _aug_kernel,
        out_shape=(jax.ShapeDtypeStruct((b, lp, LANES), BF16),
                   jax.ShapeDtypeStruct((b, LANES, lp), F32)),
        grid=(b, lp // AUG_ROWS),
        in_specs=[pl.BlockSpec((1, AUG_ROWS, h), lambda i, j: (i, j, 0))],
        out_specs=(pl.BlockSpec((1, AUG_ROWS, LANES), lambda i, j: (i, j, 0)),
                   pl.BlockSpec((1, LANES, AUG_ROWS), lambda i, j: (i, 0, j))),
        compiler_params=_cparams(("arbitrary", "arbitrary")),
        name="fox_augment",
    )(cum)


def _shifted_rows(buf, load, store, u, past_ref, starts_batch, nb):
    tm, c = u.shape
    store(slice(8, 8 + tm), u)
    if nb == 1:
        @pl.when(starts_batch)
        def _():
            store(slice(6, 8), past_ref[0])
    s1 = load(slice(7, 7 + tm))
    s2 = load(slice(6, 6 + tm))
    if nb > 1:
        r = lax.broadcasted_iota(I32, (nb, tm // nb, 1), 1)
        p0 = past_ref[:, 0:1, :]
        p1 = past_ref[:, 1:2, :]
        s1 = jnp.where(r == 0, p1, _group(s1, nb)).reshape(tm, c)
        s2 = jnp.where(r == 0, p0, jnp.where(r == 1, p1, _group(s2, nb))).reshape(tm, c)
    store(slice(0, 8), load(slice(tm, tm + 8)))
    return s1, s2


def _merge_kernel(x_ref, ya_ref, yb_ref, cb_ref, u_ref, gl_ref, past_ref, cw_ref, wbr_ref,
                  wo_ref, g_ref, mod_ref, o_ref, ub_ref, *, nb, rows_per_batch):
    i = pl.program_id(0)
    tm = x_ref.shape[0]
    u = u_ref[...]

    def load(rows):
        return ub_ref[rows, :]

    def store(rows, v):
        ub_ref[rows, :] = v

    s1, s2 = _shifted_rows(ub_ref, load, store, u, past_ref, (i * tm) % rows_per_batch == 0, nb)
    conv = cw_ref[0:1, :] * s2 + cw_ref[1:2, :] * s1 + cw_ref[2:3, :] * u
    yc = (cb_ref[...] * conv).astype(BF16)
    mix = jnp.zeros((tm, D_MODEL), F32)
    for n, y in enumerate((ya_ref[...], yb_ref[...], yc)):
        br = jnp.dot(y, wbr_ref[n], preferred_element_type=F32)
        mix = mix + jax.nn.sigmoid(gl_ref[:, n * D_MODEL:(n + 1) * D_MODEL]) * br
    mo = jnp.dot(mix.astype(BF16), wo_ref[...], preferred_element_type=F32)
    nm = _group(_rms(mo, g_ref[1:2, :]), nb)
    o_ref[...] = x_ref[...] + (mod_ref[:, 2:3, :] * nm).reshape(tm, D_MODEL)


def _mod_spec(tm, nb, rows_per_batch, ngrid):
    if nb == 1:
        f = lambda i, *_: ((i * tm) // rows_per_batch, 0, 0)
    else:
        f = lambda i, *_: (i, 0, 0)
    return pl.BlockSpec((nb, 6, D_MODEL), f)


def _merge(x, ya, yb, cb, u, gl, past, cw, wbr, wo, g, mod, *, tm, rows_per_batch):
    rows = x.shape[0]
    nb = max(1, tm // rows_per_batch)
    row = lambda w: pl.BlockSpec((tm, w), lambda i: (i, 0))
    full = lambda shape: pl.BlockSpec(shape, lambda i: (0,) * len(shape))
    if nb == 1:
        past_spec = pl.BlockSpec((1, 2, D_BRANCH), lambda i: ((i * tm) // rows_per_batch, 0, 0))
    else:
        past_spec = pl.BlockSpec((nb, 2, D_BRANCH), lambda i: (i, 0, 0))
    return pl.pallas_call(
        functools.partial(_merge_kernel, nb=nb, rows_per_batch=rows_per_batch),
        out_shape=jax.ShapeDtypeStruct((rows, D_MODEL), F32),
        grid=(rows // tm,),
        in_specs=[
            row(D_MODEL), row(D_BRANCH), row(D_BRANCH), row(D_BRANCH), row(D_BRANCH),
            row(3 * D_MODEL), past_spec, full((CONV_W, D_BRANCH)),
            full((3, D_BRANCH, D_MODEL)), full((D_MODEL, D_MODEL)), full((4, D_MODEL)),
            _mod_spec(tm, nb, rows_per_batch, 1),
        ],
        out_specs=row(D_MODEL),
        scratch_shapes=[pltpu.VMEM((tm + 8, D_BRANCH), F32)],
        compiler_params=_cparams(("arbitrary",)),
        name="branch_merge",
    )(x, ya, yb, cb, u, gl, past, cw, wbr, wo, g, mod)


FF_CHUNK = 1408


def _ffn_kernel(x_ref, g_ref, mod_ref, wg_ref, wv_ref, wd_ref, cw_ref, past_ref,
                o_ref, tail_ref, hs_ref, acc_ref, ub_ref, *, nb, rows_per_batch, nj):
    i = pl.program_id(0)
    j = pl.program_id(1)
    tm = x_ref.shape[0]

    @pl.when(j == 0)
    def _():
        y = _group(_rms(x_ref[...], g_ref[2:3, :]), nb)
        h = y * (1.0 + mod_ref[:, 4:5, :]) + mod_ref[:, 3:4, :]
        hs_ref[...] = h.reshape(tm, D_MODEL).astype(BF16)
        acc_ref[...] = jnp.zeros(acc_ref.shape, F32)

    hs = hs_ref[...]
    ug = jnp.dot(hs, wg_ref[...], preferred_element_type=F32)
    uv = jnp.dot(hs, wv_ref[...], preferred_element_type=F32)

    def load(rows):
        return ub_ref[j, rows, :]

    def store(rows, v):
        ub_ref[j, rows, :] = v

    s1, s2 = _shifted_rows(ub_ref, load, store, ug, past_ref, (i * tm) % rows_per_batch == 0, nb)
    conv = cw_ref[0:1, :] * s2 + cw_ref[1:2, :] * s1 + cw_ref[2:3, :] * ug
    f = conv * jax.nn.sigmoid(conv) * uv
    acc_ref[...] += jnp.dot(f.astype(BF16), wd_ref[...], preferred_element_type=F32)
    grp = tm // nb
    tail_ref[0] = _group(ug, nb)[:, grp - 8:, :]

    @pl.when(j == nj - 1)
    def _():
        nm = _group(_rms(acc_ref[...], g_ref[3:4, :]), nb)
        o_ref[...] = x_ref[...] + (mod_ref[:, 5:6, :] * nm).reshape(tm, D_MODEL)


def _conv_ffn(x, g, mod, w_up, w_down, cw, past, *, tm, rows_per_batch):
    rows = x.shape[0]
    dff = w_down.shape[0]
    nj = dff // FF_CHUNK
    nb = max(1, tm // rows_per_batch)
    nt = rows // tm
    if nb == 1:
        past_spec = pl.BlockSpec((1, 2, FF_CHUNK), lambda i, j: ((i * tm) // rows_per_batch, 0, j))
        mod_map = lambda i, j: ((i * tm) // rows_per_batch, 0, 0)
    else:
        past_spec = pl.BlockSpec((nb, 2, FF_CHUNK), lambda i, j: (i, 0, j))
        mod_map = lambda i, j: (i, 0, 0)
    out, tails = pl.pallas_call(
        functools.partial(_ffn_kernel, nb=nb, rows_per_batch=rows_per_batch, nj=nj),
        out_shape=(jax.ShapeDtypeStruct((rows, D_MODEL), F32),
                   jax.ShapeDtypeStruct((nt, nb, 8, dff), F32)),
        grid=(nt, nj),
        in_specs=[
            pl.BlockSpec((tm, D_MODEL), lambda i, j: (i, 0)),
            pl.BlockSpec((4, D_MODEL), lambda i, j: (0, 0)),
            pl.BlockSpec((nb, 6, D_MODEL), mod_map),
            pl.BlockSpec((D_MODEL, FF_CHUNK), lambda i, j: (0, j)),
            pl.BlockSpec((D_MODEL, FF_CHUNK), lambda i, j: (0, j + nj)),
            pl.BlockSpec((FF_CHUNK, D_MODEL), lambda i, j: (j, 0)),
            pl.BlockSpec((CONV_W, FF_CHUNK), lambda i, j: (0, j)),
            past_spec,
        ],
        out_specs=(pl.BlockSpec((tm, D_MODEL), lambda i, j: (i, 0)),
                   pl.BlockSpec((1, nb, 8, FF_CHUNK), lambda i, j: (i, 0, 0, j))),
        scratch_shapes=[
            pltpu.VMEM((tm, D_MODEL), BF16),
            pltpu.VMEM((tm, D_MODEL), F32),
            pltpu.VMEM((nj, tm + 8, FF_CHUNK), F32),
        ],
        compiler_params=_cparams(("arbitrary", "arbitrary")),
        name="conv_ffn",
    )(x, g, mod, w_up, w_up, w_down, cw, past)
    return out, tails


def _relayout_w_in(w_in, b_forget):
    db = D_BRANCH
    o_qi = 3 * db
    o_ki = o_qi + N_HEADS * IDX_DIM
    o_wi = o_ki + IDX_DIM
    o_qb = o_wi + N_HEADS
    o_fl = o_qb + 3 * db
    o_cb = o_fl + N_HEADS
    o_gl = o_cb + 3 * db
    main = jnp.concatenate([w_in[:, 0:o_ki], w_in[:, o_qb:o_fl], w_in[:, o_cb:]], axis=1)
    small = jnp.concatenate([w_in[:, o_ki:o_wi], w_in[:, o_wi:o_qb], w_in[:, o_fl:o_cb]], axis=1)
    small = jnp.pad(small, ((0, 0), (0, LANES - small.shape[1])))
    bf = jnp.zeros((1, LANES), F32).at[0, _FL_OFF:_FL_OFF + N_HEADS].set(b_forget)
    del o_gl
    return main.astype(BF16), small.astype(BF16), bf


def _round_up(x, m):
    return (x + m - 1) // m * m


def _layer(x, mod, caches, weights, *, batch, t, past, cfg):
    (g, w_main, w_small, bf_pad, cw_mix, wbr, wo, w_up, cw_ffn, w_down) = weights
    rows = batch * t
    length = past + t
    (qaT, ka32, ka16, va32, vaTc, qiT, qbT, kb32, kb16, vb32, vbTc, cb, u, gl,
     ki32, ki16, wiT, logf) = _in_projection(
        x, mod, g, w_main, w_small, bf_pad, tm=cfg["tm_in"], rows_per_batch=t)

    tq, tk = cfg["tq"], cfg["tk"]
    tqp = _round_up(t, tq)
    lp = _round_up(length, tk)

    def per_batch_T(aT):
        c = aT.shape[0]
        a = jnp.swapaxes(aT.reshape(c, batch, t), 0, 1)
        return jnp.pad(a, ((0, 0), (0, 0), (0, tqp - t)))

    def chunked_T(a):
        return jnp.swapaxes(a.reshape(batch, lp // KEY_CHUNK, KEY_CHUNK, a.shape[-1]), 2, 3)

    def with_cache(cache, new, dtype):
        new = new.reshape(batch, t, -1)
        if cache is not None:
            new = jnp.concatenate([cache.reshape(batch, past, -1).astype(dtype), new.astype(dtype)],
                                  axis=1)
        return jnp.pad(new.astype(dtype), ((0, 0), (0, lp - length), (0, 0)))

    if caches is None:
        c_idx = c_dk = c_dv = c_fk = c_fv = c_lf = None
        past_mix = jnp.zeros((batch, CONV_W - 1, D_BRANCH), F32)
        past_ffn = jnp.zeros((batch, CONV_W - 1, w_down.shape[0]), F32)
    else:
        c_idx, c_dk, c_dv, c_fk, c_fv, c_lf, past_mix, past_ffn = caches

    ki_all = with_cache(c_idx, ki16, BF16)
    ka_all = with_cache(c_dk, ka16, BF16)
    kb_all = with_cache(c_fk, kb16, BF16)
    if caches is None and lp == rows:
        vaT_all = vaTc.reshape(batch, lp // KEY_CHUNK, D_BRANCH, KEY_CHUNK)
        vbT_all = vbTc.reshape(batch, lp // KEY_CHUNK, D_BRANCH, KEY_CHUNK)
    else:
        vaT_all = chunked_T(with_cache(c_dv, va32, BF16))
        vbT_all = chunked_T(with_cache(c_fv, vb32, BF16))
    logf_all = with_cache(c_lf, logf, F32)

    ya = _dsa_attention(per_batch_T(qiT), per_batch_T(wiT), per_batch_T(qaT), ki_all, ka_all,
                        vaT_all, past=past, length=length, tq=tq, tk=tk)
    cum = _forget_cumsum(logf_all)
    kaug, qaug = _fox_augment(cum)
    qaug = jnp.pad(qaug[:, :, past:length], ((0, 0), (0, 0), (0, tqp - t)))
    yb = _fox_attention(per_batch_T(qbT), qaug, kb_all, kaug, vbT_all, past=past, tq=tq, tk=tk)
    ya = ya[:, :t].reshape(rows, D_BRANCH)
    yb = yb[:, :t].reshape(rows, D_BRANCH)

    x1 = _merge(x, ya, yb, cb, u, gl, past_mix, cw_mix, wbr, wo, g, mod,
                tm=cfg["tm_merge"], rows_per_batch=t)
    x2, tails = _conv_ffn(x1, g, mod, w_up, w_down, cw_ffn, past_ffn,
                          tm=cfg["tm_ffn"], rows_per_batch=t)

    new_mix = u.reshape(batch, t, D_BRANCH)[:, t - (CONV_W - 1):]
    nb = max(1, cfg["tm_ffn"] // t)
    if nb == 1:
        tiles_per_batch = t // cfg["tm_ffn"]
        last = tails.reshape(batch, tiles_per_batch, 8, -1)[:, -1]
    else:
        last = tails.reshape(batch, 8, -1)
    new_ffn = last[:, 8 - (CONV_W - 1):]
    hd = (batch, t, N_HEADS, HEAD_DIM)
    state = (ki32.reshape(batch, t, IDX_DIM), ka32.reshape(hd), va32.reshape(hd),
             kb32.reshape(hd), vb32.reshape(hd), logf.reshape(batch, t, N_HEADS), new_mix, new_ffn)
    return x2, state


_PROMPT_CFG = dict(tm_in=512, tm_merge=256, tm_ffn=256, tq=256, tk=4096)
_SAMPLE_CFG = dict(tm_in=256, tm_merge=256, tm_ffn=256, tq=128, tk=3072)


def kernel(x_prompt, x_sample, c_prompt, c_sample, cache_idx_k, cache_dsa_k, cache_dsa_v,
           cache_fox_k, cache_fox_v, cache_fox_logf, state_conv_mix, state_conv_ffn,
           w_ada, b_ada, norm_g, w_in, b_forget, conv_mix_w, w_branch, w_out, w_up,
           conv_ffn_w, w_down):
    bp, tp, _ = x_prompt.shape
    bs, ts, _ = x_sample.shape
    past = cache_dsa_k.shape[2]
    depth = w_ada.shape[0]

    c_all = jnp.concatenate([c_prompt, c_sample], axis=0)
    pad_rows = _round_up(c_all.shape[0], 8) - c_all.shape[0]
    c_all = jnp.pad(c_all, ((0, pad_rows), (0, 0)))
    mod_all = _modulation(c_all, w_ada, b_ada).reshape(depth, -1, 6, D_MODEL)

    yp = x_prompt.reshape(bp * tp, D_MODEL)
    ys = x_sample.reshape(bs * ts, D_MODEL)
    p_states, s_states = [], []
    for l in range(depth):
        w_main, w_small, bf_pad = _relayout_w_in(w_in[l], b_forget[l])
        weights = (norm_g[l], w_main, w_small, bf_pad, conv_mix_w[l], w_branch[l].astype(BF16),
                   w_out[l].astype(BF16), w_up[l].astype(BF16), conv_ffn_w[l],
                   w_down[l].astype(BF16))
        yp, st_p = _layer(yp, mod_all[l, :bp], None, weights, batch=bp, t=tp, past=0,
                          cfg=_PROMPT_CFG)
        caches = (cache_idx_k[l], cache_dsa_k[l], cache_dsa_v[l], cache_fox_k[l], cache_fox_v[l],
                  cache_fox_logf[l], state_conv_mix[l], state_conv_ffn[l])
        ys, st_s = _layer(ys, mod_all[l, bp:bp + bs], caches, weights, batch=bs, t=ts, past=past,
                          cfg=_SAMPLE_CFG)
        p_states.append(st_p)
        s_states.append(st_s)

    stack = lambda states: [jnp.stack([st[k] for st in states], axis=0) for k in range(8)]
    return (yp.reshape(bp, tp, D_MODEL), ys.reshape(bs, ts, D_MODEL), *stack(p_states),
            *stack(s_states))
```

```python
import functools

import jax
import jax.numpy as jnp
from jax import lax
from jax.experimental import pallas as pl
from jax.experimental.pallas import tpu as pltpu

F32 = jnp.float32
BF16 = jnp.bfloat16
I32 = jnp.int32

D_MODEL = 1024
HEAD_DIM = 64
D_BRANCH = 512
N_HEADS = 8
IDX_DIM = 64
CHUNK = 64
CHUNK_SHIFT = 6
TOPK_MAX = 256
CONV_W = 3
EPS = 1e-6
NEG_INF = -1e30
LOG2E = 1.4426950408889634
F32_MAX = 3.4028234663852886e38

LANES = 128
N_MAIN = 16 * D_BRANCH
KEY_CHUNK = 256
KEY_PAIR = 2 * KEY_CHUNK
KEY_QUAD = 2 * KEY_PAIR
SCORE_ROWS = 128
COUNT_ROWS = KEY_QUAD
GROUPS = TOPK_MAX
FLOAT_STEPS = 12
VMEM_LIMIT = 56 * 1024 * 1024


def _cparams(sem):
    return pltpu.CompilerParams(dimension_semantics=sem, vmem_limit_bytes=VMEM_LIMIT)


def _rms(x, g_row):
    return x * lax.rsqrt(jnp.mean(x * x, axis=-1, keepdims=True) + EPS) * g_row


def _group(x, nb):
    return x.reshape(nb, x.shape[0] // nb, x.shape[1])


def _mod_kernel(c_ref, w_ref, b_ref, o_ref):
    c = c_ref[...]
    s = (c * jax.nn.sigmoid(c)).astype(BF16)
    o_ref[0] = jnp.dot(s, w_ref[0].astype(BF16), preferred_element_type=F32) + b_ref[0]


def _modulation(c_all, w_ada, b_ada):
    depth = w_ada.shape[0]
    rows = c_all.shape[0]
    n = w_ada.shape[2]
    tn = D_MODEL
    return pl.pallas_call(
        _mod_kernel,
        out_shape=jax.ShapeDtypeStruct((depth, rows, n), F32),
        grid=(depth, n // tn),
        in_specs=[
            pl.BlockSpec((rows, D_MODEL), lambda l, j: (0, 0)),
            pl.BlockSpec((1, D_MODEL, tn), lambda l, j: (l, 0, j)),
            pl.BlockSpec((1, 1, tn), lambda l, j: (l, 0, j)),
        ],
        out_specs=pl.BlockSpec((1, rows, tn), lambda l, j: (l, 0, j)),
        compiler_params=_cparams(("arbitrary", "arbitrary")),
        name="adaln_mod",
    )(c_all, w_ada, b_ada.reshape(depth, 1, n))


IN_STEP = 2 * D_BRANCH
_GL_STEP = 5
_WI_OFF = IDX_DIM
_FL_OFF = IDX_DIM + N_HEADS


def _inproj_kernel(x_ref, mod_ref, g_ref, wm_ref, ws_ref, bf_ref,
                   qaT_ref, ka32_ref, ka16_ref, va32_ref, vaT_ref, qiT_ref, qbT_ref,
                   kb32_ref, kb16_ref, vb32_ref, vbT_ref, cb_ref, u_ref, gl_ref,
                   ki32_ref, ki16_ref, wiT_ref, logf_ref,
                   hs_ref, *, nb):
    j = pl.program_id(1)
    tm = x_ref.shape[0]

    @pl.when(j == 0)
    def _():
        y = _rms(x_ref[...], g_ref[0:1, :])
        h = _group(y, nb) * (1.0 + mod_ref[:, 1:2, :]) + mod_ref[:, 0:1, :]
        hb = h.reshape(tm, D_MODEL).astype(BF16)
        hs_ref[...] = hb
        sm = jnp.dot(hb, ws_ref[...], preferred_element_type=F32)
        ki32_ref[...] = sm[:, :IDX_DIM]
        ki16_ref[...] = sm[:, :IDX_DIM].astype(BF16)
        wiT_ref[...] = sm.T[_WI_OFF:_WI_OFF + N_HEADS, :]
        z = sm + bf_ref[...]
        lf = -(jnp.maximum(-z, 0.0) + jnp.log1p(jnp.exp(-jnp.abs(z))))
        logf_ref[...] = lf[:, _FL_OFF:_FL_OFF + N_HEADS]

    acc = jnp.dot(hs_ref[...], wm_ref[...], preferred_element_type=F32)
    lo = acc[:, :D_BRANCH]
    hi = acc[:, D_BRANCH:]
    scale = LOG2E * HEAD_DIM ** -0.5

    def store_chunked_T(ref, a):
        aT = a.T.astype(BF16)
        for c in range(tm // KEY_CHUNK):
            ref[c] = aT[:, c * KEY_CHUNK:(c + 1) * KEY_CHUNK]

    @pl.when(j == 0)
    def _():
        qaT_ref[...] = (lo * scale).T.astype(BF16)
        ka32_ref[...] = hi
        ka16_ref[...] = hi.astype(BF16)

    @pl.when(j == 1)
    def _():
        va32_ref[...] = lo
        store_chunked_T(vaT_ref, lo)
        qiT_ref[...] = hi.T.astype(BF16)

    @pl.when(j == 2)
    def _():
        qbT_ref[...] = (lo * scale).T.astype(BF16)
        kb32_ref[...] = hi
        kb16_ref[...] = hi.astype(BF16)

    @pl.when(j == 3)
    def _():
        vb32_ref[...] = lo
        store_chunked_T(vbT_ref, lo)
        cb_ref[...] = hi

    @pl.when(j == 4)
    def _():
        u_ref[...] = lo * hi

    @pl.when(j >= _GL_STEP)
    def _():
        gl_ref[...] = acc


def _in_projection(x, mod, g, w_main, w_small, bf_pad, *, tm, rows_per_batch):
    rows = x.shape[0]
    nt = rows // tm
    nb = max(1, tm // rows_per_batch)
    nstep = N_MAIN // IN_STEP
    if nb == 1:
        mod_map = lambda i, j: ((i * tm) // rows_per_batch, 0, 0)
    else:
        mod_map = lambda i, j: (i, 0, 0)
    row_blk = lambda w: pl.BlockSpec((tm, w), lambda i, j: (i, 0))
    col_blk = lambda h: pl.BlockSpec((h, tm), lambda i, j: (0, i))
    chunkT = pl.BlockSpec((tm // KEY_CHUNK, D_BRANCH, KEY_CHUNK), lambda i, j: (i, 0, 0))
    f32o = lambda w: jax.ShapeDtypeStruct((rows, w), F32)
    b16o = lambda w: jax.ShapeDtypeStruct((rows, w), BF16)
    b16T = jax.ShapeDtypeStruct((D_BRANCH, rows), BF16)
    b16c = jax.ShapeDtypeStruct((rows // KEY_CHUNK, D_BRANCH, KEY_CHUNK), BF16)
    out_shape = (
        b16T, f32o(D_BRANCH), b16o(D_BRANCH), f32o(D_BRANCH), b16c, b16T, b16T,
        f32o(D_BRANCH), b16o(D_BRANCH), f32o(D_BRANCH), b16c, f32o(D_BRANCH), f32o(D_BRANCH),
        f32o(6 * D_BRANCH),
        f32o(IDX_DIM), b16o(IDX_DIM), jax.ShapeDtypeStruct((N_HEADS, rows), F32), f32o(N_HEADS),
    )
    out_specs = (
        col_blk(D_BRANCH), row_blk(D_BRANCH), row_blk(D_BRANCH), row_blk(D_BRANCH), chunkT,
        col_blk(D_BRANCH), col_blk(D_BRANCH),
        row_blk(D_BRANCH), row_blk(D_BRANCH), row_blk(D_BRANCH), chunkT, row_blk(D_BRANCH),
        row_blk(D_BRANCH),
        pl.BlockSpec((tm, IN_STEP), lambda i, j: (i, jnp.clip(j - _GL_STEP, 0, 2))),
        row_blk(IDX_DIM), row_blk(IDX_DIM), col_blk(N_HEADS), row_blk(N_HEADS),
    )
    return pl.pallas_call(
        functools.partial(_inproj_kernel, nb=nb),
        out_shape=out_shape,
        grid=(nt, nstep),
        in_specs=[
            pl.BlockSpec((tm, D_MODEL), lambda i, j: (i, 0)),
            pl.BlockSpec((nb, 6, D_MODEL), mod_map),
            pl.BlockSpec((4, D_MODEL), lambda i, j: (0, 0)),
            pl.BlockSpec((D_MODEL, IN_STEP), lambda i, j: (0, j)),
            pl.BlockSpec((D_MODEL, LANES), lambda i, j: (0, 0)),
            pl.BlockSpec((1, LANES), lambda i, j: (0, 0)),
        ],
        out_specs=out_specs,
        scratch_shapes=[pltpu.VMEM((tm, D_MODEL), BF16)],
        compiler_params=_cparams(("arbitrary", "arbitrary")),
        name="in_projection",
    )(x, mod, g, w_main, w_small, bf_pad)


def _cumsum_kernel(x_ref, o_ref):
    x = x_ref[0]
    n = x.shape[0]
    lane = lax.broadcasted_iota(I32, x.shape, 1)
    row = lax.broadcasted_iota(I32, x.shape, 0)
    s = N_HEADS
    while s < LANES:
        x = x + jnp.where(lane >= s, pltpu.roll(x, s, axis=1), 0.0)
        s *= 2
    t = jnp.where(lane >= LANES - N_HEADS, x, 0.0)
    s = N_HEADS
    while s < LANES:
        t = t + pltpu.roll(t, s, axis=1)
        s *= 2
    t = jnp.where(row >= 1, pltpu.roll(t, 1, axis=0), 0.0)
    s = 1
    while s < n:
        if s < 8:
            sh = jnp.where(row >= s, pltpu.roll(t, s, axis=0), 0.0)
        else:
            sh = jnp.concatenate([jnp.zeros((s, LANES), F32), t[:n - s]], axis=0)
        t = t + sh
        s *= 2
    o_ref[0] = x + t


def _forget_cumsum(logf_all):
    b, lp, h = logf_all.shape
    n = lp * h // LANES
    out = pl.pallas_call(
        _cumsum_kernel,
        out_shape=jax.ShapeDtypeStruct((b, n, LANES), F32),
        grid=(b,),
        in_specs=[pl.BlockSpec((1, n, LANES), lambda i: (i, 0, 0))],
        out_specs=pl.BlockSpec((1, n, LANES), lambda i: (i, 0, 0)),
        compiler_params=_cparams(("arbitrary",)),
        name="forget_cumsum",
    )(logf_all.reshape(b, n, LANES))
    return out.reshape(b, lp, h)


def _pair_padded(qT_ref, h):
    pr, half = divmod(h, 2)
    blk = qT_ref[0, pr * LANES + half * HEAD_DIM:pr * LANES + (half + 1) * HEAD_DIM, :]
    z = jnp.zeros_like(blk)
    return jnp.concatenate([blk, z] if half == 0 else [z, blk], axis=0)


def _rows_to_heads(rows):
    tq = rows[0].shape[1]
    sub = lax.broadcasted_iota(I32, (N_HEADS, tq), 0)
    out = jnp.broadcast_to(rows[0], (N_HEADS, tq))
    for h in range(1, N_HEADS):
        out = jnp.where(sub == h, rows[h], out)
    return out


def _stage_and_consume(stage, consume, m_cur, pend, l, acc_ref):
    m_next, sums, alphas = [], [], []
    for h in range(N_HEADS):
        if stage is not None:
            logits, mask, buf = stage
            lg = logits(h)
            if mask is not None:
                lg = jnp.where(mask, lg, NEG_INF)
            buf[h] = lg
            m_next.append(jnp.maximum(m_cur[h:h + 1, :], jnp.max(lg, axis=0, keepdims=True)))
        if consume is not None:
            v_t, buf = consume
            before, after = pend
            alpha = jnp.exp2(before[h:h + 1, :] - after[h:h + 1, :])
            p = jnp.exp2(buf[h] - after[h:h + 1, :]).astype(BF16)
            ones = jnp.ones((16, p.shape[0]), BF16)
            pv = jnp.dot(jnp.concatenate([v_t(h), ones], axis=0), p, preferred_element_type=F32)
            rows = slice(h * HEAD_DIM, (h + 1) * HEAD_DIM)
            acc_ref[rows, :] = alpha * acc_ref[rows, :] + pv[:HEAD_DIM, :]
            sums.append(pv[HEAD_DIM:HEAD_DIM + 1, :])
            alphas.append(alpha)
    if consume is not None:
        l = _rows_to_heads(alphas) * l + _rows_to_heads(sums)
    return (_rows_to_heads(m_next) if stage is not None else m_cur), l


def _attend_block(n_free, n_used, logits, mask, v_t, m_ref, l_ref, acc_ref, buf0, buf1):
    def fused(p_new, masked, buf_new, buf_old, m_old, m_cur, l):
        return _stage_and_consume((lambda h: logits(h, p_new), mask(p_new) if masked else None,
                                   buf_new),
                                  (lambda h: v_t(h, p_new - 1), buf_old), m_cur, (m_old, m_cur), l,
                                  acc_ref)

    def trip(masked, t, carry):
        m_a, m_b, l = carry
        m_c, l = fused(2 * t + 1, masked, buf1, buf0, m_a, m_b, l)
        m_d, l = fused(2 * t + 2, masked, buf0, buf1, m_b, m_c, l)
        return m_c, m_d, l

    m0 = m_ref[...]
    m1, l = _stage_and_consume((lambda h: logits(h, 0), mask(0), buf0), None, m0, None,
                               l_ref[...], acc_ref)
    split = jnp.maximum(n_free - 1, 0)
    carry = lax.fori_loop(0, split, functools.partial(trip, False), (m0, m1, l))
    m_a, m_b, l = lax.fori_loop(split, n_used - 1, functools.partial(trip, True), carry)
    last = 2 * n_used - 1
    m_c, l = fused(last, True, buf1, buf0, m_a, m_b, l)
    _, l = _stage_and_consume(None, (lambda h: v_t(h, last), buf1), m_c, (m_b, m_c), l, acc_ref)
    m_ref[...] = m_c
    l_ref[...] = l


def _softmax_init(m_ref, l_ref, acc_ref):
    m_ref[...] = jnp.full(m_ref.shape, NEG_INF, F32)
    l_ref[...] = jnp.zeros(l_ref.shape, F32)
    acc_ref[...] = jnp.zeros(acc_ref.shape, F32)


def _softmax_finish(o_ref, l_ref, acc_ref):
    inv = 1.0 / l_ref[...]
    parts = [acc_ref[h * HEAD_DIM:(h + 1) * HEAD_DIM, :] * inv[h:h + 1, :] for h in range(N_HEADS)]
    o_ref[0] = jnp.concatenate(parts, axis=0).T.astype(o_ref.dtype)


def _f32_to_ordered(x):
    b = lax.bitcast_convert_type(x, I32)
    return jnp.where(b >= 0, b, b ^ jnp.int32(0x7FFFFFFF))


def _ordered_to_f32(o):
    return lax.bitcast_convert_type(jnp.where(o >= 0, o, o ^ jnp.int32(0x7FFFFFFF)), F32)


def _dsa_kernel(qiT_ref, wiT_ref, qaT_ref, ki_ref, ka_ref, vaT_ref, o_ref,
                s_ref, gm_ref, thr_ref, cut_ref, qp_ref, m_ref, l_ref, acc_ref, lg0_ref, lg1_ref,
                *, past, length, n_sel, tq, tk, nk):
    i = pl.program_id(1)
    j = pl.program_id(2)
    q0 = past + i * tq
    qpos = q0 + lax.broadcasted_iota(I32, (1, tq), 1)
    vis = jnp.minimum(((qpos >> CHUNK_SHIFT) + 1) * CHUNK, length)
    vmax = jnp.minimum((((q0 + tq - 1) >> CHUNK_SHIFT) + 1) * CHUNK, length)
    nvis = (vmax + tk - 1) // tk
    vcap = ((vmax + KEY_QUAD - 1) // KEY_QUAD) * KEY_QUAD

    def rows_here(jb, step):
        return jnp.clip((vcap - jb * tk) // step, 0, tk // step)

    @pl.when(jnp.logical_and(j < nk, j < nvis))
    def _():
        def body(c, carry):
            for half in range(GROUPS // SCORE_ROWS):
                r0 = pl.multiple_of(c * GROUPS + half * SCORE_ROWS, SCORE_ROWS)
                kblk = ki_ref[0, pl.ds(r0, SCORE_ROWS), :]
                acc = jnp.zeros((SCORE_ROWS, tq), F32)
                for h in range(N_HEADS):
                    d = jnp.dot(kblk, qiT_ref[0, h * IDX_DIM:(h + 1) * IDX_DIM, :],
                                preferred_element_type=F32)
                    acc = acc + jnp.maximum(d, 0.0) * wiT_ref[0, h:h + 1, :]
                kpos = j * tk + r0 + lax.broadcasted_iota(I32, (SCORE_ROWS, 1), 0)
                acc = jnp.where(kpos < vis, acc, -jnp.inf)
                s_ref[pl.ds(pl.multiple_of(j * tk + r0, SCORE_ROWS), SCORE_ROWS), :] = acc
                grp = slice(half * SCORE_ROWS, (half + 1) * SCORE_ROWS)
                gm_ref[grp, :] = jnp.maximum(gm_ref[grp, :], acc)
            return carry

        @pl.when(j == 0)
        def _():
            gm_ref[...] = jnp.full(gm_ref.shape, -jnp.inf, F32)

        lax.fori_loop(0, rows_here(j, GROUPS), body, 0)

    @pl.when(j == nk - 1)
    def _():
        nch = vcap // COUNT_ROWS

        def count(pred):
            sub = 64

            def body(c, accs):
                accs = list(accs)
                for k in range(COUNT_ROWS // sub):
                    r0 = pl.multiple_of(c * COUNT_ROWS + k * sub, sub)
                    hit = jnp.where(pred(s_ref[pl.ds(r0, sub), :], r0), 1.0, 0.0)
                    accs[k % 4] = accs[k % 4] + jnp.sum(hit.reshape(sub // 8, 8, tq), axis=0)
                return tuple(accs)
            zero = jnp.zeros((8, tq), F32)
            a0, a1, a2, a3 = lax.fori_loop(0, nch, body, (zero, zero, zero, zero))
            return jnp.sum((a0 + a1) + (a2 + a3), axis=0, keepdims=True)

        def count_ge(t):
            return count(lambda blk, r0: blk >= t)

        kf = float(n_sel)
        no_cut = jnp.full((1, tq), 2 ** 30, I32)

        gm = gm_ref[...]
        few = vis < n_sel
        lo0 = _f32_to_ordered(jnp.min(gm, axis=0, keepdims=True))
        hi0 = _f32_to_ordered(jnp.max(gm, axis=0, keepdims=True)) + 1

        pos_norm = jnp.int32(0x00800000)
        neg_norm = -pos_norm - 1

        def finished(lo, hi, c_lo):
            point = jnp.logical_or(hi <= lo + 1,
                                   jnp.logical_or(jnp.logical_and(lo >= 0, hi <= pos_norm),
                                                  jnp.logical_and(lo >= neg_norm, hi <= 0)))
            return jnp.logical_or(few, jnp.logical_or(c_lo == kf, point))

        def more(state):
            _, lo, hi, c_lo = state
            return jnp.max(jnp.where(finished(lo, hi, c_lo), 0.0, 1.0)) > 0.0

        def step(state):
            k, lo, hi, c_lo = state
            mid_f = _f32_to_ordered(0.5 * _ordered_to_f32(lo) + 0.5 * _ordered_to_f32(hi - 1))
            mid_i = (lo >> 1) + (hi >> 1) + (lo & hi & 1)
            wide = ((lo ^ (hi - 1)) >> 23) != 0
            piv = jnp.where(jnp.logical_and(wide, k < FLOAT_STEPS), mid_f, mid_i)
            piv = jnp.where(jnp.logical_and(jnp.logical_and(lo >= 0, lo < pos_norm), hi > pos_norm),
                            pos_norm, piv)
            piv = jnp.where(jnp.logical_and(jnp.logical_and(lo < neg_norm, hi > neg_norm), hi <= 0),
                            neg_norm, piv)
            piv = jnp.where(jnp.logical_and(lo < 0, hi > 0), 0, piv)
            piv = jnp.clip(piv, lo + 1, hi - 1)
            c = count_ge(_ordered_to_f32(piv))
            live = jnp.logical_not(finished(lo, hi, c_lo))
            up = jnp.logical_and(live, c >= kf)
            down = jnp.logical_and(live, c < kf)
            return (k + 1, jnp.where(up, piv, lo), jnp.where(down, piv, hi),
                    jnp.where(up, c, c_lo))

        state = (jnp.int32(0), lo0, hi0, jnp.full((1, tq), -1.0, F32))
        _, cur, _, c_cur = lax.while_loop(more, step, state)
        thr = jnp.where(few, -F32_MAX, _ordered_to_f32(cur))

        def with_ties():
            n_gt = count(lambda blk, r0: blk > thr)
            n_ge = count_ge(thr)
            need = kf - n_gt
            tied = (n_ge - n_gt) > need

            def tie_cut():
                rr = lax.broadcasted_iota(I32, (KEY_CHUNK, KEY_CHUNK), 0)
                cc = lax.broadcasted_iota(I32, (KEY_CHUNK, KEY_CHUNK), 1)
                lower = jnp.where(cc <= rr, 1.0, 0.0).astype(BF16)

                def body(c, carry):
                    run, last = carry
                    for u in range(KEY_QUAD // KEY_CHUNK):
                        r0 = pl.multiple_of(c * KEY_QUAD + u * KEY_CHUNK, KEY_CHUNK)
                        eq = s_ref[pl.ds(r0, KEY_CHUNK), :] == thr
                        hit = jnp.where(eq, 1.0, 0.0)
                        ordinal = run + jnp.dot(lower, hit.astype(BF16),
                                                preferred_element_type=F32)
                        kpos = (r0 + lax.broadcasted_iota(I32, (KEY_CHUNK, 1), 0)).astype(F32)
                        take = jnp.logical_and(eq, ordinal <= need)
                        last = jnp.maximum(last, jnp.max(jnp.where(take, kpos, -1.0), axis=0,
                                                         keepdims=True))
                        run = run + jnp.sum(hit, axis=0, keepdims=True)
                    return run, last

                init = (jnp.zeros((1, tq), F32), jnp.full((1, tq), -1.0, F32))
                _, last = lax.fori_loop(0, vcap // KEY_QUAD, body, init)
                return jnp.where(tied, last.astype(I32), no_cut)

            return lax.cond(jnp.max(jnp.where(tied, 1.0, 0.0)) > 0.0, tie_cut, lambda: no_cut)

        unsure = jnp.logical_or(c_cur != kf, few)
        cut = lax.cond(jnp.max(jnp.where(unsure, 1.0, 0.0)) > 0.0, with_ties, lambda: no_cut)
        thr_ref[0:1, :] = thr
        cut_ref[0:1, :] = cut
        for h in range(N_HEADS):
            qp_ref[h] = _pair_padded(qaT_ref, h)
        _softmax_init(m_ref, l_ref, acc_ref)

    jj = j - nk

    @pl.when(jnp.logical_and(j >= nk, jj < nvis))
    def _():
        thr = thr_ref[0:1, :]
        cut = cut_ref[0:1, :]

        def selected(p):
            g0 = pl.multiple_of(jj * tk + p * KEY_PAIR, KEY_PAIR)
            sblk = s_ref[pl.ds(g0, KEY_PAIR), :]
            kpos = g0 + lax.broadcasted_iota(I32, (KEY_PAIR, 1), 0)
            return jnp.logical_or(sblk > thr, jnp.logical_and(sblk == thr, kpos <= cut))

        def logits(h, p):
            r0 = pl.multiple_of(p * KEY_PAIR, KEY_PAIR)
            kblk = ka_ref[0, pl.ds(r0, KEY_PAIR), (h // 2) * LANES:(h // 2 + 1) * LANES]
            return jnp.dot(kblk, qp_ref[h], preferred_element_type=F32)

        def v_t(h, p):
            rows = slice(h * HEAD_DIM, (h + 1) * HEAD_DIM)
            return jnp.concatenate([vaT_ref[0, 2 * p, rows, :], vaT_ref[0, 2 * p + 1, rows, :]],
                                   axis=1)

        _attend_block(0, rows_here(jj, KEY_QUAD), logits, selected, v_t, m_ref, l_ref, acc_ref,
                      lg0_ref, lg1_ref)

    @pl.when(j == 2 * nk - 1)
    def _():
        _softmax_finish(o_ref, l_ref, acc_ref)


def _dsa_attention(qiT, wiT, qaT, ki, ka, vaTc, *, past, length, tq, tk):
    b, _, tqp = qiT.shape
    lp = ki.shape[1]
    nq, nk = tqp // tq, lp // tk
    n_sel = min(TOPK_MAX, length // 4)

    def nvis(i):
        vmax = jnp.minimum(((past + (i + 1) * tq - 1) // CHUNK + 1) * CHUNK, length)
        return (vmax + tk - 1) // tk

    k1 = lambda bb, i, j: (bb, jnp.minimum(j, nvis(i) - 1), 0)
    k3 = lambda bb, i, j: (bb, jnp.clip(j - nk, 0, nvis(i) - 1), 0)
    k3c = lambda bb, i, j: (bb, jnp.clip(j - nk, 0, nvis(i) - 1), 0, 0)
    qmap = lambda bb, i, j: (bb, 0, i)
    kern = functools.partial(_dsa_kernel, past=past, length=length, n_sel=n_sel, tq=tq, tk=tk,
                             nk=nk)
    return pl.pallas_call(
        kern,
        out_shape=jax.ShapeDtypeStruct((b, tqp, D_BRANCH), BF16),
        grid=(b, nq, 2 * nk),
        in_specs=[
            pl.BlockSpec((1, D_BRANCH, tq), qmap),
            pl.BlockSpec((1, N_HEADS, tq), qmap),
            pl.BlockSpec((1, D_BRANCH, tq), qmap),
            pl.BlockSpec((1, tk, IDX_DIM), k1),
            pl.BlockSpec((1, tk, D_BRANCH), k3),
            pl.BlockSpec((1, tk // KEY_CHUNK, D_BRANCH, KEY_CHUNK), k3c),
        ],
        out_specs=pl.BlockSpec((1, tq, D_BRANCH), lambda bb, i, j: (bb, i, 0)),
        scratch_shapes=[
            pltpu.VMEM((lp, tq), F32),
            pltpu.VMEM((GROUPS, tq), F32),
            pltpu.VMEM((8, tq), F32),
            pltpu.VMEM((8, tq), I32),
            pltpu.VMEM((N_HEADS, LANES, tq), BF16),
            pltpu.VMEM((N_HEADS, tq), F32),
            pltpu.VMEM((N_HEADS, tq), F32),
            pltpu.VMEM((D_BRANCH, tq), F32),
            pltpu.VMEM((N_HEADS, KEY_PAIR, tq), F32),
            pltpu.VMEM((N_HEADS, KEY_PAIR, tq), F32),
        ],
        compiler_params=_cparams(("arbitrary", "arbitrary", "arbitrary")),
        name="dsa_attention",
    )(qiT, wiT, qaT, ki, ka, vaTc)


_AUG = 6


def _fox_kernel(qbT_ref, qaug_ref, kb_ref, kaug_ref, vbT_ref, o_ref,
                qf_ref, m_ref, l_ref, acc_ref, lg0_ref, lg1_ref, *, past, tq, tk, nk):
    i = pl.program_id(1)
    j = pl.program_id(2)
    q0 = past + i * tq
    qpos = q0 + lax.broadcasted_iota(I32, (1, tq), 1)
    nvis = (q0 + tq + tk - 1) // tk
    nquad = tk // KEY_QUAD
    n_free = jnp.clip((q0 + 1 - j * tk) // KEY_QUAD, 0, nquad)
    n_used = jnp.clip((q0 + tq - j * tk + KEY_QUAD - 1) // KEY_QUAD, 0, nquad)

    @pl.when(j == 0)
    def _():
        arow = lax.broadcasted_iota(I32, (LANES, 1), 0)
        qa = qaug_ref[0]
        for h in range(N_HEADS):
            mine = jnp.logical_and(arow >= _AUG * h, arow < _AUG * (h + 1))
            qf_ref[h, 0:LANES, :] = _pair_padded(qbT_ref, h)
            qf_ref[h, LANES:2 * LANES, :] = jnp.where(mine, qa, 0.0).astype(BF16)
        _softmax_init(m_ref, l_ref, acc_ref)

    @pl.when(j < nvis)
    def _():
        def causal(p):
            kpos = j * tk + p * KEY_PAIR + lax.broadcasted_iota(I32, (KEY_PAIR, 1), 0)
            return kpos <= qpos

        def logits(h, p):
            r0 = pl.multiple_of(p * KEY_PAIR, KEY_PAIR)
            kblk = kb_ref[0, pl.ds(r0, KEY_PAIR), (h // 2) * LANES:(h // 2 + 1) * LANES]
            lhs = jnp.concatenate([kblk, kaug_ref[0, pl.ds(r0, KEY_PAIR), :]], axis=1)
            return jnp.dot(lhs, qf_ref[h], preferred_element_type=F32)

        def v_t(h, p):
            rows = slice(h * HEAD_DIM, (h + 1) * HEAD_DIM)
            return jnp.concatenate([vbT_ref[0, 2 * p, rows, :], vbT_ref[0, 2 * p + 1, rows, :]],
                                   axis=1)

        _attend_block(n_free, n_used, logits, causal, v_t, m_ref, l_ref, acc_ref,
                      lg0_ref, lg1_ref)

    @pl.when(j == nk - 1)
    def _():
        _softmax_finish(o_ref, l_ref, acc_ref)


def _fox_attention(qbT, qaug, kb, kaug, vbTc, *, past, tq, tk):
    b, _, tqp = qbT.shape
    lp = kb.shape[1]
    nq, nk = tqp // tq, lp // tk

    def last(i):
        return jnp.minimum((past + (i + 1) * tq + tk - 1) // tk, nk) - 1

    kmap = lambda bb, i, j: (bb, jnp.minimum(j, last(i)), 0)
    kmapc = lambda bb, i, j: (bb, jnp.minimum(j, last(i)), 0, 0)
    qmap = lambda bb, i, j: (bb, 0, i)
    return pl.pallas_call(
        functools.partial(_fox_kernel, past=past, tq=tq, tk=tk, nk=nk),
        out_shape=jax.ShapeDtypeStruct((b, tqp, D_BRANCH), BF16),
        grid=(b, nq, nk),
        in_specs=[
            pl.BlockSpec((1, D_BRANCH, tq), qmap),
            pl.BlockSpec((1, LANES, tq), qmap),
            pl.BlockSpec((1, tk, D_BRANCH), kmap),
            pl.BlockSpec((1, tk, LANES), kmap),
            pl.BlockSpec((1, tk // KEY_CHUNK, D_BRANCH, KEY_CHUNK), kmapc),
        ],
        out_specs=pl.BlockSpec((1, tq, D_BRANCH), lambda bb, i, j: (bb, i, 0)),
        scratch_shapes=[
            pltpu.VMEM((N_HEADS, 2 * LANES, tq), BF16),
            pltpu.VMEM((N_HEADS, tq), F32),
            pltpu.VMEM((N_HEADS, tq), F32),
            pltpu.VMEM((D_BRANCH, tq), F32),
            pltpu.VMEM((N_HEADS, KEY_PAIR, tq), F32),
            pltpu.VMEM((N_HEADS, KEY_PAIR, tq), F32),
        ],
        compiler_params=_cparams(("arbitrary", "arbitrary", "arbitrary")),
        name="fox_attention",
    )(qbT, qaug, kb, kaug, vbTc)


AUG_ROWS = 1024


def _aug_kernel(c_ref, kaug_ref, qaugT_ref):
    x = c_ref[0] * LOG2E
    a = x.astype(BF16).astype(F32)
    r = x - a
    b = r.astype(BF16).astype(F32)
    c = (r - b).astype(BF16).astype(F32)
    lane = lax.broadcasted_iota(I32, (N_HEADS, LANES), 1)
    head = lax.broadcasted_iota(I32, (N_HEADS, LANES), 0)

    def place(v, slot):
        sel = jnp.where(lane == _AUG * head + slot, 1.0, 0.0)
        return jnp.dot(v, sel, preferred_element_type=F32)

    slot = lane - _AUG * head
    ones_k = jnp.sum(jnp.where(jnp.logical_and(slot >= 0, slot < 3), 1.0, 0.0), axis=0, keepdims=True)
    ones_q = jnp.sum(jnp.where(jnp.logical_and(slot >= 3, slot < 6), 1.0, 0.0), axis=0, keepdims=True)
    kaug_ref[0] = (ones_k - (place(a, 3) + place(b, 4) + place(c, 5))).astype(BF16)
    qaugT_ref[0] = (ones_q + (place(a, 0) + place(b, 1) + place(c, 2))).T


def _fox_augment(cum):
    b, lp, h = cum.shape
    return pl.pallas_call(
        _aug_kernel,
        out_shape=(jax.ShapeDtypeStruct((b, lp, LANES), BF16),
                   jax.ShapeDtypeStruct((b, LANES, lp), F32)),
        grid=(b, lp // AUG_ROWS),
        in_specs=[pl.BlockSpec((1, AUG_ROWS, h), lambda i, j: (i, j, 0))],
        out_specs=(pl.BlockSpec((1, AUG_ROWS, LANES), lambda i, j: (i, j, 0)),
                   pl.BlockSpec((1, LANES, AUG_ROWS), lambda i, j: (i, 0, j))),
        compiler_params=_cparams(("arbitrary", "arbitrary")),
        name="fox_augment",
    )(cum)


def _shifted_rows(buf, load, store, u, past_ref, starts_batch, nb):
    tm, c = u.shape
    store(slice(8, 8 + tm), u)
    if nb == 1:
        @pl.when(starts_batch)
        def _():
            store(slice(6, 8), past_ref[0])
    s1 = load(slice(7, 7 + tm))
    s2 = load(slice(6, 6 + tm))
    if nb > 1:
        r = lax.broadcasted_iota(I32, (nb, tm // nb, 1), 1)
        p0 = past_ref[:, 0:1, :]
        p1 = past_ref[:, 1:2, :]
        s1 = jnp.where(r == 0, p1, _group(s1, nb)).reshape(tm, c)
        s2 = jnp.where(r == 0, p0, jnp.where(r == 1, p1, _group(s2, nb))).reshape(tm, c)
    store(slice(0, 8), load(slice(tm, tm + 8)))
    return s1, s2


def _merge_kernel(x_ref, ya_ref, yb_ref, cb_ref, u_ref, gl_ref, past_ref, cw_ref, wbr_ref,
                  wo_ref, g_ref, mod_ref, o_ref, ub_ref, *, nb, rows_per_batch):
    i = pl.program_id(0)
    tm = x_ref.shape[0]
    u = u_ref[...]

    def load(rows):
        return ub_ref[rows, :]

    def store(rows, v):
        ub_ref[rows, :] = v

    s1, s2 = _shifted_rows(ub_ref, load, store, u, past_ref, (i * tm) % rows_per_batch == 0, nb)
    conv = cw_ref[0:1, :] * s2 + cw_ref[1:2, :] * s1 + cw_ref[2:3, :] * u
    yc = (cb_ref[...] * conv).astype(BF16)
    mix = jnp.zeros((tm, D_MODEL), F32)
    for n, y in enumerate((ya_ref[...], yb_ref[...], yc)):
        br = jnp.dot(y, wbr_ref[n], preferred_element_type=F32)
        mix = mix + jax.nn.sigmoid(gl_ref[:, n * D_MODEL:(n + 1) * D_MODEL]) * br
    mo = jnp.dot(mix.astype(BF16), wo_ref[...], preferred_element_type=F32)
    nm = _group(_rms(mo, g_ref[1:2, :]), nb)
    o_ref[...] = x_ref[...] + (mod_ref[:, 2:3, :] * nm).reshape(tm, D_MODEL)


def _mod_spec(tm, nb, rows_per_batch, ngrid):
    if nb == 1:
        f = lambda i, *_: ((i * tm) // rows_per_batch, 0, 0)
    else:
        f = lambda i, *_: (i, 0, 0)
    return pl.BlockSpec((nb, 6, D_MODEL), f)


def _merge(x, ya, yb, cb, u, gl, past, cw, wbr, wo, g, mod, *, tm, rows_per_batch):
    rows = x.shape[0]
    nb = max(1, tm // rows_per_batch)
    row = lambda w: pl.BlockSpec((tm, w), lambda i: (i, 0))
    full = lambda shape: pl.BlockSpec(shape, lambda i: (0,) * len(shape))
    if nb == 1:
        past_spec = pl.BlockSpec((1, 2, D_BRANCH), lambda i: ((i * tm) // rows_per_batch, 0, 0))
    else:
        past_spec = pl.BlockSpec((nb, 2, D_BRANCH), lambda i: (i, 0, 0))
    return pl.pallas_call(
        functools.partial(_merge_kernel, nb=nb, rows_per_batch=rows_per_batch),
        out_shape=jax.ShapeDtypeStruct((rows, D_MODEL), F32),
        grid=(rows // tm,),
        in_specs=[
            row(D_MODEL), row(D_BRANCH), row(D_BRANCH), row(D_BRANCH), row(D_BRANCH),
            row(3 * D_MODEL), past_spec, full((CONV_W, D_BRANCH)),
            full((3, D_BRANCH, D_MODEL)), full((D_MODEL, D_MODEL)), full((4, D_MODEL)),
            _mod_spec(tm, nb, rows_per_batch, 1),
        ],
        out_specs=row(D_MODEL),
        scratch_shapes=[pltpu.VMEM((tm + 8, D_BRANCH), F32)],
        compiler_params=_cparams(("arbitrary",)),
        name="branch_merge",
    )(x, ya, yb, cb, u, gl, past, cw, wbr, wo, g, mod)


FF_CHUNK = 1408


def _ffn_kernel(x_ref, g_ref, mod_ref, wg_ref, wv_ref, wd_ref, cw_ref, past_ref,
                o_ref, tail_ref, hs_ref, acc_ref, ub_ref, *, nb, rows_per_batch, nj):
    i = pl.program_id(0)
    j = pl.program_id(1)
    tm = x_ref.shape[0]

    @pl.when(j == 0)
    def _():
        y = _group(_rms(x_ref[...], g_ref[2:3, :]), nb)
        h = y * (1.0 + mod_ref[:, 4:5, :]) + mod_ref[:, 3:4, :]
        hs_ref[...] = h.reshape(tm, D_MODEL).astype(BF16)
        acc_ref[...] = jnp.zeros(acc_ref.shape, F32)

    hs = hs_ref[...]
    ug = jnp.dot(hs, wg_ref[...], preferred_element_type=F32)
    uv = jnp.dot(hs, wv_ref[...], preferred_element_type=F32)

    def load(rows):
        return ub_ref[j, rows, :]

    def store(rows, v):
        ub_ref[j, rows, :] = v

    s1, s2 = _shifted_rows(ub_ref, load, store, ug, past_ref, (i * tm) % rows_per_batch == 0, nb)
    conv = cw_ref[0:1, :] * s2 + cw_ref[1:2, :] * s1 + cw_ref[2:3, :] * ug
    f = conv * jax.nn.sigmoid(conv) * uv
    acc_ref[...] += jnp.dot(f.astype(BF16), wd_ref[...], preferred_element_type=F32)
    grp = tm // nb
    tail_ref[0] = _group(ug, nb)[:, grp - 8:, :]

    @pl.when(j == nj - 1)
    def _():
        nm = _group(_rms(acc_ref[...], g_ref[3:4, :]), nb)
        o_ref[...] = x_ref[...] + (mod_ref[:, 5:6, :] * nm).reshape(tm, D_MODEL)


def _conv_ffn(x, g, mod, w_up, w_down, cw, past, *, tm, rows_per_batch):
    rows = x.shape[0]
    dff = w_down.shape[0]
    nj = dff // FF_CHUNK
    nb = max(1, tm // rows_per_batch)
    nt = rows // tm
    if nb == 1:
        past_spec = pl.BlockSpec((1, 2, FF_CHUNK), lambda i, j: ((i * tm) // rows_per_batch, 0, j))
        mod_map = lambda i, j: ((i * tm) // rows_per_batch, 0, 0)
    else:
        past_spec = pl.BlockSpec((nb, 2, FF_CHUNK), lambda i, j: (i, 0, j))
        mod_map = lambda i, j: (i, 0, 0)
    out, tails = pl.pallas_call(
        functools.partial(_ffn_kernel, nb=nb, rows_per_batch=rows_per_batch, nj=nj),
        out_shape=(jax.ShapeDtypeStruct((rows, D_MODEL), F32),
                   jax.ShapeDtypeStruct((nt, nb, 8, dff), F32)),
        grid=(nt, nj),
        in_specs=[
            pl.BlockSpec((tm, D_MODEL), lambda i, j: (i, 0)),
            pl.BlockSpec((4, D_MODEL), lambda i, j: (0, 0)),
            pl.BlockSpec((nb, 6, D_MODEL), mod_map),
            pl.BlockSpec((D_MODEL, FF_CHUNK), lambda i, j: (0, j)),
            pl.BlockSpec((D_MODEL, FF_CHUNK), lambda i, j: (0, j + nj)),
            pl.BlockSpec((FF_CHUNK, D_MODEL), lambda i, j: (j, 0)),
            pl.BlockSpec((CONV_W, FF_CHUNK), lambda i, j: (0, j)),
            past_spec,
        ],
        out_specs=(pl.BlockSpec((tm, D_MODEL), lambda i, j: (i, 0)),
                   pl.BlockSpec((1, nb, 8, FF_CHUNK), lambda i, j: (i, 0, 0, j))),
        scratch_shapes=[
            pltpu.VMEM((tm, D_MODEL), BF16),
            pltpu.VMEM((tm, D_MODEL), F32),
            pltpu.VMEM((nj, tm + 8, FF_CHUNK), F32),
        ],
        compiler_params=_cparams(("arbitrary", "arbitrary")),
        name="conv_ffn",
    )(x, g, mod, w_up, w_up, w_down, cw, past)
    return out, tails


def _relayout_w_in(w_in, b_forget):
    db = D_BRANCH
    o_qi = 3 * db
    o_ki = o_qi + N_HEADS * IDX_DIM
    o_wi = o_ki + IDX_DIM
    o_qb = o_wi + N_HEADS
    o_fl = o_qb + 3 * db
    o_cb = o_fl + N_HEADS
    o_gl = o_cb + 3 * db
    main = jnp.concatenate([w_in[:, 0:o_ki], w_in[:, o_qb:o_fl], w_in[:, o_cb:]], axis=1)
    small = jnp.concatenate([w_in[:, o_ki:o_wi], w_in[:, o_wi:o_qb], w_in[:, o_fl:o_cb]], axis=1)
    small = jnp.pad(small, ((0, 0), (0, LANES - small.shape[1])))
    bf = jnp.zeros((1, LANES), F32).at[0, _FL_OFF:_FL_OFF + N_HEADS].set(b_forget)
    del o_gl
    return main.astype(BF16), small.astype(BF16), bf


def _round_up(x, m):
    return (x + m - 1) // m * m


def _layer(x, mod, caches, weights, *, batch, t, past, cfg):
    (g, w_main, w_small, bf_pad, cw_mix, wbr, wo, w_up, cw_ffn, w_down) = weights
    rows = batch * t
    length = past + t
    (qaT, ka32, ka16, va32, vaTc, qiT, qbT, kb32, kb16, vb32, vbTc, cb, u, gl,
     ki32, ki16, wiT, logf) = _in_projection(
        x, mod, g, w_main, w_small, bf_pad, tm=cfg["tm_in"], rows_per_batch=t)

    tq, tk = cfg["tq"], cfg["tk"]
    tqp = _round_up(t, tq)
    lp = _round_up(length, tk)

    def per_batch_T(aT):
        c = aT.shape[0]
        a = jnp.swapaxes(aT.reshape(c, batch, t), 0, 1)
        return jnp.pad(a, ((0, 0), (0, 0), (0, tqp - t)))

    def chunked_T(a):
        return jnp.swapaxes(a.reshape(batch, lp // KEY_CHUNK, KEY_CHUNK, a.shape[-1]), 2, 3)

    def with_cache(cache, new, dtype):
        new = new.reshape(batch, t, -1)
        if cache is not None:
            new = jnp.concatenate([cache.reshape(batch, past, -1).astype(dtype), new.astype(dtype)],
                                  axis=1)
        return jnp.pad(new.astype(dtype), ((0, 0), (0, lp - length), (0, 0)))

    if caches is None:
        c_idx = c_dk = c_dv = c_fk = c_fv = c_lf = None
        past_mix = jnp.zeros((batch, CONV_W - 1, D_BRANCH), F32)
        past_ffn = jnp.zeros((batch, CONV_W - 1, w_down.shape[0]), F32)
    else:
        c_idx, c_dk, c_dv, c_fk, c_fv, c_lf, past_mix, past_ffn = caches

    ki_all = with_cache(c_idx, ki16, BF16)
    ka_all = with_cache(c_dk, ka16, BF16)
    kb_all = with_cache(c_fk, kb16, BF16)
    if caches is None and lp == rows:
        vaT_all = vaTc.reshape(batch, lp // KEY_CHUNK, D_BRANCH, KEY_CHUNK)
        vbT_all = vbTc.reshape(batch, lp // KEY_CHUNK, D_BRANCH, KEY_CHUNK)
    else:
        vaT_all = chunked_T(with_cache(c_dv, va32, BF16))
        vbT_all = chunked_T(with_cache(c_fv, vb32, BF16))
    logf_all = with_cache(c_lf, logf, F32)

    ya = _dsa_attention(per_batch_T(qiT), per_batch_T(wiT), per_batch_T(qaT), ki_all, ka_all,
                        vaT_all, past=past, length=length, tq=tq, tk=tk)
    cum = _forget_cumsum(logf_all)
    kaug, qaug = _fox_augment(cum)
    qaug = jnp.pad(qaug[:, :, past:length], ((0, 0), (0, 0), (0, tqp - t)))
    yb = _fox_attention(per_batch_T(qbT), qaug, kb_all, kaug, vbT_all, past=past, tq=tq, tk=tk)
    ya = ya[:, :t].reshape(rows, D_BRANCH)
    yb = yb[:, :t].reshape(rows, D_BRANCH)

    x1 = _merge(x, ya, yb, cb, u, gl, past_mix, cw_mix, wbr, wo, g, mod,
                tm=cfg["tm_merge"], rows_per_batch=t)
    x2, tails = _conv_ffn(x1, g, mod, w_up, w_down, cw_ffn, past_ffn,
                          tm=cfg["tm_ffn"], rows_per_batch=t)

    new_mix = u.reshape(batch, t, D_BRANCH)[:, t - (CONV_W - 1):]
    nb = max(1, cfg["tm_ffn"] // t)
    if nb == 1:
        tiles_per_batch = t // cfg["tm_ffn"]
        last = tails.reshape(batch, tiles_per_batch, 8, -1)[:, -1]
    else:
        last = tails.reshape(batch, 8, -1)
    new_ffn = last[:, 8 - (CONV_W - 1):]
    hd = (batch, t, N_HEADS, HEAD_DIM)
    state = (ki32.reshape(batch, t, IDX_DIM), ka32.reshape(hd), va32.reshape(hd),
             kb32.reshape(hd), vb32.reshape(hd), logf.reshape(batch, t, N_HEADS), new_mix, new_ffn)
    return x2, state


_PROMPT_CFG = dict(tm_in=512, tm_merge=256, tm_ffn=256, tq=256, tk=4096)
_SAMPLE_CFG = dict(tm_in=256, tm_merge=256, tm_ffn=256, tq=128, tk=3072)


def kernel(x_prompt, x_sample, c_prompt, c_sample, cache_idx_k, cache_dsa_k, cache_dsa_v,
           cache_fox_k, cache_fox_v, cache_fox_logf, state_conv_mix, state_conv_ffn,
           w_ada, b_ada, norm_g, w_in, b_forget, conv_mix_w, w_branch, w_out, w_up,
           conv_ffn_w, w_down):
    bp, tp, _ = x_prompt.shape
    bs, ts, _ = x_sample.shape
    past = cache_dsa_k.shape[2]
    depth = w_ada.shape[0]

    c_all = jnp.concatenate([c_prompt, c_sample], axis=0)
    pad_rows = _round_up(c_all.shape[0], 8) - c_all.shape[0]
    c_all = jnp.pad(c_all, ((0, pad_rows), (0, 0)))
    mod_all = _modulation(c_all, w_ada, b_ada).reshape(depth, -1, 6, D_MODEL)

    yp = x_prompt.reshape(bp * tp, D_MODEL)
    ys = x_sample.reshape(bs * ts, D_MODEL)
    p_states, s_states = [], []
    for l in range(depth):
        w_main, w_small, bf_pad = _relayout_w_in(w_in[l], b_forget[l])
        weights = (norm_g[l], w_main, w_small, bf_pad, conv_mix_w[l], w_branch[l].astype(BF16),
                   w_out[l].astype(BF16), w_up[l].astype(BF16), conv_ffn_w[l],
                   w_down[l].astype(BF16))
        yp, st_p = _layer(yp, mod_all[l, :bp], None, weights, batch=bp, t=tp, past=0,
                          cfg=_PROMPT_CFG)
        caches = (cache_idx_k[l], cache_dsa_k[l], cache_dsa_v[l], cache_fox_k[l], cache_fox_v[l],
                  cache_fox_logf[l], state_conv_mix[l], state_conv_ffn[l])
        ys, st_s = _layer(ys, mod_all[l, bp:bp + bs], caches, weights, batch=bs, t=ts, past=past,
                          cfg=_SAMPLE_CFG)
        p_states.append(st_p)
        s_states.append(st_s)

    stack = lambda states: [jnp.stack([st[k] for st in states], axis=0) for k in range(8)]
    return (yp.reshape(bp, tp, D_MODEL), ys.reshape(bs, ts, D_MODEL), *stack(p_states),
            *stack(s_states))
```

```python
import functools

import jax
import jax.numpy as jnp
from jax import lax
from jax.experimental import pallas as pl
from jax.experimental.pallas import tpu as pltpu

F32 = jnp.float32
BF16 = jnp.bfloat16
I32 = jnp.int32

D_MODEL = 1024
HEAD_DIM = 64
D_BRANCH = 512
N_HEADS = 8
IDX_DIM = 64
CHUNK = 64
CHUNK_SHIFT = 6
TOPK_MAX = 256
CONV_W = 3
EPS = 1e-6
NEG_INF = -1e30
LOG2E = 1.4426950408889634
F32_MAX = 3.4028234663852886e38

LANES = 128
N_MAIN = 16 * D_BRANCH
KEY_CHUNK = 256
KEY_PAIR = 2 * KEY_CHUNK
KEY_QUAD = 2 * KEY_PAIR
SCORE_ROWS = 128
COUNT_ROWS = KEY_QUAD
GROUPS = TOPK_MAX
FLOAT_STEPS = 12
VMEM_LIMIT = 56 * 1024 * 1024


def _cparams(sem):
    return pltpu.CompilerParams(dimension_semantics=sem, vmem_limit_bytes=VMEM_LIMIT)


def _rms(x, g_row):
    return x * lax.rsqrt(jnp.mean(x * x, axis=-1, keepdims=True) + EPS) * g_row


def _group(x, nb):
    return x.reshape(nb, x.shape[0] // nb, x.shape[1])


def _mod_kernel(c_ref, w_ref, b_ref, o_ref):
    c = c_ref[...]
    s = (c * jax.nn.sigmoid(c)).astype(BF16)
    o_ref[0] = jnp.dot(s, w_ref[0].astype(BF16), preferred_element_type=F32) + b_ref[0]


def _modulation(c_all, w_ada, b_ada):
    depth = w_ada.shape[0]
    rows = c_all.shape[0]
    n = w_ada.shape[2]
    tn = D_MODEL
    return pl.pallas_call(
        _mod_kernel,
        out_shape=jax.ShapeDtypeStruct((depth, rows, n), F32),
        grid=(depth, n // tn),
        in_specs=[
            pl.BlockSpec((rows, D_MODEL), lambda l, j: (0, 0)),
            pl.BlockSpec((1, D_MODEL, tn), lambda l, j: (l, 0, j)),
            pl.BlockSpec((1, 1, tn), lambda l, j: (l, 0, j)),
        ],
        out_specs=pl.BlockSpec((1, rows, tn), lambda l, j: (l, 0, j)),
        compiler_params=_cparams(("arbitrary", "arbitrary")),
        name="adaln_mod",
    )(c_all, w_ada, b_ada.reshape(depth, 1, n))


IN_STEP = 2 * D_BRANCH
_GL_STEP = 5
_WI_OFF = IDX_DIM
_FL_OFF = IDX_DIM + N_HEADS


def _inproj_kernel(x_ref, mod_ref, g_ref, wm_ref, ws_ref, bf_ref,
                   qaT_ref, ka32_ref, ka16_ref, va32_ref, vaT_ref, qiT_ref, qbT_ref,
                   kb32_ref, kb16_ref, vb32_ref, vbT_ref, cb_ref, u_ref, gl_ref,
                   ki32_ref, ki16_ref, wiT_ref, logf_ref,
                   hs_ref, *, nb):
    j = pl.program_id(1)
    tm = x_ref.shape[0]

    @pl.when(j == 0)
    def _():
        y = _rms(x_ref[...], g_ref[0:1, :])
        h = _group(y, nb) * (1.0 + mod_ref[:, 1:2, :]) + mod_ref[:, 0:1, :]
        hb = h.reshape(tm, D_MODEL).astype(BF16)
        hs_ref[...] = hb
        sm = jnp.dot(hb, ws_ref[...], preferred_element_type=F32)
        ki32_ref[...] = sm[:, :IDX_DIM]
        ki16_ref[...] = sm[:, :IDX_DIM].astype(BF16)
        wiT_ref[...] = sm.T[_WI_OFF:_WI_OFF + N_HEADS, :]
        z = sm + bf_ref[...]
        lf = -(jnp.maximum(-z, 0.0) + jnp.log1p(jnp.exp(-jnp.abs(z))))
        logf_ref[...] = lf[:, _FL_OFF:_FL_OFF + N_HEADS]

    acc = jnp.dot(hs_ref[...], wm_ref[...], preferred_element_type=F32)
    lo = acc[:, :D_BRANCH]
    hi = acc[:, D_BRANCH:]
    scale = LOG2E * HEAD_DIM ** -0.5

    def store_chunked_T(ref, a):
        aT = a.T.astype(BF16)
        for c in range(tm // KEY_CHUNK):
            ref[c] = aT[:, c * KEY_CHUNK:(c + 1) * KEY_CHUNK]

    @pl.when(j == 0)
    def _():
        qaT_ref[...] = (lo * scale).T.astype(BF16)
        ka32_ref[...] = hi
        ka16_ref[...] = hi.astype(BF16)

    @pl.when(j == 1)
    def _():
        va32_ref[...] = lo
        store_chunked_T(vaT_ref, lo)
        qiT_ref[...] = hi.T.astype(BF16)

    @pl.when(j == 2)
    def _():
        qbT_ref[...] = (lo * scale).T.astype(BF16)
        kb32_ref[...] = hi
        kb16_ref[...] = hi.astype(BF16)

    @pl.when(j == 3)
    def _():
        vb32_ref[...] = lo
        store_chunked_T(vbT_ref, lo)
        cb_ref[...] = hi

    @pl.when(j == 4)
    def _():
        u_ref[...] = lo * hi

    @pl.when(j >= _GL_STEP)
    def _():
        gl_ref[...] = acc


def _in_projection(x, mod, g, w_main, w_small, bf_pad, *, tm, rows_per_batch):
    rows = x.shape[0]
    nt = rows // tm
    nb = max(1, tm // rows_per_batch)
    nstep = N_MAIN // IN_STEP
    if nb == 1:
        mod_map = lambda i, j: ((i * tm) // rows_per_batch, 0, 0)
    else:
        mod_map = lambda i, j: (i, 0, 0)
    row_blk = lambda w: pl.BlockSpec((tm, w), lambda i, j: (i, 0))
    col_blk = lambda h: pl.BlockSpec((h, tm), lambda i, j: (0, i))
    chunkT = pl.BlockSpec((tm // KEY_CHUNK, D_BRANCH, KEY_CHUNK), lambda i, j: (i, 0, 0))
    f32o = lambda w: jax.ShapeDtypeStruct((rows, w), F32)
    b16o = lambda w: jax.ShapeDtypeStruct((rows, w), BF16)
    b16T = jax.ShapeDtypeStruct((D_BRANCH, rows), BF16)
    b16c = jax.ShapeDtypeStruct((rows // KEY_CHUNK, D_BRANCH, KEY_CHUNK), BF16)
    out_shape = (
        b16T, f32o(D_BRANCH), b16o(D_BRANCH), f32o(D_BRANCH), b16c, b16T, b16T,
        f32o(D_BRANCH), b16o(D_BRANCH), f32o(D_BRANCH), b16c, f32o(D_BRANCH), f32o(D_BRANCH),
        f32o(6 * D_BRANCH),
        f32o(IDX_DIM), b16o(IDX_DIM), jax.ShapeDtypeStruct((N_HEADS, rows), F32), f32o(N_HEADS),
    )
    out_specs = (
        col_blk(D_BRANCH), row_blk(D_BRANCH), row_blk(D_BRANCH), row_blk(D_BRANCH), chunkT,
        col_blk(D_BRANCH), col_blk(D_BRANCH),
        row_blk(D_BRANCH), row_blk(D_BRANCH), row_blk(D_BRANCH), chunkT, row_blk(D_BRANCH),
        row_blk(D_BRANCH),
        pl.BlockSpec((tm, IN_STEP), lambda i, j: (i, jnp.clip(j - _GL_STEP, 0, 2))),
        row_blk(IDX_DIM), row_blk(IDX_DIM), col_blk(N_HEADS), row_blk(N_HEADS),
    )
    return pl.pallas_call(
        functools.partial(_inproj_kernel, nb=nb),
        out_shape=out_shape,
        grid=(nt, nstep),
        in_specs=[
            pl.BlockSpec((tm, D_MODEL), lambda i, j: (i, 0)),
            pl.BlockSpec((nb, 6, D_MODEL), mod_map),
            pl.BlockSpec((4, D_MODEL), lambda i, j: (0, 0)),
            pl.BlockSpec((D_MODEL, IN_STEP), lambda i, j: (0, j)),
            pl.BlockSpec((D_MODEL, LANES), lambda i, j: (0, 0)),
            pl.BlockSpec((1, LANES), lambda i, j: (0, 0)),
        ],
        out_specs=out_specs,
        scratch_shapes=[pltpu.VMEM((tm, D_MODEL), BF16)],
        compiler_params=_cparams(("arbitrary", "arbitrary")),
        name="in_projection",
    )(x, mod, g, w_main, w_small, bf_pad)


def _cumsum_kernel(x_ref, o_ref):
    x = x_ref[0]
    n = x.shape[0]
    lane = lax.broadcasted_iota(I32, x.shape, 1)
    row = lax.broadcasted_iota(I32, x.shape, 0)
    s = N_HEADS
    while s < LANES:
        x = x + jnp.where(lane >= s, pltpu.roll(x, s, axis=1), 0.0)
        s *= 2
    t = jnp.where(lane >= LANES - N_HEADS, x, 0.0)
    s = N_HEADS
    while s < LANES:
        t = t + pltpu.roll(t, s, axis=1)
        s *= 2
    t = jnp.where(row >= 1, pltpu.roll(t, 1, axis=0), 0.0)
    s = 1
    while s < n:
        if s < 8:
            sh = jnp.where(row >= s, pltpu.roll(t, s, axis=0), 0.0)
        else:
            sh = jnp.concatenate([jnp.zeros((s, LANES), F32), t[:n - s]], axis=0)
        t = t + sh
        s *= 2
    o_ref[0] = x + t


def _forget_cumsum(logf_all):
    b, lp, h = logf_all.shape
    n = lp * h // LANES
    out = pl.pallas_call(
        _cumsum_kernel,
        out_shape=jax.ShapeDtypeStruct((b, n, LANES), F32),
        grid=(b,),
        in_specs=[pl.BlockSpec((1, n, LANES), lambda i: (i, 0, 0))],
        out_specs=pl.BlockSpec((1, n, LANES), lambda i: (i, 0, 0)),
        compiler_params=_cparams(("arbitrary",)),
        name="forget_cumsum",
    )(logf_all.reshape(b, n, LANES))
    return out.reshape(b, lp, h)


def _pair_padded(qT_ref, h):
    pr, half = divmod(h, 2)
    blk = qT_ref[0, pr * LANES + half * HEAD_DIM:pr * LANES + (half + 1) * HEAD_DIM, :]
    z = jnp.zeros_like(blk)
    return jnp.concatenate([blk, z] if half == 0 else [z, blk], axis=0)


def _rows_to_heads(rows):
    tq = rows[0].shape[1]
    sub = lax.broadcasted_iota(I32, (N_HEADS, tq), 0)
    out = jnp.broadcast_to(rows[0], (N_HEADS, tq))
    for h in range(1, N_HEADS):
        out = jnp.where(sub == h, rows[h], out)
    return out


def _stage_and_consume(stage, consume, m_cur, pend, l, acc_ref):
    m_next, sums, alphas = [], [], []
    for h in range(N_HEADS):
        if stage is not None:
            logits, mask, buf = stage
            lg = logits(h)
            if mask is not None:
                lg = jnp.where(mask, lg, NEG_INF)
            buf[h] = lg
            m_next.append(jnp.maximum(m_cur[h:h + 1, :], jnp.max(lg, axis=0, keepdims=True)))
        if consume is not None:
            v_t, buf = consume
            before, after = pend
            alpha = jnp.exp2(before[h:h + 1, :] - after[h:h + 1, :])
            p = jnp.exp2(buf[h] - after[h:h + 1, :]).astype(BF16)
            ones = jnp.ones((16, p.shape[0]), BF16)
            pv = jnp.dot(jnp.concatenate([v_t(h), ones], axis=0), p, preferred_element_type=F32)
            rows = slice(h * HEAD_DIM, (h + 1) * HEAD_DIM)
            acc_ref[rows, :] = alpha * acc_ref[rows, :] + pv[:HEAD_DIM, :]
            sums.append(pv[HEAD_DIM:HEAD_DIM + 1, :])
            alphas.append(alpha)
    if consume is not None:
        l = _rows_to_heads(alphas) * l + _rows_to_heads(sums)
    return (_rows_to_heads(m_next) if stage is not None else m_cur), l


def _attend_block(n_free, n_used, logits, mask, v_t, m_ref, l_ref, acc_ref, buf0, buf1):
    def fused(p_new, masked, buf_new, buf_old, m_old, m_cur, l):
        return _stage_and_consume((lambda h: logits(h, p_new), mask(p_new) if masked else None,
                                   buf_new),
                                  (lambda h: v_t(h, p_new - 1), buf_old), m_cur, (m_old, m_cur), l,
                                  acc_ref)

    def trip(masked, t, carry):
        m_a, m_b, l = carry
        m_c, l = fused(2 * t + 1, masked, buf1, buf0, m_a, m_b, l)
        m_d, l = fused(2 * t + 2, masked, buf0, buf1, m_b, m_c, l)
        return m_c, m_d, l

    m0 = m_ref[...]
    m1, l = _stage_and_consume((lambda h: logits(h, 0), mask(0), buf0), None, m0, None,
                               l_ref[...], acc_ref)
    split = jnp.maximum(n_free - 1, 0)
    carry = lax.fori_loop(0, split, functools.partial(trip, False), (m0, m1, l))
    m_a, m_b, l = lax.fori_loop(split, n_used - 1, functools.partial(trip, True), carry)
    last = 2 * n_used - 1
    m_c, l = fused(last, True, buf1, buf0, m_a, m_b, l)
    _, l = _stage_and_consume(None, (lambda h: v_t(h, last), buf1), m_c, (m_b, m_c), l, acc_ref)
    m_ref[...] = m_c
    l_ref[...] = l


def _softmax_init(m_ref, l_ref, acc_ref):
    m_ref[...] = jnp.full(m_ref.shape, NEG_INF, F32)
    l_ref[...] = jnp.zeros(l_ref.shape, F32)
    acc_ref[...] = jnp.zeros(acc_ref.shape, F32)


def _softmax_finish(o_ref, l_ref, acc_ref):
    inv = 1.0 / l_ref[...]
    parts = [acc_ref[h * HEAD_DIM:(h + 1) * HEAD_DIM, :] * inv[h:h + 1, :] for h in range(N_HEADS)]
    o_ref[0] = jnp.concatenate(parts, axis=0).T.astype(o_ref.dtype)


def _f32_to_ordered(x):
    b = lax.bitcast_convert_type(x, I32)
    return jnp.where(b >= 0, b, b ^ jnp.int32(0x7FFFFFFF))


def _ordered_to_f32(o):
    return lax.bitcast_convert_type(jnp.where(o >= 0, o, o ^ jnp.int32(0x7FFFFFFF)), F32)


def _dsa_kernel(qiT_ref, wiT_ref, qaT_ref, ki_ref, ka_ref, vaT_ref, o_ref,
                s_ref, gm_ref, thr_ref, cut_ref, qp_ref, m_ref, l_ref, acc_ref, lg0_ref, lg1_ref,
                *, past, length, n_sel, tq, tk, nk):
    i = pl.program_id(1)
    j = pl.program_id(2)
    q0 = past + i * tq
    qpos = q0 + lax.broadcasted_iota(I32, (1, tq), 1)
    vis = jnp.minimum(((qpos >> CHUNK_SHIFT) + 1) * CHUNK, length)
    vmax = jnp.minimum((((q0 + tq - 1) >> CHUNK_SHIFT) + 1) * CHUNK, length)
    nvis = (vmax + tk - 1) // tk
    vcap = ((vmax + KEY_QUAD - 1) // KEY_QUAD) * KEY_QUAD

    def rows_here(jb, step):
        return jnp.clip((vcap - jb * tk) // step, 0, tk // step)

    @pl.when(jnp.logical_and(j < nk, j < nvis))
    def _():
        def body(c, carry):
            for half in range(GROUPS // SCORE_ROWS):
                r0 = pl.multiple_of(c * GROUPS + half * SCORE_ROWS, SCORE_ROWS)
                kblk = ki_ref[0, pl.ds(r0, SCORE_ROWS), :]
                acc = jnp.zeros((SCORE_ROWS, tq), F32)
                for h in range(N_HEADS):
                    d = jnp.dot(kblk, qiT_ref[0, h * IDX_DIM:(h + 1) * IDX_DIM, :],
                                preferred_element_type=F32)
                    acc = acc + jnp.maximum(d, 0.0) * wiT_ref[0, h:h + 1, :]
                kpos = j * tk + r0 + lax.broadcasted_iota(I32, (SCORE_ROWS, 1), 0)
                acc = jnp.where(kpos < vis, acc, -jnp.inf)
                s_ref[pl.ds(pl.multiple_of(j * tk + r0, SCORE_ROWS), SCORE_ROWS), :] = acc
                grp = slice(half * SCORE_ROWS, (half + 1) * SCORE_ROWS)
                gm_ref[grp, :] = jnp.maximum(gm_ref[grp, :], acc)
            return carry

        @pl.when(j == 0)
        def _():
            gm_ref[...] = jnp.full(gm_ref.shape, -jnp.inf, F32)

        lax.fori_loop(0, rows_here(j, GROUPS), body, 0)

    @pl.when(j == nk - 1)
    def _():
        nch = vcap // COUNT_ROWS

        def count(pred):
            sub = 64

            def body(c, accs):
                accs = list(accs)
                for k in range(COUNT_ROWS // sub):
                    r0 = pl.multiple_of(c * COUNT_ROWS + k * sub, sub)
                    hit = jnp.where(pred(s_ref[pl.ds(r0, sub), :], r0), 1.0, 0.0)
                    accs[k % 4] = accs[k % 4] + jnp.sum(hit.reshape(sub // 8, 8, tq), axis=0)
                return tuple(accs)
            zero = jnp.zeros((8, tq), F32)
            a0, a1, a2, a3 = lax.fori_loop(0, nch, body, (zero, zero, zero, zero))
            return jnp.sum((a0 + a1) + (a2 + a3), axis=0, keepdims=True)

        def count_ge(t):
            return count(lambda blk, r0: blk >= t)

        kf = float(n_sel)
        no_cut = jnp.full((1, tq), 2 ** 30, I32)

        gm = gm_ref[...]
        few = vis < n_sel
        lo0 = _f32_to_ordered(jnp.min(gm, axis=0, keepdims=True))
        hi0 = _f32_to_ordered(jnp.max(gm, axis=0, keepdims=True)) + 1

        pos_norm = jnp.int32(0x00800000)
        neg_norm = -pos_norm - 1

        def finished(lo, hi, c_lo):
            point = jnp.logical_or(hi <= lo + 1,
                                   jnp.logical_or(jnp.logical_and(lo >= 0, hi <= pos_norm),
                                                  jnp.logical_and(lo >= neg_norm, hi <= 0)))
            return jnp.logical_or(few, jnp.logical_or(c_lo == kf, point))

        def more(state):
            _, lo, hi, c_lo = state
            return jnp.max(jnp.where(finished(lo, hi, c_lo), 0.0, 1.0)) > 0.0

        def step(state):
            k, lo, hi, c_lo = state
            mid_f = _f32_to_ordered(0.5 * _ordered_to_f32(lo) + 0.5 * _ordered_to_f32(hi - 1))
            mid_i = (lo >> 1) + (hi >> 1) + (lo & hi & 1)
            wide = ((lo ^ (hi - 1)) >> 23) != 0
            piv = jnp.where(jnp.logical_and(wide, k < FLOAT_STEPS), mid_f, mid_i)
            piv = jnp.where(jnp.logical_and(jnp.logical_and(lo >= 0, lo < pos_norm), hi > pos_norm),
                            pos_norm, piv)
            piv = jnp.where(jnp.logical_and(jnp.logical_and(lo < neg_norm, hi > neg_norm), hi <= 0),
                            neg_norm, piv)
            piv = jnp.where(jnp.logical_and(lo < 0, hi > 0), 0, piv)
            piv = jnp.clip(piv, lo + 1, hi - 1)
            c = count_ge(_ordered_to_f32(piv))
            live = jnp.logical_not(finished(lo, hi, c_lo))
            up = jnp.logical_and(live, c >= kf)
            down = jnp.logical_and(live, c < kf)
            return (k + 1, jnp.where(up, piv, lo), jnp.where(down, piv, hi),
                    jnp.where(up, c, c_lo))

        state = (jnp.int32(0), lo0, hi0, jnp.full((1, tq), -1.0, F32))
        _, cur, _, c_cur = lax.while_loop(more, step, state)
        thr = jnp.where(few, -F32_MAX, _ordered_to_f32(cur))

        def with_ties():
            n_gt = count(lambda blk, r0: blk > thr)
            n_ge = count_ge(thr)
            need = kf - n_gt
            tied = (n_ge - n_gt) > need

            def tie_cut():
                rr = lax.broadcasted_iota(I32, (KEY_CHUNK, KEY_CHUNK), 0)
                cc = lax.broadcasted_iota(I32, (KEY_CHUNK, KEY_CHUNK), 1)
                lower = jnp.where(cc <= rr, 1.0, 0.0).astype(BF16)

                def body(c, carry):
                    run, last = carry
                    for u in range(KEY_QUAD // KEY_CHUNK):
                        r0 = pl.multiple_of(c * KEY_QUAD + u * KEY_CHUNK, KEY_CHUNK)
                        eq = s_ref[pl.ds(r0, KEY_CHUNK), :] == thr
                        hit = jnp.where(eq, 1.0, 0.0)
                        ordinal = run + jnp.dot(lower, hit.astype(BF16),
                                                preferred_element_type=F32)
                        kpos = (r0 + lax.broadcasted_iota(I32, (KEY_CHUNK, 1), 0)).astype(F32)
                        take = jnp.logical_and(eq, ordinal <= need)
                        last = jnp.maximum(last, jnp.max(jnp.where(take, kpos, -1.0), axis=0,
                                                         keepdims=True))
                        run = run + jnp.sum(hit, axis=0, keepdims=True)
                    return run, last

                init = (jnp.zeros((1, tq), F32), jnp.full((1, tq), -1.0, F32))
                _, last = lax.fori_loop(0, vcap // KEY_QUAD, body, init)
                return jnp.where(tied, last.astype(I32), no_cut)

            return lax.cond(jnp.max(jnp.where(tied, 1.0, 0.0)) > 0.0, tie_cut, lambda: no_cut)

        unsure = jnp.logical_or(c_cur != kf, few)
        cut = lax.cond(jnp.max(jnp.where(unsure, 1.0, 0.0)) > 0.0, with_ties, lambda: no_cut)
        thr_ref[0:1, :] = thr
        cut_ref[0:1, :] = cut
        for h in range(N_HEADS):
            qp_ref[h] = _pair_padded(qaT_ref, h)
        _softmax_init(m_ref, l_ref, acc_ref)

    jj = j - nk

    @pl.when(jnp.logical_and(j >= nk, jj < nvis))
    def _():
        thr = thr_ref[0:1, :]
        cut = cut_ref[0:1, :]

        def selected(p):
            g0 = pl.multiple_of(jj * tk + p * KEY_PAIR, KEY_PAIR)
            sblk = s_ref[pl.ds(g0, KEY_PAIR), :]
            kpos = g0 + lax.broadcasted_iota(I32, (KEY_PAIR, 1), 0)
            return jnp.logical_or(sblk > thr, jnp.logical_and(sblk == thr, kpos <= cut))

        def logits(h, p):
            r0 = pl.multiple_of(p * KEY_PAIR, KEY_PAIR)
            kblk = ka_ref[0, pl.ds(r0, KEY_PAIR), (h // 2) * LANES:(h // 2 + 1) * LANES]
            return jnp.dot(kblk, qp_ref[h], preferred_element_type=F32)

        def v_t(h, p):
            rows = slice(h * HEAD_DIM, (h + 1) * HEAD_DIM)
            return jnp.concatenate([vaT_ref[0, 2 * p, rows, :], vaT_ref[0, 2 * p + 1, rows, :]],
                                   axis=1)

        _attend_block(0, rows_here(jj, KEY_QUAD), logits, selected, v_t, m_ref, l_ref, acc_ref,
                      lg0_ref, lg1_ref)

    @pl.when(j == 2 * nk - 1)
    def _():
        _softmax_finish(o_ref, l_ref, acc_ref)


def _dsa_attention(qiT, wiT, qaT, ki, ka, vaTc, *, past, length, tq, tk):
    b, _, tqp = qiT.shape
    lp = ki.shape[1]
    nq, nk = tqp // tq, lp // tk
    n_sel = min(TOPK_MAX, length // 4)

    def nvis(i):
        vmax = jnp.minimum(((past + (i + 1) * tq - 1) // CHUNK + 1) * CHUNK, length)
        return (vmax + tk - 1) // tk

    k1 = lambda bb, i, j: (bb, jnp.minimum(j, nvis(i) - 1), 0)
    k3 = lambda bb, i, j: (bb, jnp.clip(j - nk, 0, nvis(i) - 1), 0)
    k3c = lambda bb, i, j: (bb, jnp.clip(j - nk, 0, nvis(i) - 1), 0, 0)
    qmap = lambda bb, i, j: (bb, 0, i)
    kern = functools.partial(_dsa_kernel, past=past, length=length, n_sel=n_sel, tq=tq, tk=tk,
                             nk=nk)
    return pl.pallas_call(
        kern,
        out_shape=jax.ShapeDtypeStruct((b, tqp, D_BRANCH), BF16),
        grid=(b, nq, 2 * nk),
        in_specs=[
            pl.BlockSpec((1, D_BRANCH, tq), qmap),
            pl.BlockSpec((1, N_HEADS, tq), qmap),
            pl.BlockSpec((1, D_BRANCH, tq), qmap),
            pl.BlockSpec((1, tk, IDX_DIM), k1),
            pl.BlockSpec((1, tk, D_BRANCH), k3),
            pl.BlockSpec((1, tk // KEY_CHUNK, D_BRANCH, KEY_CHUNK), k3c),
        ],
        out_specs=pl.BlockSpec((1, tq, D_BRANCH), lambda bb, i, j: (bb, i, 0)),
        scratch_shapes=[
            pltpu.VMEM((lp, tq), F32),
            pltpu.VMEM((GROUPS, tq), F32),
            pltpu.VMEM((8, tq), F32),
            pltpu.VMEM((8, tq), I32),
            pltpu.VMEM((N_HEADS, LANES, tq), BF16),
            pltpu.VMEM((N_HEADS, tq), F32),
            pltpu.VMEM((N_HEADS, tq), F32),
            pltpu.VMEM((D_BRANCH, tq), F32),
            pltpu.VMEM((N_HEADS, KEY_PAIR, tq), F32),
            pltpu.VMEM((N_HEADS, KEY_PAIR, tq), F32),
        ],
        compiler_params=_cparams(("arbitrary", "arbitrary", "arbitrary")),
        name="dsa_attention",
    )(qiT, wiT, qaT, ki, ka, vaTc)


_AUG = 6


def _fox_kernel(qbT_ref, qaug_ref, kb_ref, kaug_ref, vbT_ref, o_ref,
                qf_ref, m_ref, l_ref, acc_ref, lg0_ref, lg1_ref, *, past, tq, tk, nk):
    i = pl.program_id(1)
    j = pl.program_id(2)
    q0 = past + i * tq
    qpos = q0 + lax.broadcasted_iota(I32, (1, tq), 1)
    nvis = (q0 + tq + tk - 1) // tk
    nquad = tk // KEY_QUAD
    n_free = jnp.clip((q0 + 1 - j * tk) // KEY_QUAD, 0, nquad)
    n_used = jnp.clip((q0 + tq - j * tk + KEY_QUAD - 1) // KEY_QUAD, 0, nquad)

    @pl.when(j == 0)
    def _():
        arow = lax.broadcasted_iota(I32, (LANES, 1), 0)
        qa = qaug_ref[0]
        for h in range(N_HEADS):
            mine = jnp.logical_and(arow >= _AUG * h, arow < _AUG * (h + 1))
            qf_ref[h, 0:LANES, :] = _pair_padded(qbT_ref, h)
            qf_ref[h, LANES:2 * LANES, :] = jnp.where(mine, qa, 0.0).astype(BF16)
        _softmax_init(m_ref, l_ref, acc_ref)

    @pl.when(j < nvis)
    def _():
        def causal(p):
            kpos = j * tk + p * KEY_PAIR + lax.broadcasted_iota(I32, (KEY_PAIR, 1), 0)
            return kpos <= qpos

        def logits(h, p):
            r0 = pl.multiple_of(p * KEY_PAIR, KEY_PAIR)
            kblk = kb_ref[0, pl.ds(r0, KEY_PAIR), (h // 2) * LANES:(h // 2 + 1) * LANES]
            lhs = jnp.concatenate([kblk, kaug_ref[0, pl.ds(r0, KEY_PAIR), :]], axis=1)
            return jnp.dot(lhs, qf_ref[h], preferred_element_type=F32)

        def v_t(h, p):
            rows = slice(h * HEAD_DIM, (h + 1) * HEAD_DIM)
            return jnp.concatenate([vbT_ref[0, 2 * p, rows, :], vbT_ref[0, 2 * p + 1, rows, :]],
                                   axis=1)

        _attend_block(n_free, n_used, logits, causal, v_t, m_ref, l_ref, acc_ref,
                      lg0_ref, lg1_ref)

    @pl.when(j == nk - 1)
    def _():
        _softmax_finish(o_ref, l_ref, acc_ref)


def _fox_attention(qbT, qaug, kb, kaug, vbTc, *, past, tq, tk):
    b, _, tqp = qbT.shape
    lp = kb.shape[1]
    nq, nk = tqp // tq, lp // tk

    def last(i):
        return jnp.minimum((past + (i + 1) * tq + tk - 1) // tk, nk) - 1

    kmap = lambda bb, i, j: (bb, jnp.minimum(j, last(i)), 0)
    kmapc = lambda bb, i, j: (bb, jnp.minimum(j, last(i)), 0, 0)
    qmap = lambda bb, i, j: (bb, 0, i)
    return pl.pallas_call(
        functools.partial(_fox_kernel, past=past, tq=tq, tk=tk, nk=nk),
        out_shape=jax.ShapeDtypeStruct((b, tqp, D_BRANCH), BF16),
        grid=(b, nq, nk),
        in_specs=[
            pl.BlockSpec((1, D_BRANCH, tq), qmap),
            pl.BlockSpec((1, LANES, tq), qmap),
            pl.BlockSpec((1, tk, D_BRANCH), kmap),
            pl.BlockSpec((1, tk, LANES), kmap),
            pl.BlockSpec((1, tk // KEY_CHUNK, D_BRANCH, KEY_CHUNK), kmapc),
        ],
        out_specs=pl.BlockSpec((1, tq, D_BRANCH), lambda bb, i, j: (bb, i, 0)),
        scratch_shapes=[
            pltpu.VMEM((N_HEADS, 2 * LANES, tq), BF16),
            pltpu.VMEM((N_HEADS, tq), F32),
            pltpu.VMEM((N_HEADS, tq), F32),
            pltpu.VMEM((D_BRANCH, tq), F32),
            pltpu.VMEM((N_HEADS, KEY_PAIR, tq), F32),
            pltpu.VMEM((N_HEADS, KEY_PAIR, tq), F32),
        ],
        compiler_params=_cparams(("arbitrary", "arbitrary", "arbitrary")),
        name="fox_attention",
    )(qbT, qaug, kb, kaug, vbTc)


AUG_ROWS = 1024


def _aug_kernel(c_ref, kaug_ref, qaugT_ref):
    x = c_ref[0] * LOG2E
    a = x.astype(BF16).astype(F32)
    r = x - a
    b = r.astype(BF16).astype(F32)
    c = (r - b).astype(BF16).astype(F32)
    lane = lax.broadcasted_iota(I32, (N_HEADS, LANES), 1)
    head = lax.broadcasted_iota(I32, (N_HEADS, LANES), 0)

    def place(v, slot):
        sel = jnp.where(lane == _AUG * head + slot, 1.0, 0.0)
        return jnp.dot(v, sel, preferred_element_type=F32)

    slot = lane - _AUG * head
    ones_k = jnp.sum(jnp.where(jnp.logical_and(slot >= 0, slot < 3), 1.0, 0.0), axis=0, keepdims=True)
    ones_q = jnp.sum(jnp.where(jnp.logical_and(slot >= 3, slot < 6), 1.0, 0.0), axis=0, keepdims=True)
    kaug_ref[0] = (ones_k - (place(a, 3) + place(b, 4) + place(c, 5))).astype(BF16)
    qaugT_ref[0] = (ones_q + (place(a, 0) + place(b, 1) + place(c, 2))).T


def _fox_augment(cum):
    b, lp, h = cum.shape
    return pl.pallas_call(
        _aug_kernel,
        out_shape=(jax.ShapeDtypeStruct((b, lp, LANES), BF16),
                   jax.ShapeDtypeStruct((b, LANES, lp), F32)),
        grid=(b, lp // AUG_ROWS),
        in_specs=[pl.BlockSpec((1, AUG_ROWS, h), lambda i, j: (i, j, 0))],
        out_specs=(pl.BlockSpec((1, AUG_ROWS, LANES), lambda i, j: (i, j, 0)),
                   pl.BlockSpec((1, LANES, AUG_ROWS), lambda i, j: (i, 0, j))),
        compiler_params=_cparams(("arbitrary", "arbitrary")),
        name="fox_augment",
    )(cum)


def _shifted_rows(load, store, u, past, starts_batch, nb):
    tm, c = u.shape
    store(slice(8, 8 + tm), u)
    if nb == 1:
        @pl.when(starts_batch)
        def _():
            store(slice(6, 8), past(0, slice(None)))
    s1 = load(slice(7, 7 + tm))
    s2 = load(slice(6, 6 + tm))
    if nb > 1:
        r = lax.broadcasted_iota(I32, (nb, tm // nb, 1), 1)
        p0 = past(slice(None), slice(0, 1))
        p1 = past(slice(None), slice(1, 2))
        s1 = jnp.where(r == 0, p1, _group(s1, nb)).reshape(tm, c)
        s2 = jnp.where(r == 0, p0, jnp.where(r == 1, p1, _group(s2, nb))).reshape(tm, c)
    store(slice(0, 8), load(slice(tm, tm + 8)))
    return s1, s2


def _merge_kernel(x_ref, ya_ref, yb_ref, cb_ref, u_ref, gl_ref, past_ref, cw_ref, wbr_ref,
                  wo_ref, g_ref, mod_ref, o_ref, ub_ref, *, nb, rows_per_batch):
    i = pl.program_id(0)
    tm = x_ref.shape[0]
    u = u_ref[...]

    def load(rows):
        return ub_ref[rows, :]

    def store(rows, v):
        ub_ref[rows, :] = v

    s1, s2 = _shifted_rows(load, store, u, lambda b, r: past_ref[b, r, :],
                           (i * tm) % rows_per_batch == 0, nb)
    conv = cw_ref[0:1, :] * s2 + cw_ref[1:2, :] * s1 + cw_ref[2:3, :] * u
    yc = (cb_ref[...] * conv).astype(BF16)
    mix = jnp.zeros((tm, D_MODEL), F32)
    for n, y in enumerate((ya_ref[...], yb_ref[...], yc)):
        br = jnp.dot(y, wbr_ref[n], preferred_element_type=F32)
        mix = mix + jax.nn.sigmoid(gl_ref[:, n * D_MODEL:(n + 1) * D_MODEL]) * br
    mo = jnp.dot(mix.astype(BF16), wo_ref[...], preferred_element_type=F32)
    nm = _group(_rms(mo, g_ref[1:2, :]), nb)
    o_ref[...] = x_ref[...] + (mod_ref[:, 2:3, :] * nm).reshape(tm, D_MODEL)


def _mod_spec(tm, nb, rows_per_batch, ngrid):
    if nb == 1:
        f = lambda i, *_: ((i * tm) // rows_per_batch, 0, 0)
    else:
        f = lambda i, *_: (i, 0, 0)
    return pl.BlockSpec((nb, 6, D_MODEL), f)


def _merge(x, ya, yb, cb, u, gl, past, cw, wbr, wo, g, mod, *, tm, rows_per_batch):
    rows = x.shape[0]
    nb = max(1, tm // rows_per_batch)
    row = lambda w: pl.BlockSpec((tm, w), lambda i: (i, 0))
    full = lambda shape: pl.BlockSpec(shape, lambda i: (0,) * len(shape))
    if nb == 1:
        past_spec = pl.BlockSpec((1, 2, D_BRANCH), lambda i: ((i * tm) // rows_per_batch, 0, 0))
    else:
        past_spec = pl.BlockSpec((nb, 2, D_BRANCH), lambda i: (i, 0, 0))
    return pl.pallas_call(
        functools.partial(_merge_kernel, nb=nb, rows_per_batch=rows_per_batch),
        out_shape=jax.ShapeDtypeStruct((rows, D_MODEL), F32),
        grid=(rows // tm,),
        in_specs=[
            row(D_MODEL), row(D_BRANCH), row(D_BRANCH), row(D_BRANCH), row(D_BRANCH),
            row(3 * D_MODEL), past_spec, full((CONV_W, D_BRANCH)),
            full((3, D_BRANCH, D_MODEL)), full((D_MODEL, D_MODEL)), full((4, D_MODEL)),
            _mod_spec(tm, nb, rows_per_batch, 1),
        ],
        out_specs=row(D_MODEL),
        scratch_shapes=[pltpu.VMEM((tm + 8, D_BRANCH), F32)],
        compiler_params=_cparams(("arbitrary",)),
        name="branch_merge",
    )(x, ya, yb, cb, u, gl, past, cw, wbr, wo, g, mod)


FF_CHUNK = 1408
FF_PARTS = ((0, 5 * LANES), (5 * LANES, FF_CHUNK))


def _ffn_kernel(x_ref, g_ref, mod_ref, wg_ref, wv_ref, wd_ref, cw_ref, past_ref,
                o_ref, tail_ref, hs_ref, acc_ref, ub_ref, *, nb, rows_per_batch, nj):
    i = pl.program_id(0)
    j = pl.program_id(1)
    tm = x_ref.shape[0]

    @pl.when(j == 0)
    def _():
        y = _group(_rms(x_ref[...], g_ref[2:3, :]), nb)
        h = y * (1.0 + mod_ref[:, 4:5, :]) + mod_ref[:, 3:4, :]
        hs_ref[...] = h.reshape(tm, D_MODEL).astype(BF16)
        acc_ref[...] = jnp.zeros(acc_ref.shape, F32)

    hs = hs_ref[...]
    grp = tm // nb
    for a, b in FF_PARTS:
        cols = slice(a, b)
        ug = jnp.dot(hs, wg_ref[:, cols], preferred_element_type=F32)
        uv = jnp.dot(hs, wv_ref[:, cols], preferred_element_type=F32)

        def load(rows, cols=cols):
            return ub_ref[j, rows, cols]

        def store(rows, v, cols=cols):
            ub_ref[j, rows, cols] = v

        s1, s2 = _shifted_rows(load, store, ug, lambda bb, r, cols=cols: past_ref[bb, r, cols],
                               (i * tm) % rows_per_batch == 0, nb)
        conv = cw_ref[0:1, cols] * s2 + cw_ref[1:2, cols] * s1 + cw_ref[2:3, cols] * ug
        f = conv * jax.nn.sigmoid(conv) * uv
        acc_ref[...] += jnp.dot(f.astype(BF16), wd_ref[cols, :], preferred_element_type=F32)
        tail_ref[0, :, :, cols] = _group(ug, nb)[:, grp - 8:, :]

    @pl.when(j == nj - 1)
    def _():
        nm = _group(_rms(acc_ref[...], g_ref[3:4, :]), nb)
        o_ref[...] = x_ref[...] + (mod_ref[:, 5:6, :] * nm).reshape(tm, D_MODEL)


def _conv_ffn(x, g, mod, w_up, w_down, cw, past, *, tm, rows_per_batch):
    rows = x.shape[0]
    dff = w_down.shape[0]
    nj = dff // FF_CHUNK
    nb = max(1, tm // rows_per_batch)
    nt = rows // tm
    if nb == 1:
        past_spec = pl.BlockSpec((1, 2, FF_CHUNK), lambda i, j: ((i * tm) // rows_per_batch, 0, j))
        mod_map = lambda i, j: ((i * tm) // rows_per_batch, 0, 0)
    else:
        past_spec = pl.BlockSpec((nb, 2, FF_CHUNK), lambda i, j: (i, 0, j))
        mod_map = lambda i, j: (i, 0, 0)
    out, tails = pl.pallas_call(
        functools.partial(_ffn_kernel, nb=nb, rows_per_batch=rows_per_batch, nj=nj),
        out_shape=(jax.ShapeDtypeStruct((rows, D_MODEL), F32),
                   jax.ShapeDtypeStruct((nt, nb, 8, dff), F32)),
        grid=(nt, nj),
        in_specs=[
            pl.BlockSpec((tm, D_MODEL), lambda i, j: (i, 0)),
            pl.BlockSpec((4, D_MODEL), lambda i, j: (0, 0)),
            pl.BlockSpec((nb, 6, D_MODEL), mod_map),
            pl.BlockSpec((D_MODEL, FF_CHUNK), lambda i, j: (0, j)),
            pl.BlockSpec((D_MODEL, FF_CHUNK), lambda i, j: (0, j + nj)),
            pl.BlockSpec((FF_CHUNK, D_MODEL), lambda i, j: (j, 0)),
            pl.BlockSpec((CONV_W, FF_CHUNK), lambda i, j: (0, j)),
            past_spec,
        ],
        out_specs=(pl.BlockSpec((tm, D_MODEL), lambda i, j: (i, 0)),
                   pl.BlockSpec((1, nb, 8, FF_CHUNK), lambda i, j: (i, 0, 0, j))),
        scratch_shapes=[
            pltpu.VMEM((tm, D_MODEL), BF16),
            pltpu.VMEM((tm, D_MODEL), F32),
            pltpu.VMEM((nj, tm + 8, FF_CHUNK), F32),
        ],
        compiler_params=_cparams(("arbitrary", "arbitrary")),
        name="conv_ffn",
    )(x, g, mod, w_up, w_up, w_down, cw, past)
    return out, tails


def _relayout_w_in(w_in, b_forget):
    db = D_BRANCH
    o_qi = 3 * db
    o_ki = o_qi + N_HEADS * IDX_DIM
    o_wi = o_ki + IDX_DIM
    o_qb = o_wi + N_HEADS
    o_fl = o_qb + 3 * db
    o_cb = o_fl + N_HEADS
    o_gl = o_cb + 3 * db
    main = jnp.concatenate([w_in[:, 0:o_ki], w_in[:, o_qb:o_fl], w_in[:, o_cb:]], axis=1)
    small = jnp.concatenate([w_in[:, o_ki:o_wi], w_in[:, o_wi:o_qb], w_in[:, o_fl:o_cb]], axis=1)
    small = jnp.pad(small, ((0, 0), (0, LANES - small.shape[1])))
    bf = jnp.zeros((1, LANES), F32).at[0, _FL_OFF:_FL_OFF + N_HEADS].set(b_forget)
    del o_gl
    return main.astype(BF16), small.astype(BF16), bf


def _round_up(x, m):
    return (x + m - 1) // m * m


def _layer(x, mod, caches, weights, *, batch, t, past, cfg):
    (g, w_main, w_small, bf_pad, cw_mix, wbr, wo, w_up, cw_ffn, w_down) = weights
    rows = batch * t
    length = past + t
    (qaT, ka32, ka16, va32, vaTc, qiT, qbT, kb32, kb16, vb32, vbTc, cb, u, gl,
     ki32, ki16, wiT, logf) = _in_projection(
        x, mod, g, w_main, w_small, bf_pad, tm=cfg["tm_in"], rows_per_batch=t)

    tq, tk = cfg["tq"], cfg["tk"]
    tqp = _round_up(t, tq)
    lp = _round_up(length, tk)

    def per_batch_T(aT):
        c = aT.shape[0]
        a = jnp.swapaxes(aT.reshape(c, batch, t), 0, 1)
        return jnp.pad(a, ((0, 0), (0, 0), (0, tqp - t)))

    def chunked_T(a):
        return jnp.swapaxes(a.reshape(batch, lp // KEY_CHUNK, KEY_CHUNK, a.shape[-1]), 2, 3)

    def with_cache(cache, new, dtype):
        new = new.reshape(batch, t, -1)
        if cache is not None:
            new = jnp.concatenate([cache.reshape(batch, past, -1).astype(dtype), new.astype(dtype)],
                                  axis=1)
        return jnp.pad(new.astype(dtype), ((0, 0), (0, lp - length), (0, 0)))

    if caches is None:
        c_idx = c_dk = c_dv = c_fk = c_fv = c_lf = None
        past_mix = jnp.zeros((batch, CONV_W - 1, D_BRANCH), F32)
        past_ffn = jnp.zeros((batch, CONV_W - 1, w_down.shape[0]), F32)
    else:
        c_idx, c_dk, c_dv, c_fk, c_fv, c_lf, past_mix, past_ffn = caches

    ki_all = with_cache(c_idx, ki16, BF16)
    ka_all = with_cache(c_dk, ka16, BF16)
    kb_all = with_cache(c_fk, kb16, BF16)
    if caches is None and lp == rows:
        vaT_all = vaTc.reshape(batch, lp // KEY_CHUNK, D_BRANCH, KEY_CHUNK)
        vbT_all = vbTc.reshape(batch, lp // KEY_CHUNK, D_BRANCH, KEY_CHUNK)
    else:
        vaT_all = chunked_T(with_cache(c_dv, va32, BF16))
        vbT_all = chunked_T(with_cache(c_fv, vb32, BF16))
    logf_all = with_cache(c_lf, logf, F32)

    ya = _dsa_attention(per_batch_T(qiT), per_batch_T(wiT), per_batch_T(qaT), ki_all, ka_all,
                        vaT_all, past=past, length=length, tq=tq, tk=tk)
    cum = _forget_cumsum(logf_all)
    kaug, qaug = _fox_augment(cum)
    qaug = jnp.pad(qaug[:, :, past:length], ((0, 0), (0, 0), (0, tqp - t)))
    yb = _fox_attention(per_batch_T(qbT), qaug, kb_all, kaug, vbT_all, past=past, tq=tq, tk=tk)
    ya = ya[:, :t].reshape(rows, D_BRANCH)
    yb = yb[:, :t].reshape(rows, D_BRANCH)

    x1 = _merge(x, ya, yb, cb, u, gl, past_mix, cw_mix, wbr, wo, g, mod,
                tm=cfg["tm_merge"], rows_per_batch=t)
    x2, tails = _conv_ffn(x1, g, mod, w_up, w_down, cw_ffn, past_ffn,
                          tm=cfg["tm_ffn"], rows_per_batch=t)

    new_mix = u.reshape(batch, t, D_BRANCH)[:, t - (CONV_W - 1):]
    nb = max(1, cfg["tm_ffn"] // t)
    if nb == 1:
        tiles_per_batch = t // cfg["tm_ffn"]
        last = tails.reshape(batch, tiles_per_batch, 8, -1)[:, -1]
    else:
        last = tails.reshape(batch, 8, -1)
    new_ffn = last[:, 8 - (CONV_W - 1):]
    hd = (batch, t, N_HEADS, HEAD_DIM)
    state = (ki32.reshape(batch, t, IDX_DIM), ka32.reshape(hd), va32.reshape(hd),
             kb32.reshape(hd), vb32.reshape(hd), logf.reshape(batch, t, N_HEADS), new_mix, new_ffn)
    return x2, state


_PROMPT_CFG = dict(tm_in=512, tm_merge=256, tm_ffn=512, tq=256, tk=4096)
_SAMPLE_CFG = dict(tm_in=256, tm_merge=256, tm_ffn=256, tq=128, tk=3072)


def kernel(x_prompt, x_sample, c_prompt, c_sample, cache_idx_k, cache_dsa_k, cache_dsa_v,
           cache_fox_k, cache_fox_v, cache_fox_logf, state_conv_mix, state_conv_ffn,
           w_ada, b_ada, norm_g, w_in, b_forget, conv_mix_w, w_branch, w_out, w_up,
           conv_ffn_w, w_down):
    bp, tp, _ = x_prompt.shape
    bs, ts, _ = x_sample.shape
    past = cache_dsa_k.shape[2]
    depth = w_ada.shape[0]

    c_all = jnp.concatenate([c_prompt, c_sample], axis=0)
    pad_rows = _round_up(c_all.shape[0], 8) - c_all.shape[0]
    c_all = jnp.pad(c_all, ((0, pad_rows), (0, 0)))
    mod_all = _modulation(c_all, w_ada, b_ada).reshape(depth, -1, 6, D_MODEL)

    yp = x_prompt.reshape(bp * tp, D_MODEL)
    ys = x_sample.reshape(bs * ts, D_MODEL)
    p_states, s_states = [], []
    for l in range(depth):
        w_main, w_small, bf_pad = _relayout_w_in(w_in[l], b_forget[l])
        weights = (norm_g[l], w_main, w_small, bf_pad, conv_mix_w[l], w_branch[l].astype(BF16),
                   w_out[l].astype(BF16), w_up[l].astype(BF16), conv_ffn_w[l],
                   w_down[l].astype(BF16))
        yp, st_p = _layer(yp, mod_all[l, :bp], None, weights, batch=bp, t=tp, past=0,
                          cfg=_PROMPT_CFG)
        caches = (cache_idx_k[l], cache_dsa_k[l], cache_dsa_v[l], cache_fox_k[l], cache_fox_v[l],
                  cache_fox_logf[l], state_conv_mix[l], state_conv_ffn[l])
        ys, st_s = _layer(ys, mod_all[l, bp:bp + bs], caches, weights, batch=bs, t=ts, past=past,
                          cfg=_SAMPLE_CFG)
        p_states.append(st_p)
        s_states.append(st_s)

    stack = lambda states: [jnp.stack([st[k] for st in states], axis=0) for k in range(8)]
    return (yp.reshape(bp, tp, D_MODEL), ys.reshape(bs, ts, D_MODEL), *stack(p_states),
            *stack(s_states))
```

```python
import functools

import jax
import jax.numpy as jnp
from jax import lax
from jax.experimental import pallas as pl
from jax.experimental.pallas import tpu as pltpu

F32 = jnp.float32
BF16 = jnp.bfloat16
I32 = jnp.int32

D_MODEL = 1024
HEAD_DIM = 64
D_BRANCH = 512
N_HEADS = 8
IDX_DIM = 64
CHUNK = 64
CHUNK_SHIFT = 6
TOPK_MAX = 256
CONV_W = 3
EPS = 1e-6
NEG_INF = -1e30
LOG2E = 1.4426950408889634
F32_MAX = 3.4028234663852886e38

LANES = 128
N_MAIN = 16 * D_BRANCH
KEY_CHUNK = 256
KEY_PAIR = 2 * KEY_CHUNK
KEY_QUAD = 2 * KEY_PAIR
SCORE_ROWS = 128
COUNT_ROWS = KEY_QUAD
GROUPS = TOPK_MAX
FLOAT_STEPS = 12
VMEM_LIMIT = 56 * 1024 * 1024


def _cparams(sem):
    return pltpu.CompilerParams(dimension_semantics=sem, vmem_limit_bytes=VMEM_LIMIT)


def _rms(x, g_row):
    return x * lax.rsqrt(jnp.mean(x * x, axis=-1, keepdims=True) + EPS) * g_row


def _group(x, nb):
    return x.reshape(nb, x.shape[0] // nb, x.shape[1])


def _mod_kernel(c_ref, w_ref, b_ref, o_ref):
    c = c_ref[...]
    s = (c * jax.nn.sigmoid(c)).astype(BF16)
    o_ref[0] = jnp.dot(s, w_ref[0].astype(BF16), preferred_element_type=F32) + b_ref[0]


def _modulation(c_all, w_ada, b_ada):
    depth = w_ada.shape[0]
    rows = c_all.shape[0]
    n = w_ada.shape[2]
    tn = D_MODEL
    return pl.pallas_call(
        _mod_kernel,
        out_shape=jax.ShapeDtypeStruct((depth, rows, n), F32),
        grid=(depth, n // tn),
        in_specs=[
            pl.BlockSpec((rows, D_MODEL), lambda l, j: (0, 0)),
            pl.BlockSpec((1, D_MODEL, tn), lambda l, j: (l, 0, j)),
            pl.BlockSpec((1, 1, tn), lambda l, j: (l, 0, j)),
        ],
        out_specs=pl.BlockSpec((1, rows, tn), lambda l, j: (l, 0, j)),
        compiler_params=_cparams(("arbitrary", "arbitrary")),
        name="adaln_mod",
    )(c_all, w_ada, b_ada.reshape(depth, 1, n))


IN_STEP = 2 * D_BRANCH
_GL_STEP = 5
_WI_OFF = IDX_DIM
_FL_OFF = IDX_DIM + N_HEADS


def _inproj_kernel(x_ref, mod_ref, g_ref, wm_ref, ws_ref, bf_ref,
                   qaT_ref, ka32_ref, ka16_ref, va32_ref, vaT_ref, qiT_ref, qbT_ref,
                   kb32_ref, kb16_ref, vb32_ref, vbT_ref, cb_ref, u_ref, gl_ref,
                   ki32_ref, ki16_ref, wiT_ref, logf_ref,
                   hs_ref, *, nb):
    j = pl.program_id(1)
    tm = x_ref.shape[0]

    @pl.when(j == 0)
    def _():
        y = _rms(x_ref[...], g_ref[0:1, :])
        h = _group(y, nb) * (1.0 + mod_ref[:, 1:2, :]) + mod_ref[:, 0:1, :]
        hb = h.reshape(tm, D_MODEL).astype(BF16)
        hs_ref[...] = hb
        sm = jnp.dot(hb, ws_ref[...], preferred_element_type=F32)
        ki32_ref[...] = sm[:, :IDX_DIM]
        ki16_ref[...] = sm[:, :IDX_DIM].astype(BF16)
        wiT_ref[...] = sm.T[_WI_OFF:_WI_OFF + N_HEADS, :]
        z = sm + bf_ref[...]
        lf = -(jnp.maximum(-z, 0.0) + jnp.log1p(jnp.exp(-jnp.abs(z))))
        logf_ref[...] = lf[:, _FL_OFF:_FL_OFF + N_HEADS]

    acc = jnp.dot(hs_ref[...], wm_ref[...], preferred_element_type=F32)
    lo = acc[:, :D_BRANCH]
    hi = acc[:, D_BRANCH:]
    scale = LOG2E * HEAD_DIM ** -0.5

    def store_chunked_T(ref, a):
        aT = a.T.astype(BF16)
        for c in range(tm // KEY_CHUNK):
            ref[c] = aT[:, c * KEY_CHUNK:(c + 1) * KEY_CHUNK]

    @pl.when(j == 0)
    def _():
        qaT_ref[...] = (lo * scale).T.astype(BF16)
        ka32_ref[...] = hi
        ka16_ref[...] = hi.astype(BF16)

    @pl.when(j == 1)
    def _():
        va32_ref[...] = lo
        store_chunked_T(vaT_ref, lo)
        qiT_ref[...] = hi.T.astype(BF16)

    @pl.when(j == 2)
    def _():
        qbT_ref[...] = (lo * scale).T.astype(BF16)
        kb32_ref[...] = hi
        kb16_ref[...] = hi.astype(BF16)

    @pl.when(j == 3)
    def _():
        vb32_ref[...] = lo
        store_chunked_T(vbT_ref, lo)
        cb_ref[...] = hi

    @pl.when(j == 4)
    def _():
        u_ref[...] = lo * hi

    @pl.when(j >= _GL_STEP)
    def _():
        gl_ref[...] = acc


def _in_projection(x, mod, g, w_main, w_small, bf_pad, *, tm, rows_per_batch):
    rows = x.shape[0]
    nt = rows // tm
    nb = max(1, tm // rows_per_batch)
    nstep = N_MAIN // IN_STEP
    if nb == 1:
        mod_map = lambda i, j: ((i * tm) // rows_per_batch, 0, 0)
    else:
        mod_map = lambda i, j: (i, 0, 0)
    row_blk = lambda w: pl.BlockSpec((tm, w), lambda i, j: (i, 0))
    col_blk = lambda h: pl.BlockSpec((h, tm), lambda i, j: (0, i))
    chunkT = pl.BlockSpec((tm // KEY_CHUNK, D_BRANCH, KEY_CHUNK), lambda i, j: (i, 0, 0))
    f32o = lambda w: jax.ShapeDtypeStruct((rows, w), F32)
    b16o = lambda w: jax.ShapeDtypeStruct((rows, w), BF16)
    b16T = jax.ShapeDtypeStruct((D_BRANCH, rows), BF16)
    b16c = jax.ShapeDtypeStruct((rows // KEY_CHUNK, D_BRANCH, KEY_CHUNK), BF16)
    out_shape = (
        b16T, f32o(D_BRANCH), b16o(D_BRANCH), f32o(D_BRANCH), b16c, b16T, b16T,
        f32o(D_BRANCH), b16o(D_BRANCH), f32o(D_BRANCH), b16c, f32o(D_BRANCH), f32o(D_BRANCH),
        f32o(6 * D_BRANCH),
        f32o(IDX_DIM), b16o(IDX_DIM), jax.ShapeDtypeStruct((N_HEADS, rows), F32), f32o(N_HEADS),
    )
    out_specs = (
        col_blk(D_BRANCH), row_blk(D_BRANCH), row_blk(D_BRANCH), row_blk(D_BRANCH), chunkT,
        col_blk(D_BRANCH), col_blk(D_BRANCH),
        row_blk(D_BRANCH), row_blk(D_BRANCH), row_blk(D_BRANCH), chunkT, row_blk(D_BRANCH),
        row_blk(D_BRANCH),
        pl.BlockSpec((tm, IN_STEP), lambda i, j: (i, jnp.clip(j - _GL_STEP, 0, 2))),
        row_blk(IDX_DIM), row_blk(IDX_DIM), col_blk(N_HEADS), row_blk(N_HEADS),
    )
    return pl.pallas_call(
        functools.partial(_inproj_kernel, nb=nb),
        out_shape=out_shape,
        grid=(nt, nstep),
        in_specs=[
            pl.BlockSpec((tm, D_MODEL), lambda i, j: (i, 0)),
            pl.BlockSpec((nb, 6, D_MODEL), mod_map),
            pl.BlockSpec((4, D_MODEL), lambda i, j: (0, 0)),
            pl.BlockSpec((D_MODEL, IN_STEP), lambda i, j: (0, j)),
            pl.BlockSpec((D_MODEL, LANES), lambda i, j: (0, 0)),
            pl.BlockSpec((1, LANES), lambda i, j: (0, 0)),
        ],
        out_specs=out_specs,
        scratch_shapes=[pltpu.VMEM((tm, D_MODEL), BF16)],
        compiler_params=_cparams(("arbitrary", "arbitrary")),
        name="in_projection",
    )(x, mod, g, w_main, w_small, bf_pad)


def _cumsum_kernel(x_ref, o_ref):
    x = x_ref[0]
    n = x.shape[0]
    lane = lax.broadcasted_iota(I32, x.shape, 1)
    row = lax.broadcasted_iota(I32, x.shape, 0)
    s = N_HEADS
    while s < LANES:
        x = x + jnp.where(lane >= s, pltpu.roll(x, s, axis=1), 0.0)
        s *= 2
    t = jnp.where(lane >= LANES - N_HEADS, x, 0.0)
    s = N_HEADS
    while s < LANES:
        t = t + pltpu.roll(t, s, axis=1)
        s *= 2
    t = jnp.where(row >= 1, pltpu.roll(t, 1, axis=0), 0.0)
    s = 1
    while s < n:
        if s < 8:
            sh = jnp.where(row >= s, pltpu.roll(t, s, axis=0), 0.0)
        else:
            sh = jnp.concatenate([jnp.zeros((s, LANES), F32), t[:n - s]], axis=0)
        t = t + sh
        s *= 2
    o_ref[0] = x + t


def _forget_cumsum(logf_all):
    b, lp, h = logf_all.shape
    n = lp * h // LANES
    out = pl.pallas_call(
        _cumsum_kernel,
        out_shape=jax.ShapeDtypeStruct((b, n, LANES), F32),
        grid=(b,),
        in_specs=[pl.BlockSpec((1, n, LANES), lambda i: (i, 0, 0))],
        out_specs=pl.BlockSpec((1, n, LANES), lambda i: (i, 0, 0)),
        compiler_params=_cparams(("arbitrary",)),
        name="forget_cumsum",
    )(logf_all.reshape(b, n, LANES))
    return out.reshape(b, lp, h)


def _pair_padded(qT_ref, h):
    pr, half = divmod(h, 2)
    blk = qT_ref[0, pr * LANES + half * HEAD_DIM:pr * LANES + (half + 1) * HEAD_DIM, :]
    z = jnp.zeros_like(blk)
    return jnp.concatenate([blk, z] if half == 0 else [z, blk], axis=0)


def _rows_to_heads(rows):
    tq = rows[0].shape[1]
    sub = lax.broadcasted_iota(I32, (N_HEADS, tq), 0)
    out = jnp.broadcast_to(rows[0], (N_HEADS, tq))
    for h in range(1, N_HEADS):
        out = jnp.where(sub == h, rows[h], out)
    return out


def _stage_and_consume(stage, consume, m_cur, pend, l, acc_ref):
    m_next, sums, alphas = [], [], []
    for h in range(N_HEADS):
        if stage is not None:
            logits, mask, buf = stage
            lg = logits(h)
            if mask is not None:
                lg = jnp.where(mask, lg, NEG_INF)
            buf[h] = lg
            m_next.append(jnp.maximum(m_cur[h:h + 1, :], jnp.max(lg, axis=0, keepdims=True)))
        if consume is not None:
            v_t, buf = consume
            before, after = pend
            alpha = jnp.exp2(before[h:h + 1, :] - after[h:h + 1, :])
            p = jnp.exp2(buf[h] - after[h:h + 1, :]).astype(BF16)
            ones = jnp.ones((16, p.shape[0]), BF16)
            pv = jnp.dot(jnp.concatenate([v_t(h), ones], axis=0), p, preferred_element_type=F32)
            rows = slice(h * HEAD_DIM, (h + 1) * HEAD_DIM)
            acc_ref[rows, :] = alpha * acc_ref[rows, :] + pv[:HEAD_DIM, :]
            sums.append(pv[HEAD_DIM:HEAD_DIM + 1, :])
            alphas.append(alpha)
    if consume is not None:
        l = _rows_to_heads(alphas) * l + _rows_to_heads(sums)
    return (_rows_to_heads(m_next) if stage is not None else m_cur), l


def _attend_block(n_free, n_used, logits, mask, v_t, m_ref, l_ref, acc_ref, buf0, buf1):
    def fused(p_new, masked, buf_new, buf_old, m_old, m_cur, l):
        return _stage_and_consume((lambda h: logits(h, p_new), mask(p_new) if masked else None,
                                   buf_new),
                                  (lambda h: v_t(h, p_new - 1), buf_old), m_cur, (m_old, m_cur), l,
                                  acc_ref)

    def trip(masked, t, carry):
        m_a, m_b, l = carry
        m_c, l = fused(2 * t + 1, masked, buf1, buf0, m_a, m_b, l)
        m_d, l = fused(2 * t + 2, masked, buf0, buf1, m_b, m_c, l)
        return m_c, m_d, l

    m0 = m_ref[...]
    m1, l = _stage_and_consume((lambda h: logits(h, 0), mask(0), buf0), None, m0, None,
                               l_ref[...], acc_ref)
    split = jnp.maximum(n_free - 1, 0)
    carry = lax.fori_loop(0, split, functools.partial(trip, False), (m0, m1, l))
    m_a, m_b, l = lax.fori_loop(split, n_used - 1, functools.partial(trip, True), carry)
    last = 2 * n_used - 1
    m_c, l = fused(last, True, buf1, buf0, m_a, m_b, l)
    _, l = _stage_and_consume(None, (lambda h: v_t(h, last), buf1), m_c, (m_b, m_c), l, acc_ref)
    m_ref[...] = m_c
    l_ref[...] = l


def _softmax_init(m_ref, l_ref, acc_ref):
    m_ref[...] = jnp.full(m_ref.shape, NEG_INF, F32)
    l_ref[...] = jnp.zeros(l_ref.shape, F32)
    acc_ref[...] = jnp.zeros(acc_ref.shape, F32)


def _softmax_finish(o_ref, l_ref, acc_ref):
    inv = 1.0 / l_ref[...]
    parts = [acc_ref[h * HEAD_DIM:(h + 1) * HEAD_DIM, :] * inv[h:h + 1, :] for h in range(N_HEADS)]
    o_ref[0] = jnp.concatenate(parts, axis=0).T.astype(o_ref.dtype)


def _f32_to_ordered(x):
    b = lax.bitcast_convert_type(x, I32)
    return jnp.where(b >= 0, b, b ^ jnp.int32(0x7FFFFFFF))


def _ordered_to_f32(o):
    return lax.bitcast_convert_type(jnp.where(o >= 0, o, o ^ jnp.int32(0x7FFFFFFF)), F32)


def _dsa_kernel(qiT_ref, wiT_ref, qaT_ref, ki_ref, ka_ref, vaT_ref, o_ref,
                s_ref, gm_ref, thr_ref, cut_ref, qp_ref, m_ref, l_ref, acc_ref, lg0_ref, lg1_ref,
                *, past, length, n_sel, tq, tk, nk):
    i = pl.program_id(1)
    j = pl.program_id(2)
    q0 = past + i * tq
    qpos = q0 + lax.broadcasted_iota(I32, (1, tq), 1)
    vis = jnp.minimum(((qpos >> CHUNK_SHIFT) + 1) * CHUNK, length)
    vmax = jnp.minimum((((q0 + tq - 1) >> CHUNK_SHIFT) + 1) * CHUNK, length)
    nvis = (vmax + tk - 1) // tk
    vcap = ((vmax + KEY_QUAD - 1) // KEY_QUAD) * KEY_QUAD

    def rows_here(jb, step):
        return jnp.clip((vcap - jb * tk) // step, 0, tk // step)

    @pl.when(jnp.logical_and(j < nk, j < nvis))
    def _():
        def body(c, carry):
            for half in range(GROUPS // SCORE_ROWS):
                r0 = pl.multiple_of(c * GROUPS + half * SCORE_ROWS, SCORE_ROWS)
                kblk = ki_ref[0, pl.ds(r0, SCORE_ROWS), :]
                acc = jnp.zeros((SCORE_ROWS, tq), F32)
                for h in range(N_HEADS):
                    d = jnp.dot(kblk, qiT_ref[0, h * IDX_DIM:(h + 1) * IDX_DIM, :],
                                preferred_element_type=F32)
                    acc = acc + jnp.maximum(d, 0.0) * wiT_ref[0, h:h + 1, :]
                kpos = j * tk + r0 + lax.broadcasted_iota(I32, (SCORE_ROWS, 1), 0)
                acc = jnp.where(kpos < vis, acc, -jnp.inf)
                s_ref[pl.ds(pl.multiple_of(j * tk + r0, SCORE_ROWS), SCORE_ROWS), :] = acc
                grp = slice(half * SCORE_ROWS, (half + 1) * SCORE_ROWS)
                gm_ref[grp, :] = jnp.maximum(gm_ref[grp, :], acc)
            return carry

        @pl.when(j == 0)
        def _():
            gm_ref[...] = jnp.full(gm_ref.shape, -jnp.inf, F32)

        lax.fori_loop(0, rows_here(j, GROUPS), body, 0)

    @pl.when(j == nk - 1)
    def _():
        nch = vcap // COUNT_ROWS

        def count(pred):
            sub = 64

            def body(c, accs):
                accs = list(accs)
                for k in range(COUNT_ROWS // sub):
                    r0 = pl.multiple_of(c * COUNT_ROWS + k * sub, sub)
                    hit = jnp.where(pred(s_ref[pl.ds(r0, sub), :], r0), 1.0, 0.0)
                    accs[k % 4] = accs[k % 4] + jnp.sum(hit.reshape(sub // 8, 8, tq), axis=0)
                return tuple(accs)
            zero = jnp.zeros((8, tq), F32)
            a0, a1, a2, a3 = lax.fori_loop(0, nch, body, (zero, zero, zero, zero))
            return jnp.sum((a0 + a1) + (a2 + a3), axis=0, keepdims=True)

        def count_ge(t):
            return count(lambda blk, r0: blk >= t)

        kf = float(n_sel)
        no_cut = jnp.full((1, tq), 2 ** 30, I32)

        gm = gm_ref[...]
        few = vis < n_sel
        lo0 = _f32_to_ordered(jnp.min(gm, axis=0, keepdims=True))
        hi0 = _f32_to_ordered(jnp.max(gm, axis=0, keepdims=True)) + 1

        pos_norm = jnp.int32(0x00800000)
        neg_norm = -pos_norm - 1

        def finished(lo, hi, c_lo):
            point = jnp.logical_or(hi <= lo + 1,
                                   jnp.logical_or(jnp.logical_and(lo >= 0, hi <= pos_norm),
                                                  jnp.logical_and(lo >= neg_norm, hi <= 0)))
            return jnp.logical_or(few, jnp.logical_or(c_lo == kf, point))

        def more(state):
            _, lo, hi, c_lo = state
            return jnp.max(jnp.where(finished(lo, hi, c_lo), 0.0, 1.0)) > 0.0

        def step(state):
            k, lo, hi, c_lo = state
            mid_f = _f32_to_ordered(0.5 * _ordered_to_f32(lo) + 0.5 * _ordered_to_f32(hi - 1))
            mid_i = (lo >> 1) + (hi >> 1) + (lo & hi & 1)
            wide = ((lo ^ (hi - 1)) >> 23) != 0
            piv = jnp.where(jnp.logical_and(wide, k < FLOAT_STEPS), mid_f, mid_i)
            piv = jnp.where(jnp.logical_and(jnp.logical_and(lo >= 0, lo < pos_norm), hi > pos_norm),
                            pos_norm, piv)
            piv = jnp.where(jnp.logical_and(jnp.logical_and(lo < neg_norm, hi > neg_norm), hi <= 0),
                            neg_norm, piv)
            piv = jnp.where(jnp.logical_and(lo < 0, hi > 0), 0, piv)
            piv = jnp.clip(piv, lo + 1, hi - 1)
            c = count_ge(_ordered_to_f32(piv))
            live = jnp.logical_not(finished(lo, hi, c_lo))
            up = jnp.logical_and(live, c >= kf)
            down = jnp.logical_and(live, c < kf)
            return (k + 1, jnp.where(up, piv, lo), jnp.where(down, piv, hi),
                    jnp.where(up, c, c_lo))

        state = (jnp.int32(0), lo0, hi0, jnp.full((1, tq), -1.0, F32))
        _, cur, _, c_cur = lax.while_loop(more, step, state)
        thr = jnp.where(few, -F32_MAX, _ordered_to_f32(cur))

        def with_ties():
            n_gt = count(lambda blk, r0: blk > thr)
            n_ge = count_ge(thr)
            need = kf - n_gt
            tied = (n_ge - n_gt) > need

            def tie_cut():
                rr = lax.broadcasted_iota(I32, (KEY_CHUNK, KEY_CHUNK), 0)
                cc = lax.broadcasted_iota(I32, (KEY_CHUNK, KEY_CHUNK), 1)
                lower = jnp.where(cc <= rr, 1.0, 0.0).astype(BF16)

                def body(c, carry):
                    run, last = carry
                    for u in range(KEY_QUAD // KEY_CHUNK):
                        r0 = pl.multiple_of(c * KEY_QUAD + u * KEY_CHUNK, KEY_CHUNK)
                        eq = s_ref[pl.ds(r0, KEY_CHUNK), :] == thr
                        hit = jnp.where(eq, 1.0, 0.0)
                        ordinal = run + jnp.dot(lower, hit.astype(BF16),
                                                preferred_element_type=F32)
                        kpos = (r0 + lax.broadcasted_iota(I32, (KEY_CHUNK, 1), 0)).astype(F32)
                        take = jnp.logical_and(eq, ordinal <= need)
                        last = jnp.maximum(last, jnp.max(jnp.where(take, kpos, -1.0), axis=0,
                                                         keepdims=True))
                        run = run + jnp.sum(hit, axis=0, keepdims=True)
                    return run, last

                init = (jnp.zeros((1, tq), F32), jnp.full((1, tq), -1.0, F32))
                _, last = lax.fori_loop(0, vcap // KEY_QUAD, body, init)
                return jnp.where(tied, last.astype(I32), no_cut)

            return lax.cond(jnp.max(jnp.where(tied, 1.0, 0.0)) > 0.0, tie_cut, lambda: no_cut)

        unsure = jnp.logical_or(c_cur != kf, few)
        cut = lax.cond(jnp.max(jnp.where(unsure, 1.0, 0.0)) > 0.0, with_ties, lambda: no_cut)
        thr_ref[0:1, :] = thr
        cut_ref[0:1, :] = cut
        for h in range(N_HEADS):
            qp_ref[h] = _pair_padded(qaT_ref, h)
        _softmax_init(m_ref, l_ref, acc_ref)

    jj = j - nk

    @pl.when(jnp.logical_and(j >= nk, jj < nvis))
    def _():
        thr = thr_ref[0:1, :]
        cut = cut_ref[0:1, :]

        def selected(p):
            g0 = pl.multiple_of(jj * tk + p * KEY_PAIR, KEY_PAIR)
            sblk = s_ref[pl.ds(g0, KEY_PAIR), :]
            kpos = g0 + lax.broadcasted_iota(I32, (KEY_PAIR, 1), 0)
            return jnp.logical_or(sblk > thr, jnp.logical_and(sblk == thr, kpos <= cut))

        def logits(h, p):
            r0 = pl.multiple_of(p * KEY_PAIR, KEY_PAIR)
            kblk = ka_ref[0, pl.ds(r0, KEY_PAIR), (h // 2) * LANES:(h // 2 + 1) * LANES]
            return jnp.dot(kblk, qp_ref[h], preferred_element_type=F32)

        def v_t(h, p):
            rows = slice(h * HEAD_DIM, (h + 1) * HEAD_DIM)
            return jnp.concatenate([vaT_ref[0, 2 * p, rows, :], vaT_ref[0, 2 * p + 1, rows, :]],
                                   axis=1)

        _attend_block(0, rows_here(jj, KEY_QUAD), logits, selected, v_t, m_ref, l_ref, acc_ref,
                      lg0_ref, lg1_ref)

    @pl.when(j == 2 * nk - 1)
    def _():
        _softmax_finish(o_ref, l_ref, acc_ref)


def _dsa_attention(qiT, wiT, qaT, ki, ka, vaTc, *, past, length, tq, tk):
    b, _, tqp = qiT.shape
    lp = ki.shape[1]
    nq, nk = tqp // tq, lp // tk
    n_sel = min(TOPK_MAX, length // 4)

    def nvis(i):
        vmax = jnp.minimum(((past + (i + 1) * tq - 1) // CHUNK + 1) * CHUNK, length)
        return (vmax + tk - 1) // tk

    k1 = lambda bb, i, j: (bb, jnp.minimum(j, nvis(i) - 1), 0)
    k3 = lambda bb, i, j: (bb, jnp.clip(j - nk, 0, nvis(i) - 1), 0)
    k3c = lambda bb, i, j: (bb, jnp.clip(j - nk, 0, nvis(i) - 1), 0, 0)
    qmap = lambda bb, i, j: (bb, 0, i)
    kern = functools.partial(_dsa_kernel, past=past, length=length, n_sel=n_sel, tq=tq, tk=tk,
                             nk=nk)
    return pl.pallas_call(
        kern,
        out_shape=jax.ShapeDtypeStruct((b, tqp, D_BRANCH), BF16),
        grid=(b, nq, 2 * nk),
        in_specs=[
            pl.BlockSpec((1, D_BRANCH, tq), qmap),
            pl.BlockSpec((1, N_HEADS, tq), qmap),
            pl.BlockSpec((1, D_BRANCH, tq), qmap),
            pl.BlockSpec((1, tk, IDX_DIM), k1),
            pl.BlockSpec((1, tk, D_BRANCH), k3),
            pl.BlockSpec((1, tk // KEY_CHUNK, D_BRANCH, KEY_CHUNK), k3c),
        ],
        out_specs=pl.BlockSpec((1, tq, D_BRANCH), lambda bb, i, j: (bb, i, 0)),
        scratch_shapes=[
            pltpu.VMEM((lp, tq), F32),
            pltpu.VMEM((GROUPS, tq), F32),
            pltpu.VMEM((8, tq), F32),
            pltpu.VMEM((8, tq), I32),
            pltpu.VMEM((N_HEADS, LANES, tq), BF16),
            pltpu.VMEM((N_HEADS, tq), F32),
            pltpu.VMEM((N_HEADS, tq), F32),
            pltpu.VMEM((D_BRANCH, tq), F32),
            pltpu.VMEM((N_HEADS, KEY_PAIR, tq), F32),
            pltpu.VMEM((N_HEADS, KEY_PAIR, tq), F32),
        ],
        compiler_params=_cparams(("arbitrary", "arbitrary", "arbitrary")),
        name="dsa_attention",
    )(qiT, wiT, qaT, ki, ka, vaTc)


_AUG = 6


def _fox_kernel(qbT_ref, qaug_ref, kb_ref, kaug_ref, vbT_ref, o_ref,
                qf_ref, m_ref, l_ref, acc_ref, lg0_ref, lg1_ref, *, past, tq, tk, nk):
    i = pl.program_id(1)
    j = pl.program_id(2)
    q0 = past + i * tq
    qpos = q0 + lax.broadcasted_iota(I32, (1, tq), 1)
    nvis = (q0 + tq + tk - 1) // tk
    nquad = tk // KEY_QUAD
    n_free = jnp.clip((q0 + 1 - j * tk) // KEY_QUAD, 0, nquad)
    n_used = jnp.clip((q0 + tq - j * tk + KEY_QUAD - 1) // KEY_QUAD, 0, nquad)

    @pl.when(j == 0)
    def _():
        arow = lax.broadcasted_iota(I32, (LANES, 1), 0)
        qa = qaug_ref[0]
        for h in range(N_HEADS):
            mine = jnp.logical_and(arow >= _AUG * h, arow < _AUG * (h + 1))
            qf_ref[h, 0:LANES, :] = _pair_padded(qbT_ref, h)
            qf_ref[h, LANES:2 * LANES, :] = jnp.where(mine, qa, 0.0).astype(BF16)
        _softmax_init(m_ref, l_ref, acc_ref)

    @pl.when(j < nvis)
    def _():
        def causal(p):
            kpos = j * tk + p * KEY_PAIR + lax.broadcasted_iota(I32, (KEY_PAIR, 1), 0)
            return kpos <= qpos

        def logits(h, p):
            r0 = pl.multiple_of(p * KEY_PAIR, KEY_PAIR)
            kblk = kb_ref[0, pl.ds(r0, KEY_PAIR), (h // 2) * LANES:(h // 2 + 1) * LANES]
            lhs = jnp.concatenate([kblk, kaug_ref[0, pl.ds(r0, KEY_PAIR), :]], axis=1)
            return jnp.dot(lhs, qf_ref[h], preferred_element_type=F32)

        def v_t(h, p):
            rows = slice(h * HEAD_DIM, (h + 1) * HEAD_DIM)
            return jnp.concatenate([vbT_ref[0, 2 * p, rows, :], vbT_ref[0, 2 * p + 1, rows, :]],
                                   axis=1)

        _attend_block(n_free, n_used, logits, causal, v_t, m_ref, l_ref, acc_ref,
                      lg0_ref, lg1_ref)

    @pl.when(j == nk - 1)
    def _():
        _softmax_finish(o_ref, l_ref, acc_ref)


def _fox_attention(qbT, qaug, kb, kaug, vbTc, *, past, tq, tk):
    b, _, tqp = qbT.shape
    lp = kb.shape[1]
    nq, nk = tqp // tq, lp // tk

    def last(i):
        return jnp.minimum((past + (i + 1) * tq + tk - 1) // tk, nk) - 1

    kmap = lambda bb, i, j: (bb, jnp.minimum(j, last(i)), 0)
    kmapc = lambda bb, i, j: (bb, jnp.minimum(j, last(i)), 0, 0)
    qmap = lambda bb, i, j: (bb, 0, i)
    return pl.pallas_call(
        functools.partial(_fox_kernel, past=past, tq=tq, tk=tk, nk=nk),
        out_shape=jax.ShapeDtypeStruct((b, tqp, D_BRANCH), BF16),
        grid=(b, nq, nk),
        in_specs=[
            pl.BlockSpec((1, D_BRANCH, tq), qmap),
            pl.BlockSpec((1, LANES, tq), qmap),
            pl.BlockSpec((1, tk, D_BRANCH), kmap),
            pl.BlockSpec((1, tk, LANES), kmap),
            pl.BlockSpec((1, tk // KEY_CHUNK, D_BRANCH, KEY_CHUNK), kmapc),
        ],
        out_specs=pl.BlockSpec((1, tq, D_BRANCH), lambda bb, i, j: (bb, i, 0)),
        scratch_shapes=[
            pltpu.VMEM((N_HEADS, 2 * LANES, tq), BF16),
            pltpu.VMEM((N_HEADS, tq), F32),
            pltpu.VMEM((N_HEADS, tq), F32),
            pltpu.VMEM((D_BRANCH, tq), F32),
            pltpu.VMEM((N_HEADS, KEY_PAIR, tq), F32),
            pltpu.VMEM((N_HEADS, KEY_PAIR, tq), F32),
        ],
        compiler_params=_cparams(("arbitrary", "arbitrary", "arbitrary")),
        name="fox_attention",
    )(qbT, qaug, kb, kaug, vbTc)


AUG_ROWS = 1024


def _aug_kernel(c_ref, kaug_ref, qaugT_ref):
    x = c_ref[0] * LOG2E
    a = x.astype(BF16).astype(F32)
    r = x - a
    b = r.astype(BF16).astype(F32)
    c = (r - b).astype(BF16).astype(F32)
    lane = lax.broadcasted_iota(I32, (N_HEADS, LANES), 1)
    head = lax.broadcasted_iota(I32, (N_HEADS, LANES), 0)

    def place(v, slot):
        sel = jnp.where(lane == _AUG * head + slot, 1.0, 0.0)
        return jnp.dot(v, sel, preferred_element_type=F32)

    slot = lane - _AUG * head
    ones_k = jnp.sum(jnp.where(jnp.logical_and(slot >= 0, slot < 3), 1.0, 0.0), axis=0, keepdims=True)
    ones_q = jnp.sum(jnp.where(jnp.logical_and(slot >= 3, slot < 6), 1.0, 0.0), axis=0, keepdims=True)
    kaug_ref[0] = (ones_k - (place(a, 3) + place(b, 4) + place(c, 5))).astype(BF16)
    qaugT_ref[0] = (ones_q + (place(a, 0) + place(b, 1) + place(c, 2))).T


def _fox_augment(cum):
    b, lp, h = cum.shape
    return pl.pallas_call(
        _aug_kernel,
        out_shape=(jax.ShapeDtypeStruct((b, lp, LANES), BF16),
                   jax.ShapeDtypeStruct((b, LANES, lp), F32)),
        grid=(b, lp // AUG_ROWS),
        in_specs=[pl.BlockSpec((1, AUG_ROWS, h), lambda i, j: (i, j, 0))],
        out_specs=(pl.BlockSpec((1, AUG_ROWS, LANES), lambda i, j: (i, j, 0)),
                   pl.BlockSpec((1, LANES, AUG_ROWS), lambda i, j: (i, 0, j))),
        compiler_params=_cparams(("arbitrary", "arbitrary")),
        name="fox_augment",
    )(cum)


def _shifted_rows(load, store, u, past, starts_batch, nb):
    tm, c = u.shape
    store(slice(8, 8 + tm), u)
    if nb == 1:
        @pl.when(starts_batch)
        def _():
            store(slice(6, 8), past(0, slice(None)))
    s1 = load(slice(7, 7 + tm))
    s2 = load(slice(6, 6 + tm))
    if nb > 1:
        r = lax.broadcasted_iota(I32, (nb, tm // nb, 1), 1)
        p0 = past(slice(None), slice(0, 1))
        p1 = past(slice(None), slice(1, 2))
        s1 = jnp.where(r == 0, p1, _group(s1, nb)).reshape(tm, c)
        s2 = jnp.where(r == 0, p0, jnp.where(r == 1, p1, _group(s2, nb))).reshape(tm, c)
    store(slice(0, 8), load(slice(tm, tm + 8)))
    return s1, s2


def _merge_kernel(x_ref, ya_ref, yb_ref, cb_ref, u_ref, gl_ref, past_ref, cw_ref, wbr_ref,
                  wo_ref, g_ref, mod_ref, o_ref, ub_ref, *, nb, rows_per_batch):
    i = pl.program_id(0)
    tm = x_ref.shape[0]
    u = u_ref[...]

    def load(rows):
        return ub_ref[rows, :]

    def store(rows, v):
        ub_ref[rows, :] = v

    s1, s2 = _shifted_rows(load, store, u, lambda b, r: past_ref[b, r, :],
                           (i * tm) % rows_per_batch == 0, nb)
    conv = cw_ref[0:1, :] * s2 + cw_ref[1:2, :] * s1 + cw_ref[2:3, :] * u
    yc = (cb_ref[...] * conv).astype(BF16)
    mix = jnp.zeros((tm, D_MODEL), F32)
    for n, y in enumerate((ya_ref[...], yb_ref[...], yc)):
        br = jnp.dot(y, wbr_ref[n], preferred_element_type=F32)
        mix = mix + jax.nn.sigmoid(gl_ref[:, n * D_MODEL:(n + 1) * D_MODEL]) * br
    mo = jnp.dot(mix.astype(BF16), wo_ref[...], preferred_element_type=F32)
    nm = _group(_rms(mo, g_ref[1:2, :]), nb)
    o_ref[...] = x_ref[...] + (mod_ref[:, 2:3, :] * nm).reshape(tm, D_MODEL)


def _mod_spec(tm, nb, rows_per_batch, ngrid):
    if nb == 1:
        f = lambda i, *_: ((i * tm) // rows_per_batch, 0, 0)
    else:
        f = lambda i, *_: (i, 0, 0)
    return pl.BlockSpec((nb, 6, D_MODEL), f)


def _merge(x, ya, yb, cb, u, gl, past, cw, wbr, wo, g, mod, *, tm, rows_per_batch):
    rows = x.shape[0]
    nb = max(1, tm // rows_per_batch)
    row = lambda w: pl.BlockSpec((tm, w), lambda i: (i, 0))
    full = lambda shape: pl.BlockSpec(shape, lambda i: (0,) * len(shape))
    if nb == 1:
        past_spec = pl.BlockSpec((1, 2, D_BRANCH), lambda i: ((i * tm) // rows_per_batch, 0, 0))
    else:
        past_spec = pl.BlockSpec((nb, 2, D_BRANCH), lambda i: (i, 0, 0))
    return pl.pallas_call(
        functools.partial(_merge_kernel, nb=nb, rows_per_batch=rows_per_batch),
        out_shape=jax.ShapeDtypeStruct((rows, D_MODEL), F32),
        grid=(rows // tm,),
        in_specs=[
            row(D_MODEL), row(D_BRANCH), row(D_BRANCH), row(D_BRANCH), row(D_BRANCH),
            row(3 * D_MODEL), past_spec, full((CONV_W, D_BRANCH)),
            full((3, D_BRANCH, D_MODEL)), full((D_MODEL, D_MODEL)), full((4, D_MODEL)),
            _mod_spec(tm, nb, rows_per_batch, 1),
        ],
        out_specs=row(D_MODEL),
        scratch_shapes=[pltpu.VMEM((tm + 8, D_BRANCH), F32)],
        compiler_params=_cparams(("arbitrary",)),
        name="branch_merge",
    )(x, ya, yb, cb, u, gl, past, cw, wbr, wo, g, mod)


FF_CHUNK = 1408
FF_PARTS = ((0, 5 * LANES), (5 * LANES, FF_CHUNK))


def _ffn_kernel(x_ref, g_ref, mod_ref, wg_ref, wv_ref, wd_ref, cw_ref, past_ref,
                o_ref, tail_ref, hs_ref, acc_ref, ub_ref, *, nb, rows_per_batch, nj):
    i = pl.program_id(0)
    j = pl.program_id(1)
    tm = x_ref.shape[0]

    @pl.when(j == 0)
    def _():
        y = _group(_rms(x_ref[...], g_ref[2:3, :]), nb)
        h = y * (1.0 + mod_ref[:, 4:5, :]) + mod_ref[:, 3:4, :]
        hs_ref[...] = h.reshape(tm, D_MODEL).astype(BF16)
        acc_ref[...] = jnp.zeros(acc_ref.shape, F32)

    hs = hs_ref[...]
    grp = tm // nb
    for a, b in FF_PARTS:
        cols = slice(a, b)
        ug = jnp.dot(hs, wg_ref[:, cols], preferred_element_type=F32)
        uv = jnp.dot(hs, wv_ref[:, cols], preferred_element_type=F32)

        def load(rows, cols=cols):
            return ub_ref[j, rows, cols]

        def store(rows, v, cols=cols):
            ub_ref[j, rows, cols] = v

        s1, s2 = _shifted_rows(load, store, ug, lambda bb, r, cols=cols: past_ref[bb, r, cols],
                               (i * tm) % rows_per_batch == 0, nb)
        conv = cw_ref[0:1, cols] * s2 + cw_ref[1:2, cols] * s1 + cw_ref[2:3, cols] * ug
        f = conv * jax.nn.sigmoid(conv) * uv
        acc_ref[...] += jnp.dot(f.astype(BF16), wd_ref[cols, :], preferred_element_type=F32)
        tail_ref[0, :, :, cols] = _group(ug, nb)[:, grp - 8:, :]

    @pl.when(j == nj - 1)
    def _():
        nm = _group(_rms(acc_ref[...], g_ref[3:4, :]), nb)
        o_ref[...] = x_ref[...] + (mod_ref[:, 5:6, :] * nm).reshape(tm, D_MODEL)


def _conv_ffn(x, g, mod, w_up, w_down, cw, past, *, tm, rows_per_batch):
    rows = x.shape[0]
    dff = w_down.shape[0]
    nj = dff // FF_CHUNK
    nb = max(1, tm // rows_per_batch)
    nt = rows // tm
    if nb == 1:
        past_spec = pl.BlockSpec((1, 2, FF_CHUNK), lambda i, j: ((i * tm) // rows_per_batch, 0, j))
        mod_map = lambda i, j: ((i * tm) // rows_per_batch, 0, 0)
    else:
        past_spec = pl.BlockSpec((nb, 2, FF_CHUNK), lambda i, j: (i, 0, j))
        mod_map = lambda i, j: (i, 0, 0)
    out, tails = pl.pallas_call(
        functools.partial(_ffn_kernel, nb=nb, rows_per_batch=rows_per_batch, nj=nj),
        out_shape=(jax.ShapeDtypeStruct((rows, D_MODEL), F32),
                   jax.ShapeDtypeStruct((nt, nb, 8, dff), F32)),
        grid=(nt, nj),
        in_specs=[
            pl.BlockSpec((tm, D_MODEL), lambda i, j: (i, 0)),
            pl.BlockSpec((4, D_MODEL), lambda i, j: (0, 0)),
            pl.BlockSpec((nb, 6, D_MODEL), mod_map),
            pl.BlockSpec((D_MODEL, FF_CHUNK), lambda i, j: (0, j)),
            pl.BlockSpec((D_MODEL, FF_CHUNK), lambda i, j: (0, j + nj)),
            pl.BlockSpec((FF_CHUNK, D_MODEL), lambda i, j: (j, 0)),
            pl.BlockSpec((CONV_W, FF_CHUNK), lambda i, j: (0, j)),
            past_spec,
        ],
        out_specs=(pl.BlockSpec((tm, D_MODEL), lambda i, j: (i, 0)),
                   pl.BlockSpec((1, nb, 8, FF_CHUNK), lambda i, j: (i, 0, 0, j))),
        scratch_shapes=[
            pltpu.VMEM((tm, D_MODEL), BF16),
            pltpu.VMEM((tm, D_MODEL), F32),
            pltpu.VMEM((nj, tm + 8, FF_CHUNK), F32),
        ],
        compiler_params=_cparams(("arbitrary", "arbitrary")),
        name="conv_ffn",
    )(x, g, mod, w_up, w_up, w_down, cw, past)
    return out, tails


def _relayout_w_in(w_in, b_forget):
    db = D_BRANCH
    o_qi = 3 * db
    o_ki = o_qi + N_HEADS * IDX_DIM
    o_wi = o_ki + IDX_DIM
    o_qb = o_wi + N_HEADS
    o_fl = o_qb + 3 * db
    o_cb = o_fl + N_HEADS
    o_gl = o_cb + 3 * db
    main = jnp.concatenate([w_in[:, 0:o_ki], w_in[:, o_qb:o_fl], w_in[:, o_cb:]], axis=1)
    small = jnp.concatenate([w_in[:, o_ki:o_wi], w_in[:, o_wi:o_qb], w_in[:, o_fl:o_cb]], axis=1)
    small = jnp.pad(small, ((0, 0), (0, LANES - small.shape[1])))
    bf = jnp.zeros((1, LANES), F32).at[0, _FL_OFF:_FL_OFF + N_HEADS].set(b_forget)
    del o_gl
    return main.astype(BF16), small.astype(BF16), bf


def _round_up(x, m):
    return (x + m - 1) // m * m


def _layer(x, mod, caches, weights, *, batch, t, past, cfg):
    (g, w_main, w_small, bf_pad, cw_mix, wbr, wo, w_up, cw_ffn, w_down) = weights
    rows = batch * t
    length = past + t
    (qaT, ka32, ka16, va32, vaTc, qiT, qbT, kb32, kb16, vb32, vbTc, cb, u, gl,
     ki32, ki16, wiT, logf) = _in_projection(
        x, mod, g, w_main, w_small, bf_pad, tm=cfg["tm_in"], rows_per_batch=t)

    tq, tk = cfg["tq"], cfg["tk"]
    tqp = _round_up(t, tq)
    lp = _round_up(length, tk)

    def per_batch_T(aT):
        c = aT.shape[0]
        a = jnp.swapaxes(aT.reshape(c, batch, t), 0, 1)
        return jnp.pad(a, ((0, 0), (0, 0), (0, tqp - t)))

    def chunked_T(a):
        return jnp.swapaxes(a.reshape(batch, lp // KEY_CHUNK, KEY_CHUNK, a.shape[-1]), 2, 3)

    def with_cache(cache, new, dtype):
        new = new.reshape(batch, t, -1)
        if cache is not None:
            new = jnp.concatenate([cache.reshape(batch, past, -1).astype(dtype), new.astype(dtype)],
                                  axis=1)
        return jnp.pad(new.astype(dtype), ((0, 0), (0, lp - length), (0, 0)))

    if caches is None:
        c_idx = c_dk = c_dv = c_fk = c_fv = c_lf = None
        past_mix = jnp.zeros((batch, CONV_W - 1, D_BRANCH), F32)
        past_ffn = jnp.zeros((batch, CONV_W - 1, w_down.shape[0]), F32)
    else:
        c_idx, c_dk, c_dv, c_fk, c_fv, c_lf, past_mix, past_ffn = caches

    ki_all = with_cache(c_idx, ki16, BF16)
    ka_all = with_cache(c_dk, ka16, BF16)
    kb_all = with_cache(c_fk, kb16, BF16)
    if caches is None and lp == rows:
        vaT_all = vaTc.reshape(batch, lp // KEY_CHUNK, D_BRANCH, KEY_CHUNK)
        vbT_all = vbTc.reshape(batch, lp // KEY_CHUNK, D_BRANCH, KEY_CHUNK)
    else:
        vaT_all = chunked_T(with_cache(c_dv, va32, BF16))
        vbT_all = chunked_T(with_cache(c_fv, vb32, BF16))
    logf_all = with_cache(c_lf, logf, F32)

    ya = _dsa_attention(per_batch_T(qiT), per_batch_T(wiT), per_batch_T(qaT), ki_all, ka_all,
                        vaT_all, past=past, length=length, tq=tq, tk=tk)
    cum = _forget_cumsum(logf_all)
    kaug, qaug = _fox_augment(cum)
    qaug = jnp.pad(qaug[:, :, past:length], ((0, 0), (0, 0), (0, tqp - t)))
    yb = _fox_attention(per_batch_T(qbT), qaug, kb_all, kaug, vbT_all, past=past, tq=tq,
                        tk=cfg["tk_fox"])
    ya = ya[:, :t].reshape(rows, D_BRANCH)
    yb = yb[:, :t].reshape(rows, D_BRANCH)

    x1 = _merge(x, ya, yb, cb, u, gl, past_mix, cw_mix, wbr, wo, g, mod,
                tm=cfg["tm_merge"], rows_per_batch=t)
    x2, tails = _conv_ffn(x1, g, mod, w_up, w_down, cw_ffn, past_ffn,
                          tm=cfg["tm_ffn"], rows_per_batch=t)

    new_mix = u.reshape(batch, t, D_BRANCH)[:, t - (CONV_W - 1):]
    nb = max(1, cfg["tm_ffn"] // t)
    if nb == 1:
        tiles_per_batch = t // cfg["tm_ffn"]
        last = tails.reshape(batch, tiles_per_batch, 8, -1)[:, -1]
    else:
        last = tails.reshape(batch, 8, -1)
    new_ffn = last[:, 8 - (CONV_W - 1):]
    hd = (batch, t, N_HEADS, HEAD_DIM)
    state = (ki32.reshape(batch, t, IDX_DIM), ka32.reshape(hd), va32.reshape(hd),
             kb32.reshape(hd), vb32.reshape(hd), logf.reshape(batch, t, N_HEADS), new_mix, new_ffn)
    return x2, state


_PROMPT_CFG = dict(tm_in=512, tm_merge=256, tm_ffn=512, tq=256, tk=4096, tk_fox=8192)
_SAMPLE_CFG = dict(tm_in=256, tm_merge=256, tm_ffn=256, tq=128, tk=3072, tk_fox=3072)


def kernel(x_prompt, x_sample, c_prompt, c_sample, cache_idx_k, cache_dsa_k, cache_dsa_v,
           cache_fox_k, cache_fox_v, cache_fox_logf, state_conv_mix, state_conv_ffn,
           w_ada, b_ada, norm_g, w_in, b_forget, conv_mix_w, w_branch, w_out, w_up,
           conv_ffn_w, w_down):
    bp, tp, _ = x_prompt.shape
    bs, ts, _ = x_sample.shape
    past = cache_dsa_k.shape[2]
    depth = w_ada.shape[0]

    c_all = jnp.concatenate([c_prompt, c_sample], axis=0)
    pad_rows = _round_up(c_all.shape[0], 8) - c_all.shape[0]
    c_all = jnp.pad(c_all, ((0, pad_rows), (0, 0)))
    mod_all = _modulation(c_all, w_ada, b_ada).reshape(depth, -1, 6, D_MODEL)

    yp = x_prompt.reshape(bp * tp, D_MODEL)
    ys = x_sample.reshape(bs * ts, D_MODEL)
    p_states, s_states = [], []
    for l in range(depth):
        w_main, w_small, bf_pad = _relayout_w_in(w_in[l], b_forget[l])
        weights = (norm_g[l], w_main, w_small, bf_pad, conv_mix_w[l], w_branch[l].astype(BF16),
                   w_out[l].astype(BF16), w_up[l].astype(BF16), conv_ffn_w[l],
                   w_down[l].astype(BF16))
        yp, st_p = _layer(yp, mod_all[l, :bp], None, weights, batch=bp, t=tp, past=0,
                          cfg=_PROMPT_CFG)
        caches = (cache_idx_k[l], cache_dsa_k[l], cache_dsa_v[l], cache_fox_k[l], cache_fox_v[l],
                  cache_fox_logf[l], state_conv_mix[l], state_conv_ffn[l])
        ys, st_s = _layer(ys, mod_all[l, bp:bp + bs], caches, weights, batch=bs, t=ts, past=past,
                          cfg=_SAMPLE_CFG)
        p_states.append(st_p)
        s_states.append(st_s)

    stack = lambda states: [jnp.stack([st[k] for st in states], axis=0) for k in range(8)]
    return (yp.reshape(bp, tp, D_MODEL), ys.reshape(bs, ts, D_MODEL), *stack(p_states),
            *stack(s_states))
```
